```python
import jax, jax.numpy as jnp
from jax import lax
import numpy as np

D_MODEL = 1024
BATCH = 8
SEQ = 2048
DEPTH = 1

D_MIX = D_MODEL
HEAD_DIM = 64
N_HEADS = 8
N_KV_HEADS = 2
GQA_GROUP = N_HEADS // N_KV_HEADS
D_ATTN = N_HEADS * HEAD_DIM
D_KV = N_KV_HEADS * HEAD_DIM
D_POOL = D_MIX - D_ATTN
POOL_WINDOWS = (2, 4, 8, 16)
N_POOL_GROUPS = len(POOL_WINDOWS)
POOL_GROUP_DIM = D_POOL // N_POOL_GROUPS
D_IN = D_ATTN + 2 * D_KV + D_POOL
WINDOW = 128
BLOCK = 128
N_BUCKETS = 32
MAX_DISTANCE = 128
D_FF = 2816
EPS = 1e-6
NEG = -1e30

kernel_name = "hybrid_swa_sink_pool_macaron"


def _rmsnorm(x, g):
    x32 = x.astype(jnp.float32)
    y = x32 * lax.rsqrt(jnp.mean(x32 * x32, axis=-1, keepdims=True) + EPS)
    return (y * g.astype(jnp.float32)).astype(x.dtype)


def _swiglu(h, w_gate, w_up, w_down):
    return (jax.nn.silu(h @ w_gate) * (h @ w_up)) @ w_down


def _t5_bucket(dist):
    n = np.maximum(dist, 0)
    max_exact = N_BUCKETS // 2
    large = max_exact + (np.log(np.maximum(n, 1) / max_exact)
                         / np.log(MAX_DISTANCE / max_exact)
                         * (N_BUCKETS - max_exact)).astype(np.int32)
    large = np.minimum(large, N_BUCKETS - 1)
    return np.where(n < max_exact, n, large).astype(np.int32)


def _band_structure(n_blocks):
    ql = np.arange(BLOCK)[:, None]
    kl = np.arange(2 * BLOCK)[None, :]
    dist = ql + BLOCK - kl
    bucket = _t5_bucket(dist)
    blk = np.arange(n_blocks)[:, None, None]
    k_abs = blk * BLOCK - BLOCK + kl[None]
    mask = (dist[None] >= 0) & (dist[None] < WINDOW) & (k_abs >= 0)
    return bucket, mask


def _sliding_window_attention(q, k, v, q_gain, k_gain, sinks, rel_bias):
    B, S = q.shape[0], q.shape[1]
    nb = S // BLOCK
    q = _rmsnorm(q, q_gain)
    k = _rmsnorm(k, k_gain)
    bucket, mask = _band_structure(nb)
    bias = jnp.transpose(rel_bias[jnp.asarray(bucket)], (2, 0, 1)).astype(jnp.float32)
    bias = bias.reshape(N_KV_HEADS, GQA_GROUP, BLOCK, 2 * BLOCK)
    mask = jnp.asarray(mask)[None, :, None, None]

    qb = q.reshape(B, nb, BLOCK, N_KV_HEADS, GQA_GROUP, HEAD_DIM)
    pad = ((0, 0), (BLOCK, 0), (0, 0), (0, 0))
    kp = jnp.pad(k, pad).reshape(B, nb + 1, BLOCK, N_KV_HEADS, HEAD_DIM)
    vp = jnp.pad(v, pad).reshape(B, nb + 1, BLOCK, N_KV_HEADS, HEAD_DIM)
    kb = jnp.concatenate([kp[:, :-1], kp[:, 1:]], axis=2)
    vb = jnp.concatenate([vp[:, :-1], vp[:, 1:]], axis=2)

    logits = jnp.einsum('bnqkgd,bnskd->bnkgqs', qb, kb).astype(jnp.float32)
    logits = logits * (HEAD_DIM ** -0.5) + bias[None, None]
    logits = jnp.where(mask, logits, NEG)
    sink = sinks.astype(jnp.float32).reshape(N_KV_HEADS, GQA_GROUP)[None, None, :, :, None, None]
    m = jnp.maximum(jnp.max(logits, axis=-1, keepdims=True), sink)
    e = jnp.exp(logits - m)
    probs = e / (jnp.sum(e, axis=-1, keepdims=True) + jnp.exp(sink - m))
    out = jnp.einsum('bnkgqs,bnskd->bnqkgd', probs.astype(vb.dtype), vb)
    return out.reshape(B, S, D_ATTN)


def _pool_mixer(u, w_pool, scale):
    B, S = u.shape[0], u.shape[1]
    ug = u.reshape(B, S, N_POOL_GROUPS, POOL_GROUP_DIM)
    csum = jnp.cumsum(ug.astype(jnp.float32), axis=1)
    pos = jnp.arange(1, S + 1, dtype=jnp.float32)
    means = []
    for g, w in enumerate(POOL_WINDOWS):
        cg = csum[:, :, g]
        prev = jnp.pad(cg, ((0, 0), (w, 0), (0, 0)))[:, :S]
        cnt = jnp.minimum(pos, float(w))[None, :, None]
        means.append((cg - prev) / cnt)
    pooled = jnp.stack(means, axis=2).astype(u.dtype) - ug
    y = jnp.einsum('bsgc,gcd->bsgd', pooled, w_pool).reshape(B, S, D_POOL)
    return y * scale


def setup_inputs(seed: int = 0) -> dict:
    key = jax.random.key(seed)
    ks = jax.random.split(key, 20)
    nrm = lambda k, shape, fan_in: jax.random.normal(k, shape, jnp.float32) * fan_in ** -0.5
    gain = lambda k, shape: 1.0 + 0.02 * jax.random.normal(k, shape, jnp.float32)
    L = DEPTH
    return {
        "x": jax.random.normal(ks[0], (BATCH, SEQ, D_MODEL), jnp.float32),
        "ffn1_norm": gain(ks[1], (L, D_MODEL)),
        "ffn1_w_gate": nrm(ks[2], (L, D_MODEL, D_FF), D_MODEL),
        "ffn1_w_up": nrm(ks[3], (L, D_MODEL, D_FF), D_MODEL),
        "ffn1_w_down": nrm(ks[4], (L, D_FF, D_MODEL), D_FF),
        "mix_norm": gain(ks[5], (L, D_MODEL)),
        "w_in": nrm(ks[6], (L, D_MODEL, D_IN), D_MODEL),
        "q_norm": gain(ks[7], (L, HEAD_DIM)),
        "k_norm": gain(ks[8], (L, HEAD_DIM)),
        "attn_sinks": 0.5 * jax.random.normal(ks[9], (L, N_HEADS), jnp.float32),
        "rel_bias": 0.1 * jax.random.normal(ks[10], (N_BUCKETS, N_HEADS), jnp.float32),
        "pool_w": nrm(ks[11], (L, N_POOL_GROUPS, POOL_GROUP_DIM, POOL_GROUP_DIM), POOL_GROUP_DIM),
        "pool_scale": gain(ks[12], (L, D_POOL)),
        "w_out": nrm(ks[13], (L, D_MIX, D_MODEL), D_MIX),
        "ffn2_norm": gain(ks[14], (L, D_MODEL)),
        "ffn2_w_gate": nrm(ks[15], (L, D_MODEL, D_FF), D_MODEL),
        "ffn2_w_up": nrm(ks[16], (L, D_MODEL, D_FF), D_MODEL),
        "ffn2_w_down": nrm(ks[17], (L, D_FF, D_MODEL), D_FF),
    }


def reference(x, ffn1_norm, ffn1_w_gate, ffn1_w_up, ffn1_w_down, mix_norm, w_in,
              q_norm, k_norm, attn_sinks, rel_bias, pool_w, pool_scale, w_out,
              ffn2_norm, ffn2_w_gate, ffn2_w_up, ffn2_w_down):
    B, S = x.shape[0], x.shape[1]
    for l in range(DEPTH):
        h = _rmsnorm(x, ffn1_norm[l])
        x = x + 0.5 * _swiglu(h, ffn1_w_gate[l], ffn1_w_up[l], ffn1_w_down[l])
        h = _rmsnorm(x, mix_norm[l])
        z = h @ w_in[l]
        q = z[..., :D_ATTN].reshape(B, S, N_HEADS, HEAD_DIM)
        k = z[..., D_ATTN:D_ATTN + D_KV].reshape(B, S, N_KV_HEADS, HEAD_DIM)
        v = z[..., D_ATTN + D_KV:D_ATTN + 2 * D_KV].reshape(B, S, N_KV_HEADS, HEAD_DIM)
        u = z[..., D_ATTN + 2 * D_KV:]
        y_attn = _sliding_window_attention(q, k, v, q_norm[l], k_norm[l],
                                           attn_sinks[l], rel_bias)
        y_pool = _pool_mixer(u, pool_w[l], pool_scale[l])
        x = x + jnp.concatenate([y_attn, y_pool], axis=-1) @ w_out[l]
        h = _rmsnorm(x, ffn2_norm[l])
        x = x + 0.5 * _swiglu(h, ffn2_w_gate[l], ffn2_w_up[l], ffn2_w_down[l])
    return x
```

```python
import functools

import numpy as np
import jax
import jax.numpy as jnp
from jax import lax
from jax.experimental import pallas as pl
from jax.experimental.pallas import tpu as pltpu

D_MODEL = 1024
HEAD_DIM = 64
N_HEADS = 8
N_KV_HEADS = 2
D_ATTN = N_HEADS * HEAD_DIM
D_KV = N_KV_HEADS * HEAD_DIM
D_POOL = D_MODEL - D_ATTN
POOL_WINDOWS = (2, 4, 8, 16)
POOL_GROUP_DIM = D_POOL // len(POOL_WINDOWS)
D_IN = D_ATTN + 2 * D_KV + D_POOL
WINDOW = 128
BLOCK = 128
N_BUCKETS = 32
MAX_DISTANCE = 128
D_FF = 2816
EPS = 1e-6
NEG = -1e30

V7X_LANES = 128
V7X_MXU_DIM = 256
V7X_VMEM_BYTES = 64 * 1024 * 1024

HEADS_PLAIN = (0, 2, 5, 7)
HEADS_ROLLED = (1, 3, 4, 6)

FFN_TILE_M = 512
FFN_CHUNK_F = 256
PROJ_TILE_M = 512
MIX_TILE_Q = 512


def _vmem_limit(estimate_bytes):
    return int(min(V7X_VMEM_BYTES - (4 << 20), max(estimate_bytes, 16 << 20)))


def _rmsnorm_rows(x32, gain_row):
    ms = jnp.mean(x32 * x32, axis=-1, keepdims=True)
    return x32 * lax.rsqrt(ms + EPS) * gain_row


def _ffn_kernel(x_ref, g_ref, wg_ref, wu_ref, wd_ref, o_ref, h_ref, a_ref):
    x = x_ref[...]
    h_ref[...] = _rmsnorm_rows(x, g_ref[...]).astype(jnp.bfloat16)
    for c in range(D_FF // FFN_CHUNK_F):
        cols = slice(c * FFN_CHUNK_F, (c + 1) * FFN_CHUNK_F)
        h = h_ref[...]
        gate = jnp.dot(h, wg_ref[:, cols], preferred_element_type=jnp.float32)
        up = jnp.dot(h, wu_ref[:, cols], preferred_element_type=jnp.float32)
        act = gate * (1.0 / (1.0 + jnp.exp(-gate))) * up
        a_ref[:, cols] = act.astype(jnp.bfloat16)
    y = jnp.dot(a_ref[...], wd_ref[...], preferred_element_type=jnp.float32)
    o_ref[...] = x_ref[...] + 0.5 * y


def _ffn(x2d, gain, w_gate, w_up, w_down):
    m = x2d.shape[0]
    tm = FFN_TILE_M
    resident = functools.partial(pl.BlockSpec, pipeline_mode=pl.Buffered(1))
    est = (2 * 3 * D_MODEL * D_FF
           + 2 * 2 * tm * D_MODEL * 4
           + tm * D_MODEL * 2 + tm * D_FF * 2
           + 6 * tm * FFN_CHUNK_F * 4 + tm * D_MODEL * 4)
    return pl.pallas_call(
        _ffn_kernel,
        out_shape=jax.ShapeDtypeStruct((m, D_MODEL), jnp.float32),
        grid=(m // tm,),
        in_specs=[
            pl.BlockSpec((tm, D_MODEL), lambda i: (i, 0)),
            resident((1, D_MODEL), lambda i: (0, 0)),
            resident((D_MODEL, D_FF), lambda i: (0, 0)),
            resident((D_MODEL, D_FF), lambda i: (0, 0)),
            resident((D_FF, D_MODEL), lambda i: (0, 0)),
        ],
        out_specs=pl.BlockSpec((tm, D_MODEL), lambda i: (i, 0)),
        scratch_shapes=[
            pltpu.VMEM((tm, D_MODEL), jnp.bfloat16),
            pltpu.VMEM((tm, D_FF), jnp.bfloat16),
        ],
        compiler_params=pltpu.CompilerParams(
            dimension_semantics=("arbitrary",),
            vmem_limit_bytes=_vmem_limit(est + (8 << 20))),
        name="ffn",
    )(x2d, gain.reshape(1, D_MODEL), w_gate.astype(jnp.bfloat16),
      w_up.astype(jnp.bfloat16), w_down.astype(jnp.bfloat16))


def _proj_kernel(x_ref, g_ref, w_ref, qg_ref, kg_ref, dq_ref, dk_ref,
                 q_ref, k_ref, v_ref, u_ref):
    h = _rmsnorm_rows(x_ref[...], g_ref[...]).astype(jnp.bfloat16)
    z = jnp.dot(h, w_ref[...], preferred_element_type=jnp.float32)
    q = z[:, :D_ATTN]
    k = z[:, D_ATTN:D_ATTN + D_KV]
    q_ms = jnp.dot((q * q).astype(jnp.bfloat16), dq_ref[...], preferred_element_type=jnp.float32)
    k_ms = jnp.dot((k * k).astype(jnp.bfloat16), dk_ref[...], preferred_element_type=jnp.float32)
    q_ref[...] = (q * lax.rsqrt(q_ms + EPS) * qg_ref[...]).astype(jnp.bfloat16)
    k_ref[...] = (k * lax.rsqrt(k_ms + EPS) * kg_ref[...]).astype(jnp.bfloat16)
    v_ref[...] = z[:, D_ATTN + D_KV:D_ATTN + 2 * D_KV].astype(jnp.bfloat16)
    u_ref[...] = z[:, D_ATTN + 2 * D_KV:]


def _head_mean_matrix(width):
    idx = np.arange(width) // HEAD_DIM
    return jnp.asarray((idx[:, None] == idx[None, :]).astype(np.float32) / HEAD_DIM, jnp.bfloat16)


def _proj(x2d, gain, w_in, q_gain, k_gain):
    m = x2d.shape[0]
    tm = PROJ_TILE_M
    resident = functools.partial(pl.BlockSpec, pipeline_mode=pl.Buffered(1))
    qg = (jnp.tile(q_gain, N_HEADS) * (HEAD_DIM ** -0.5)).reshape(1, D_ATTN)
    kg = jnp.tile(k_gain, N_KV_HEADS).reshape(1, D_KV)
    est = (2 * D_MODEL * D_IN + 2 * 2 * tm * D_MODEL * 4 + 4 * tm * D_IN * 4
           + 2 * 2 * tm * D_IN * 4)
    return pl.pallas_call(
        _proj_kernel,
        out_shape=(
            jax.ShapeDtypeStruct((m, D_ATTN), jnp.bfloat16),
            jax.ShapeDtypeStruct((m, D_KV), jnp.bfloat16),
            jax.ShapeDtypeStruct((m, D_KV), jnp.bfloat16),
            jax.ShapeDtypeStruct((m, D_POOL), jnp.float32),
        ),
        grid=(m // tm,),
        in_specs=[
            pl.BlockSpec((tm, D_MODEL), lambda i: (i, 0)),
            resident((1, D_MODEL), lambda i: (0, 0)),
            resident((D_MODEL, D_IN), lambda i: (0, 0)),
            resident((1, D_ATTN), lambda i: (0, 0)),
            resident((1, D_KV), lambda i: (0, 0)),
            resident((D_ATTN, D_ATTN), lambda i: (0, 0)),
            resident((D_KV, D_KV), lambda i: (0, 0)),
        ],
        out_specs=(
            pl.BlockSpec((tm, D_ATTN), lambda i: (i, 0)),
            pl.BlockSpec((tm, D_KV), lambda i: (i, 0)),
            pl.BlockSpec((tm, D_KV), lambda i: (i, 0)),
            pl.BlockSpec((tm, D_POOL), lambda i: (i, 0)),
        ),
        compiler_params=pltpu.CompilerParams(
            dimension_semantics=("arbitrary",),
            vmem_limit_bytes=_vmem_limit(est + (8 << 20))),
        name="proj",
    )(x2d, gain.reshape(1, D_MODEL), w_in.astype(jnp.bfloat16), qg, kg,
      _head_mean_matrix(D_ATTN), _head_mean_matrix(D_KV))


def _t5_bucket(dist):
    n = np.maximum(dist, 0)
    max_exact = N_BUCKETS // 2
    large = max_exact + (np.log(np.maximum(n, 1) / max_exact)
                         / np.log(MAX_DISTANCE / max_exact)
                         * (N_BUCKETS - max_exact)).astype(np.int32)
    large = np.minimum(large, N_BUCKETS - 1)
    return np.where(n < max_exact, n, large).astype(np.int32)


def _band_bucket_table():
    ql = np.arange(BLOCK)[:, None]
    kl = np.arange(2 * BLOCK)[None, :]
    dist = ql + BLOCK - kl
    in_band = (dist >= 0) & (dist < WINDOW)
    return np.where(in_band, _t5_bucket(dist), -1).astype(np.int32)


def _bias_kernel(rel_ref, bucket_ref, o_ref):
    bucket = bucket_ref[...]
    for slot, head in enumerate(HEADS_PLAIN + HEADS_ROLLED):
        acc = jnp.full((BLOCK, 2 * BLOCK), NEG, jnp.float32)
        for b in range(N_BUCKETS):
            acc = jnp.where(bucket == b, rel_ref[b, head], acc)
        o_ref[slot] = acc


def _band_bias(rel_bias):
    return pl.pallas_call(
        _bias_kernel,
        out_shape=jax.ShapeDtypeStruct((N_HEADS, BLOCK, 2 * BLOCK), jnp.float32),
        in_specs=[
            pl.BlockSpec(memory_space=pltpu.SMEM),
            pl.BlockSpec(memory_space=pltpu.VMEM),
        ],
        out_specs=pl.BlockSpec(memory_space=pltpu.VMEM),
        name="band_bias",
    )(rel_bias, jnp.asarray(_band_bucket_table()))


def _window_sum_matrices():
    ql = np.arange(BLOCK)[:, None]
    kl = np.arange(2 * BLOCK)[None, :]
    dist = ql + BLOCK - kl
    mats = [((dist >= 0) & (dist < w)).astype(np.float32) for w in POOL_WINDOWS]
    return jnp.asarray(np.stack(mats), jnp.bfloat16)


def _mix_kernel(sink_ref, x_ref, q_ref, k_ref, kh_ref, v_ref, vh_ref, u_ref, uh_ref,
                bias_ref, wsum_ref, pw_ref, ps_ref, wo_ref, o_ref,
                kx_ref, kxr_ref, vx_ref, vxr_ref, ux_ref, y_ref):
    tq = q_ref.shape[1]
    first = pl.program_id(1) == 0
    half = HEAD_DIM

    k_halo = jnp.where(first, jnp.zeros_like(kh_ref[0]), kh_ref[0])
    v_halo = jnp.where(first, jnp.zeros_like(vh_ref[0]), vh_ref[0])
    u_halo = jnp.where(first, jnp.zeros_like(uh_ref[0]), uh_ref[0])
    kx_ref[:BLOCK] = k_halo
    kx_ref[BLOCK:] = k_ref[0]
    vx_ref[:BLOCK] = v_halo
    vx_ref[BLOCK:] = v_ref[0]
    kxr_ref[...] = pltpu.roll(kx_ref[...].astype(jnp.float32), half, 1).astype(jnp.bfloat16)
    vxr_ref[...] = pltpu.roll(vx_ref[...].astype(jnp.float32), half, 1).astype(jnp.bfloat16)
    ux_ref[:BLOCK] = u_halo.astype(jnp.bfloat16)
    ux_ref[BLOCK:] = u_ref[0].astype(jnp.bfloat16)

    lane = lax.broadcasted_iota(jnp.int32, (BLOCK, 2 * half), 1)
    low = lane < half
    col = lax.broadcasted_iota(jnp.int32, (1, 2 * BLOCK), 1)
    halo_penalty = jnp.where(first & (col < BLOCK), NEG, 0.0).astype(jnp.float32)
    row = lax.broadcasted_iota(jnp.int32, (BLOCK, 1), 0)

    for j in range(tq // BLOCK):
        rows = slice(j * BLOCK, (j + 1) * BLOCK)
        keys = slice(j * BLOCK, (j + 2) * BLOCK)
        q = q_ref[0, rows, :]
        zero = jnp.zeros((BLOCK, 2 * half), q.dtype)
        outs = {}
        for heads, k_src, v_src, slot0 in ((HEADS_PLAIN, kx_ref, vx_ref, 0),
                                           (HEADS_ROLLED, kxr_ref, vxr_ref, len(HEADS_PLAIN))):
            pieces = []
            for hd in heads:
                pair = q[:, (hd // 2) * 2 * half:(hd // 2 + 1) * 2 * half]
                pieces.append(jnp.where(low if hd % 2 == 0 else ~low, pair, zero))
            qs = jnp.concatenate(pieces, axis=0)
            logits = lax.dot_general(qs, k_src[keys, :], (((1,), (1,)), ((), ())),
                                     preferred_element_type=jnp.float32)
            es, inv = [], []
            for s, hd in enumerate(heads):
                lg = logits[s * BLOCK:(s + 1) * BLOCK] + bias_ref[slot0 + s]
                if j == 0:
                    lg = lg + halo_penalty
                sink = sink_ref[hd]
                m = jnp.maximum(jnp.max(lg, axis=-1, keepdims=True), sink)
                e = jnp.exp(lg - m)
                denom = jnp.sum(e, axis=-1, keepdims=True) + jnp.exp(sink - m)
                es.append(e.astype(jnp.bfloat16))
                inv.append(1.0 / denom)
            pv = jnp.dot(jnp.concatenate(es, axis=0), v_src[keys, :],
                         preferred_element_type=jnp.float32)
            for s, hd in enumerate(heads):
                outs[hd] = pv[s * BLOCK:(s + 1) * BLOCK] * inv[s]
        for p in range(N_HEADS // 2):
            y_ref[rows, p * 2 * half:(p + 1) * 2 * half] = jnp.where(
                low, outs[2 * p], outs[2 * p + 1]).astype(jnp.bfloat16)

        t_glob = pl.program_id(1) * tq + j * BLOCK + row
        for g, w in enumerate(POOL_WINDOWS):
            gc = slice(g * POOL_GROUP_DIM, (g + 1) * POOL_GROUP_DIM)
            wsum = jnp.dot(wsum_ref[g], ux_ref[keys, gc], preferred_element_type=jnp.float32)
            cnt = jnp.minimum(t_glob + 1, w).astype(jnp.float32)
            pooled = wsum * (1.0 / cnt) - u_ref[0, rows, gc]
            yp = jnp.dot(pooled.astype(jnp.bfloat16), pw_ref[g], preferred_element_type=jnp.float32)
            y_ref[rows, D_ATTN + g * POOL_GROUP_DIM:D_ATTN + (g + 1) * POOL_GROUP_DIM] = (
                yp * ps_ref[:, gc]).astype(jnp.bfloat16)

    o_ref[0] = x_ref[0] + jnp.dot(y_ref[...], wo_ref[...], preferred_element_type=jnp.float32)


def _mix(x3d, q, k, v, u, sinks, bias, pool_w, pool_scale, w_out):
    b, s, _ = x3d.shape
    tq = MIX_TILE_Q
    nblk = tq // BLOCK
    resident = functools.partial(pl.BlockSpec, pipeline_mode=pl.Buffered(1))
    tile = lambda width: pl.BlockSpec((1, tq, width), lambda bi, i: (bi, i, 0))
    halo = lambda width: pl.BlockSpec(
        (1, BLOCK, width), lambda bi, i: (bi, jnp.maximum(i * nblk - 1, 0), 0))
    est = (2 * 2 * tq * D_MODEL * 4 + 2 * tq * (D_ATTN + 2 * D_KV) * 2 + 2 * tq * D_POOL * 4
           + 2 * BLOCK * (2 * D_KV * 2 + D_POOL * 4)
           + N_HEADS * BLOCK * 2 * BLOCK * 4 + 2 * D_MODEL * D_MODEL
           + 4 * (tq + BLOCK) * D_KV * 2 + (tq + BLOCK) * D_POOL * 2 + tq * D_MODEL * 2
           + 16 * BLOCK * 2 * BLOCK * 4 * 4)
    return pl.pallas_call(
        _mix_kernel,
        out_shape=jax.ShapeDtypeStruct((b, s, D_MODEL), jnp.float32),
        grid=(b, s // tq),
        in_specs=[
            pl.BlockSpec(memory_space=pltpu.SMEM),
            tile(D_MODEL),
            tile(D_ATTN),
            tile(D_KV), halo(D_KV),
            tile(D_KV), halo(D_KV),
            tile(D_POOL), halo(D_POOL),
            resident((N_HEADS, BLOCK, 2 * BLOCK), lambda bi, i: (0, 0, 0)),
            resident((len(POOL_WINDOWS), BLOCK, 2 * BLOCK), lambda bi, i: (0, 0, 0)),
            resident((len(POOL_WINDOWS), POOL_GROUP_DIM, POOL_GROUP_DIM), lambda bi, i: (0, 0, 0)),
            resident((1, D_POOL), lambda bi, i: (0, 0)),
            resident((D_MODEL, D_MODEL), lambda bi, i: (0, 0)),
        ],
        out_specs=tile(D_MODEL),
        scratch_shapes=[
            pltpu.VMEM((tq + BLOCK, D_KV), jnp.bfloat16),
            pltpu.VMEM((tq + BLOCK, D_KV), jnp.bfloat16),
            pltpu.VMEM((tq + BLOCK, D_KV), jnp.bfloat16),
            pltpu.VMEM((tq + BLOCK, D_KV), jnp.bfloat16),
            pltpu.VMEM((tq + BLOCK, D_POOL), jnp.bfloat16),
            pltpu.VMEM((tq, D_MODEL), jnp.bfloat16),
        ],
        compiler_params=pltpu.CompilerParams(
            dimension_semantics=("arbitrary", "arbitrary"),
            vmem_limit_bytes=_vmem_limit(est + (8 << 20))),
        name="mix",
    )(sinks, x3d, q, k, k, v, v, u, u, bias, _window_sum_matrices(),
      pool_w.astype(jnp.bfloat16), pool_scale.reshape(1, D_POOL), w_out.astype(jnp.bfloat16))


def kernel(x, ffn1_norm, ffn1_w_gate, ffn1_w_up, ffn1_w_down, mix_norm, w_in, q_norm, k_norm,
           attn_sinks, rel_bias, pool_w, pool_scale, w_out, ffn2_norm, ffn2_w_gate, ffn2_w_up,
           ffn2_w_down):
    b, s, d = x.shape
    assert (d, s % MIX_TILE_Q, (b * s) % FFN_TILE_M) == (D_MODEL, 0, 0)
    bias = _band_bias(rel_bias)
    for l in range(ffn1_norm.shape[0]):
        x2d = x.reshape(b * s, d)
        x2d = _ffn(x2d, ffn1_norm[l], ffn1_w_gate[l], ffn1_w_up[l], ffn1_w_down[l])
        q, k, v, u = _proj(x2d, mix_norm[l], w_in[l], q_norm[l], k_norm[l])
        x3d = _mix(x2d.reshape(b, s, d), q.reshape(b, s, D_ATTN), k.reshape(b, s, D_KV),
                   v.reshape(b, s, D_KV), u.reshape(b, s, D_POOL), attn_sinks[l], bias,
                   pool_w[l], pool_scale[l], w_out[l])
        x2d = _ffn(x3d.reshape(b * s, d), ffn2_norm[l], ffn2_w_gate[l], ffn2_w_up[l], ffn2_w_down[l])
        x = x2d.reshape(b, s, d)
    return x
```

```python
import functools

import numpy as np
import jax
import jax.numpy as jnp
from jax import lax
from jax.experimental import pallas as pl
from jax.experimental.pallas import tpu as pltpu

D_MODEL = 1024
HEAD_DIM = 64
N_HEADS = 8
N_KV_HEADS = 2
D_ATTN = N_HEADS * HEAD_DIM
D_KV = N_KV_HEADS * HEAD_DIM
D_POOL = D_MODEL - D_ATTN
POOL_WINDOWS = (2, 4, 8, 16)
POOL_GROUP_DIM = D_POOL // len(POOL_WINDOWS)
D_IN = D_ATTN + 2 * D_KV + D_POOL
WINDOW = 128
BLOCK = 128
N_BUCKETS = 32
MAX_DISTANCE = 128
D_FF = 2816
EPS = 1e-6
NEG = -1e30

V7X_LANES = 128
V7X_MXU_DIM = 256
V7X_VMEM_BYTES = 64 * 1024 * 1024

HEADS_PLAIN = (0, 2, 5, 7)
HEADS_ROLLED = (1, 3, 4, 6)

FFN_TILE_M = 512
FFN_CHUNK_F = 256
PROJ_TILE_M = 512
MIX_TILE_Q = 512


def _vmem_limit(estimate_bytes):
    return int(min(V7X_VMEM_BYTES - (4 << 20), max(estimate_bytes, 16 << 20)))


def _rmsnorm_rows(x32, gain_row):
    ms = jnp.mean(x32 * x32, axis=-1, keepdims=True)
    return x32 * lax.rsqrt(ms + EPS) * gain_row


def _ffn_kernel(x_ref, g_ref, wg_ref, wu_ref, wd_ref, o_ref, h_ref, a_ref):
    h_ref[...] = _rmsnorm_rows(x_ref[...], g_ref[...]).astype(jnp.bfloat16)
    for c in range(D_FF // FFN_CHUNK_F):
        cols = slice(c * FFN_CHUNK_F, (c + 1) * FFN_CHUNK_F)
        h = h_ref[...]
        gate = jnp.dot(h, wg_ref[:, cols], preferred_element_type=jnp.float32)
        up = jnp.dot(h, wu_ref[:, cols], preferred_element_type=jnp.float32)
        act = gate * (1.0 / (1.0 + jnp.exp(-gate))) * up
        a_ref[:, cols] = act.astype(jnp.bfloat16)
    y = jnp.dot(a_ref[...], wd_ref[...], preferred_element_type=jnp.float32)
    o_ref[...] = x_ref[...] + 0.5 * y


def _ffn(x2d, gain, w_gate, w_up, w_down):
    m = x2d.shape[0]
    tm = FFN_TILE_M
    resident = functools.partial(pl.BlockSpec, pipeline_mode=pl.Buffered(1))
    est = (2 * 3 * D_MODEL * D_FF
           + 2 * 2 * tm * D_MODEL * 4
           + tm * D_MODEL * 2 + tm * D_FF * 2
           + 6 * tm * FFN_CHUNK_F * 4 + tm * D_MODEL * 4)
    return pl.pallas_call(
        _ffn_kernel,
        out_shape=jax.ShapeDtypeStruct((m, D_MODEL), jnp.float32),
        grid=(m // tm,),
        in_specs=[
            pl.BlockSpec((tm, D_MODEL), lambda i: (i, 0)),
            resident((1, D_MODEL), lambda i: (0, 0)),
            resident((D_MODEL, D_FF), lambda i: (0, 0)),
            resident((D_MODEL, D_FF), lambda i: (0, 0)),
            resident((D_FF, D_MODEL), lambda i: (0, 0)),
        ],
        out_specs=pl.BlockSpec((tm, D_MODEL), lambda i: (i, 0)),
        scratch_shapes=[
            pltpu.VMEM((tm, D_MODEL), jnp.bfloat16),
            pltpu.VMEM((tm, D_FF), jnp.bfloat16),
        ],
        compiler_params=pltpu.CompilerParams(
            dimension_semantics=("arbitrary",),
            vmem_limit_bytes=_vmem_limit(est + (8 << 20))),
        name="ffn",
    )(x2d, gain.reshape(1, D_MODEL), w_gate.astype(jnp.bfloat16),
      w_up.astype(jnp.bfloat16), w_down.astype(jnp.bfloat16))


def _proj_kernel(x_ref, g_ref, w_ref, qg_ref, kg_ref, dm_ref,
                 q_ref, k_ref, v_ref, u_ref):
    tm = x_ref.shape[0]
    h = _rmsnorm_rows(x_ref[...], g_ref[...]).astype(jnp.bfloat16)
    z = jnp.dot(h, w_ref[...], preferred_element_type=jnp.float32)
    n_slab = (D_ATTN + 2 * D_KV) // V7X_MXU_DIM
    sq = jnp.concatenate(
        [z[:, i * V7X_MXU_DIM:(i + 1) * V7X_MXU_DIM] for i in range(n_slab)], axis=0)
    ms = jnp.dot((sq * sq).astype(jnp.bfloat16), dm_ref[...], preferred_element_type=jnp.float32)
    q_ms = jnp.concatenate([ms[i * tm:(i + 1) * tm] for i in range(n_slab - 1)], axis=1)
    k_ms = ms[(n_slab - 1) * tm:, :D_KV]
    q = z[:, :D_ATTN]
    k = z[:, D_ATTN:D_ATTN + D_KV]
    q_ref[...] = (q * lax.rsqrt(q_ms + EPS) * qg_ref[...]).astype(jnp.bfloat16)
    k_ref[...] = (k * lax.rsqrt(k_ms + EPS) * kg_ref[...]).astype(jnp.bfloat16)
    v_ref[...] = z[:, D_ATTN + D_KV:D_ATTN + 2 * D_KV].astype(jnp.bfloat16)
    u_ref[...] = z[:, D_ATTN + 2 * D_KV:]


def _head_mean_matrix(width):
    idx = np.arange(width) // HEAD_DIM
    return jnp.asarray((idx[:, None] == idx[None, :]).astype(np.float32) / HEAD_DIM, jnp.bfloat16)


def _proj(x2d, gain, w_in, q_gain, k_gain):
    m = x2d.shape[0]
    tm = PROJ_TILE_M
    resident = functools.partial(pl.BlockSpec, pipeline_mode=pl.Buffered(1))
    qg = (jnp.tile(q_gain, N_HEADS) * (HEAD_DIM ** -0.5)).reshape(1, D_ATTN)
    kg = jnp.tile(k_gain, N_KV_HEADS).reshape(1, D_KV)
    est = (2 * D_MODEL * D_IN + 2 * 2 * tm * D_MODEL * 4 + 4 * tm * D_IN * 4
           + 2 * 2 * tm * D_IN * 4)
    return pl.pallas_call(
        _proj_kernel,
        out_shape=(
            jax.ShapeDtypeStruct((m, D_ATTN), jnp.bfloat16),
            jax.ShapeDtypeStruct((m, D_KV), jnp.bfloat16),
            jax.ShapeDtypeStruct((m, D_KV), jnp.bfloat16),
            jax.ShapeDtypeStruct((m, D_POOL), jnp.float32),
        ),
        grid=(m // tm,),
        in_specs=[
            pl.BlockSpec((tm, D_MODEL), lambda i: (i, 0)),
            resident((1, D_MODEL), lambda i: (0, 0)),
            resident((D_MODEL, D_IN), lambda i: (0, 0)),
            resident((1, D_ATTN), lambda i: (0, 0)),
            resident((1, D_KV), lambda i: (0, 0)),
            resident((V7X_MXU_DIM, V7X_MXU_DIM), lambda i: (0, 0)),
        ],
        out_specs=(
            pl.BlockSpec((tm, D_ATTN), lambda i: (i, 0)),
            pl.BlockSpec((tm, D_KV), lambda i: (i, 0)),
            pl.BlockSpec((tm, D_KV), lambda i: (i, 0)),
            pl.BlockSpec((tm, D_POOL), lambda i: (i, 0)),
        ),
        compiler_params=pltpu.CompilerParams(
            dimension_semantics=("arbitrary",),
            vmem_limit_bytes=_vmem_limit(est + (8 << 20))),
        name="proj",
    )(x2d, gain.reshape(1, D_MODEL), w_in.astype(jnp.bfloat16), qg, kg,
      _head_mean_matrix(V7X_MXU_DIM))


def _t5_bucket(dist):
    n = np.maximum(dist, 0)
    max_exact = N_BUCKETS // 2
    large = max_exact + (np.log(np.maximum(n, 1) / max_exact)
                         / np.log(MAX_DISTANCE / max_exact)
                         * (N_BUCKETS - max_exact)).astype(np.int32)
    large = np.minimum(large, N_BUCKETS - 1)
    return np.where(n < max_exact, n, large).astype(np.int32)


def _band_bucket_table():
    ql = np.arange(BLOCK)[:, None]
    kl = np.arange(2 * BLOCK)[None, :]
    dist = ql + BLOCK - kl
    in_band = (dist >= 0) & (dist < WINDOW)
    return np.where(in_band, _t5_bucket(dist), -1).astype(np.int32)


def _bias_kernel(rel_ref, bucket_ref, o_ref):
    bucket = bucket_ref[...]
    for slot, head in enumerate(HEADS_PLAIN + HEADS_ROLLED):
        acc = jnp.full((BLOCK, 2 * BLOCK), NEG, jnp.float32)
        for b in range(N_BUCKETS):
            acc = jnp.where(bucket == b, rel_ref[b, head], acc)
        o_ref[slot] = acc


def _band_bias(rel_bias):
    return pl.pallas_call(
        _bias_kernel,
        out_shape=jax.ShapeDtypeStruct((N_HEADS, BLOCK, 2 * BLOCK), jnp.float32),
        in_specs=[
            pl.BlockSpec(memory_space=pltpu.SMEM),
            pl.BlockSpec(memory_space=pltpu.VMEM),
        ],
        out_specs=pl.BlockSpec(memory_space=pltpu.VMEM),
        name="band_bias",
    )(rel_bias, jnp.asarray(_band_bucket_table()))


def _window_sum_matrices():
    ql = np.arange(BLOCK)[:, None]
    kl = np.arange(2 * BLOCK)[None, :]
    dist = ql + BLOCK - kl
    mats = [((dist >= 0) & (dist < w)).astype(np.float32) for w in POOL_WINDOWS]
    return jnp.asarray(np.stack(mats), jnp.bfloat16)


def _block_diagonal(w):
    g, c, _ = w.shape
    on_diagonal = jnp.eye(g, dtype=bool)[:, None, :, None]
    return jnp.where(on_diagonal, w[:, :, None, :], jnp.zeros((), w.dtype)).reshape(g * c, g * c)


def _mix_kernel(sink_ref, x_ref, q_ref, k_ref, kh_ref, v_ref, vh_ref, u_ref, uh_ref,
                bias_ref, wsum_ref, pw_ref, ps_ref, wo_ref, o_ref,
                kx_ref, kxr_ref, vx_ref, vxr_ref, ux_ref, pooled_ref, y_ref):
    tq = q_ref.shape[1]
    n_blocks = tq // BLOCK
    first = pl.program_id(1) == 0
    half = HEAD_DIM

    k_halo = jnp.where(first, jnp.zeros_like(kh_ref[0]), kh_ref[0])
    v_halo = jnp.where(first, jnp.zeros_like(vh_ref[0]), vh_ref[0])
    u_halo = jnp.where(first, jnp.zeros_like(uh_ref[0]), uh_ref[0])
    kx_ref[:BLOCK] = k_halo
    kx_ref[BLOCK:] = k_ref[0]
    kxr_ref[...] = pltpu.roll(kx_ref[...].astype(jnp.float32), half, 1).astype(jnp.bfloat16)
    ones = jnp.ones((tq + BLOCK, D_KV), jnp.bfloat16)
    vx_ref[:BLOCK, :D_KV] = v_halo
    vx_ref[BLOCK:, :D_KV] = v_ref[0]
    vx_ref[:, D_KV:] = ones
    vxr_ref[:, :D_KV] = pltpu.roll(vx_ref[:, :D_KV].astype(jnp.float32), half, 1).astype(jnp.bfloat16)
    vxr_ref[:, D_KV:] = ones
    ux_ref[:BLOCK] = u_halo.astype(jnp.bfloat16)
    ux_ref[BLOCK:] = u_ref[0].astype(jnp.bfloat16)

    lane = lax.broadcasted_iota(jnp.int32, (BLOCK, 2 * half), 1)
    low = lane < half
    col = lax.broadcasted_iota(jnp.int32, (1, 2 * BLOCK), 1)
    halo_penalty = jnp.where(first & (col < BLOCK), NEG, 0.0).astype(jnp.float32)
    row = lax.broadcasted_iota(jnp.int32, (BLOCK, 1), 0)
    groups = ((HEADS_PLAIN, kx_ref, vx_ref, 0), (HEADS_ROLLED, kxr_ref, vxr_ref, len(HEADS_PLAIN)))

    def rows_of(j):
        return slice(j * BLOCK, (j + 1) * BLOCK)

    def keys_of(j):
        return slice(j * BLOCK, (j + 2) * BLOCK)

    def scores(j, grp):
        heads, k_src, _, _ = groups[grp]
        q = q_ref[0, rows_of(j), :]
        zero = jnp.zeros((BLOCK, 2 * half), q.dtype)
        pieces = []
        for hd in heads:
            pair = q[:, (hd // 2) * 2 * half:(hd // 2 + 1) * 2 * half]
            pieces.append(jnp.where(low if hd % 2 == 0 else ~low, pair, zero))
        qs = jnp.concatenate(pieces, axis=0)
        return lax.dot_general(qs, k_src[keys_of(j), :], (((1,), (1,)), ((), ())),
                               preferred_element_type=jnp.float32)

    def softmax_numerators(j, grp, logits):
        heads, _, _, slot0 = groups[grp]
        es, sink_terms = [], []
        for s, hd in enumerate(heads):
            lg = logits[s * BLOCK:(s + 1) * BLOCK] + bias_ref[slot0 + s]
            if j == 0:
                lg = lg + halo_penalty
            sink = sink_ref[hd]
            m = jnp.maximum(jnp.max(lg, axis=-1, keepdims=True), sink)
            es.append(jnp.exp(lg - m).astype(jnp.bfloat16))
            sink_terms.append(jnp.exp(sink - m))
        return jnp.concatenate(es, axis=0), sink_terms

    def attend(j, grp, e, sink_terms, outs):
        heads, _, v_src, _ = groups[grp]
        pv = jnp.dot(e, v_src[keys_of(j), :], preferred_element_type=jnp.float32)
        for s, hd in enumerate(heads):
            blk = pv[s * BLOCK:(s + 1) * BLOCK]
            outs[hd] = blk[:, :D_KV] / (blk[:, D_KV:] + sink_terms[s])

    def store_attention(j, outs):
        for p in range(N_HEADS // 2):
            y_ref[rows_of(j), p * 2 * half:(p + 1) * 2 * half] = jnp.where(
                low, outs[2 * p], outs[2 * p + 1]).astype(jnp.bfloat16)

    def pool_means(j):
        t_glob = pl.program_id(1) * tq + j * BLOCK + row
        for g, w in enumerate(POOL_WINDOWS):
            gc = slice(g * POOL_GROUP_DIM, (g + 1) * POOL_GROUP_DIM)
            wsum = jnp.dot(wsum_ref[g], ux_ref[keys_of(j), gc], preferred_element_type=jnp.float32)
            cnt = jnp.minimum(t_glob + 1, w).astype(jnp.float32)
            pooled = wsum * (1.0 / cnt) - u_ref[0, rows_of(j), gc]
            pooled_ref[rows_of(j), gc] = pooled.astype(jnp.bfloat16)

    def project_attention(j):
        o_ref[0, rows_of(j), :] += jnp.dot(
            y_ref[rows_of(j), :], wo_ref[:D_ATTN, :], preferred_element_type=jnp.float32)

    for j in range(n_blocks):
        pool_means(j)
    logits = scores(0, 0)
    for j in range(n_blocks):
        outs = {}
        logits_b = scores(j, 1)
        e, sink_terms = softmax_numerators(j, 0, logits)
        if j == 0:
            y_pool = jnp.dot(pooled_ref[...], pw_ref[...],
                             preferred_element_type=jnp.float32) * ps_ref[...]
        else:
            project_attention(j - 1)
        attend(j, 0, e, sink_terms, outs)
        e, sink_terms = softmax_numerators(j, 1, logits_b)
        if j == 0:
            o_ref[0] = x_ref[0] + jnp.dot(y_pool.astype(jnp.bfloat16), wo_ref[D_ATTN:, :],
                                          preferred_element_type=jnp.float32)
        if j + 1 < n_blocks:
            logits = scores(j + 1, 0)
        attend(j, 1, e, sink_terms, outs)
        store_attention(j, outs)
    project_attention(n_blocks - 1)


def _mix(x3d, q, k, v, u, sinks, bias, pool_w, pool_scale, w_out):
    b, s, _ = x3d.shape
    tq = MIX_TILE_Q
    nblk = tq // BLOCK
    resident = functools.partial(pl.BlockSpec, pipeline_mode=pl.Buffered(1))
    tile = lambda width: pl.BlockSpec((1, tq, width), lambda bi, i: (bi, i, 0))
    halo = lambda width: pl.BlockSpec(
        (1, BLOCK, width), lambda bi, i: (bi, jnp.maximum(i * nblk - 1, 0), 0))
    est = (2 * 2 * tq * D_MODEL * 4 + 2 * tq * (D_ATTN + 2 * D_KV) * 2 + 2 * tq * D_POOL * 4
           + 2 * BLOCK * (2 * D_KV * 2 + D_POOL * 4)
           + N_HEADS * BLOCK * 2 * BLOCK * 4 + 2 * D_MODEL * D_MODEL
           + 6 * (tq + BLOCK) * D_KV * 2 + (tq + BLOCK) * D_POOL * 2 + tq * D_MODEL * 2
           + 16 * BLOCK * 2 * BLOCK * 4 * 4)
    return pl.pallas_call(
        _mix_kernel,
        out_shape=jax.ShapeDtypeStruct((b, s, D_MODEL), jnp.float32),
        grid=(b, s // tq),
        in_specs=[
            pl.BlockSpec(memory_space=pltpu.SMEM),
            tile(D_MODEL),
            tile(D_ATTN),
            tile(D_KV), halo(D_KV),
            tile(D_KV), halo(D_KV),
            tile(D_POOL), halo(D_POOL),
            resident((N_HEADS, BLOCK, 2 * BLOCK), lambda bi, i: (0, 0, 0)),
            resident((len(POOL_WINDOWS), BLOCK, 2 * BLOCK), lambda bi, i: (0, 0, 0)),
            resident((D_POOL, D_POOL), lambda bi, i: (0, 0)),
            resident((1, D_POOL), lambda bi, i: (0, 0)),
            resident((D_MODEL, D_MODEL), lambda bi, i: (0, 0)),
        ],
        out_specs=tile(D_MODEL),
        scratch_shapes=[
            pltpu.VMEM((tq + BLOCK, D_KV), jnp.bfloat16),
            pltpu.VMEM((tq + BLOCK, D_KV), jnp.bfloat16),
            pltpu.VMEM((tq + BLOCK, 2 * D_KV), jnp.bfloat16),
            pltpu.VMEM((tq + BLOCK, 2 * D_KV), jnp.bfloat16),
            pltpu.VMEM((tq + BLOCK, D_POOL), jnp.bfloat16),
            pltpu.VMEM((tq, D_POOL), jnp.bfloat16),
            pltpu.VMEM((tq, D_ATTN), jnp.bfloat16),
        ],
        compiler_params=pltpu.CompilerParams(
            dimension_semantics=("arbitrary", "arbitrary"),
            vmem_limit_bytes=_vmem_limit(est + (8 << 20))),
        name="mix",
    )(sinks, x3d, q, k, k, v, v, u, u, bias, _window_sum_matrices(),
      _block_diagonal(pool_w.astype(jnp.bfloat16)), pool_scale.reshape(1, D_POOL),
      w_out.astype(jnp.bfloat16))


def kernel(x, ffn1_norm, ffn1_w_gate, ffn1_w_up, ffn1_w_down, mix_norm, w_in, q_norm, k_norm,
           attn_sinks, rel_bias, pool_w, pool_scale, w_out, ffn2_norm, ffn2_w_gate, ffn2_w_up,
           ffn2_w_down):
    b, s, d = x.shape
    assert (d, s % MIX_TILE_Q, (b * s) % FFN_TILE_M) == (D_MODEL, 0, 0)
    bias = _band_bias(rel_bias)
    for l in range(ffn1_norm.shape[0]):
        x2d = x.reshape(b * s, d)
        x2d = _ffn(x2d, ffn1_norm[l], ffn1_w_gate[l], ffn1_w_up[l], ffn1_w_down[l])
        q, k, v, u = _proj(x2d, mix_norm[l], w_in[l], q_norm[l], k_norm[l])
        x3d = _mix(x2d.reshape(b, s, d), q.reshape(b, s, D_ATTN), k.reshape(b, s, D_KV),
                   v.reshape(b, s, D_KV), u.reshape(b, s, D_POOL), attn_sinks[l], bias,
                   pool_w[l], pool_scale[l], w_out[l])
        x2d = _ffn(x3d.reshape(b * s, d), ffn2_norm[l], ffn2_w_gate[l], ffn2_w_up[l], ffn2_w_down[l])
        x = x2d.reshape(b, s, d)
    return x
```

```python
import functools

import numpy as np
import jax
import jax.numpy as jnp
from jax import lax
from jax.experimental import pallas as pl
from jax.experimental.pallas import tpu as pltpu

D_MODEL = 1024
HEAD_DIM = 64
N_HEADS = 8
N_KV_HEADS = 2
D_ATTN = N_HEADS * HEAD_DIM
D_KV = N_KV_HEADS * HEAD_DIM
D_POOL = D_MODEL - D_ATTN
POOL_WINDOWS = (2, 4, 8, 16)
POOL_GROUP_DIM = D_POOL // len(POOL_WINDOWS)
D_IN = D_ATTN + 2 * D_KV + D_POOL
WINDOW = 128
BLOCK = 128
N_BUCKETS = 32
MAX_DISTANCE = 128
D_FF = 2816
EPS = 1e-6
NEG = -1e30

V7X_LANES = 128
V7X_MXU_DIM = 256
V7X_VMEM_BYTES = 64 * 1024 * 1024

HEADS_PLAIN = (0, 2, 5, 7)
HEADS_ROLLED = (1, 3, 4, 6)

FFN_TILE_M = 512
FFN_CHUNK_F = 256
FFN_STAGE_SLOTS = 2
PROJ_TILE_M = 512
MIX_TILE_Q = 512


def _vmem_limit(estimate_bytes):
    return int(min(V7X_VMEM_BYTES - (4 << 20), max(estimate_bytes, 16 << 20)))


def _rmsnorm_rows(x32, gain_row):
    ms = jnp.mean(x32 * x32, axis=-1, keepdims=True)
    return x32 * lax.rsqrt(ms + EPS) * gain_row


def _ffn_kernel(x_ref, g_ref, wg_hbm, wu_hbm, wd_hbm, o_ref,
                wg_ref, wu_ref, wd_ref, stage_in_ref, stage_out_ref, sem_ref, h_ref, a_ref):
    n_chunks = D_FF // FFN_CHUNK_F

    def chunk_copy(which, c):
        slot = c % FFN_STAGE_SLOTS
        window = pl.ds(c * FFN_CHUNK_F, FFN_CHUNK_F)
        sem = sem_ref.at[which * FFN_STAGE_SLOTS + slot]
        if which == 2:
            return pltpu.make_async_copy(wd_hbm.at[window, :], stage_out_ref.at[slot], sem)
        src = (wg_hbm, wu_hbm)[which]
        return pltpu.make_async_copy(src.at[:, window],
                                     stage_in_ref.at[which * FFN_STAGE_SLOTS + slot], sem)

    def land(which, c):
        slot = c % FFN_STAGE_SLOTS
        window = slice(c * FFN_CHUNK_F, (c + 1) * FFN_CHUNK_F)
        chunk_copy(which, c).wait()
        if which == 2:
            wd_ref[window, :] = stage_out_ref[slot].astype(jnp.bfloat16)
        else:
            dst = (wg_ref, wu_ref)[which]
            dst[:, window] = stage_in_ref[which * FFN_STAGE_SLOTS + slot].astype(jnp.bfloat16)
        if c + FFN_STAGE_SLOTS < n_chunks:
            chunk_copy(which, c + FFN_STAGE_SLOTS).start()

    def body(stream_weights):
        if stream_weights:
            for c in range(FFN_STAGE_SLOTS):
                for which in range(3):
                    chunk_copy(which, c).start()
        h_ref[...] = _rmsnorm_rows(x_ref[...], g_ref[...]).astype(jnp.bfloat16)
        for c in range(n_chunks):
            cols = slice(c * FFN_CHUNK_F, (c + 1) * FFN_CHUNK_F)
            if stream_weights:
                for which in range(3):
                    land(which, c)
            h = h_ref[...]
            gate = jnp.dot(h, wg_ref[:, cols], preferred_element_type=jnp.float32)
            up = jnp.dot(h, wu_ref[:, cols], preferred_element_type=jnp.float32)
            act = gate * (1.0 / (1.0 + jnp.exp(-gate))) * up
            a_ref[:, cols] = act.astype(jnp.bfloat16)
        y = jnp.dot(a_ref[...], wd_ref[...], preferred_element_type=jnp.float32)
        o_ref[...] = x_ref[...] + 0.5 * y

    first = pl.program_id(0) == 0
    pl.when(first)(functools.partial(body, True))
    pl.when(jnp.logical_not(first))(functools.partial(body, False))


def _ffn(x2d, gain, w_gate, w_up, w_down):
    m = x2d.shape[0]
    tm = FFN_TILE_M
    resident = functools.partial(pl.BlockSpec, pipeline_mode=pl.Buffered(1))
    est = (2 * 3 * D_MODEL * D_FF
           + 3 * FFN_STAGE_SLOTS * D_MODEL * FFN_CHUNK_F * 4
           + 2 * 2 * tm * D_MODEL * 4
           + tm * D_MODEL * 2 + tm * D_FF * 2
           + 6 * tm * FFN_CHUNK_F * 4 + tm * D_MODEL * 4)
    return pl.pallas_call(
        _ffn_kernel,
        out_shape=jax.ShapeDtypeStruct((m, D_MODEL), jnp.float32),
        grid=(m // tm,),
        in_specs=[
            pl.BlockSpec((tm, D_MODEL), lambda i: (i, 0)),
            resident((1, D_MODEL), lambda i: (0, 0)),
            pl.BlockSpec(memory_space=pltpu.HBM),
            pl.BlockSpec(memory_space=pltpu.HBM),
            pl.BlockSpec(memory_space=pltpu.HBM),
        ],
        out_specs=pl.BlockSpec((tm, D_MODEL), lambda i: (i, 0)),
        scratch_shapes=[
            pltpu.VMEM((D_MODEL, D_FF), jnp.bfloat16),
            pltpu.VMEM((D_MODEL, D_FF), jnp.bfloat16),
            pltpu.VMEM((D_FF, D_MODEL), jnp.bfloat16),
            pltpu.VMEM((2 * FFN_STAGE_SLOTS, D_MODEL, FFN_CHUNK_F), jnp.float32),
            pltpu.VMEM((FFN_STAGE_SLOTS, FFN_CHUNK_F, D_MODEL), jnp.float32),
            pltpu.SemaphoreType.DMA((3 * FFN_STAGE_SLOTS,)),
            pltpu.VMEM((tm, D_MODEL), jnp.bfloat16),
            pltpu.VMEM((tm, D_FF), jnp.bfloat16),
        ],
        compiler_params=pltpu.CompilerParams(
            dimension_semantics=("arbitrary",),
            vmem_limit_bytes=_vmem_limit(est + (8 << 20))),
        name="ffn",
    )(x2d, gain.reshape(1, D_MODEL), w_gate, w_up, w_down)


def _proj_kernel(x_ref, g_ref, w_ref, qg_ref, kg_ref, dm_ref,
                 q_ref, k_ref, v_ref, u_ref):
    tm = x_ref.shape[0]
    h = _rmsnorm_rows(x_ref[...], g_ref[...]).astype(jnp.bfloat16)
    z = jnp.dot(h, w_ref[...], preferred_element_type=jnp.float32)
    n_slab = (D_ATTN + 2 * D_KV) // V7X_MXU_DIM
    sq = jnp.concatenate(
        [z[:, i * V7X_MXU_DIM:(i + 1) * V7X_MXU_DIM] for i in range(n_slab)], axis=0)
    ms = jnp.dot((sq * sq).astype(jnp.bfloat16), dm_ref[...], preferred_element_type=jnp.float32)
    q_ms = jnp.concatenate([ms[i * tm:(i + 1) * tm] for i in range(n_slab - 1)], axis=1)
    k_ms = ms[(n_slab - 1) * tm:, :D_KV]
    q = z[:, :D_ATTN]
    k = z[:, D_ATTN:D_ATTN + D_KV]
    q_ref[...] = (q * lax.rsqrt(q_ms + EPS) * qg_ref[...]).astype(jnp.bfloat16)
    k_ref[...] = (k * lax.rsqrt(k_ms + EPS) * kg_ref[...]).astype(jnp.bfloat16)
    v_ref[...] = z[:, D_ATTN + D_KV:D_ATTN + 2 * D_KV].astype(jnp.bfloat16)
    u_ref[...] = z[:, D_ATTN + 2 * D_KV:]


def _head_mean_matrix(width):
    idx = np.arange(width) // HEAD_DIM
    return jnp.asarray((idx[:, None] == idx[None, :]).astype(np.float32) / HEAD_DIM, jnp.bfloat16)


def _proj(x2d, gain, w_in, q_gain, k_gain):
    m = x2d.shape[0]
    tm = PROJ_TILE_M
    resident = functools.partial(pl.BlockSpec, pipeline_mode=pl.Buffered(1))
    qg = (jnp.tile(q_gain, N_HEADS) * (HEAD_DIM ** -0.5)).reshape(1, D_ATTN)
    kg = jnp.tile(k_gain, N_KV_HEADS).reshape(1, D_KV)
    est = (2 * D_MODEL * D_IN + 2 * 2 * tm * D_MODEL * 4 + 4 * tm * D_IN * 4
           + 2 * 2 * tm * D_IN * 4)
    return pl.pallas_call(
        _proj_kernel,
        out_shape=(
            jax.ShapeDtypeStruct((m, D_ATTN), jnp.bfloat16),
            jax.ShapeDtypeStruct((m, D_KV), jnp.bfloat16),
            jax.ShapeDtypeStruct((m, D_KV), jnp.bfloat16),
            jax.ShapeDtypeStruct((m, D_POOL), jnp.float32),
        ),
        grid=(m // tm,),
        in_specs=[
            pl.BlockSpec((tm, D_MODEL), lambda i: (i, 0)),
            resident((1, D_MODEL), lambda i: (0, 0)),
            resident((D_MODEL, D_IN), lambda i: (0, 0)),
            resident((1, D_ATTN), lambda i: (0, 0)),
            resident((1, D_KV), lambda i: (0, 0)),
            resident((V7X_MXU_DIM, V7X_MXU_DIM), lambda i: (0, 0)),
        ],
        out_specs=(
            pl.BlockSpec((tm, D_ATTN), lambda i: (i, 0)),
            pl.BlockSpec((tm, D_KV), lambda i: (i, 0)),
            pl.BlockSpec((tm, D_KV), lambda i: (i, 0)),
            pl.BlockSpec((tm, D_POOL), lambda i: (i, 0)),
        ),
        compiler_params=pltpu.CompilerParams(
            dimension_semantics=("arbitrary",),
            vmem_limit_bytes=_vmem_limit(est + (8 << 20))),
        name="proj",
    )(x2d, gain.reshape(1, D_MODEL), w_in.astype(jnp.bfloat16), qg, kg,
      _head_mean_matrix(V7X_MXU_DIM))


def _t5_bucket(dist):
    n = np.maximum(dist, 0)
    max_exact = N_BUCKETS // 2
    large = max_exact + (np.log(np.maximum(n, 1) / max_exact)
                         / np.log(MAX_DISTANCE / max_exact)
                         * (N_BUCKETS - max_exact)).astype(np.int32)
    large = np.minimum(large, N_BUCKETS - 1)
    return np.where(n < max_exact, n, large).astype(np.int32)


def _band_bucket_table():
    ql = np.arange(BLOCK)[:, None]
    kl = np.arange(2 * BLOCK)[None, :]
    dist = ql + BLOCK - kl
    in_band = (dist >= 0) & (dist < WINDOW)
    return np.where(in_band, _t5_bucket(dist), -1).astype(np.int32)


def _bias_kernel(rel_ref, bucket_ref, o_ref):
    bucket = bucket_ref[...]
    for slot, head in enumerate(HEADS_PLAIN + HEADS_ROLLED):
        acc = jnp.full((BLOCK, 2 * BLOCK), NEG, jnp.float32)
        for b in range(N_BUCKETS):
            acc = jnp.where(bucket == b, rel_ref[b, head], acc)
        o_ref[slot] = acc


def _band_bias(rel_bias):
    return pl.pallas_call(
        _bias_kernel,
        out_shape=jax.ShapeDtypeStruct((N_HEADS, BLOCK, 2 * BLOCK), jnp.float32),
        in_specs=[
            pl.BlockSpec(memory_space=pltpu.SMEM),
            pl.BlockSpec(memory_space=pltpu.VMEM),
        ],
        out_specs=pl.BlockSpec(memory_space=pltpu.VMEM),
        name="band_bias",
    )(rel_bias, jnp.asarray(_band_bucket_table()))


def _window_sum_matrices():
    ql = np.arange(BLOCK)[:, None]
    kl = np.arange(2 * BLOCK)[None, :]
    dist = ql + BLOCK - kl
    mats = [((dist >= 0) & (dist < w)).astype(np.float32) for w in POOL_WINDOWS]
    return jnp.asarray(np.stack(mats), jnp.bfloat16)


def _block_diagonal(w):
    g, c, _ = w.shape
    on_diagonal = jnp.eye(g, dtype=bool)[:, None, :, None]
    return jnp.where(on_diagonal, w[:, :, None, :], jnp.zeros((), w.dtype)).reshape(g * c, g * c)


def _mix_kernel(sink_ref, x_ref, q_ref, k_ref, kh_ref, v_ref, vh_ref, u_ref, uh_ref,
                bias_ref, wsum_ref, pw_ref, ps_ref, wo_ref, o_ref,
                kx_ref, kxr_ref, vx_ref, vxr_ref, ux_ref, pooled_ref, y_ref):
    tq = q_ref.shape[1]
    n_blocks = tq // BLOCK
    first = pl.program_id(1) == 0
    half = HEAD_DIM

    k_halo = jnp.where(first, jnp.zeros_like(kh_ref[0]), kh_ref[0])
    v_halo = jnp.where(first, jnp.zeros_like(vh_ref[0]), vh_ref[0])
    u_halo = jnp.where(first, jnp.zeros_like(uh_ref[0]), uh_ref[0])
    kx_ref[:BLOCK] = k_halo
    kx_ref[BLOCK:] = k_ref[0]
    kxr_ref[...] = pltpu.roll(kx_ref[...].astype(jnp.float32), half, 1).astype(jnp.bfloat16)
    ones = jnp.ones((tq + BLOCK, D_KV), jnp.bfloat16)
    vx_ref[:BLOCK, :D_KV] = v_halo
    vx_ref[BLOCK:, :D_KV] = v_ref[0]
    vx_ref[:, D_KV:] = ones
    vxr_ref[:, :D_KV] = pltpu.roll(vx_ref[:, :D_KV].astype(jnp.float32), half, 1).astype(jnp.bfloat16)
    vxr_ref[:, D_KV:] = ones
    ux_ref[:BLOCK] = u_halo.astype(jnp.bfloat16)
    ux_ref[BLOCK:] = u_ref[0].astype(jnp.bfloat16)

    lane = lax.broadcasted_iota(jnp.int32, (BLOCK, 2 * half), 1)
    low = lane < half
    col = lax.broadcasted_iota(jnp.int32, (1, 2 * BLOCK), 1)
    halo_penalty = jnp.where(first & (col < BLOCK), NEG, 0.0).astype(jnp.float32)
    row = lax.broadcasted_iota(jnp.int32, (BLOCK, 1), 0)
    groups = ((HEADS_PLAIN, kx_ref, vx_ref, 0), (HEADS_ROLLED, kxr_ref, vxr_ref, len(HEADS_PLAIN)))

    def rows_of(j):
        return slice(j * BLOCK, (j + 1) * BLOCK)

    def keys_of(j):
        return slice(j * BLOCK, (j + 2) * BLOCK)

    def scores(j, grp):
        heads, k_src, _, _ = groups[grp]
        q = q_ref[0, rows_of(j), :]
        zero = jnp.zeros((BLOCK, 2 * half), q.dtype)
        pieces = []
        for hd in heads:
            pair = q[:, (hd // 2) * 2 * half:(hd // 2 + 1) * 2 * half]
            pieces.append(jnp.where(low if hd % 2 == 0 else ~low, pair, zero))
        qs = jnp.concatenate(pieces, axis=0)
        return lax.dot_general(qs, k_src[keys_of(j), :], (((1,), (1,)), ((), ())),
                               preferred_element_type=jnp.float32)

    def softmax_numerators(j, grp, logits):
        heads, _, _, slot0 = groups[grp]
        es, sink_terms = [], []
        for s, hd in enumerate(heads):
            lg = logits[s * BLOCK:(s + 1) * BLOCK] + bias_ref[slot0 + s]
            if j == 0:
                lg = lg + halo_penalty
            sink = sink_ref[hd]
            m = jnp.maximum(jnp.max(lg, axis=-1, keepdims=True), sink)
            es.append(jnp.exp(lg - m).astype(jnp.bfloat16))
            sink_terms.append(jnp.exp(sink - m))
        return jnp.concatenate(es, axis=0), sink_terms

    def attend(j, grp, e, sink_terms, outs):
        heads, _, v_src, _ = groups[grp]
        pv = jnp.dot(e, v_src[keys_of(j), :], preferred_element_type=jnp.float32)
        for s, hd in enumerate(heads):
            blk = pv[s * BLOCK:(s + 1) * BLOCK]
            outs[hd] = blk[:, :D_KV] / (blk[:, D_KV:] + sink_terms[s])

    def store_attention(j, outs):
        for p in range(N_HEADS // 2):
            y_ref[rows_of(j), p * 2 * half:(p + 1) * 2 * half] = jnp.where(
                low, outs[2 * p], outs[2 * p + 1]).astype(jnp.bfloat16)

    def pool_means(j):
        t_glob = pl.program_id(1) * tq + j * BLOCK + row
        for g, w in enumerate(POOL_WINDOWS):
            gc = slice(g * POOL_GROUP_DIM, (g + 1) * POOL_GROUP_DIM)
            wsum = jnp.dot(wsum_ref[g], ux_ref[keys_of(j), gc], preferred_element_type=jnp.float32)
            cnt = jnp.minimum(t_glob + 1, w).astype(jnp.float32)
            pooled = wsum * (1.0 / cnt) - u_ref[0, rows_of(j), gc]
            pooled_ref[rows_of(j), gc] = pooled.astype(jnp.bfloat16)

    def project_attention(j):
        o_ref[0, rows_of(j), :] += jnp.dot(
            y_ref[rows_of(j), :], wo_ref[:D_ATTN, :], preferred_element_type=jnp.float32)

    for j in range(n_blocks):
        pool_means(j)
    logits = scores(0, 0)
    for j in range(n_blocks):
        outs = {}
        logits_b = scores(j, 1)
        e, sink_terms = softmax_numerators(j, 0, logits)
        if j == 0:
            y_pool = jnp.dot(pooled_ref[...], pw_ref[...],
                             preferred_element_type=jnp.float32) * ps_ref[...]
        else:
            project_attention(j - 1)
        attend(j, 0, e, sink_terms, outs)
        e, sink_terms = softmax_numerators(j, 1, logits_b)
        if j == 0:
            o_ref[0] = x_ref[0] + jnp.dot(y_pool.astype(jnp.bfloat16), wo_ref[D_ATTN:, :],
                                          preferred_element_type=jnp.float32)
        if j + 1 < n_blocks:
            logits = scores(j + 1, 0)
        attend(j, 1, e, sink_terms, outs)
        store_attention(j, outs)
    project_attention(n_blocks - 1)


def _mix(x3d, q, k, v, u, sinks, bias, pool_w, pool_scale, w_out):
    b, s, _ = x3d.shape
    tq = MIX_TILE_Q
    nblk = tq // BLOCK
    resident = functools.partial(pl.BlockSpec, pipeline_mode=pl.Buffered(1))
    tile = lambda width: pl.BlockSpec((1, tq, width), lambda bi, i: (bi, i, 0))
    halo = lambda width: pl.BlockSpec(
        (1, BLOCK, width), lambda bi, i: (bi, jnp.maximum(i * nblk - 1, 0), 0))
    est = (2 * 2 * tq * D_MODEL * 4 + 2 * tq * (D_ATTN + 2 * D_KV) * 2 + 2 * tq * D_POOL * 4
           + 2 * BLOCK * (2 * D_KV * 2 + D_POOL * 4)
           + N_HEADS * BLOCK * 2 * BLOCK * 4 + 2 * D_MODEL * D_MODEL
           + 6 * (tq + BLOCK) * D_KV * 2 + (tq + BLOCK) * D_POOL * 2 + tq * D_MODEL * 2
           + 16 * BLOCK * 2 * BLOCK * 4 * 4)
    return pl.pallas_call(
        _mix_kernel,
        out_shape=jax.ShapeDtypeStruct((b, s, D_MODEL), jnp.float32),
        grid=(b, s // tq),
        in_specs=[
            pl.BlockSpec(memory_space=pltpu.SMEM),
            tile(D_MODEL),
            tile(D_ATTN),
            tile(D_KV), halo(D_KV),
            tile(D_KV), halo(D_KV),
            tile(D_POOL), halo(D_POOL),
            resident((N_HEADS, BLOCK, 2 * BLOCK), lambda bi, i: (0, 0, 0)),
            resident((len(POOL_WINDOWS), BLOCK, 2 * BLOCK), lambda bi, i: (0, 0, 0)),
            resident((D_POOL, D_POOL), lambda bi, i: (0, 0)),
            resident((1, D_POOL), lambda bi, i: (0, 0)),
            resident((D_MODEL, D_MODEL), lambda bi, i: (0, 0)),
        ],
        out_specs=tile(D_MODEL),
        scratch_shapes=[
            pltpu.VMEM((tq + BLOCK, D_KV), jnp.bfloat16),
            pltpu.VMEM((tq + BLOCK, D_KV), jnp.bfloat16),
            pltpu.VMEM((tq + BLOCK, 2 * D_KV), jnp.bfloat16),
            pltpu.VMEM((tq + BLOCK, 2 * D_KV), jnp.bfloat16),
            pltpu.VMEM((tq + BLOCK, D_POOL), jnp.bfloat16),
            pltpu.VMEM((tq, D_POOL), jnp.bfloat16),
            pltpu.VMEM((tq, D_ATTN), jnp.bfloat16),
        ],
        compiler_params=pltpu.CompilerParams(
            dimension_semantics=("arbitrary", "arbitrary"),
            vmem_limit_bytes=_vmem_limit(est + (8 << 20))),
        name="mix",
    )(sinks, x3d, q, k, k, v, v, u, u, bias, _window_sum_matrices(),
      _block_diagonal(pool_w.astype(jnp.bfloat16)), pool_scale.reshape(1, D_POOL),
      w_out.astype(jnp.bfloat16))


def kernel(x, ffn1_norm, ffn1_w_gate, ffn1_w_up, ffn1_w_down, mix_norm, w_in, q_norm, k_norm,
           attn_sinks, rel_bias, pool_w, pool_scale, w_out, ffn2_norm, ffn2_w_gate, ffn2_w_up,
           ffn2_w_down):
    b, s, d = x.shape
    assert (d, s % MIX_TILE_Q, (b * s) % FFN_TILE_M) == (D_MODEL, 0, 0)
    bias = _band_bias(rel_bias)
    for l in range(ffn1_norm.shape[0]):
        x2d = x.reshape(b * s, d)
        x2d = _ffn(x2d, ffn1_norm[l], ffn1_w_gate[l], ffn1_w_up[l], ffn1_w_down[l])
        q, k, v, u = _proj(x2d, mix_norm[l], w_in[l], q_norm[l], k_norm[l])
        x3d = _mix(x2d.reshape(b, s, d), q.reshape(b, s, D_ATTN), k.reshape(b, s, D_KV),
                   v.reshape(b, s, D_KV), u.reshape(b, s, D_POOL), attn_sinks[l], bias,
                   pool_w[l], pool_scale[l], w_out[l])
        x2d = _ffn(x3d.reshape(b * s, d), ffn2_norm[l], ffn2_w_gate[l], ffn2_w_up[l], ffn2_w_down[l])
        x = x2d.reshape(b, s, d)
    return x
```

```python
import functools

import numpy as np
import jax
import jax.numpy as jnp
from jax import lax
from jax.experimental import pallas as pl
from jax.experimental.pallas import tpu as pltpu

D_MODEL = 1024
HEAD_DIM = 64
N_HEADS = 8
N_KV_HEADS = 2
D_ATTN = N_HEADS * HEAD_DIM
D_KV = N_KV_HEADS * HEAD_DIM
D_POOL = D_MODEL - D_ATTN
POOL_WINDOWS = (2, 4, 8, 16)
POOL_GROUP_DIM = D_POOL // len(POOL_WINDOWS)
D_IN = D_ATTN + 2 * D_KV + D_POOL
WINDOW = 128
BLOCK = 128
N_BUCKETS = 32
MAX_DISTANCE = 128
D_FF = 2816
EPS = 1e-6
NEG = -1e30

V7X_LANES = 128
V7X_MXU_DIM = 256
V7X_VMEM_BYTES = 64 * 1024 * 1024

HEADS_PLAIN = (0, 2, 5, 7)
HEADS_ROLLED = (1, 3, 4, 6)

FFN_TILE_M = 512
FFN_CHUNK_F = 256
FFN_STAGE_SLOTS = 4
PROJ_TILE_M = 512
MIX_TILE_Q = 512


def _vmem_limit(estimate_bytes):
    return int(min(V7X_VMEM_BYTES - (4 << 20), max(estimate_bytes, 16 << 20)))


def _rmsnorm_rows(x32, gain_row):
    ms = jnp.mean(x32 * x32, axis=-1, keepdims=True)
    return x32 * lax.rsqrt(ms + EPS) * gain_row


def _ffn_kernel(x_ref, g_ref, wg_hbm, wu_hbm, wd_hbm, o_ref,
                wg_ref, wu_ref, wd_ref, stage_in_ref, stage_out_ref, sem_ref, h_ref, a_ref):
    n_chunks = D_FF // FFN_CHUNK_F

    def chunk_copy(which, c):
        slot = c % FFN_STAGE_SLOTS
        window = pl.ds(c * FFN_CHUNK_F, FFN_CHUNK_F)
        sem = sem_ref.at[which * FFN_STAGE_SLOTS + slot]
        if which == 2:
            return pltpu.make_async_copy(wd_hbm.at[window, :], stage_out_ref.at[slot], sem)
        src = (wg_hbm, wu_hbm)[which]
        return pltpu.make_async_copy(src.at[:, window],
                                     stage_in_ref.at[which * FFN_STAGE_SLOTS + slot], sem)

    def land(which, c):
        slot = c % FFN_STAGE_SLOTS
        window = slice(c * FFN_CHUNK_F, (c + 1) * FFN_CHUNK_F)
        chunk_copy(which, c).wait()
        if which == 2:
            wd_ref[window, :] = stage_out_ref[slot].astype(jnp.bfloat16)
        else:
            dst = (wg_ref, wu_ref)[which]
            dst[:, window] = stage_in_ref[which * FFN_STAGE_SLOTS + slot].astype(jnp.bfloat16)
        if c + FFN_STAGE_SLOTS < n_chunks:
            chunk_copy(which, c + FFN_STAGE_SLOTS).start()

    def body(stream_weights):
        if stream_weights:
            for c in range(FFN_STAGE_SLOTS):
                for which in range(3):
                    chunk_copy(which, c).start()
        h_ref[...] = _rmsnorm_rows(x_ref[...], g_ref[...]).astype(jnp.bfloat16)
        for c in range(n_chunks):
            cols = slice(c * FFN_CHUNK_F, (c + 1) * FFN_CHUNK_F)
            if stream_weights:
                for which in range(3):
                    land(which, c)
            h = h_ref[...]
            gate = jnp.dot(h, wg_ref[:, cols], preferred_element_type=jnp.float32)
            up = jnp.dot(h, wu_ref[:, cols], preferred_element_type=jnp.float32)
            act = gate * (1.0 / (1.0 + jnp.exp(-gate))) * up
            a_ref[:, cols] = act.astype(jnp.bfloat16)
        y = jnp.dot(a_ref[...], wd_ref[...], preferred_element_type=jnp.float32)
        o_ref[...] = x_ref[...] + 0.5 * y

    first = pl.program_id(0) == 0
    pl.when(first)(functools.partial(body, True))
    pl.when(jnp.logical_not(first))(functools.partial(body, False))


def _ffn(x2d, gain, w_gate, w_up, w_down):
    m = x2d.shape[0]
    tm = FFN_TILE_M
    resident = functools.partial(pl.BlockSpec, pipeline_mode=pl.Buffered(1))
    est = (2 * 3 * D_MODEL * D_FF
           + 3 * FFN_STAGE_SLOTS * D_MODEL * FFN_CHUNK_F * 4
           + 2 * 2 * tm * D_MODEL * 4
           + tm * D_MODEL * 2 + tm * D_FF * 2
           + 6 * tm * FFN_CHUNK_F * 4 + tm * D_MODEL * 4)
    return pl.pallas_call(
        _ffn_kernel,
        out_shape=jax.ShapeDtypeStruct((m, D_MODEL), jnp.float32),
        grid=(m // tm,),
        in_specs=[
            pl.BlockSpec((tm, D_MODEL), lambda i: (i, 0)),
            resident((1, D_MODEL), lambda i: (0, 0)),
            pl.BlockSpec(memory_space=pltpu.HBM),
            pl.BlockSpec(memory_space=pltpu.HBM),
            pl.BlockSpec(memory_space=pltpu.HBM),
        ],
        out_specs=pl.BlockSpec((tm, D_MODEL), lambda i: (i, 0)),
        scratch_shapes=[
            pltpu.VMEM((D_MODEL, D_FF), jnp.bfloat16),
            pltpu.VMEM((D_MODEL, D_FF), jnp.bfloat16),
            pltpu.VMEM((D_FF, D_MODEL), jnp.bfloat16),
            pltpu.VMEM((2 * FFN_STAGE_SLOTS, D_MODEL, FFN_CHUNK_F), jnp.float32),
            pltpu.VMEM((FFN_STAGE_SLOTS, FFN_CHUNK_F, D_MODEL), jnp.float32),
            pltpu.SemaphoreType.DMA((3 * FFN_STAGE_SLOTS,)),
            pltpu.VMEM((tm, D_MODEL), jnp.bfloat16),
            pltpu.VMEM((tm, D_FF), jnp.bfloat16),
        ],
        compiler_params=pltpu.CompilerParams(
            dimension_semantics=("arbitrary",),
            vmem_limit_bytes=_vmem_limit(est + (8 << 20))),
        name="ffn",
    )(x2d, gain.reshape(1, D_MODEL), w_gate, w_up, w_down)


def _proj_kernel(x_ref, g_ref, w_ref, qg_ref, kg_ref, dm_ref,
                 q_ref, k_ref, v_ref, u_ref):
    tm = x_ref.shape[0]
    h = _rmsnorm_rows(x_ref[...], g_ref[...]).astype(jnp.bfloat16)
    z = jnp.dot(h, w_ref[...], preferred_element_type=jnp.float32)
    n_slab = (D_ATTN + 2 * D_KV) // V7X_MXU_DIM
    sq = jnp.concatenate(
        [z[:, i * V7X_MXU_DIM:(i + 1) * V7X_MXU_DIM] for i in range(n_slab)], axis=0)
    ms = jnp.dot((sq * sq).astype(jnp.bfloat16), dm_ref[...], preferred_element_type=jnp.float32)
    q_ms = jnp.concatenate([ms[i * tm:(i + 1) * tm] for i in range(n_slab - 1)], axis=1)
    k_ms = ms[(n_slab - 1) * tm:, :D_KV]
    q = z[:, :D_ATTN]
    k = z[:, D_ATTN:D_ATTN + D_KV]
    q_ref[...] = (q * lax.rsqrt(q_ms + EPS) * qg_ref[...]).astype(jnp.bfloat16)
    k_ref[...] = (k * lax.rsqrt(k_ms + EPS) * kg_ref[...]).astype(jnp.bfloat16)
    v_ref[...] = z[:, D_ATTN + D_KV:D_ATTN + 2 * D_KV].astype(jnp.bfloat16)
    u_ref[...] = z[:, D_ATTN + 2 * D_KV:]


def _head_mean_matrix(width):
    idx = np.arange(width) // HEAD_DIM
    return jnp.asarray((idx[:, None] == idx[None, :]).astype(np.float32) / HEAD_DIM, jnp.bfloat16)


def _proj(x2d, gain, w_in, q_gain, k_gain):
    m = x2d.shape[0]
    tm = PROJ_TILE_M
    resident = functools.partial(pl.BlockSpec, pipeline_mode=pl.Buffered(1))
    qg = (jnp.tile(q_gain, N_HEADS) * (HEAD_DIM ** -0.5)).reshape(1, D_ATTN)
    kg = jnp.tile(k_gain, N_KV_HEADS).reshape(1, D_KV)
    est = (2 * D_MODEL * D_IN + 2 * 2 * tm * D_MODEL * 4 + 4 * tm * D_IN * 4
           + 2 * 2 * tm * D_IN * 4)
    return pl.pallas_call(
        _proj_kernel,
        out_shape=(
            jax.ShapeDtypeStruct((m, D_ATTN), jnp.bfloat16),
            jax.ShapeDtypeStruct((m, D_KV), jnp.bfloat16),
            jax.ShapeDtypeStruct((m, D_KV), jnp.bfloat16),
            jax.ShapeDtypeStruct((m, D_POOL), jnp.float32),
        ),
        grid=(m // tm,),
        in_specs=[
            pl.BlockSpec((tm, D_MODEL), lambda i: (i, 0)),
            resident((1, D_MODEL), lambda i: (0, 0)),
            resident((D_MODEL, D_IN), lambda i: (0, 0)),
            resident((1, D_ATTN), lambda i: (0, 0)),
            resident((1, D_KV), lambda i: (0, 0)),
            resident((V7X_MXU_DIM, V7X_MXU_DIM), lambda i: (0, 0)),
        ],
        out_specs=(
            pl.BlockSpec((tm, D_ATTN), lambda i: (i, 0)),
            pl.BlockSpec((tm, D_KV), lambda i: (i, 0)),
            pl.BlockSpec((tm, D_KV), lambda i: (i, 0)),
            pl.BlockSpec((tm, D_POOL), lambda i: (i, 0)),
        ),
        compiler_params=pltpu.CompilerParams(
            dimension_semantics=("arbitrary",),
            vmem_limit_bytes=_vmem_limit(est + (8 << 20))),
        name="proj",
    )(x2d, gain.reshape(1, D_MODEL), w_in.astype(jnp.bfloat16), qg, kg,
      _head_mean_matrix(V7X_MXU_DIM))


def _t5_bucket(dist):
    n = np.maximum(dist, 0)
    max_exact = N_BUCKETS // 2
    large = max_exact + (np.log(np.maximum(n, 1) / max_exact)
                         / np.log(MAX_DISTANCE / max_exact)
                         * (N_BUCKETS - max_exact)).astype(np.int32)
    large = np.minimum(large, N_BUCKETS - 1)
    return np.where(n < max_exact, n, large).astype(np.int32)


def _band_bucket_table():
    ql = np.arange(BLOCK)[:, None]
    kl = np.arange(2 * BLOCK)[None, :]
    dist = ql + BLOCK - kl
    in_band = (dist >= 0) & (dist < WINDOW)
    return np.where(in_band, _t5_bucket(dist), -1).astype(np.int32)


def _bias_kernel(rel_ref, bucket_ref, o_ref):
    bucket = bucket_ref[...]
    for slot, head in enumerate(HEADS_PLAIN + HEADS_ROLLED):
        acc = jnp.full((BLOCK, 2 * BLOCK), NEG, jnp.float32)
        for b in range(N_BUCKETS):
            acc = jnp.where(bucket == b, rel_ref[b, head], acc)
        o_ref[slot] = acc


def _band_bias(rel_bias):
    return pl.pallas_call(
        _bias_kernel,
        out_shape=jax.ShapeDtypeStruct((N_HEADS, BLOCK, 2 * BLOCK), jnp.float32),
        in_specs=[
            pl.BlockSpec(memory_space=pltpu.SMEM),
            pl.BlockSpec(memory_space=pltpu.VMEM),
        ],
        out_specs=pl.BlockSpec(memory_space=pltpu.VMEM),
        name="band_bias",
    )(rel_bias, jnp.asarray(_band_bucket_table()))


def _window_sum_matrices():
    ql = np.arange(BLOCK)[:, None]
    kl = np.arange(2 * BLOCK)[None, :]
    dist = ql + BLOCK - kl
    mats = [((dist >= 0) & (dist < w)).astype(np.float32) for w in POOL_WINDOWS]
    return jnp.asarray(np.stack(mats), jnp.bfloat16)


def _block_diagonal(w):
    g, c, _ = w.shape
    on_diagonal = jnp.eye(g, dtype=bool)[:, None, :, None]
    return jnp.where(on_diagonal, w[:, :, None, :], jnp.zeros((), w.dtype)).reshape(g * c, g * c)


def _mix_kernel(sink_ref, x_ref, q_ref, k_ref, kh_ref, v_ref, vh_ref, u_ref, uh_ref,
                bias_ref, wsum_ref, pw_ref, ps_ref, wo_ref, o_ref,
                kx_ref, kxr_ref, vx_ref, vxr_ref, ux_ref, pooled_ref, y_ref):
    tq = q_ref.shape[1]
    n_blocks = tq // BLOCK
    first = pl.program_id(1) == 0
    half = HEAD_DIM

    k_halo = jnp.where(first, jnp.zeros_like(kh_ref[0]), kh_ref[0])
    v_halo = jnp.where(first, jnp.zeros_like(vh_ref[0]), vh_ref[0])
    u_halo = jnp.where(first, jnp.zeros_like(uh_ref[0]), uh_ref[0])
    kx_ref[:BLOCK] = k_halo
    kx_ref[BLOCK:] = k_ref[0]
    kxr_ref[...] = pltpu.roll(kx_ref[...].astype(jnp.float32), half, 1).astype(jnp.bfloat16)
    ones = jnp.ones((tq + BLOCK, D_KV), jnp.bfloat16)
    vx_ref[:BLOCK, :D_KV] = v_halo
    vx_ref[BLOCK:, :D_KV] = v_ref[0]
    vx_ref[:, D_KV:] = ones
    vxr_ref[:, :D_KV] = pltpu.roll(vx_ref[:, :D_KV].astype(jnp.float32), half, 1).astype(jnp.bfloat16)
    vxr_ref[:, D_KV:] = ones
    ux_ref[:BLOCK] = u_halo.astype(jnp.bfloat16)
    ux_ref[BLOCK:] = u_ref[0].astype(jnp.bfloat16)

    lane = lax.broadcasted_iota(jnp.int32, (BLOCK, 2 * half), 1)
    low = lane < half
    col = lax.broadcasted_iota(jnp.int32, (1, 2 * BLOCK), 1)
    halo_penalty = jnp.where(first & (col < BLOCK), NEG, 0.0).astype(jnp.float32)
    row = lax.broadcasted_iota(jnp.int32, (BLOCK, 1), 0)
    groups = ((HEADS_PLAIN, kx_ref, vx_ref, 0), (HEADS_ROLLED, kxr_ref, vxr_ref, len(HEADS_PLAIN)))

    def rows_of(j):
        return slice(j * BLOCK, (j + 1) * BLOCK)

    def keys_of(j):
        return slice(j * BLOCK, (j + 2) * BLOCK)

    def scores(j, grp):
        heads, k_src, _, _ = groups[grp]
        q = q_ref[0, rows_of(j), :]
        zero = jnp.zeros((BLOCK, 2 * half), q.dtype)
        pieces = []
        for hd in heads:
            pair = q[:, (hd // 2) * 2 * half:(hd // 2 + 1) * 2 * half]
            pieces.append(jnp.where(low if hd % 2 == 0 else ~low, pair, zero))
        qs = jnp.concatenate(pieces, axis=0)
        return lax.dot_general(qs, k_src[keys_of(j), :], (((1,), (1,)), ((), ())),
                               preferred_element_type=jnp.float32)

    def softmax_numerators(j, grp, logits):
        heads, _, _, slot0 = groups[grp]
        es, sink_terms = [], []
        for s, hd in enumerate(heads):
            lg = logits[s * BLOCK:(s + 1) * BLOCK] + bias_ref[slot0 + s]
            if j == 0:
                lg = lg + halo_penalty
            sink = sink_ref[hd]
            m = jnp.maximum(jnp.max(lg, axis=-1, keepdims=True), sink)
            es.append(jnp.exp(lg - m).astype(jnp.bfloat16))
            sink_terms.append(jnp.exp(sink - m))
        return jnp.concatenate(es, axis=0), sink_terms

    def attend(j, grp, e, sink_terms, outs):
        heads, _, v_src, _ = groups[grp]
        pv = jnp.dot(e, v_src[keys_of(j), :], preferred_element_type=jnp.float32)
        for s, hd in enumerate(heads):
            blk = pv[s * BLOCK:(s + 1) * BLOCK]
            outs[hd] = blk[:, :D_KV] / (blk[:, D_KV:] + sink_terms[s])

    def store_attention(j, outs):
        for p in range(N_HEADS // 2):
            y_ref[rows_of(j), p * 2 * half:(p + 1) * 2 * half] = jnp.where(
                low, outs[2 * p], outs[2 * p + 1]).astype(jnp.bfloat16)

    def pool_means(j):
        t_glob = pl.program_id(1) * tq + j * BLOCK + row
        for g, w in enumerate(POOL_WINDOWS):
            gc = slice(g * POOL_GROUP_DIM, (g + 1) * POOL_GROUP_DIM)
            wsum = jnp.dot(wsum_ref[g], ux_ref[keys_of(j), gc], preferred_element_type=jnp.float32)
            cnt = jnp.minimum(t_glob + 1, w).astype(jnp.float32)
            pooled = wsum * (1.0 / cnt) - u_ref[0, rows_of(j), gc]
            pooled_ref[rows_of(j), gc] = pooled.astype(jnp.bfloat16)

    def project_attention(j):
        o_ref[0, rows_of(j), :] += jnp.dot(
            y_ref[rows_of(j), :], wo_ref[:D_ATTN, :], preferred_element_type=jnp.float32)

    for j in range(n_blocks):
        pool_means(j)
    logits = scores(0, 0)
    for j in range(n_blocks):
        outs = {}
        logits_b = scores(j, 1)
        e, sink_terms = softmax_numerators(j, 0, logits)
        if j == 0:
            y_pool = jnp.dot(pooled_ref[...], pw_ref[...],
                             preferred_element_type=jnp.float32) * ps_ref[...]
        else:
            project_attention(j - 1)
        attend(j, 0, e, sink_terms, outs)
        e, sink_terms = softmax_numerators(j, 1, logits_b)
        if j == 0:
            o_ref[0] = x_ref[0] + jnp.dot(y_pool.astype(jnp.bfloat16), wo_ref[D_ATTN:, :],
                                          preferred_element_type=jnp.float32)
        if j + 1 < n_blocks:
            logits = scores(j + 1, 0)
        attend(j, 1, e, sink_terms, outs)
        store_attention(j, outs)
    project_attention(n_blocks - 1)


def _mix(x3d, q, k, v, u, sinks, bias, pool_w, pool_scale, w_out):
    b, s, _ = x3d.shape
    tq = MIX_TILE_Q
    nblk = tq // BLOCK
    resident = functools.partial(pl.BlockSpec, pipeline_mode=pl.Buffered(1))
    tile = lambda width: pl.BlockSpec((1, tq, width), lambda bi, i: (bi, i, 0))
    halo = lambda width: pl.BlockSpec(
        (1, BLOCK, width), lambda bi, i: (bi, jnp.maximum(i * nblk - 1, 0), 0))
    est = (2 * 2 * tq * D_MODEL * 4 + 2 * tq * (D_ATTN + 2 * D_KV) * 2 + 2 * tq * D_POOL * 4
           + 2 * BLOCK * (2 * D_KV * 2 + D_POOL * 4)
           + N_HEADS * BLOCK * 2 * BLOCK * 4 + 2 * D_MODEL * D_MODEL
           + 6 * (tq + BLOCK) * D_KV * 2 + (tq + BLOCK) * D_POOL * 2 + tq * D_MODEL * 2
           + 16 * BLOCK * 2 * BLOCK * 4 * 4)
    return pl.pallas_call(
        _mix_kernel,
        out_shape=jax.ShapeDtypeStruct((b, s, D_MODEL), jnp.float32),
        grid=(b, s // tq),
        in_specs=[
            pl.BlockSpec(memory_space=pltpu.SMEM),
            tile(D_MODEL),
            tile(D_ATTN),
            tile(D_KV), halo(D_KV),
            tile(D_KV), halo(D_KV),
            tile(D_POOL), halo(D_POOL),
            resident((N_HEADS, BLOCK, 2 * BLOCK), lambda bi, i: (0, 0, 0)),
            resident((len(POOL_WINDOWS), BLOCK, 2 * BLOCK), lambda bi, i: (0, 0, 0)),
            resident((D_POOL, D_POOL), lambda bi, i: (0, 0)),
            resident((1, D_POOL), lambda bi, i: (0, 0)),
            resident((D_MODEL, D_MODEL), lambda bi, i: (0, 0)),
        ],
        out_specs=tile(D_MODEL),
        scratch_shapes=[
            pltpu.VMEM((tq + BLOCK, D_KV), jnp.bfloat16),
            pltpu.VMEM((tq + BLOCK, D_KV), jnp.bfloat16),
            pltpu.VMEM((tq + BLOCK, 2 * D_KV), jnp.bfloat16),
            pltpu.VMEM((tq + BLOCK, 2 * D_KV), jnp.bfloat16),
            pltpu.VMEM((tq + BLOCK, D_POOL), jnp.bfloat16),
            pltpu.VMEM((tq, D_POOL), jnp.bfloat16),
            pltpu.VMEM((tq, D_ATTN), jnp.bfloat16),
        ],
        compiler_params=pltpu.CompilerParams(
            dimension_semantics=("arbitrary", "arbitrary"),
            vmem_limit_bytes=_vmem_limit(est + (8 << 20))),
        name="mix",
    )(sinks, x3d, q, k, k, v, v, u, u, bias, _window_sum_matrices(),
      _block_diagonal(pool_w.astype(jnp.bfloat16)), pool_scale.reshape(1, D_POOL),
      w_out.astype(jnp.bfloat16))


def kernel(x, ffn1_norm, ffn1_w_gate, ffn1_w_up, ffn1_w_down, mix_norm, w_in, q_norm, k_norm,
           attn_sinks, rel_bias, pool_w, pool_scale, w_out, ffn2_norm, ffn2_w_gate, ffn2_w_up,
           ffn2_w_down):
    b, s, d = x.shape
    assert (d, s % MIX_TILE_Q, (b * s) % FFN_TILE_M) == (D_MODEL, 0, 0)
    bias = _band_bias(rel_bias)
    for l in range(ffn1_norm.shape[0]):
        x2d = x.reshape(b * s, d)
        x2d = _ffn(x2d, ffn1_norm[l], ffn1_w_gate[l], ffn1_w_up[l], ffn1_w_down[l])
        q, k, v, u = _proj(x2d, mix_norm[l], w_in[l], q_norm[l], k_norm[l])
        x3d = _mix(x2d.reshape(b, s, d), q.reshape(b, s, D_ATTN), k.reshape(b, s, D_KV),
                   v.reshape(b, s, D_KV), u.reshape(b, s, D_POOL), attn_sinks[l], bias,
                   pool_w[l], pool_scale[l], w_out[l])
        x2d = _ffn(x3d.reshape(b * s, d), ffn2_norm[l], ffn2_w_gate[l], ffn2_w_up[l], ffn2_w_down[l])
        x = x2d.reshape(b, s, d)
    return x
```

```python
import functools

import numpy as np
import jax
import jax.numpy as jnp
from jax import lax
from jax.experimental import pallas as pl
from jax.experimental.pallas import tpu as pltpu

D_MODEL = 1024
HEAD_DIM = 64
N_HEADS = 8
N_KV_HEADS = 2
D_ATTN = N_HEADS * HEAD_DIM
D_KV = N_KV_HEADS * HEAD_DIM
D_POOL = D_MODEL - D_ATTN
POOL_WINDOWS = (2, 4, 8, 16)
POOL_GROUP_DIM = D_POOL // len(POOL_WINDOWS)
D_IN = D_ATTN + 2 * D_KV + D_POOL
WINDOW = 128
BLOCK = 128
N_BUCKETS = 32
MAX_DISTANCE = 128
D_FF = 2816
EPS = 1e-6
NEG = -1e30

V7X_LANES = 128
V7X_MXU_DIM = 256
V7X_VMEM_BYTES = 64 * 1024 * 1024

HEADS_PLAIN = (0, 2, 5, 7)
HEADS_ROLLED = (1, 3, 4, 6)

FFN_TILE_M = 512
FFN_CHUNK_F = 256
FFN_STAGE_SLOTS = 2
PROJ_TILE_M = 512
MIX_TILE_Q = 1024


def _vmem_limit(estimate_bytes):
    return int(min(V7X_VMEM_BYTES - (4 << 20), max(estimate_bytes, 16 << 20)))


def _rmsnorm_rows(x32, gain_row):
    ms = jnp.mean(x32 * x32, axis=-1, keepdims=True)
    return x32 * lax.rsqrt(ms + EPS) * gain_row


def _ffn_kernel(x_ref, g_ref, wg_hbm, wu_hbm, wd_hbm, o_ref,
                wg_ref, wu_ref, wd_ref, stage_in_ref, stage_out_ref, sem_ref, h_ref, a_ref):
    n_chunks = D_FF // FFN_CHUNK_F

    def chunk_copy(which, c):
        slot = c % FFN_STAGE_SLOTS
        window = pl.ds(c * FFN_CHUNK_F, FFN_CHUNK_F)
        sem = sem_ref.at[which * FFN_STAGE_SLOTS + slot]
        if which == 2:
            return pltpu.make_async_copy(wd_hbm.at[window, :], stage_out_ref.at[slot], sem)
        src = (wg_hbm, wu_hbm)[which]
        return pltpu.make_async_copy(src.at[:, window],
                                     stage_in_ref.at[which * FFN_STAGE_SLOTS + slot], sem)

    def land(which, c):
        slot = c % FFN_STAGE_SLOTS
        window = slice(c * FFN_CHUNK_F, (c + 1) * FFN_CHUNK_F)
        chunk_copy(which, c).wait()
        if which == 2:
            wd_ref[window, :] = stage_out_ref[slot].astype(jnp.bfloat16)
        else:
            dst = (wg_ref, wu_ref)[which]
            dst[:, window] = stage_in_ref[which * FFN_STAGE_SLOTS + slot].astype(jnp.bfloat16)
        if c + FFN_STAGE_SLOTS < n_chunks:
            chunk_copy(which, c + FFN_STAGE_SLOTS).start()

    def body(stream_weights):
        if stream_weights:
            for c in range(FFN_STAGE_SLOTS):
                for which in range(3):
                    chunk_copy(which, c).start()
        h_ref[...] = _rmsnorm_rows(x_ref[...], g_ref[...]).astype(jnp.bfloat16)
        for c in range(n_chunks):
            cols = slice(c * FFN_CHUNK_F, (c + 1) * FFN_CHUNK_F)
            if stream_weights:
                for which in range(3):
                    land(which, c)
            h = h_ref[...]
            gate = jnp.dot(h, wg_ref[:, cols], preferred_element_type=jnp.float32)
            up = jnp.dot(h, wu_ref[:, cols], preferred_element_type=jnp.float32)
            act = gate * (1.0 / (1.0 + jnp.exp(-gate))) * up
            a_ref[:, cols] = act.astype(jnp.bfloat16)
        y = jnp.dot(a_ref[...], wd_ref[...], preferred_element_type=jnp.float32)
        o_ref[...] = x_ref[...] + 0.5 * y

    first = pl.program_id(0) == 0
    pl.when(first)(functools.partial(body, True))
    pl.when(jnp.logical_not(first))(functools.partial(body, False))


def _ffn(x2d, gain, w_gate, w_up, w_down):
    m = x2d.shape[0]
    tm = FFN_TILE_M
    resident = functools.partial(pl.BlockSpec, pipeline_mode=pl.Buffered(1))
    est = (2 * 3 * D_MODEL * D_FF
           + 3 * FFN_STAGE_SLOTS * D_MODEL * FFN_CHUNK_F * 4
           + 2 * 2 * tm * D_MODEL * 4
           + tm * D_MODEL * 2 + tm * D_FF * 2
           + 6 * tm * FFN_CHUNK_F * 4 + tm * D_MODEL * 4)
    return pl.pallas_call(
        _ffn_kernel,
        out_shape=jax.ShapeDtypeStruct((m, D_MODEL), jnp.float32),
        grid=(m // tm,),
        in_specs=[
            pl.BlockSpec((tm, D_MODEL), lambda i: (i, 0)),
            resident((1, D_MODEL), lambda i: (0, 0)),
            pl.BlockSpec(memory_space=pltpu.HBM),
            pl.BlockSpec(memory_space=pltpu.HBM),
            pl.BlockSpec(memory_space=pltpu.HBM),
        ],
        out_specs=pl.BlockSpec((tm, D_MODEL), lambda i: (i, 0)),
        scratch_shapes=[
            pltpu.VMEM((D_MODEL, D_FF), jnp.bfloat16),
            pltpu.VMEM((D_MODEL, D_FF), jnp.bfloat16),
            pltpu.VMEM((D_FF, D_MODEL), jnp.bfloat16),
            pltpu.VMEM((2 * FFN_STAGE_SLOTS, D_MODEL, FFN_CHUNK_F), jnp.float32),
            pltpu.VMEM((FFN_STAGE_SLOTS, FFN_CHUNK_F, D_MODEL), jnp.float32),
            pltpu.SemaphoreType.DMA((3 * FFN_STAGE_SLOTS,)),
            pltpu.VMEM((tm, D_MODEL), jnp.bfloat16),
            pltpu.VMEM((tm, D_FF), jnp.bfloat16),
        ],
        compiler_params=pltpu.CompilerParams(
            dimension_semantics=("arbitrary",),
            vmem_limit_bytes=_vmem_limit(est + (8 << 20))),
        name="ffn",
    )(x2d, gain.reshape(1, D_MODEL), w_gate, w_up, w_down)


def _proj_kernel(x_ref, g_ref, w_ref, qg_ref, kg_ref, dm_ref,
                 q_ref, k_ref, v_ref, u_ref):
    tm = x_ref.shape[0]
    h = _rmsnorm_rows(x_ref[...], g_ref[...]).astype(jnp.bfloat16)
    z = jnp.dot(h, w_ref[...], preferred_element_type=jnp.float32)
    n_slab = (D_ATTN + 2 * D_KV) // V7X_MXU_DIM
    sq = jnp.concatenate(
        [z[:, i * V7X_MXU_DIM:(i + 1) * V7X_MXU_DIM] for i in range(n_slab)], axis=0)
    ms = jnp.dot((sq * sq).astype(jnp.bfloat16), dm_ref[...], preferred_element_type=jnp.float32)
    q_ms = jnp.concatenate([ms[i * tm:(i + 1) * tm] for i in range(n_slab - 1)], axis=1)
    k_ms = ms[(n_slab - 1) * tm:, :D_KV]
    q = z[:, :D_ATTN]
    k = z[:, D_ATTN:D_ATTN + D_KV]
    q_ref[...] = (q * lax.rsqrt(q_ms + EPS) * qg_ref[...]).astype(jnp.bfloat16)
    k_ref[...] = (k * lax.rsqrt(k_ms + EPS) * kg_ref[...]).astype(jnp.bfloat16)
    v_ref[...] = z[:, D_ATTN + D_KV:D_ATTN + 2 * D_KV].astype(jnp.bfloat16)
    u_ref[...] = z[:, D_ATTN + 2 * D_KV:]


def _head_mean_matrix(width):
    idx = np.arange(width) // HEAD_DIM
    return jnp.asarray((idx[:, None] == idx[None, :]).astype(np.float32) / HEAD_DIM, jnp.bfloat16)


def _proj(x2d, gain, w_in, q_gain, k_gain):
    m = x2d.shape[0]
    tm = PROJ_TILE_M
    resident = functools.partial(pl.BlockSpec, pipeline_mode=pl.Buffered(1))
    qg = (jnp.tile(q_gain, N_HEADS) * (HEAD_DIM ** -0.5)).reshape(1, D_ATTN)
    kg = jnp.tile(k_gain, N_KV_HEADS).reshape(1, D_KV)
    est = (2 * D_MODEL * D_IN + 2 * 2 * tm * D_MODEL * 4 + 4 * tm * D_IN * 4
           + 2 * 2 * tm * D_IN * 4)
    return pl.pallas_call(
        _proj_kernel,
        out_shape=(
            jax.ShapeDtypeStruct((m, D_ATTN), jnp.bfloat16),
            jax.ShapeDtypeStruct((m, D_KV), jnp.bfloat16),
            jax.ShapeDtypeStruct((m, D_KV), jnp.bfloat16),
            jax.ShapeDtypeStruct((m, D_POOL), jnp.float32),
        ),
        grid=(m // tm,),
        in_specs=[
            pl.BlockSpec((tm, D_MODEL), lambda i: (i, 0)),
            resident((1, D_MODEL), lambda i: (0, 0)),
            resident((D_MODEL, D_IN), lambda i: (0, 0)),
            resident((1, D_ATTN), lambda i: (0, 0)),
            resident((1, D_KV), lambda i: (0, 0)),
            resident((V7X_MXU_DIM, V7X_MXU_DIM), lambda i: (0, 0)),
        ],
        out_specs=(
            pl.BlockSpec((tm, D_ATTN), lambda i: (i, 0)),
            pl.BlockSpec((tm, D_KV), lambda i: (i, 0)),
            pl.BlockSpec((tm, D_KV), lambda i: (i, 0)),
            pl.BlockSpec((tm, D_POOL), lambda i: (i, 0)),
        ),
        compiler_params=pltpu.CompilerParams(
            dimension_semantics=("arbitrary",),
            vmem_limit_bytes=_vmem_limit(est + (8 << 20))),
        name="proj",
    )(x2d, gain.reshape(1, D_MODEL), w_in.astype(jnp.bfloat16), qg, kg,
      _head_mean_matrix(V7X_MXU_DIM))


def _t5_bucket(dist):
    n = np.maximum(dist, 0)
    max_exact = N_BUCKETS // 2
    large = max_exact + (np.log(np.maximum(n, 1) / max_exact)
                         / np.log(MAX_DISTANCE / max_exact)
                         * (N_BUCKETS - max_exact)).astype(np.int32)
    large = np.minimum(large, N_BUCKETS - 1)
    return np.where(n < max_exact, n, large).astype(np.int32)


def _band_bucket_table():
    ql = np.arange(BLOCK)[:, None]
    kl = np.arange(2 * BLOCK)[None, :]
    dist = ql + BLOCK - kl
    in_band = (dist >= 0) & (dist < WINDOW)
    return np.where(in_band, _t5_bucket(dist), -1).astype(np.int32)


def _bias_kernel(rel_ref, bucket_ref, o_ref):
    bucket = bucket_ref[...]
    for slot, head in enumerate(HEADS_PLAIN + HEADS_ROLLED):
        acc = jnp.full((BLOCK, 2 * BLOCK), NEG, jnp.float32)
        for b in range(N_BUCKETS):
            acc = jnp.where(bucket == b, rel_ref[b, head], acc)
        o_ref[slot] = acc


def _band_bias(rel_bias):
    return pl.pallas_call(
        _bias_kernel,
        out_shape=jax.ShapeDtypeStruct((N_HEADS, BLOCK, 2 * BLOCK), jnp.float32),
        in_specs=[
            pl.BlockSpec(memory_space=pltpu.SMEM),
            pl.BlockSpec(memory_space=pltpu.VMEM),
        ],
        out_specs=pl.BlockSpec(memory_space=pltpu.VMEM),
        name="band_bias",
    )(rel_bias, jnp.asarray(_band_bucket_table()))


def _window_sum_matrices():
    ql = np.arange(BLOCK)[:, None]
    kl = np.arange(2 * BLOCK)[None, :]
    dist = ql + BLOCK - kl
    mats = [((dist >= 0) & (dist < w)).astype(np.float32) for w in POOL_WINDOWS]
    return jnp.asarray(np.stack(mats), jnp.bfloat16)


def _block_diagonal(w):
    g, c, _ = w.shape
    on_diagonal = jnp.eye(g, dtype=bool)[:, None, :, None]
    return jnp.where(on_diagonal, w[:, :, None, :], jnp.zeros((), w.dtype)).reshape(g * c, g * c)


def _mix_kernel(sink_ref, x_ref, q_ref, k_ref, kh_ref, v_ref, vh_ref, u_ref, uh_ref,
                bias_ref, wsum_ref, pw_ref, ps_ref, wo_ref, o_ref,
                kx_ref, kxr_ref, vx_ref, vxr_ref, ux_ref, pooled_ref, y_ref):
    tq = q_ref.shape[1]
    n_blocks = tq // BLOCK
    first = pl.program_id(1) == 0
    half = HEAD_DIM

    k_halo = jnp.where(first, jnp.zeros_like(kh_ref[0]), kh_ref[0])
    v_halo = jnp.where(first, jnp.zeros_like(vh_ref[0]), vh_ref[0])
    u_halo = jnp.where(first, jnp.zeros_like(uh_ref[0]), uh_ref[0])
    kx_ref[:BLOCK] = k_halo
    kx_ref[BLOCK:] = k_ref[0]
    kxr_ref[...] = pltpu.roll(kx_ref[...].astype(jnp.float32), half, 1).astype(jnp.bfloat16)
    ones = jnp.ones((tq + BLOCK, D_KV), jnp.bfloat16)
    vx_ref[:BLOCK, :D_KV] = v_halo
    vx_ref[BLOCK:, :D_KV] = v_ref[0]
    vx_ref[:, D_KV:] = ones
    vxr_ref[:, :D_KV] = pltpu.roll(vx_ref[:, :D_KV].astype(jnp.float32), half, 1).astype(jnp.bfloat16)
    vxr_ref[:, D_KV:] = ones
    ux_ref[:BLOCK] = u_halo.astype(jnp.bfloat16)
    ux_ref[BLOCK:] = u_ref[0].astype(jnp.bfloat16)

    lane = lax.broadcasted_iota(jnp.int32, (BLOCK, 2 * half), 1)
    low = lane < half
    col = lax.broadcasted_iota(jnp.int32, (1, 2 * BLOCK), 1)
    halo_penalty = jnp.where(first & (col < BLOCK), NEG, 0.0).astype(jnp.float32)
    row = lax.broadcasted_iota(jnp.int32, (BLOCK, 1), 0)
    groups = ((HEADS_PLAIN, kx_ref, vx_ref, 0), (HEADS_ROLLED, kxr_ref, vxr_ref, len(HEADS_PLAIN)))

    def rows_of(j):
        return slice(j * BLOCK, (j + 1) * BLOCK)

    def keys_of(j):
        return slice(j * BLOCK, (j + 2) * BLOCK)

    def scores(j, grp):
        heads, k_src, _, _ = groups[grp]
        q = q_ref[0, rows_of(j), :]
        zero = jnp.zeros((BLOCK, 2 * half), q.dtype)
        pieces = []
        for hd in heads:
            pair = q[:, (hd // 2) * 2 * half:(hd // 2 + 1) * 2 * half]
            pieces.append(jnp.where(low if hd % 2 == 0 else ~low, pair, zero))
        qs = jnp.concatenate(pieces, axis=0)
        return lax.dot_general(qs, k_src[keys_of(j), :], (((1,), (1,)), ((), ())),
                               preferred_element_type=jnp.float32)

    def softmax_numerators(j, grp, logits):
        heads, _, _, slot0 = groups[grp]
        es, sink_terms = [], []
        for s, hd in enumerate(heads):
            lg = logits[s * BLOCK:(s + 1) * BLOCK] + bias_ref[slot0 + s]
            if j == 0:
                lg = lg + halo_penalty
            sink = sink_ref[hd]
            m = jnp.maximum(jnp.max(lg, axis=-1, keepdims=True), sink)
            es.append(jnp.exp(lg - m).astype(jnp.bfloat16))
            sink_terms.append(jnp.exp(sink - m))
        return jnp.concatenate(es, axis=0), sink_terms

    def attend(j, grp, e, sink_terms, outs):
        heads, _, v_src, _ = groups[grp]
        pv = jnp.dot(e, v_src[keys_of(j), :], preferred_element_type=jnp.float32)
        for s, hd in enumerate(heads):
            blk = pv[s * BLOCK:(s + 1) * BLOCK]
            outs[hd] = blk[:, :D_KV] / (blk[:, D_KV:] + sink_terms[s])

    def store_attention(j, outs):
        for p in range(N_HEADS // 2):
            y_ref[rows_of(j), p * 2 * half:(p + 1) * 2 * half] = jnp.where(
                low, outs[2 * p], outs[2 * p + 1]).astype(jnp.bfloat16)

    def pool_means(j):
        t_glob = pl.program_id(1) * tq + j * BLOCK + row
        for g, w in enumerate(POOL_WINDOWS):
            gc = slice(g * POOL_GROUP_DIM, (g + 1) * POOL_GROUP_DIM)
            wsum = jnp.dot(wsum_ref[g], ux_ref[keys_of(j), gc], preferred_element_type=jnp.float32)
            cnt = jnp.minimum(t_glob + 1, w).astype(jnp.float32)
            pooled = wsum * (1.0 / cnt) - u_ref[0, rows_of(j), gc]
            pooled_ref[rows_of(j), gc] = pooled.astype(jnp.bfloat16)

    def project_attention(j):
        o_ref[0, rows_of(j), :] += jnp.dot(
            y_ref[rows_of(j), :], wo_ref[:D_ATTN, :], preferred_element_type=jnp.float32)

    for j in range(n_blocks):
        pool_means(j)
    logits = scores(0, 0)
    for j in range(n_blocks):
        outs = {}
        logits_b = scores(j, 1)
        e, sink_terms = softmax_numerators(j, 0, logits)
        if j == 0:
            y_pool = jnp.dot(pooled_ref[...], pw_ref[...],
                             preferred_element_type=jnp.float32) * ps_ref[...]
        else:
            project_attention(j - 1)
        attend(j, 0, e, sink_terms, outs)
        e, sink_terms = softmax_numerators(j, 1, logits_b)
        if j == 0:
            o_ref[0] = x_ref[0] + jnp.dot(y_pool.astype(jnp.bfloat16), wo_ref[D_ATTN:, :],
                                          preferred_element_type=jnp.float32)
        if j + 1 < n_blocks:
            logits = scores(j + 1, 0)
        attend(j, 1, e, sink_terms, outs)
        store_attention(j, outs)
    project_attention(n_blocks - 1)


def _mix(x3d, q, k, v, u, sinks, bias, pool_w, pool_scale, w_out):
    b, s, _ = x3d.shape
    tq = MIX_TILE_Q
    nblk = tq // BLOCK
    resident = functools.partial(pl.BlockSpec, pipeline_mode=pl.Buffered(1))
    tile = lambda width: pl.BlockSpec((1, tq, width), lambda bi, i: (bi, i, 0))
    halo = lambda width: pl.BlockSpec(
        (1, BLOCK, width), lambda bi, i: (bi, jnp.maximum(i * nblk - 1, 0), 0))
    est = (2 * 2 * tq * D_MODEL * 4 + 2 * tq * (D_ATTN + 2 * D_KV) * 2 + 2 * tq * D_POOL * 4
           + 2 * BLOCK * (2 * D_KV * 2 + D_POOL * 4)
           + N_HEADS * BLOCK * 2 * BLOCK * 4 + 2 * D_MODEL * D_MODEL
           + 6 * (tq + BLOCK) * D_KV * 2 + (tq + BLOCK) * D_POOL * 2 + tq * D_MODEL * 2
           + 16 * BLOCK * 2 * BLOCK * 4 * 4)
    return pl.pallas_call(
        _mix_kernel,
        out_shape=jax.ShapeDtypeStruct((b, s, D_MODEL), jnp.float32),
        grid=(b, s // tq),
        in_specs=[
            pl.BlockSpec(memory_space=pltpu.SMEM),
            tile(D_MODEL),
            tile(D_ATTN),
            tile(D_KV), halo(D_KV),
            tile(D_KV), halo(D_KV),
            tile(D_POOL), halo(D_POOL),
            resident((N_HEADS, BLOCK, 2 * BLOCK), lambda bi, i: (0, 0, 0)),
            resident((len(POOL_WINDOWS), BLOCK, 2 * BLOCK), lambda bi, i: (0, 0, 0)),
            resident((D_POOL, D_POOL), lambda bi, i: (0, 0)),
            resident((1, D_POOL), lambda bi, i: (0, 0)),
            resident((D_MODEL, D_MODEL), lambda bi, i: (0, 0)),
        ],
        out_specs=tile(D_MODEL),
        scratch_shapes=[
            pltpu.VMEM((tq + BLOCK, D_KV), jnp.bfloat16),
            pltpu.VMEM((tq + BLOCK, D_KV), jnp.bfloat16),
            pltpu.VMEM((tq + BLOCK, 2 * D_KV), jnp.bfloat16),
            pltpu.VMEM((tq + BLOCK, 2 * D_KV), jnp.bfloat16),
            pltpu.VMEM((tq + BLOCK, D_POOL), jnp.bfloat16),
            pltpu.VMEM((tq, D_POOL), jnp.bfloat16),
            pltpu.VMEM((tq, D_ATTN), jnp.bfloat16),
        ],
        compiler_params=pltpu.CompilerParams(
            dimension_semantics=("arbitrary", "arbitrary"),
            vmem_limit_bytes=_vmem_limit(est + (8 << 20))),
        name="mix",
    )(sinks, x3d, q, k, k, v, v, u, u, bias, _window_sum_matrices(),
      _block_diagonal(pool_w.astype(jnp.bfloat16)), pool_scale.reshape(1, D_POOL),
      w_out.astype(jnp.bfloat16))


def kernel(x, ffn1_norm, ffn1_w_gate, ffn1_w_up, ffn1_w_down, mix_norm, w_in, q_norm, k_norm,
           attn_sinks, rel_bias, pool_w, pool_scale, w_out, ffn2_norm, ffn2_w_gate, ffn2_w_up,
           ffn2_w_down):
    b, s, d = x.shape
    assert (d, s % MIX_TILE_Q, (b * s) % FFN_TILE_M) == (D_MODEL, 0, 0)
    bias = _band_bias(rel_bias)
    for l in range(ffn1_norm.shape[0]):
        x2d = x.reshape(b * s, d)
        x2d = _ffn(x2d, ffn1_norm[l], ffn1_w_gate[l], ffn1_w_up[l], ffn1_w_down[l])
        q, k, v, u = _proj(x2d, mix_norm[l], w_in[l], q_norm[l], k_norm[l])
        x3d = _mix(x2d.reshape(b, s, d), q.reshape(b, s, D_ATTN), k.reshape(b, s, D_KV),
                   v.reshape(b, s, D_KV), u.reshape(b, s, D_POOL), attn_sinks[l], bias,
                   pool_w[l], pool_scale[l], w_out[l])
        x2d = _ffn(x3d.reshape(b * s, d), ffn2_norm[l], ffn2_w_gate[l], ffn2_w_up[l], ffn2_w_down[l])
        x = x2d.reshape(b, s, d)
    return x
```

```python
import functools

import numpy as np
import jax
import jax.numpy as jnp
from jax import lax
from jax.experimental import pallas as pl
from jax.experimental.pallas import tpu as pltpu

D_MODEL = 1024
HEAD_DIM = 64
N_HEADS = 8
N_KV_HEADS = 2
D_ATTN = N_HEADS * HEAD_DIM
D_KV = N_KV_HEADS * HEAD_DIM
D_POOL = D_MODEL - D_ATTN
POOL_WINDOWS = (2, 4, 8, 16)
POOL_GROUP_DIM = D_POOL // len(POOL_WINDOWS)
D_IN = D_ATTN + 2 * D_KV + D_POOL
WINDOW = 128
BLOCK = 128
N_BUCKETS = 32
MAX_DISTANCE = 128
D_FF = 2816
EPS = 1e-6
NEG = -1e30

V7X_LANES = 128
V7X_MXU_DIM = 256
V7X_VMEM_BYTES = 64 * 1024 * 1024

HEADS_PLAIN = (0, 2, 5, 7)
HEADS_ROLLED = (1, 3, 4, 6)

TILE_M = 512
FFN_CHUNK_F = 256
FFN_STAGE_SLOTS = 2


def _vmem_limit(estimate_bytes):
    return int(min(V7X_VMEM_BYTES - (4 << 20), max(estimate_bytes, 16 << 20)))


def _rmsnorm_rows(x32, gain_row):
    ms = jnp.mean(x32 * x32, axis=-1, keepdims=True)
    return x32 * lax.rsqrt(ms + EPS) * gain_row


def _ffn_scratch(tm):
    return [
        pltpu.VMEM((D_MODEL, D_FF), jnp.bfloat16),
        pltpu.VMEM((D_MODEL, D_FF), jnp.bfloat16),
        pltpu.VMEM((D_FF, D_MODEL), jnp.bfloat16),
        pltpu.VMEM((2 * FFN_STAGE_SLOTS, D_MODEL, FFN_CHUNK_F), jnp.float32),
        pltpu.VMEM((FFN_STAGE_SLOTS, FFN_CHUNK_F, D_MODEL), jnp.float32),
        pltpu.SemaphoreType.DMA((3 * FFN_STAGE_SLOTS,)),
        pltpu.VMEM((tm, D_MODEL), jnp.bfloat16),
        pltpu.VMEM((tm, D_FF), jnp.bfloat16),
    ]


def _ffn_vmem_bytes(tm):
    return (2 * 3 * D_MODEL * D_FF + 3 * FFN_STAGE_SLOTS * D_MODEL * FFN_CHUNK_F * 4
            + tm * D_MODEL * 2 + tm * D_FF * 2 + 6 * tm * FFN_CHUNK_F * 4 + tm * D_MODEL * 4)


def _swiglu_tile(x32, g_ref, w_hbm, ffn_scratch, stream_weights):
    wg_hbm, wu_hbm, wd_hbm = w_hbm
    wg_ref, wu_ref, wd_ref, stage_in_ref, stage_out_ref, sem_ref, h_ref, a_ref = ffn_scratch
    n_chunks = D_FF // FFN_CHUNK_F

    def chunk_copy(which, c):
        slot = c % FFN_STAGE_SLOTS
        window = pl.ds(c * FFN_CHUNK_F, FFN_CHUNK_F)
        sem = sem_ref.at[which * FFN_STAGE_SLOTS + slot]
        if which == 2:
            return pltpu.make_async_copy(wd_hbm.at[window, :], stage_out_ref.at[slot], sem)
        src = (wg_hbm, wu_hbm)[which]
        return pltpu.make_async_copy(src.at[:, window],
                                     stage_in_ref.at[which * FFN_STAGE_SLOTS + slot], sem)

    def land(which, c):
        slot = c % FFN_STAGE_SLOTS
        window = slice(c * FFN_CHUNK_F, (c + 1) * FFN_CHUNK_F)
        chunk_copy(which, c).wait()
        if which == 2:
            wd_ref[window, :] = stage_out_ref[slot].astype(jnp.bfloat16)
        else:
            dst = (wg_ref, wu_ref)[which]
            dst[:, window] = stage_in_ref[which * FFN_STAGE_SLOTS + slot].astype(jnp.bfloat16)
        if c + FFN_STAGE_SLOTS < n_chunks:
            chunk_copy(which, c + FFN_STAGE_SLOTS).start()

    if stream_weights:
        for c in range(FFN_STAGE_SLOTS):
            for which in range(3):
                chunk_copy(which, c).start()
    h_ref[...] = _rmsnorm_rows(x32, g_ref[...]).astype(jnp.bfloat16)
    for c in range(n_chunks):
        cols = slice(c * FFN_CHUNK_F, (c + 1) * FFN_CHUNK_F)
        if stream_weights:
            for which in range(3):
                land(which, c)
        h = h_ref[...]
        gate = jnp.dot(h, wg_ref[:, cols], preferred_element_type=jnp.float32)
        up = jnp.dot(h, wu_ref[:, cols], preferred_element_type=jnp.float32)
        act = gate * (1.0 / (1.0 + jnp.exp(-gate))) * up
        a_ref[:, cols] = act.astype(jnp.bfloat16)
    return jnp.dot(a_ref[...], wd_ref[...], preferred_element_type=jnp.float32)


def _ffn_proj_kernel(x_ref, g1_ref, wg_hbm, wu_hbm, wd_hbm, gm_ref, win_ref, qg_ref, kg_ref, dm_ref,
                     x1_ref, q_ref, k_ref, v_ref, u_ref, *ffn_scratch):
    tm = x_ref.shape[0]

    def body(stream_weights):
        y = _swiglu_tile(x_ref[...], g1_ref, (wg_hbm, wu_hbm, wd_hbm), ffn_scratch, stream_weights)
        x1 = x_ref[...] + 0.5 * y
        x1_ref[...] = x1
        h = _rmsnorm_rows(x1, gm_ref[...]).astype(jnp.bfloat16)
        z = jnp.dot(h, win_ref[...], preferred_element_type=jnp.float32)
        n_slab = (D_ATTN + 2 * D_KV) // V7X_MXU_DIM
        sq = jnp.concatenate(
            [z[:, i * V7X_MXU_DIM:(i + 1) * V7X_MXU_DIM] for i in range(n_slab)], axis=0)
        ms = jnp.dot((sq * sq).astype(jnp.bfloat16), dm_ref[...], preferred_element_type=jnp.float32)
        q_ms = jnp.concatenate([ms[i * tm:(i + 1) * tm] for i in range(n_slab - 1)], axis=1)
        k_ms = ms[(n_slab - 1) * tm:, :D_KV]
        q = z[:, :D_ATTN]
        k = z[:, D_ATTN:D_ATTN + D_KV]
        q_ref[...] = (q * lax.rsqrt(q_ms + EPS) * qg_ref[...]).astype(jnp.bfloat16)
        k_ref[...] = (k * lax.rsqrt(k_ms + EPS) * kg_ref[...]).astype(jnp.bfloat16)
        v_ref[...] = z[:, D_ATTN + D_KV:D_ATTN + 2 * D_KV].astype(jnp.bfloat16)
        u_ref[...] = z[:, D_ATTN + 2 * D_KV:]

    first = pl.program_id(0) == 0
    pl.when(first)(functools.partial(body, True))
    pl.when(jnp.logical_not(first))(functools.partial(body, False))


def _head_mean_matrix(width):
    idx = np.arange(width) // HEAD_DIM
    return jnp.asarray((idx[:, None] == idx[None, :]).astype(np.float32) / HEAD_DIM, jnp.bfloat16)


def _ffn_proj(x2d, ffn_gain, w_gate, w_up, w_down, mix_gain, w_in, q_gain, k_gain):
    m = x2d.shape[0]
    tm = TILE_M
    resident = functools.partial(pl.BlockSpec, pipeline_mode=pl.Buffered(1))
    rows = lambda width: pl.BlockSpec((tm, width), lambda i: (i, 0))
    hbm = pl.BlockSpec(memory_space=pltpu.HBM)
    qg = (jnp.tile(q_gain, N_HEADS) * (HEAD_DIM ** -0.5)).reshape(1, D_ATTN)
    kg = jnp.tile(k_gain, N_KV_HEADS).reshape(1, D_KV)
    est = (_ffn_vmem_bytes(tm) + 2 * D_MODEL * D_IN
           + 2 * tm * D_MODEL * 4 * 2
           + 2 * tm * (D_ATTN * 2 + 2 * D_KV * 2 + D_POOL * 4)
           + 4 * tm * D_IN * 4)
    return pl.pallas_call(
        _ffn_proj_kernel,
        out_shape=(
            jax.ShapeDtypeStruct((m, D_MODEL), jnp.float32),
            jax.ShapeDtypeStruct((m, D_ATTN), jnp.bfloat16),
            jax.ShapeDtypeStruct((m, D_KV), jnp.bfloat16),
            jax.ShapeDtypeStruct((m, D_KV), jnp.bfloat16),
            jax.ShapeDtypeStruct((m, D_POOL), jnp.float32),
        ),
        grid=(m // tm,),
        in_specs=[
            rows(D_MODEL),
            resident((1, D_MODEL), lambda i: (0, 0)),
            hbm, hbm, hbm,
            resident((1, D_MODEL), lambda i: (0, 0)),
            resident((D_MODEL, D_IN), lambda i: (0, 0)),
            resident((1, D_ATTN), lambda i: (0, 0)),
            resident((1, D_KV), lambda i: (0, 0)),
            resident((V7X_MXU_DIM, V7X_MXU_DIM), lambda i: (0, 0)),
        ],
        out_specs=(rows(D_MODEL), rows(D_ATTN), rows(D_KV), rows(D_KV), rows(D_POOL)),
        scratch_shapes=_ffn_scratch(tm),
        compiler_params=pltpu.CompilerParams(
            dimension_semantics=("arbitrary",),
            vmem_limit_bytes=_vmem_limit(est + (8 << 20))),
        name="ffn_proj",
    )(x2d, ffn_gain.reshape(1, D_MODEL), w_gate, w_up, w_down, mix_gain.reshape(1, D_MODEL),
      w_in.astype(jnp.bfloat16), qg, kg, _head_mean_matrix(V7X_MXU_DIM))


def _t5_bucket(dist):
    n = np.maximum(dist, 0)
    max_exact = N_BUCKETS // 2
    large = max_exact + (np.log(np.maximum(n, 1) / max_exact)
                         / np.log(MAX_DISTANCE / max_exact)
                         * (N_BUCKETS - max_exact)).astype(np.int32)
    large = np.minimum(large, N_BUCKETS - 1)
    return np.where(n < max_exact, n, large).astype(np.int32)


def _band_bucket_table():
    ql = np.arange(BLOCK)[:, None]
    kl = np.arange(2 * BLOCK)[None, :]
    dist = ql + BLOCK - kl
    in_band = (dist >= 0) & (dist < WINDOW)
    return np.where(in_band, _t5_bucket(dist), -1).astype(np.int32)


def _bias_kernel(rel_ref, bucket_ref, o_ref):
    bucket = bucket_ref[...]
    for slot, head in enumerate(HEADS_PLAIN + HEADS_ROLLED):
        acc = jnp.full((BLOCK, 2 * BLOCK), NEG, jnp.float32)
        for b in range(N_BUCKETS):
            acc = jnp.where(bucket == b, rel_ref[b, head], acc)
        o_ref[slot] = acc


def _band_bias(rel_bias):
    return pl.pallas_call(
        _bias_kernel,
        out_shape=jax.ShapeDtypeStruct((N_HEADS, BLOCK, 2 * BLOCK), jnp.float32),
        in_specs=[
            pl.BlockSpec(memory_space=pltpu.SMEM),
            pl.BlockSpec(memory_space=pltpu.VMEM),
        ],
        out_specs=pl.BlockSpec(memory_space=pltpu.VMEM),
        name="band_bias",
    )(rel_bias, jnp.asarray(_band_bucket_table()))


def _window_sum_matrices():
    ql = np.arange(BLOCK)[:, None]
    kl = np.arange(2 * BLOCK)[None, :]
    dist = ql + BLOCK - kl
    mats = [((dist >= 0) & (dist < w)).astype(np.float32) for w in POOL_WINDOWS]
    return jnp.asarray(np.stack(mats), jnp.bfloat16)


def _block_diagonal(w):
    g, c, _ = w.shape
    on_diagonal = jnp.eye(g, dtype=bool)[:, None, :, None]
    return jnp.where(on_diagonal, w[:, :, None, :], jnp.zeros((), w.dtype)).reshape(g * c, g * c)


def _mix_tile(sink_ref, x_ref, q_ref, k_ref, kh_ref, v_ref, vh_ref, u_ref, uh_ref,
              bias_ref, wsum_ref, pw_ref, ps_ref, wo_ref, x2_ref, mix_scratch):
    kx_ref, kxr_ref, vx_ref, vxr_ref, ux_ref, pooled_ref, y_ref = mix_scratch
    tq = q_ref.shape[1]
    n_blocks = tq // BLOCK
    first = pl.program_id(1) == 0
    half = HEAD_DIM

    k_halo = jnp.where(first, jnp.zeros_like(kh_ref[0]), kh_ref[0])
    v_halo = jnp.where(first, jnp.zeros_like(vh_ref[0]), vh_ref[0])
    u_halo = jnp.where(first, jnp.zeros_like(uh_ref[0]), uh_ref[0])
    kx_ref[:BLOCK] = k_halo
    kx_ref[BLOCK:] = k_ref[0]
    kxr_ref[...] = pltpu.roll(kx_ref[...].astype(jnp.float32), half, 1).astype(jnp.bfloat16)
    ones = jnp.ones((tq + BLOCK, D_KV), jnp.bfloat16)
    vx_ref[:BLOCK, :D_KV] = v_halo
    vx_ref[BLOCK:, :D_KV] = v_ref[0]
    vx_ref[:, D_KV:] = ones
    vxr_ref[:, :D_KV] = pltpu.roll(vx_ref[:, :D_KV].astype(jnp.float32), half, 1).astype(jnp.bfloat16)
    vxr_ref[:, D_KV:] = ones
    ux_ref[:BLOCK] = u_halo.astype(jnp.bfloat16)
    ux_ref[BLOCK:] = u_ref[0].astype(jnp.bfloat16)

    lane = lax.broadcasted_iota(jnp.int32, (BLOCK, 2 * half), 1)
    low = lane < half
    col = lax.broadcasted_iota(jnp.int32, (1, 2 * BLOCK), 1)
    halo_penalty = jnp.where(first & (col < BLOCK), NEG, 0.0).astype(jnp.float32)
    row = lax.broadcasted_iota(jnp.int32, (BLOCK, 1), 0)
    groups = ((HEADS_PLAIN, kx_ref, vx_ref, 0), (HEADS_ROLLED, kxr_ref, vxr_ref, len(HEADS_PLAIN)))

    def rows_of(j):
        return slice(j * BLOCK, (j + 1) * BLOCK)

    def keys_of(j):
        return slice(j * BLOCK, (j + 2) * BLOCK)

    def scores(j, grp):
        heads, k_src, _, _ = groups[grp]
        q = q_ref[0, rows_of(j), :]
        zero = jnp.zeros((BLOCK, 2 * half), q.dtype)
        pieces = []
        for hd in heads:
            pair = q[:, (hd // 2) * 2 * half:(hd // 2 + 1) * 2 * half]
            pieces.append(jnp.where(low if hd % 2 == 0 else ~low, pair, zero))
        qs = jnp.concatenate(pieces, axis=0)
        return lax.dot_general(qs, k_src[keys_of(j), :], (((1,), (1,)), ((), ())),
                               preferred_element_type=jnp.float32)

    def softmax_numerators(j, grp, logits):
        heads, _, _, slot0 = groups[grp]
        es, sink_terms = [], []
        for s, hd in enumerate(heads):
            lg = logits[s * BLOCK:(s + 1) * BLOCK] + bias_ref[slot0 + s]
            if j == 0:
                lg = lg + halo_penalty
            sink = sink_ref[hd]
            m = jnp.maximum(jnp.max(lg, axis=-1, keepdims=True), sink)
            es.append(jnp.exp(lg - m).astype(jnp.bfloat16))
            sink_terms.append(jnp.exp(sink - m))
        return jnp.concatenate(es, axis=0), sink_terms

    def attend(j, grp, e, sink_terms, outs):
        heads, _, v_src, _ = groups[grp]
        pv = jnp.dot(e, v_src[keys_of(j), :], preferred_element_type=jnp.float32)
        for s, hd in enumerate(heads):
            blk = pv[s * BLOCK:(s + 1) * BLOCK]
            outs[hd] = blk[:, :D_KV] / (blk[:, D_KV:] + sink_terms[s])

    def store_attention(j, outs):
        for p in range(N_HEADS // 2):
            y_ref[rows_of(j), p * 2 * half:(p + 1) * 2 * half] = jnp.where(
                low, outs[2 * p], outs[2 * p + 1]).astype(jnp.bfloat16)

    def pool_means(j):
        t_glob = pl.program_id(1) * tq + j * BLOCK + row
        for g, w in enumerate(POOL_WINDOWS):
            gc = slice(g * POOL_GROUP_DIM, (g + 1) * POOL_GROUP_DIM)
            wsum = jnp.dot(wsum_ref[g], ux_ref[keys_of(j), gc], preferred_element_type=jnp.float32)
            cnt = jnp.minimum(t_glob + 1, w).astype(jnp.float32)
            pooled = wsum * (1.0 / cnt) - u_ref[0, rows_of(j), gc]
            pooled_ref[rows_of(j), gc] = pooled.astype(jnp.bfloat16)

    def project_attention(j):
        x2_ref[rows_of(j), :] += jnp.dot(
            y_ref[rows_of(j), :], wo_ref[:D_ATTN, :], preferred_element_type=jnp.float32)

    for j in range(n_blocks):
        pool_means(j)
    logits = scores(0, 0)
    for j in range(n_blocks):
        outs = {}
        logits_b = scores(j, 1)
        e, sink_terms = softmax_numerators(j, 0, logits)
        if j == 0:
            y_pool = jnp.dot(pooled_ref[...], pw_ref[...],
                             preferred_element_type=jnp.float32) * ps_ref[...]
        else:
            project_attention(j - 1)
        attend(j, 0, e, sink_terms, outs)
        e, sink_terms = softmax_numerators(j, 1, logits_b)
        if j == 0:
            x2_ref[...] = x_ref[0] + jnp.dot(y_pool.astype(jnp.bfloat16), wo_ref[D_ATTN:, :],
                                             preferred_element_type=jnp.float32)
        if j + 1 < n_blocks:
            logits = scores(j + 1, 0)
        attend(j, 1, e, sink_terms, outs)
        store_attention(j, outs)
    project_attention(n_blocks - 1)


N_MIX_SCRATCH = 8


def _mix_ffn_kernel(sink_ref, x_ref, q_ref, k_ref, kh_ref, v_ref, vh_ref, u_ref, uh_ref,
                    bias_ref, wsum_ref, pw_ref, ps_ref, wo_ref, g2_ref, wg_hbm, wu_hbm, wd_hbm,
                    o_ref, *scratch):
    mix_scratch, x2_ref = scratch[:N_MIX_SCRATCH - 1], scratch[N_MIX_SCRATCH - 1]
    ffn_scratch = scratch[N_MIX_SCRATCH:]

    def body(stream_weights):
        _mix_tile(sink_ref, x_ref, q_ref, k_ref, kh_ref, v_ref, vh_ref, u_ref, uh_ref,
                  bias_ref, wsum_ref, pw_ref, ps_ref, wo_ref, x2_ref, mix_scratch)
        y = _swiglu_tile(x2_ref[...], g2_ref, (wg_hbm, wu_hbm, wd_hbm), ffn_scratch, stream_weights)
        o_ref[0] = x2_ref[...] + 0.5 * y

    first = jnp.logical_and(pl.program_id(0) == 0, pl.program_id(1) == 0)
    pl.when(first)(functools.partial(body, True))
    pl.when(jnp.logical_not(first))(functools.partial(body, False))


def _mix_ffn(x3d, q, k, v, u, sinks, bias, pool_w, pool_scale, w_out, ffn_gain, w_gate, w_up, w_down):
    b, s, _ = x3d.shape
    tq = TILE_M
    nblk = tq // BLOCK
    resident = functools.partial(pl.BlockSpec, pipeline_mode=pl.Buffered(1))
    hbm = pl.BlockSpec(memory_space=pltpu.HBM)
    tile = lambda width: pl.BlockSpec((1, tq, width), lambda bi, i: (bi, i, 0))
    halo = lambda width: pl.BlockSpec(
        (1, BLOCK, width), lambda bi, i: (bi, jnp.maximum(i * nblk - 1, 0), 0))
    est = (_ffn_vmem_bytes(tq)
           + 2 * 2 * tq * D_MODEL * 4 + 2 * tq * (D_ATTN + 2 * D_KV) * 2 + 2 * tq * D_POOL * 4
           + 2 * BLOCK * (2 * D_KV * 2 + D_POOL * 4)
           + N_HEADS * BLOCK * 2 * BLOCK * 4 + 2 * D_MODEL * D_MODEL + 2 * D_POOL * D_POOL
           + 6 * (tq + BLOCK) * D_KV * 2 + (tq + BLOCK) * D_POOL * 2 + tq * D_MODEL * 2
           + tq * D_MODEL * 4 + 16 * BLOCK * 2 * BLOCK * 4 * 4)
    return pl.pallas_call(
        _mix_ffn_kernel,
        out_shape=jax.ShapeDtypeStruct((b, s, D_MODEL), jnp.float32),
        grid=(b, s // tq),
        in_specs=[
            pl.BlockSpec(memory_space=pltpu.SMEM),
            tile(D_MODEL),
            tile(D_ATTN),
            tile(D_KV), halo(D_KV),
            tile(D_KV), halo(D_KV),
            tile(D_POOL), halo(D_POOL),
            resident((N_HEADS, BLOCK, 2 * BLOCK), lambda bi, i: (0, 0, 0)),
            resident((len(POOL_WINDOWS), BLOCK, 2 * BLOCK), lambda bi, i: (0, 0, 0)),
            resident((D_POOL, D_POOL), lambda bi, i: (0, 0)),
            resident((1, D_POOL), lambda bi, i: (0, 0)),
            resident((D_MODEL, D_MODEL), lambda bi, i: (0, 0)),
            resident((1, D_MODEL), lambda bi, i: (0, 0)),
            hbm, hbm, hbm,
        ],
        out_specs=tile(D_MODEL),
        scratch_shapes=[
            pltpu.VMEM((tq + BLOCK, D_KV), jnp.bfloat16),
            pltpu.VMEM((tq + BLOCK, D_KV), jnp.bfloat16),
            pltpu.VMEM((tq + BLOCK, 2 * D_KV), jnp.bfloat16),
            pltpu.VMEM((tq + BLOCK, 2 * D_KV), jnp.bfloat16),
            pltpu.VMEM((tq + BLOCK, D_POOL), jnp.bfloat16),
            pltpu.VMEM((tq, D_POOL), jnp.bfloat16),
            pltpu.VMEM((tq, D_ATTN), jnp.bfloat16),
            pltpu.VMEM((tq, D_MODEL), jnp.float32),
        ] + _ffn_scratch(tq),
        compiler_params=pltpu.CompilerParams(
            dimension_semantics=("arbitrary", "arbitrary"),
            vmem_limit_bytes=_vmem_limit(est + (8 << 20))),
        name="mix_ffn",
    )(sinks, x3d, q, k, k, v, v, u, u, bias, _window_sum_matrices(),
      _block_diagonal(pool_w.astype(jnp.bfloat16)), pool_scale.reshape(1, D_POOL),
      w_out.astype(jnp.bfloat16), ffn_gain.reshape(1, D_MODEL), w_gate, w_up, w_down)


def kernel(x, ffn1_norm, ffn1_w_gate, ffn1_w_up, ffn1_w_down, mix_norm, w_in, q_norm, k_norm,
           attn_sinks, rel_bias, pool_w, pool_scale, w_out, ffn2_norm, ffn2_w_gate, ffn2_w_up,
           ffn2_w_down):
    b, s, d = x.shape
    assert (d, s % TILE_M) == (D_MODEL, 0)
    bias = _band_bias(rel_bias)
    for l in range(ffn1_norm.shape[0]):
        x1, q, k, v, u = _ffn_proj(x.reshape(b * s, d), ffn1_norm[l], ffn1_w_gate[l], ffn1_w_up[l],
                                   ffn1_w_down[l], mix_norm[l], w_in[l], q_norm[l], k_norm[l])
        x = _mix_ffn(x1.reshape(b, s, d), q.reshape(b, s, D_ATTN), k.reshape(b, s, D_KV),
                     v.reshape(b, s, D_KV), u.reshape(b, s, D_POOL), attn_sinks[l], bias,
                     pool_w[l], pool_scale[l], w_out[l], ffn2_norm[l], ffn2_w_gate[l],
                     ffn2_w_up[l], ffn2_w_down[l])
    return x
```

```python
import functools

import numpy as np
import jax
import jax.numpy as jnp
from jax import lax
from jax.experimental import pallas as pl
from jax.experimental.pallas import tpu as pltpu

D_MODEL = 1024
HEAD_DIM = 64
N_HEADS = 8
N_KV_HEADS = 2
D_ATTN = N_HEADS * HEAD_DIM
D_KV = N_KV_HEADS * HEAD_DIM
D_POOL = D_MODEL - D_ATTN
POOL_WINDOWS = (2, 4, 8, 16)
POOL_GROUP_DIM = D_POOL // len(POOL_WINDOWS)
D_IN = D_ATTN + 2 * D_KV + D_POOL
WINDOW = 128
BLOCK = 128
N_BUCKETS = 32
MAX_DISTANCE = 128
D_FF = 2816
EPS = 1e-6
NEG = -1e30

V7X_LANES = 128
V7X_MXU_DIM = 256
V7X_VMEM_BYTES = 64 * 1024 * 1024

HEADS_PLAIN = (0, 2, 5, 7)
HEADS_ROLLED = (1, 3, 4, 6)

TILE_M = 512
FFN_CHUNK_F = 256
FFN_STAGE_SLOTS = 2


def _vmem_limit(estimate_bytes):
    return int(min(V7X_VMEM_BYTES - (4 << 20), max(estimate_bytes, 16 << 20)))


def _rmsnorm_rows(x32, gain_row):
    ms = jnp.mean(x32 * x32, axis=-1, keepdims=True)
    return x32 * lax.rsqrt(ms + EPS) * gain_row


N_FFN_CHUNKS = D_FF // FFN_CHUNK_F
N_FFN_BASE_SCRATCH = 5


def _ffn_base_scratch(tm):
    return [
        pltpu.VMEM((D_MODEL, D_FF), jnp.bfloat16),
        pltpu.VMEM((D_MODEL, D_FF), jnp.bfloat16),
        pltpu.VMEM((D_FF, D_MODEL), jnp.bfloat16),
        pltpu.VMEM((tm, D_MODEL), jnp.bfloat16),
        pltpu.VMEM((tm, D_FF), jnp.bfloat16),
    ]


def _ffn_round_scratch():
    return [
        pltpu.VMEM((2 * FFN_STAGE_SLOTS, D_MODEL, FFN_CHUNK_F), jnp.float32),
        pltpu.VMEM((FFN_STAGE_SLOTS, FFN_CHUNK_F, D_MODEL), jnp.float32),
        pltpu.SemaphoreType.DMA((3 * FFN_STAGE_SLOTS,)),
    ]


def _ffn_copy_scratch():
    return [pltpu.SemaphoreType.DMA((3, N_FFN_CHUNKS))]


def _ffn_vmem_bytes(tm):
    return (2 * 3 * D_MODEL * D_FF + 3 * FFN_STAGE_SLOTS * D_MODEL * FFN_CHUNK_F * 4
            + tm * D_MODEL * 2 + tm * D_FF * 2 + 6 * tm * FFN_CHUNK_F * 4 + tm * D_MODEL * 4)


def _swiglu_tile(x32, g_ref, w_hbm, base_scratch, stream, stream_scratch):
    wg_hbm, wu_hbm, wd_hbm = w_hbm
    wg_ref, wu_ref, wd_ref, h_ref, a_ref = base_scratch

    def window(c):
        return pl.ds(c * FFN_CHUNK_F, FFN_CHUNK_F)

    def chunk_copy(which, c):
        if stream == "copy":
            (sem_ref,) = stream_scratch
            if which == 2:
                return pltpu.make_async_copy(wd_hbm.at[window(c), :], wd_ref.at[window(c), :],
                                             sem_ref.at[which, c])
            src, dst = ((wg_hbm, wg_ref), (wu_hbm, wu_ref))[which]
            return pltpu.make_async_copy(src.at[:, window(c)], dst.at[:, window(c)],
                                         sem_ref.at[which, c])
        stage_in_ref, stage_out_ref, sem_ref = stream_scratch
        slot = c % FFN_STAGE_SLOTS
        sem = sem_ref.at[which * FFN_STAGE_SLOTS + slot]
        if which == 2:
            return pltpu.make_async_copy(wd_hbm.at[window(c), :], stage_out_ref.at[slot], sem)
        src = (wg_hbm, wu_hbm)[which]
        return pltpu.make_async_copy(src.at[:, window(c)],
                                     stage_in_ref.at[which * FFN_STAGE_SLOTS + slot], sem)

    def land(which, c):
        chunk_copy(which, c).wait()
        if stream == "copy":
            return
        stage_in_ref, stage_out_ref, _ = stream_scratch
        slot = c % FFN_STAGE_SLOTS
        rows_or_cols = slice(c * FFN_CHUNK_F, (c + 1) * FFN_CHUNK_F)
        if which == 2:
            wd_ref[rows_or_cols, :] = stage_out_ref[slot].astype(jnp.bfloat16)
        else:
            dst = (wg_ref, wu_ref)[which]
            dst[:, rows_or_cols] = stage_in_ref[which * FFN_STAGE_SLOTS + slot].astype(jnp.bfloat16)
        if c + FFN_STAGE_SLOTS < N_FFN_CHUNKS:
            chunk_copy(which, c + FFN_STAGE_SLOTS).start()

    if stream is not None:
        for c in range(N_FFN_CHUNKS if stream == "copy" else FFN_STAGE_SLOTS):
            for which in range(3):
                chunk_copy(which, c).start()
    if x32 is not None:
        h_ref[...] = _rmsnorm_rows(x32, g_ref[...]).astype(jnp.bfloat16)
    for c in range(N_FFN_CHUNKS):
        cols = slice(c * FFN_CHUNK_F, (c + 1) * FFN_CHUNK_F)
        if stream is not None:
            for which in range(3 if stream == "round" else 2):
                land(which, c)
        h = h_ref[...]
        gate = jnp.dot(h, wg_ref[:, cols], preferred_element_type=jnp.float32)
        up = jnp.dot(h, wu_ref[:, cols], preferred_element_type=jnp.float32)
        act = gate * (1.0 / (1.0 + jnp.exp(-gate))) * up
        a_ref[:, cols] = act.astype(jnp.bfloat16)
    if stream == "copy":
        for c in range(N_FFN_CHUNKS):
            land(2, c)
    return jnp.dot(a_ref[...], wd_ref[...], preferred_element_type=jnp.float32)


def _ffn_proj_kernel(x_ref, g1_ref, wg_hbm, wu_hbm, wd_hbm, gm_ref, win_ref, qg_ref, kg_ref, dm_ref,
                     wg2_ref, wu2_ref, wd2_ref, wo_ref,
                     x1_ref, q_ref, k_ref, v_ref, u_ref, wg2_bf_ref, wu2_bf_ref, wd2_bf_ref, wo_bf_ref,
                     *scratch):
    tm = x_ref.shape[0]
    base_scratch, round_scratch = scratch[:N_FFN_BASE_SCRATCH], scratch[N_FFN_BASE_SCRATCH:]

    for src, dst in ((wg2_ref, wg2_bf_ref), (wu2_ref, wu2_bf_ref), (wd2_ref, wd2_bf_ref),
                     (wo_ref, wo_bf_ref)):
        dst[...] = src[...].astype(jnp.bfloat16)

    def body(stream):
        y = _swiglu_tile(x_ref[...], g1_ref, (wg_hbm, wu_hbm, wd_hbm), base_scratch, stream,
                         round_scratch)
        x1 = x_ref[...] + 0.5 * y
        x1_ref[...] = x1
        h = _rmsnorm_rows(x1, gm_ref[...]).astype(jnp.bfloat16)
        z = jnp.dot(h, win_ref[...], preferred_element_type=jnp.float32)
        n_slab = (D_ATTN + 2 * D_KV) // V7X_MXU_DIM
        sq = jnp.concatenate(
            [z[:, i * V7X_MXU_DIM:(i + 1) * V7X_MXU_DIM] for i in range(n_slab)], axis=0)
        ms = jnp.dot((sq * sq).astype(jnp.bfloat16), dm_ref[...], preferred_element_type=jnp.float32)
        q_ms = jnp.concatenate([ms[i * tm:(i + 1) * tm] for i in range(n_slab - 1)], axis=1)
        k_ms = ms[(n_slab - 1) * tm:, :D_KV]
        q = z[:, :D_ATTN]
        k = z[:, D_ATTN:D_ATTN + D_KV]
        q_ref[...] = (q * lax.rsqrt(q_ms + EPS) * qg_ref[...]).astype(jnp.bfloat16)
        k_ref[...] = (k * lax.rsqrt(k_ms + EPS) * kg_ref[...]).astype(jnp.bfloat16)
        v_ref[...] = z[:, D_ATTN + D_KV:D_ATTN + 2 * D_KV].astype(jnp.bfloat16)
        u_ref[...] = z[:, D_ATTN + 2 * D_KV:]

    first = pl.program_id(0) == 0
    pl.when(first)(functools.partial(body, "round"))
    pl.when(jnp.logical_not(first))(functools.partial(body, None))


def _head_mean_matrix(width):
    idx = np.arange(width) // HEAD_DIM
    return jnp.asarray((idx[:, None] == idx[None, :]).astype(np.float32) / HEAD_DIM, jnp.bfloat16)


def _ffn_proj(x2d, ffn_gain, w_gate, w_up, w_down, mix_gain, w_in, q_gain, k_gain,
              next_w_gate, next_w_up, next_w_down, w_out):
    m = x2d.shape[0]
    tm = TILE_M
    n_steps = m // tm
    resident = functools.partial(pl.BlockSpec, pipeline_mode=pl.Buffered(1))
    rows = lambda width: pl.BlockSpec((tm, width), lambda i: (i, 0))
    hbm = pl.BlockSpec(memory_space=pltpu.HBM)
    wr = D_MODEL // n_steps
    wdr = 2 * D_FF // n_steps
    assert wr * n_steps == D_MODEL and wdr * n_steps == 2 * D_FF and wr % 16 == 0 and wdr % 16 == 0
    w_rows = lambda width: pl.BlockSpec((wr, width), lambda i: (i, 0))
    wd_rows = pl.BlockSpec((wdr, D_MODEL), lambda i: (jnp.minimum(i, n_steps // 2 - 1), 0))
    qg = (jnp.tile(q_gain, N_HEADS) * (HEAD_DIM ** -0.5)).reshape(1, D_ATTN)
    kg = jnp.tile(k_gain, N_KV_HEADS).reshape(1, D_KV)
    est = (_ffn_vmem_bytes(tm) + 2 * D_MODEL * D_IN
           + 2 * tm * D_MODEL * 4 * 2
           + 2 * tm * (D_ATTN * 2 + 2 * D_KV * 2 + D_POOL * 4)
           + 4 * tm * D_IN * 4
           + 2 * 6 * (2 * wr * D_FF + wdr * D_MODEL + wr * D_MODEL))
    bf16 = lambda shape: jax.ShapeDtypeStruct(shape, jnp.bfloat16)
    return pl.pallas_call(
        _ffn_proj_kernel,
        out_shape=(
            jax.ShapeDtypeStruct((m, D_MODEL), jnp.float32),
            bf16((m, D_ATTN)), bf16((m, D_KV)), bf16((m, D_KV)),
            jax.ShapeDtypeStruct((m, D_POOL), jnp.float32),
            bf16((D_MODEL, D_FF)), bf16((D_MODEL, D_FF)), bf16((D_FF, D_MODEL)),
            bf16((D_MODEL, D_MODEL)),
        ),
        grid=(n_steps,),
        in_specs=[
            rows(D_MODEL),
            resident((1, D_MODEL), lambda i: (0, 0)),
            hbm, hbm, hbm,
            resident((1, D_MODEL), lambda i: (0, 0)),
            resident((D_MODEL, D_IN), lambda i: (0, 0)),
            resident((1, D_ATTN), lambda i: (0, 0)),
            resident((1, D_KV), lambda i: (0, 0)),
            resident((V7X_MXU_DIM, V7X_MXU_DIM), lambda i: (0, 0)),
            w_rows(D_FF), w_rows(D_FF), wd_rows, w_rows(D_MODEL),
        ],
        out_specs=(rows(D_MODEL), rows(D_ATTN), rows(D_KV), rows(D_KV), rows(D_POOL),
                   w_rows(D_FF), w_rows(D_FF), wd_rows, w_rows(D_MODEL)),
        scratch_shapes=_ffn_base_scratch(tm) + _ffn_round_scratch(),
        compiler_params=pltpu.CompilerParams(
            dimension_semantics=("arbitrary",),
            vmem_limit_bytes=_vmem_limit(est + (8 << 20))),
        name="ffn_proj",
    )(x2d, ffn_gain.reshape(1, D_MODEL), w_gate, w_up, w_down, mix_gain.reshape(1, D_MODEL),
      w_in.astype(jnp.bfloat16), qg, kg, _head_mean_matrix(V7X_MXU_DIM),
      next_w_gate, next_w_up, next_w_down, w_out)


def _t5_bucket(dist):
    n = np.maximum(dist, 0)
    max_exact = N_BUCKETS // 2
    large = max_exact + (np.log(np.maximum(n, 1) / max_exact)
                         / np.log(MAX_DISTANCE / max_exact)
                         * (N_BUCKETS - max_exact)).astype(np.int32)
    large = np.minimum(large, N_BUCKETS - 1)
    return np.where(n < max_exact, n, large).astype(np.int32)


def _band_bucket_table():
    ql = np.arange(BLOCK)[:, None]
    kl = np.arange(2 * BLOCK)[None, :]
    dist = ql + BLOCK - kl
    in_band = (dist >= 0) & (dist < WINDOW)
    return np.where(in_band, _t5_bucket(dist), -1).astype(np.int32)


def _bias_kernel(rel_ref, bucket_ref, o_ref):
    bucket = bucket_ref[...]
    for slot, head in enumerate(HEADS_PLAIN + HEADS_ROLLED):
        acc = jnp.full((BLOCK, 2 * BLOCK), NEG, jnp.float32)
        for b in range(N_BUCKETS):
            acc = jnp.where(bucket == b, rel_ref[b, head], acc)
        o_ref[slot] = acc


def _band_bias(rel_bias):
    return pl.pallas_call(
        _bias_kernel,
        out_shape=jax.ShapeDtypeStruct((N_HEADS, BLOCK, 2 * BLOCK), jnp.float32),
        in_specs=[
            pl.BlockSpec(memory_space=pltpu.SMEM),
            pl.BlockSpec(memory_space=pltpu.VMEM),
        ],
        out_specs=pl.BlockSpec(memory_space=pltpu.VMEM),
        name="band_bias",
    )(rel_bias, jnp.asarray(_band_bucket_table()))


def _window_sum_matrices():
    ql = np.arange(BLOCK)[:, None]
    kl = np.arange(2 * BLOCK)[None, :]
    dist = ql + BLOCK - kl
    mats = [((dist >= 0) & (dist < w)).astype(np.float32) for w in POOL_WINDOWS]
    return jnp.asarray(np.stack(mats), jnp.bfloat16)


def _block_diagonal(w):
    g, c, _ = w.shape
    on_diagonal = jnp.eye(g, dtype=bool)[:, None, :, None]
    return jnp.where(on_diagonal, w[:, :, None, :], jnp.zeros((), w.dtype)).reshape(g * c, g * c)


def _mix_tile(sink_ref, x_ref, q_ref, k_ref, kh_ref, v_ref, vh_ref, u_ref, uh_ref,
              bias_ref, wsum_ref, pw_ref, ps_ref, wo_ref, x2_ref, mix_scratch, g_ref, h_ref):
    kx_ref, kxr_ref, vx_ref, vxr_ref, ux_ref, pooled_ref, y_ref = mix_scratch
    tq = q_ref.shape[1]
    n_blocks = tq // BLOCK
    first = pl.program_id(1) == 0
    half = HEAD_DIM

    k_halo = jnp.where(first, jnp.zeros_like(kh_ref[0]), kh_ref[0])
    v_halo = jnp.where(first, jnp.zeros_like(vh_ref[0]), vh_ref[0])
    u_halo = jnp.where(first, jnp.zeros_like(uh_ref[0]), uh_ref[0])
    kx_ref[:BLOCK] = k_halo
    kx_ref[BLOCK:] = k_ref[0]
    kxr_ref[...] = pltpu.roll(kx_ref[...].astype(jnp.float32), half, 1).astype(jnp.bfloat16)
    ones = jnp.ones((tq + BLOCK, D_KV), jnp.bfloat16)
    vx_ref[:BLOCK, :D_KV] = v_halo
    vx_ref[BLOCK:, :D_KV] = v_ref[0]
    vx_ref[:, D_KV:] = ones
    vxr_ref[:, :D_KV] = pltpu.roll(vx_ref[:, :D_KV].astype(jnp.float32), half, 1).astype(jnp.bfloat16)
    vxr_ref[:, D_KV:] = ones
    ux_ref[:BLOCK] = u_halo.astype(jnp.bfloat16)
    ux_ref[BLOCK:] = u_ref[0].astype(jnp.bfloat16)

    lane = lax.broadcasted_iota(jnp.int32, (BLOCK, 2 * half), 1)
    low = lane < half
    col = lax.broadcasted_iota(jnp.int32, (1, 2 * BLOCK), 1)
    halo_penalty = jnp.where(first & (col < BLOCK), NEG, 0.0).astype(jnp.float32)
    row = lax.broadcasted_iota(jnp.int32, (BLOCK, 1), 0)
    groups = ((HEADS_PLAIN, kx_ref, vx_ref, 0), (HEADS_ROLLED, kxr_ref, vxr_ref, len(HEADS_PLAIN)))

    def rows_of(j):
        return slice(j * BLOCK, (j + 1) * BLOCK)

    def keys_of(j):
        return slice(j * BLOCK, (j + 2) * BLOCK)

    def scores(j, grp):
        heads, k_src, _, _ = groups[grp]
        q = q_ref[0, rows_of(j), :]
        zero = jnp.zeros((BLOCK, 2 * half), q.dtype)
        pieces = []
        for hd in heads:
            pair = q[:, (hd // 2) * 2 * half:(hd // 2 + 1) * 2 * half]
            pieces.append(jnp.where(low if hd % 2 == 0 else ~low, pair, zero))
        qs = jnp.concatenate(pieces, axis=0)
        return lax.dot_general(qs, k_src[keys_of(j), :], (((1,), (1,)), ((), ())),
                               preferred_element_type=jnp.float32)

    def softmax_numerators(j, grp, logits):
        heads, _, _, slot0 = groups[grp]
        es, sink_terms = [], []
        for s, hd in enumerate(heads):
            lg = logits[s * BLOCK:(s + 1) * BLOCK] + bias_ref[slot0 + s]
            if j == 0:
                lg = lg + halo_penalty
            sink = sink_ref[hd]
            m = jnp.maximum(jnp.max(lg, axis=-1, keepdims=True), sink)
            es.append(jnp.exp(lg - m).astype(jnp.bfloat16))
            sink_terms.append(jnp.exp(sink - m))
        return jnp.concatenate(es, axis=0), sink_terms

    def attend(j, grp, e, sink_terms, outs):
        heads, _, v_src, _ = groups[grp]
        pv = jnp.dot(e, v_src[keys_of(j), :], preferred_element_type=jnp.float32)
        for s, hd in enumerate(heads):
            blk = pv[s * BLOCK:(s + 1) * BLOCK]
            outs[hd] = blk[:, :D_KV] / (blk[:, D_KV:] + sink_terms[s])

    def store_attention(j, outs):
        for p in range(N_HEADS // 2):
            y_ref[rows_of(j), p * 2 * half:(p + 1) * 2 * half] = jnp.where(
                low, outs[2 * p], outs[2 * p + 1]).astype(jnp.bfloat16)

    def pool_means(j):
        t_glob = pl.program_id(1) * tq + j * BLOCK + row
        for g, w in enumerate(POOL_WINDOWS):
            gc = slice(g * POOL_GROUP_DIM, (g + 1) * POOL_GROUP_DIM)
            wsum = jnp.dot(wsum_ref[g], ux_ref[keys_of(j), gc], preferred_element_type=jnp.float32)
            cnt = jnp.minimum(t_glob + 1, w).astype(jnp.float32)
            pooled = wsum * (1.0 / cnt) - u_ref[0, rows_of(j), gc]
            pooled_ref[rows_of(j), gc] = pooled.astype(jnp.bfloat16)

    def project_attention(j):
        x2 = x2_ref[rows_of(j), :] + jnp.dot(
            y_ref[rows_of(j), :], wo_ref[:D_ATTN, :], preferred_element_type=jnp.float32)
        x2_ref[rows_of(j), :] = x2
        h_ref[rows_of(j), :] = _rmsnorm_rows(x2, g_ref[...]).astype(jnp.bfloat16)

    for j in range(n_blocks):
        pool_means(j)
    logits = scores(0, 0)
    for j in range(n_blocks):
        outs = {}
        logits_b = scores(j, 1)
        e, sink_terms = softmax_numerators(j, 0, logits)
        if j == 0:
            y_pool = jnp.dot(pooled_ref[...], pw_ref[...],
                             preferred_element_type=jnp.float32) * ps_ref[...]
        else:
            project_attention(j - 1)
        attend(j, 0, e, sink_terms, outs)
        e, sink_terms = softmax_numerators(j, 1, logits_b)
        if j == 0:
            x2_ref[...] = x_ref[0] + jnp.dot(y_pool.astype(jnp.bfloat16), wo_ref[D_ATTN:, :],
                                             preferred_element_type=jnp.float32)
        if j + 1 < n_blocks:
            logits = scores(j + 1, 0)
        attend(j, 1, e, sink_terms, outs)
        store_attention(j, outs)
    project_attention(n_blocks - 1)


N_MIX_SCRATCH = 8


def _mix_ffn_kernel(sink_ref, x_ref, q_ref, k_ref, kh_ref, v_ref, vh_ref, u_ref, uh_ref,
                    bias_ref, wsum_ref, pw_ref, ps_ref, wo_ref, g2_ref, wg_hbm, wu_hbm, wd_hbm,
                    o_ref, *scratch):
    mix_scratch, x2_ref = scratch[:N_MIX_SCRATCH - 1], scratch[N_MIX_SCRATCH - 1]
    base_scratch = scratch[N_MIX_SCRATCH:N_MIX_SCRATCH + N_FFN_BASE_SCRATCH]
    copy_scratch = scratch[N_MIX_SCRATCH + N_FFN_BASE_SCRATCH:]
    h_ref = base_scratch[3]

    def body(stream):
        _mix_tile(sink_ref, x_ref, q_ref, k_ref, kh_ref, v_ref, vh_ref, u_ref, uh_ref,
                  bias_ref, wsum_ref, pw_ref, ps_ref, wo_ref, x2_ref, mix_scratch, g2_ref, h_ref)
        y = _swiglu_tile(None, g2_ref, (wg_hbm, wu_hbm, wd_hbm), base_scratch, stream, copy_scratch)
        o_ref[0] = x2_ref[...] + 0.5 * y

    first = jnp.logical_and(pl.program_id(0) == 0, pl.program_id(1) == 0)
    pl.when(first)(functools.partial(body, "copy"))
    pl.when(jnp.logical_not(first))(functools.partial(body, None))


def _mix_ffn(x3d, q, k, v, u, sinks, bias, pool_w, pool_scale, w_out_bf16, ffn_gain,
             w_gate_bf16, w_up_bf16, w_down_bf16):
    b, s, _ = x3d.shape
    tq = TILE_M
    nblk = tq // BLOCK
    resident = functools.partial(pl.BlockSpec, pipeline_mode=pl.Buffered(1))
    hbm = pl.BlockSpec(memory_space=pltpu.HBM)
    tile = lambda width: pl.BlockSpec((1, tq, width), lambda bi, i: (bi, i, 0))
    halo = lambda width: pl.BlockSpec(
        (1, BLOCK, width), lambda bi, i: (bi, jnp.maximum(i * nblk - 1, 0), 0))
    est = (_ffn_vmem_bytes(tq)
           + 2 * 2 * tq * D_MODEL * 4 + 2 * tq * (D_ATTN + 2 * D_KV) * 2 + 2 * tq * D_POOL * 4
           + 2 * BLOCK * (2 * D_KV * 2 + D_POOL * 4)
           + N_HEADS * BLOCK * 2 * BLOCK * 4 + 2 * D_MODEL * D_MODEL + 2 * D_POOL * D_POOL
           + 6 * (tq + BLOCK) * D_KV * 2 + (tq + BLOCK) * D_POOL * 2 + tq * D_MODEL * 2
           + tq * D_MODEL * 4 + 16 * BLOCK * 2 * BLOCK * 4 * 4)
    return pl.pallas_call(
        _mix_ffn_kernel,
        out_shape=jax.ShapeDtypeStruct((b, s, D_MODEL), jnp.float32),
        grid=(b, s // tq),
        in_specs=[
            pl.BlockSpec(memory_space=pltpu.SMEM),
            tile(D_MODEL),
            tile(D_ATTN),
            tile(D_KV), halo(D_KV),
            tile(D_KV), halo(D_KV),
            tile(D_POOL), halo(D_POOL),
            resident((N_HEADS, BLOCK, 2 * BLOCK), lambda bi, i: (0, 0, 0)),
            resident((len(POOL_WINDOWS), BLOCK, 2 * BLOCK), lambda bi, i: (0, 0, 0)),
            resident((D_POOL, D_POOL), lambda bi, i: (0, 0)),
            resident((1, D_POOL), lambda bi, i: (0, 0)),
            resident((D_MODEL, D_MODEL), lambda bi, i: (0, 0)),
            resident((1, D_MODEL), lambda bi, i: (0, 0)),
            hbm, hbm, hbm,
        ],
        out_specs=tile(D_MODEL),
        scratch_shapes=[
            pltpu.VMEM((tq + BLOCK, D_KV), jnp.bfloat16),
            pltpu.VMEM((tq + BLOCK, D_KV), jnp.bfloat16),
            pltpu.VMEM((tq + BLOCK, 2 * D_KV), jnp.bfloat16),
            pltpu.VMEM((tq + BLOCK, 2 * D_KV), jnp.bfloat16),
            pltpu.VMEM((tq + BLOCK, D_POOL), jnp.bfloat16),
            pltpu.VMEM((tq, D_POOL), jnp.bfloat16),
            pltpu.VMEM((tq, D_ATTN), jnp.bfloat16),
            pltpu.VMEM((tq, D_MODEL), jnp.float32),
        ] + _ffn_base_scratch(tq) + _ffn_copy_scratch(),
        compiler_params=pltpu.CompilerParams(
            dimension_semantics=("arbitrary", "arbitrary"),
            vmem_limit_bytes=_vmem_limit(est + (8 << 20))),
        name="mix_ffn",
    )(sinks, x3d, q, k, k, v, v, u, u, bias, _window_sum_matrices(),
      _block_diagonal(pool_w.astype(jnp.bfloat16)), pool_scale.reshape(1, D_POOL),
      w_out_bf16, ffn_gain.reshape(1, D_MODEL), w_gate_bf16, w_up_bf16, w_down_bf16)


def kernel(x, ffn1_norm, ffn1_w_gate, ffn1_w_up, ffn1_w_down, mix_norm, w_in, q_norm, k_norm,
           attn_sinks, rel_bias, pool_w, pool_scale, w_out, ffn2_norm, ffn2_w_gate, ffn2_w_up,
           ffn2_w_down):
    b, s, d = x.shape
    assert (d, s % TILE_M) == (D_MODEL, 0)
    bias = _band_bias(rel_bias)
    for l in range(ffn1_norm.shape[0]):
        x1, q, k, v, u, wg2, wu2, wd2, wo = _ffn_proj(
            x.reshape(b * s, d), ffn1_norm[l], ffn1_w_gate[l], ffn1_w_up[l], ffn1_w_down[l],
            mix_norm[l], w_in[l], q_norm[l], k_norm[l],
            ffn2_w_gate[l], ffn2_w_up[l], ffn2_w_down[l], w_out[l])
        x = _mix_ffn(x1.reshape(b, s, d), q.reshape(b, s, D_ATTN), k.reshape(b, s, D_KV),
                     v.reshape(b, s, D_KV), u.reshape(b, s, D_POOL), attn_sinks[l], bias,
                     pool_w[l], pool_scale[l], wo, ffn2_norm[l], wg2, wu2, wd2)
    return x
```

```python
import functools

import numpy as np
import jax
import jax.numpy as jnp
from jax import lax
from jax.experimental import pallas as pl
from jax.experimental.pallas import tpu as pltpu

D_MODEL = 1024
HEAD_DIM = 64
N_HEADS = 8
N_KV_HEADS = 2
D_ATTN = N_HEADS * HEAD_DIM
D_KV = N_KV_HEADS * HEAD_DIM
D_POOL = D_MODEL - D_ATTN
POOL_WINDOWS = (2, 4, 8, 16)
POOL_GROUP_DIM = D_POOL // len(POOL_WINDOWS)
D_IN = D_ATTN + 2 * D_KV + D_POOL
WINDOW = 128
BLOCK = 128
N_BUCKETS = 32
MAX_DISTANCE = 128
D_FF = 2816
EPS = 1e-6
NEG = -1e30

V7X_LANES = 128
V7X_MXU_DIM = 256
V7X_VMEM_BYTES = 64 * 1024 * 1024

HEADS_PLAIN = (0, 2, 5, 7)
HEADS_ROLLED = (1, 3, 4, 6)

TILE_M = 512
FFN_CHUNK_F = 256
FFN_STAGE_SLOTS = 2


def _vmem_limit(estimate_bytes):
    return int(min(V7X_VMEM_BYTES - (4 << 20), max(estimate_bytes, 16 << 20)))


def _rmsnorm_rows(x32, gain_row):
    ms = jnp.mean(x32 * x32, axis=-1, keepdims=True)
    return x32 * lax.rsqrt(ms + EPS) * gain_row


N_FFN_CHUNKS = D_FF // FFN_CHUNK_F
N_FFN_BASE_SCRATCH = 5


def _ffn_base_scratch(tm):
    return [
        pltpu.VMEM((D_MODEL, D_FF), jnp.bfloat16),
        pltpu.VMEM((D_MODEL, D_FF), jnp.bfloat16),
        pltpu.VMEM((D_FF, D_MODEL), jnp.bfloat16),
        pltpu.VMEM((tm, D_MODEL), jnp.bfloat16),
        pltpu.VMEM((tm, D_FF), jnp.bfloat16),
    ]


def _ffn_round_scratch():
    return [
        pltpu.VMEM((2 * FFN_STAGE_SLOTS, D_MODEL, FFN_CHUNK_F), jnp.float32),
        pltpu.VMEM((FFN_STAGE_SLOTS, FFN_CHUNK_F, D_MODEL), jnp.float32),
        pltpu.SemaphoreType.DMA((3 * FFN_STAGE_SLOTS,)),
    ]


def _ffn_copy_scratch():
    return [pltpu.SemaphoreType.DMA((3, N_FFN_CHUNKS))]


def _ffn_vmem_bytes(tm):
    return (2 * 3 * D_MODEL * D_FF + 3 * FFN_STAGE_SLOTS * D_MODEL * FFN_CHUNK_F * 4
            + tm * D_MODEL * 2 + tm * D_FF * 2 + 6 * tm * FFN_CHUNK_F * 4 + tm * D_MODEL * 4)


def _ffn_window(c):
    return pl.ds(c * FFN_CHUNK_F, FFN_CHUNK_F)


def _ffn_weight_copy(which, c, w_hbm, base_scratch, copy_scratch):
    (sem_ref,) = copy_scratch
    src, dst = w_hbm[which], base_scratch[which]
    if which == 2:
        return pltpu.make_async_copy(src.at[_ffn_window(c), :], dst.at[_ffn_window(c), :],
                                     sem_ref.at[which, c])
    return pltpu.make_async_copy(src.at[:, _ffn_window(c)], dst.at[:, _ffn_window(c)],
                                 sem_ref.at[which, c])


def _ffn_start_weight_copies(w_hbm, base_scratch, copy_scratch):
    for c in range(N_FFN_CHUNKS):
        for which in range(3):
            _ffn_weight_copy(which, c, w_hbm, base_scratch, copy_scratch).start()


def _swiglu_tile(x32, g_ref, w_hbm, base_scratch, stream, stream_scratch, between=(), before_down=()):
    wg_hbm, wu_hbm, wd_hbm = w_hbm
    wg_ref, wu_ref, wd_ref, h_ref, a_ref = base_scratch

    def staged_copy(which, c):
        stage_in_ref, stage_out_ref, sem_ref = stream_scratch
        slot = c % FFN_STAGE_SLOTS
        sem = sem_ref.at[which * FFN_STAGE_SLOTS + slot]
        if which == 2:
            return pltpu.make_async_copy(wd_hbm.at[_ffn_window(c), :], stage_out_ref.at[slot], sem)
        src = (wg_hbm, wu_hbm)[which]
        return pltpu.make_async_copy(src.at[:, _ffn_window(c)],
                                     stage_in_ref.at[which * FFN_STAGE_SLOTS + slot], sem)

    def land(which, c):
        if stream == "wait":
            _ffn_weight_copy(which, c, w_hbm, base_scratch, stream_scratch).wait()
            return
        staged_copy(which, c).wait()
        stage_in_ref, stage_out_ref, _ = stream_scratch
        slot = c % FFN_STAGE_SLOTS
        rows_or_cols = slice(c * FFN_CHUNK_F, (c + 1) * FFN_CHUNK_F)
        if which == 2:
            wd_ref[rows_or_cols, :] = stage_out_ref[slot].astype(jnp.bfloat16)
        else:
            dst = (wg_ref, wu_ref)[which]
            dst[:, rows_or_cols] = stage_in_ref[which * FFN_STAGE_SLOTS + slot].astype(jnp.bfloat16)
        if c + FFN_STAGE_SLOTS < N_FFN_CHUNKS:
            staged_copy(which, c + FFN_STAGE_SLOTS).start()

    if stream == "round":
        for c in range(FFN_STAGE_SLOTS):
            for which in range(3):
                staged_copy(which, c).start()
    if x32 is not None:
        h_ref[...] = _rmsnorm_rows(x32, g_ref[...]).astype(jnp.bfloat16)
    for c in range(N_FFN_CHUNKS):
        cols = slice(c * FFN_CHUNK_F, (c + 1) * FFN_CHUNK_F)
        if stream is not None:
            for which in range(3 if stream == "round" else 2):
                land(which, c)
        h = h_ref[...]
        gate = jnp.dot(h, wg_ref[:, cols], preferred_element_type=jnp.float32)
        up = jnp.dot(h, wu_ref[:, cols], preferred_element_type=jnp.float32)
        act = gate * (1.0 / (1.0 + jnp.exp(-gate))) * up
        a_ref[:, cols] = act.astype(jnp.bfloat16)
        if c < len(between):
            between[c]()
    for stage in tuple(between[N_FFN_CHUNKS:]) + tuple(before_down):
        stage()
    if stream == "wait":
        for c in range(N_FFN_CHUNKS):
            land(2, c)
    return jnp.dot(a_ref[...], wd_ref[...], preferred_element_type=jnp.float32)


def _ffn_proj_kernel(x_ref, g1_ref, wg_hbm, wu_hbm, wd_hbm, gm_ref, win_ref, qg_ref, kg_ref, dm_ref,
                     wg2_ref, wu2_ref, wd2_ref, wo_ref,
                     x1_ref, q_ref, k_ref, v_ref, u_ref, wg2_bf_ref, wu2_bf_ref, wd2_bf_ref, wo_bf_ref,
                     *scratch):
    tm = x_ref.shape[0]
    base_scratch, round_scratch = scratch[:N_FFN_BASE_SCRATCH], scratch[N_FFN_BASE_SCRATCH:]

    for src, dst in ((wg2_ref, wg2_bf_ref), (wu2_ref, wu2_bf_ref), (wd2_ref, wd2_bf_ref),
                     (wo_ref, wo_bf_ref)):
        dst[...] = src[...].astype(jnp.bfloat16)

    def body(stream):
        y = _swiglu_tile(x_ref[...], g1_ref, (wg_hbm, wu_hbm, wd_hbm), base_scratch, stream,
                         round_scratch)
        x1 = x_ref[...] + 0.5 * y
        x1_ref[...] = x1
        h = _rmsnorm_rows(x1, gm_ref[...]).astype(jnp.bfloat16)
        z = jnp.dot(h, win_ref[...], preferred_element_type=jnp.float32)
        n_slab = (D_ATTN + 2 * D_KV) // V7X_MXU_DIM
        sq = jnp.concatenate(
            [z[:, i * V7X_MXU_DIM:(i + 1) * V7X_MXU_DIM] for i in range(n_slab)], axis=0)
        ms = jnp.dot((sq * sq).astype(jnp.bfloat16), dm_ref[...], preferred_element_type=jnp.float32)
        q_ms = jnp.concatenate([ms[i * tm:(i + 1) * tm] for i in range(n_slab - 1)], axis=1)
        k_ms = ms[(n_slab - 1) * tm:, :D_KV]
        q = z[:, :D_ATTN]
        k = z[:, D_ATTN:D_ATTN + D_KV]
        q_ref[...] = (q * lax.rsqrt(q_ms + EPS) * qg_ref[...]).astype(jnp.bfloat16)
        k_ref[...] = (k * lax.rsqrt(k_ms + EPS) * kg_ref[...]).astype(jnp.bfloat16)
        v_ref[...] = z[:, D_ATTN + D_KV:D_ATTN + 2 * D_KV].astype(jnp.bfloat16)
        u_ref[...] = z[:, D_ATTN + 2 * D_KV:]

    first = pl.program_id(0) == 0
    pl.when(first)(functools.partial(body, "round"))
    pl.when(jnp.logical_not(first))(functools.partial(body, None))


def _head_mean_matrix(width):
    idx = np.arange(width) // HEAD_DIM
    return jnp.asarray((idx[:, None] == idx[None, :]).astype(np.float32) / HEAD_DIM, jnp.bfloat16)


def _ffn_proj(x2d, ffn_gain, w_gate, w_up, w_down, mix_gain, w_in, q_gain, k_gain,
              next_w_gate, next_w_up, next_w_down, w_out):
    m = x2d.shape[0]
    tm = TILE_M
    n_steps = m // tm
    resident = functools.partial(pl.BlockSpec, pipeline_mode=pl.Buffered(1))
    rows = lambda width: pl.BlockSpec((tm, width), lambda i: (i, 0))
    hbm = pl.BlockSpec(memory_space=pltpu.HBM)
    wr = D_MODEL // n_steps
    wdr = 2 * D_FF // n_steps
    assert wr * n_steps == D_MODEL and wdr * n_steps == 2 * D_FF and wr % 16 == 0 and wdr % 16 == 0
    w_rows = lambda width: pl.BlockSpec((wr, width), lambda i: (i, 0))
    wd_rows = pl.BlockSpec((wdr, D_MODEL), lambda i: (jnp.minimum(i, n_steps // 2 - 1), 0))
    qg = (jnp.tile(q_gain, N_HEADS) * (HEAD_DIM ** -0.5)).reshape(1, D_ATTN)
    kg = jnp.tile(k_gain, N_KV_HEADS).reshape(1, D_KV)
    est = (_ffn_vmem_bytes(tm) + 2 * D_MODEL * D_IN
           + 2 * tm * D_MODEL * 4 * 2
           + 2 * tm * (D_ATTN * 2 + 2 * D_KV * 2 + D_POOL * 4)
           + 4 * tm * D_IN * 4
           + 2 * 6 * (2 * wr * D_FF + wdr * D_MODEL + wr * D_MODEL))
    bf16 = lambda shape: jax.ShapeDtypeStruct(shape, jnp.bfloat16)
    return pl.pallas_call(
        _ffn_proj_kernel,
        out_shape=(
            jax.ShapeDtypeStruct((m, D_MODEL), jnp.float32),
            bf16((m, D_ATTN)), bf16((m, D_KV)), bf16((m, D_KV)),
            jax.ShapeDtypeStruct((m, D_POOL), jnp.float32),
            bf16((D_MODEL, D_FF)), bf16((D_MODEL, D_FF)), bf16((D_FF, D_MODEL)),
            bf16((D_MODEL, D_MODEL)),
        ),
        grid=(n_steps,),
        in_specs=[
            rows(D_MODEL),
            resident((1, D_MODEL), lambda i: (0, 0)),
            hbm, hbm, hbm,
            resident((1, D_MODEL), lambda i: (0, 0)),
            resident((D_MODEL, D_IN), lambda i: (0, 0)),
            resident((1, D_ATTN), lambda i: (0, 0)),
            resident((1, D_KV), lambda i: (0, 0)),
            resident((V7X_MXU_DIM, V7X_MXU_DIM), lambda i: (0, 0)),
            w_rows(D_FF), w_rows(D_FF), wd_rows, w_rows(D_MODEL),
        ],
        out_specs=(rows(D_MODEL), rows(D_ATTN), rows(D_KV), rows(D_KV), rows(D_POOL),
                   w_rows(D_FF), w_rows(D_FF), wd_rows, w_rows(D_MODEL)),
        scratch_shapes=_ffn_base_scratch(tm) + _ffn_round_scratch(),
        compiler_params=pltpu.CompilerParams(
            dimension_semantics=("arbitrary",),
            vmem_limit_bytes=_vmem_limit(est + (8 << 20))),
        name="ffn_proj",
    )(x2d, ffn_gain.reshape(1, D_MODEL), w_gate, w_up, w_down, mix_gain.reshape(1, D_MODEL),
      w_in.astype(jnp.bfloat16), qg, kg, _head_mean_matrix(V7X_MXU_DIM),
      next_w_gate, next_w_up, next_w_down, w_out)


def _t5_bucket(dist):
    n = np.maximum(dist, 0)
    max_exact = N_BUCKETS // 2
    large = max_exact + (np.log(np.maximum(n, 1) / max_exact)
                         / np.log(MAX_DISTANCE / max_exact)
                         * (N_BUCKETS - max_exact)).astype(np.int32)
    large = np.minimum(large, N_BUCKETS - 1)
    return np.where(n < max_exact, n, large).astype(np.int32)


def _band_bucket_table():
    ql = np.arange(BLOCK)[:, None]
    kl = np.arange(2 * BLOCK)[None, :]
    dist = ql + BLOCK - kl
    in_band = (dist >= 0) & (dist < WINDOW)
    return np.where(in_band, _t5_bucket(dist), -1).astype(np.int32)


def _bias_kernel(rel_ref, bucket_ref, o_ref):
    bucket = bucket_ref[...]
    for slot, head in enumerate(HEADS_PLAIN + HEADS_ROLLED):
        acc = jnp.full((BLOCK, 2 * BLOCK), NEG, jnp.float32)
        for b in range(N_BUCKETS):
            acc = jnp.where(bucket == b, rel_ref[b, head], acc)
        o_ref[slot] = acc


def _band_bias(rel_bias):
    return pl.pallas_call(
        _bias_kernel,
        out_shape=jax.ShapeDtypeStruct((N_HEADS, BLOCK, 2 * BLOCK), jnp.float32),
        in_specs=[
            pl.BlockSpec(memory_space=pltpu.SMEM),
            pl.BlockSpec(memory_space=pltpu.VMEM),
        ],
        out_specs=pl.BlockSpec(memory_space=pltpu.VMEM),
        name="band_bias",
    )(rel_bias, jnp.asarray(_band_bucket_table()))


def _window_sum_matrices():
    ql = np.arange(BLOCK)[:, None]
    kl = np.arange(2 * BLOCK)[None, :]
    dist = ql + BLOCK - kl
    mats = [((dist >= 0) & (dist < w)).astype(np.float32) for w in POOL_WINDOWS]
    return jnp.asarray(np.stack(mats), jnp.bfloat16)


def _block_diagonal(w):
    g, c, _ = w.shape
    on_diagonal = jnp.eye(g, dtype=bool)[:, None, :, None]
    return jnp.where(on_diagonal, w[:, :, None, :], jnp.zeros((), w.dtype)).reshape(g * c, g * c)


def _mix_stages(seq_first, t_base, sink_ref, x_ref, q_ref, k_ref, kh_ref, v_ref, vh_ref, u_ref, uh_ref,
                bias_ref, wsum_ref, pw_ref, ps_ref, wo_ref, x2_ref, mix_scratch, g_ref, h_ref):
    kx_ref, kxr_ref, vx_ref, vxr_ref, ux_ref, pooled_ref, y_ref = mix_scratch
    tq = q_ref.shape[1]
    n_blocks = tq // BLOCK
    half = HEAD_DIM
    groups = ((HEADS_PLAIN, kx_ref, vx_ref, 0), (HEADS_ROLLED, kxr_ref, vxr_ref, len(HEADS_PLAIN)))
    lane = lax.broadcasted_iota(jnp.int32, (BLOCK, 2 * half), 1)
    low = lane < half
    state = {}

    def rows_of(j):
        return slice(j * BLOCK, (j + 1) * BLOCK)

    def keys_of(j):
        return slice(j * BLOCK, (j + 2) * BLOCK)

    def build_slabs():
        k_halo = jnp.where(seq_first, jnp.zeros_like(kh_ref[0]), kh_ref[0])
        v_halo = jnp.where(seq_first, jnp.zeros_like(vh_ref[0]), vh_ref[0])
        u_halo = jnp.where(seq_first, jnp.zeros_like(uh_ref[0]), uh_ref[0])
        kx_ref[:BLOCK] = k_halo
        kx_ref[BLOCK:] = k_ref[0]
        kxr_ref[...] = pltpu.roll(kx_ref[...].astype(jnp.float32), half, 1).astype(jnp.bfloat16)
        ones = jnp.ones((tq + BLOCK, D_KV), jnp.bfloat16)
        vx_ref[:BLOCK, :D_KV] = v_halo
        vx_ref[BLOCK:, :D_KV] = v_ref[0]
        vx_ref[:, D_KV:] = ones
        vxr_ref[:, :D_KV] = pltpu.roll(
            vx_ref[:, :D_KV].astype(jnp.float32), half, 1).astype(jnp.bfloat16)
        vxr_ref[:, D_KV:] = ones
        ux_ref[:BLOCK] = u_halo.astype(jnp.bfloat16)
        ux_ref[BLOCK:] = u_ref[0].astype(jnp.bfloat16)

    def scores(j, grp):
        heads, k_src, _, _ = groups[grp]
        q = q_ref[0, rows_of(j), :]
        zero = jnp.zeros((BLOCK, 2 * half), q.dtype)
        pieces = []
        for hd in heads:
            pair = q[:, (hd // 2) * 2 * half:(hd // 2 + 1) * 2 * half]
            pieces.append(jnp.where(low if hd % 2 == 0 else ~low, pair, zero))
        qs = jnp.concatenate(pieces, axis=0)
        state["logits", j, grp] = lax.dot_general(
            qs, k_src[keys_of(j), :], (((1,), (1,)), ((), ())),
            preferred_element_type=jnp.float32)

    def softmax_numerators(j, grp):
        heads, _, _, slot0 = groups[grp]
        logits = state.pop(("logits", j, grp))
        es, sink_terms = [], []
        for s, hd in enumerate(heads):
            lg = logits[s * BLOCK:(s + 1) * BLOCK] + bias_ref[slot0 + s]
            if j == 0:
                col = lax.broadcasted_iota(jnp.int32, (1, 2 * BLOCK), 1)
                lg = lg + jnp.where(seq_first & (col < BLOCK), NEG, 0.0).astype(jnp.float32)
            sink = sink_ref[hd]
            m = jnp.maximum(jnp.max(lg, axis=-1, keepdims=True), sink)
            es.append(jnp.exp(lg - m).astype(jnp.bfloat16))
            sink_terms.append(jnp.exp(sink - m))
        state["e", j, grp] = (jnp.concatenate(es, axis=0), sink_terms)

    def attend(j, grp):
        heads, _, v_src, _ = groups[grp]
        e, sink_terms = state.pop(("e", j, grp))
        pv = jnp.dot(e, v_src[keys_of(j), :], preferred_element_type=jnp.float32)
        for s, hd in enumerate(heads):
            blk = pv[s * BLOCK:(s + 1) * BLOCK]
            state["out", j, hd] = blk[:, :D_KV] / (blk[:, D_KV:] + sink_terms[s])

    def store_attention(j):
        for p in range(N_HEADS // 2):
            y_ref[rows_of(j), p * 2 * half:(p + 1) * 2 * half] = jnp.where(
                low, state.pop(("out", j, 2 * p)), state.pop(("out", j, 2 * p + 1))
            ).astype(jnp.bfloat16)

    def pool_means(j):
        row = lax.broadcasted_iota(jnp.int32, (BLOCK, 1), 0)
        t_glob = t_base + j * BLOCK + row
        for g, w in enumerate(POOL_WINDOWS):
            gc = slice(g * POOL_GROUP_DIM, (g + 1) * POOL_GROUP_DIM)
            wsum = jnp.dot(wsum_ref[g], ux_ref[keys_of(j), gc], preferred_element_type=jnp.float32)
            cnt = jnp.minimum(t_glob + 1, w).astype(jnp.float32)
            pooled = wsum * (1.0 / cnt) - u_ref[0, rows_of(j), gc]
            pooled_ref[rows_of(j), gc] = pooled.astype(jnp.bfloat16)

    def pool_map():
        state["y_pool"] = jnp.dot(pooled_ref[...], pw_ref[...],
                                  preferred_element_type=jnp.float32) * ps_ref[...]

    def project_pool():
        x2_ref[...] = x_ref[0] + jnp.dot(state.pop("y_pool").astype(jnp.bfloat16), wo_ref[D_ATTN:, :],
                                         preferred_element_type=jnp.float32)

    def project_attention(j):
        x2 = x2_ref[rows_of(j), :] + jnp.dot(
            y_ref[rows_of(j), :], wo_ref[:D_ATTN, :], preferred_element_type=jnp.float32)
        x2_ref[rows_of(j), :] = x2
        h_ref[rows_of(j), :] = _rmsnorm_rows(x2, g_ref[...]).astype(jnp.bfloat16)

    def seq(*calls):
        def run():
            for fn, *args in calls:
                fn(*args)
        return run

    early = [
        seq((build_slabs,)),
        seq(*[(pool_means, j) for j in range(n_blocks // 2)]),
        seq(*[(pool_means, j) for j in range(n_blocks // 2, n_blocks)]),
        seq((scores, 0, 0), (scores, 0, 1), (softmax_numerators, 0, 0)),
        seq((pool_map,), (attend, 0, 0), (softmax_numerators, 0, 1)),
        seq((project_pool,), (scores, 1, 0), (attend, 0, 1), (store_attention, 0)),
    ]
    for j in range(1, n_blocks):
        early.append(seq((scores, j, 1), (softmax_numerators, j, 0), (attend, j, 0)))
        tail = [(softmax_numerators, j, 1)]
        if j + 1 < n_blocks:
            tail.append((scores, j + 1, 0))
        tail += [(attend, j, 1), (store_attention, j)]
        early.append(seq(*tail))
    late = [seq((project_attention, j)) for j in range(n_blocks)]
    return early, late


N_MIX_SCRATCH = 8


def _mix_ffn_kernel(sink_ref, x_ref, q_ref, k_ref, kh_ref, v_ref, vh_ref, u_ref, uh_ref,
                    bias_ref, wsum_ref, pw_ref, ps_ref, wo_ref, g2_ref, wg_hbm, wu_hbm, wd_hbm,
                    o_ref, *scratch, tiles_per_seq):
    mix_scratch, x2_ref = scratch[:N_MIX_SCRATCH - 1], scratch[N_MIX_SCRATCH - 1]
    base_scratch = scratch[N_MIX_SCRATCH:N_MIX_SCRATCH + N_FFN_BASE_SCRATCH]
    copy_scratch = scratch[N_MIX_SCRATCH + N_FFN_BASE_SCRATCH:]
    h_ref = base_scratch[3]
    t = pl.program_id(0)
    tq = q_ref.shape[1]
    t_mix = jnp.minimum(t, pl.num_programs(0) - 2)
    i_seq = lax.rem(t_mix, tiles_per_seq)
    stages = functools.partial(
        _mix_stages, i_seq == 0, i_seq * tq, sink_ref, x_ref, q_ref, k_ref, kh_ref, v_ref, vh_ref,
        u_ref, uh_ref, bias_ref, wsum_ref, pw_ref, ps_ref, wo_ref, x2_ref, mix_scratch, g2_ref, h_ref)
    w_hbm = (wg_hbm, wu_hbm, wd_hbm)

    def first_step():
        _ffn_start_weight_copies(w_hbm, base_scratch, copy_scratch)
        early, late = stages()
        for stage in early + late:
            stage()

    def later_step(wait_weights):
        early, late = stages()
        o_ref[0] = x2_ref[...]
        y = _swiglu_tile(None, g2_ref, w_hbm, base_scratch, "wait" if wait_weights else None,
                         copy_scratch, between=early, before_down=late)
        o_ref[0] += 0.5 * y

    pl.when(t == 0)(first_step)
    pl.when(t == 1)(functools.partial(later_step, True))
    pl.when(t > 1)(functools.partial(later_step, False))


def _mix_ffn(x3d, q, k, v, u, sinks, bias, pool_w, pool_scale, w_out_bf16, ffn_gain,
             w_gate_bf16, w_up_bf16, w_down_bf16):
    b, s, _ = x3d.shape
    tq = TILE_M
    nblk = tq // BLOCK
    tiles_per_seq = s // tq
    n_tiles = b * tiles_per_seq
    resident = functools.partial(pl.BlockSpec, pipeline_mode=pl.Buffered(1))
    hbm = pl.BlockSpec(memory_space=pltpu.HBM)

    def mixed(t):
        tm = jnp.minimum(t, n_tiles - 1)
        return tm // tiles_per_seq, lax.rem(tm, tiles_per_seq)

    def tile(width):
        return pl.BlockSpec((1, tq, width), lambda t: (*mixed(t), 0))

    def halo(width):
        def index(t):
            bi, i = mixed(t)
            return bi, jnp.maximum(i * nblk - 1, 0), 0
        return pl.BlockSpec((1, BLOCK, width), index)

    def out_index(t):
        tf = jnp.maximum(t - 1, 0)
        return tf // tiles_per_seq, lax.rem(tf, tiles_per_seq), 0

    est = (_ffn_vmem_bytes(tq)
           + 2 * 2 * tq * D_MODEL * 4 + 2 * tq * (D_ATTN + 2 * D_KV) * 2 + 2 * tq * D_POOL * 4
           + 2 * BLOCK * (2 * D_KV * 2 + D_POOL * 4)
           + N_HEADS * BLOCK * 2 * BLOCK * 4 + 2 * D_MODEL * D_MODEL + 2 * D_POOL * D_POOL
           + 6 * (tq + BLOCK) * D_KV * 2 + (tq + BLOCK) * D_POOL * 2 + tq * D_MODEL * 2
           + tq * D_MODEL * 4 + 16 * BLOCK * 2 * BLOCK * 4 * 4)
    return pl.pallas_call(
        functools.partial(_mix_ffn_kernel, tiles_per_seq=tiles_per_seq),
        out_shape=jax.ShapeDtypeStruct((b, s, D_MODEL), jnp.float32),
        grid=(n_tiles + 1,),
        in_specs=[
            pl.BlockSpec(memory_space=pltpu.SMEM),
            tile(D_MODEL),
            tile(D_ATTN),
            tile(D_KV), halo(D_KV),
            tile(D_KV), halo(D_KV),
            tile(D_POOL), halo(D_POOL),
            resident((N_HEADS, BLOCK, 2 * BLOCK), lambda t: (0, 0, 0)),
            resident((len(POOL_WINDOWS), BLOCK, 2 * BLOCK), lambda t: (0, 0, 0)),
            resident((D_POOL, D_POOL), lambda t: (0, 0)),
            resident((1, D_POOL), lambda t: (0, 0)),
            resident((D_MODEL, D_MODEL), lambda t: (0, 0)),
            resident((1, D_MODEL), lambda t: (0, 0)),
            hbm, hbm, hbm,
        ],
        out_specs=pl.BlockSpec((1, tq, D_MODEL), out_index),
        scratch_shapes=[
            pltpu.VMEM((tq + BLOCK, D_KV), jnp.bfloat16),
            pltpu.VMEM((tq + BLOCK, D_KV), jnp.bfloat16),
            pltpu.VMEM((tq + BLOCK, 2 * D_KV), jnp.bfloat16),
            pltpu.VMEM((tq + BLOCK, 2 * D_KV), jnp.bfloat16),
            pltpu.VMEM((tq + BLOCK, D_POOL), jnp.bfloat16),
            pltpu.VMEM((tq, D_POOL), jnp.bfloat16),
            pltpu.VMEM((tq, D_ATTN), jnp.bfloat16),
            pltpu.VMEM((tq, D_MODEL), jnp.float32),
        ] + _ffn_base_scratch(tq) + _ffn_copy_scratch(),
        compiler_params=pltpu.CompilerParams(
            dimension_semantics=("arbitrary",),
            vmem_limit_bytes=_vmem_limit(est + (8 << 20))),
        name="mix_ffn",
    )(sinks, x3d, q, k, k, v, v, u, u, bias, _window_sum_matrices(),
      _block_diagonal(pool_w.astype(jnp.bfloat16)), pool_scale.reshape(1, D_POOL),
      w_out_bf16, ffn_gain.reshape(1, D_MODEL), w_gate_bf16, w_up_bf16, w_down_bf16)


def kernel(x, ffn1_norm, ffn1_w_gate, ffn1_w_up, ffn1_w_down, mix_norm, w_in, q_norm, k_norm,
           attn_sinks, rel_bias, pool_w, pool_scale, w_out, ffn2_norm, ffn2_w_gate, ffn2_w_up,
           ffn2_w_down):
    b, s, d = x.shape
    assert (d, s % TILE_M) == (D_MODEL, 0)
    bias = _band_bias(rel_bias)
    for l in range(ffn1_norm.shape[0]):
        x1, q, k, v, u, wg2, wu2, wd2, wo = _ffn_proj(
            x.reshape(b * s, d), ffn1_norm[l], ffn1_w_gate[l], ffn1_w_up[l], ffn1_w_down[l],
            mix_norm[l], w_in[l], q_norm[l], k_norm[l],
            ffn2_w_gate[l], ffn2_w_up[l], ffn2_w_down[l], w_out[l])
        x = _mix_ffn(x1.reshape(b, s, d), q.reshape(b, s, D_ATTN), k.reshape(b, s, D_KV),
                     v.reshape(b, s, D_KV), u.reshape(b, s, D_POOL), attn_sinks[l], bias,
                     pool_w[l], pool_scale[l], wo, ffn2_norm[l], wg2, wu2, wd2)
    return x
```

```python
import functools

import numpy as np
import jax
import jax.numpy as jnp
from jax import lax
from jax.experimental import pallas as pl
from jax.experimental.pallas import tpu as pltpu

D_MODEL = 1024
HEAD_DIM = 64
N_HEADS = 8
N_KV_HEADS = 2
D_ATTN = N_HEADS * HEAD_DIM
D_KV = N_KV_HEADS * HEAD_DIM
D_POOL = D_MODEL - D_ATTN
POOL_WINDOWS = (2, 4, 8, 16)
POOL_GROUP_DIM = D_POOL // len(POOL_WINDOWS)
D_IN = D_ATTN + 2 * D_KV + D_POOL
WINDOW = 128
BLOCK = 128
N_BUCKETS = 32
MAX_DISTANCE = 128
D_FF = 2816
EPS = 1e-6
NEG = -1e30

V7X_LANES = 128
V7X_MXU_DIM = 256
V7X_VMEM_BYTES = 64 * 1024 * 1024

HEADS_PLAIN = (0, 2, 5, 7)
HEADS_ROLLED = (1, 3, 4, 6)

TILE_M = 512
FFN_CHUNK_F = 256
FFN_STAGE_SLOTS = 2


def _vmem_limit(estimate_bytes):
    return int(min(V7X_VMEM_BYTES - (4 << 20), max(estimate_bytes, 16 << 20)))


def _rmsnorm_rows(x32, gain_row):
    ms = jnp.mean(x32 * x32, axis=-1, keepdims=True)
    return x32 * lax.rsqrt(ms + EPS) * gain_row


N_FFN_CHUNKS = D_FF // FFN_CHUNK_F
N_FFN_BASE_SCRATCH = 5


def _ffn_base_scratch(tm):
    return [
        pltpu.VMEM((D_MODEL, D_FF), jnp.bfloat16),
        pltpu.VMEM((D_MODEL, D_FF), jnp.bfloat16),
        pltpu.VMEM((D_FF, D_MODEL), jnp.bfloat16),
        pltpu.VMEM((tm, D_MODEL), jnp.bfloat16),
        pltpu.VMEM((tm, D_FF), jnp.bfloat16),
    ]


def _ffn_round_scratch():
    return [
        pltpu.VMEM((2 * FFN_STAGE_SLOTS, D_MODEL, FFN_CHUNK_F), jnp.float32),
        pltpu.VMEM((FFN_STAGE_SLOTS, FFN_CHUNK_F, D_MODEL), jnp.float32),
        pltpu.SemaphoreType.DMA((3 * FFN_STAGE_SLOTS,)),
    ]


def _ffn_copy_scratch():
    return [pltpu.SemaphoreType.DMA((3, N_FFN_CHUNKS))]


def _ffn_vmem_bytes(tm):
    return (2 * 3 * D_MODEL * D_FF + 3 * FFN_STAGE_SLOTS * D_MODEL * FFN_CHUNK_F * 4
            + tm * D_MODEL * 2 + tm * D_FF * 2 + 6 * tm * FFN_CHUNK_F * 4 + tm * D_MODEL * 4)


def _ffn_window(c):
    return pl.ds(c * FFN_CHUNK_F, FFN_CHUNK_F)


def _ffn_weight_copy(which, c, w_hbm, base_scratch, copy_scratch):
    (sem_ref,) = copy_scratch
    src, dst = w_hbm[which], base_scratch[which]
    if which == 2:
        return pltpu.make_async_copy(src.at[_ffn_window(c), :], dst.at[_ffn_window(c), :],
                                     sem_ref.at[which, c])
    return pltpu.make_async_copy(src.at[:, _ffn_window(c)], dst.at[:, _ffn_window(c)],
                                 sem_ref.at[which, c])


def _ffn_start_weight_copies(w_hbm, base_scratch, copy_scratch):
    for c in range(N_FFN_CHUNKS):
        for which in range(3):
            _ffn_weight_copy(which, c, w_hbm, base_scratch, copy_scratch).start()


def _swiglu_tile(x32, g_ref, w_hbm, base_scratch, stream, stream_scratch, between=(), before_down=()):
    wg_hbm, wu_hbm, wd_hbm = w_hbm
    wg_ref, wu_ref, wd_ref, h_ref, a_ref = base_scratch

    def staged_copy(which, c):
        stage_in_ref, stage_out_ref, sem_ref = stream_scratch
        slot = c % FFN_STAGE_SLOTS
        sem = sem_ref.at[which * FFN_STAGE_SLOTS + slot]
        if which == 2:
            return pltpu.make_async_copy(wd_hbm.at[_ffn_window(c), :], stage_out_ref.at[slot], sem)
        src = (wg_hbm, wu_hbm)[which]
        return pltpu.make_async_copy(src.at[:, _ffn_window(c)],
                                     stage_in_ref.at[which * FFN_STAGE_SLOTS + slot], sem)

    def land(which, c):
        if stream == "wait":
            _ffn_weight_copy(which, c, w_hbm, base_scratch, stream_scratch).wait()
            return
        staged_copy(which, c).wait()
        stage_in_ref, stage_out_ref, _ = stream_scratch
        slot = c % FFN_STAGE_SLOTS
        rows_or_cols = slice(c * FFN_CHUNK_F, (c + 1) * FFN_CHUNK_F)
        if which == 2:
            wd_ref[rows_or_cols, :] = stage_out_ref[slot].astype(jnp.bfloat16)
        else:
            dst = (wg_ref, wu_ref)[which]
            dst[:, rows_or_cols] = stage_in_ref[which * FFN_STAGE_SLOTS + slot].astype(jnp.bfloat16)
        if c + FFN_STAGE_SLOTS < N_FFN_CHUNKS:
            staged_copy(which, c + FFN_STAGE_SLOTS).start()

    if stream == "round":
        for c in range(FFN_STAGE_SLOTS):
            for which in range(3):
                staged_copy(which, c).start()
    if x32 is not None:
        h_ref[...] = _rmsnorm_rows(x32, g_ref[...]).astype(jnp.bfloat16)
    for c in range(N_FFN_CHUNKS):
        cols = slice(c * FFN_CHUNK_F, (c + 1) * FFN_CHUNK_F)
        if stream is not None:
            for which in range(3 if stream == "round" else 2):
                land(which, c)
        h = h_ref[...]
        gate = jnp.dot(h, wg_ref[:, cols], preferred_element_type=jnp.float32)
        up = jnp.dot(h, wu_ref[:, cols], preferred_element_type=jnp.float32)
        act = gate * (1.0 / (1.0 + jnp.exp(-gate))) * up
        a_ref[:, cols] = act.astype(jnp.bfloat16)
        if c < len(between):
            between[c]()
    for stage in tuple(between[N_FFN_CHUNKS:]) + tuple(before_down):
        stage()
    if stream == "wait":
        for c in range(N_FFN_CHUNKS):
            land(2, c)
    return jnp.dot(a_ref[...], wd_ref[...], preferred_element_type=jnp.float32)


def _ffn_proj_kernel(x_ref, g1_ref, wg_hbm, wu_hbm, wd_hbm, gm_ref, win_ref, qg_ref, kg_ref, dm_ref,
                     wg2_ref, wu2_ref, wd2_ref, wo_ref,
                     x1_ref, z_ref, wg2_bf_ref, wu2_bf_ref, wd2_bf_ref, wo_bf_ref,
                     *scratch):
    tm = x_ref.shape[0]
    base_scratch, round_scratch = scratch[:N_FFN_BASE_SCRATCH], scratch[N_FFN_BASE_SCRATCH:]

    first_half = pl.program_id(0) < pl.num_programs(0) // 2

    @pl.when(first_half)
    def _():
        wg2_bf_ref[...] = wg2_ref[...].astype(jnp.bfloat16)
        wd2_bf_ref[...] = wd2_ref[...].astype(jnp.bfloat16)

    @pl.when(jnp.logical_not(first_half))
    def _():
        wu2_bf_ref[...] = wu2_ref[...].astype(jnp.bfloat16)
        wo_bf_ref[...] = wo_ref[...].astype(jnp.bfloat16)

    def body(stream):
        y = _swiglu_tile(x_ref[...], g1_ref, (wg_hbm, wu_hbm, wd_hbm), base_scratch, stream,
                         round_scratch)
        x1 = x_ref[...] + 0.5 * y
        x1_ref[...] = x1
        h = _rmsnorm_rows(x1, gm_ref[...]).astype(jnp.bfloat16)
        z = jnp.dot(h, win_ref[...], preferred_element_type=jnp.float32)
        n_slab = (D_ATTN + 2 * D_KV) // V7X_MXU_DIM
        sq = jnp.concatenate(
            [z[:, i * V7X_MXU_DIM:(i + 1) * V7X_MXU_DIM] for i in range(n_slab)], axis=0)
        ms = jnp.dot((sq * sq).astype(jnp.bfloat16), dm_ref[...], preferred_element_type=jnp.float32)
        q_ms = jnp.concatenate([ms[i * tm:(i + 1) * tm] for i in range(n_slab - 1)], axis=1)
        k_ms = ms[(n_slab - 1) * tm:, :D_KV]
        q = z[:, :D_ATTN]
        k = z[:, D_ATTN:D_ATTN + D_KV]
        z_ref[:, :D_ATTN] = (q * lax.rsqrt(q_ms + EPS) * qg_ref[...]).astype(jnp.bfloat16)
        z_ref[:, D_ATTN:D_ATTN + D_KV] = (k * lax.rsqrt(k_ms + EPS) * kg_ref[...]).astype(jnp.bfloat16)
        z_ref[:, D_ATTN + D_KV:] = z[:, D_ATTN + D_KV:].astype(jnp.bfloat16)

    first = pl.program_id(0) == 0
    pl.when(first)(functools.partial(body, "round"))
    pl.when(jnp.logical_not(first))(functools.partial(body, None))


def _head_mean_matrix(width):
    idx = np.arange(width) // HEAD_DIM
    return jnp.asarray((idx[:, None] == idx[None, :]).astype(np.float32) / HEAD_DIM, jnp.bfloat16)


def _ffn_proj(x2d, ffn_gain, w_gate, w_up, w_down, mix_gain, w_in, q_gain, k_gain,
              next_w_gate, next_w_up, next_w_down, w_out):
    m = x2d.shape[0]
    tm = TILE_M
    n_steps = m // tm
    resident = functools.partial(pl.BlockSpec, pipeline_mode=pl.Buffered(1))
    rows = lambda width: pl.BlockSpec((tm, width), lambda i: (i, 0))
    hbm = pl.BlockSpec(memory_space=pltpu.HBM)
    half_steps = n_steps // 2
    wr = D_MODEL // half_steps
    wdr = D_FF // half_steps
    assert 2 * half_steps == n_steps and wr * half_steps == D_MODEL and wdr * half_steps == D_FF
    assert wr % 16 == 0 and wdr % 16 == 0
    early_slice = lambda i: (jnp.minimum(i, half_steps - 1), 0)
    late_slice = lambda i: (jnp.maximum(i - half_steps, 0), 0)
    wg_rows = pl.BlockSpec((wr, D_FF), early_slice)
    wd_rows = pl.BlockSpec((wdr, D_MODEL), early_slice)
    wu_rows = pl.BlockSpec((wr, D_FF), late_slice)
    wo_rows = pl.BlockSpec((wr, D_MODEL), late_slice)
    qg = (jnp.tile(q_gain, N_HEADS) * (HEAD_DIM ** -0.5)).reshape(1, D_ATTN)
    kg = jnp.tile(k_gain, N_KV_HEADS).reshape(1, D_KV)
    est = (_ffn_vmem_bytes(tm) + 2 * D_MODEL * D_IN
           + 2 * tm * D_MODEL * 4 * 2
           + 2 * tm * D_IN * 2
           + 4 * tm * D_IN * 4
           + 2 * 6 * (2 * wr * D_FF + wdr * D_MODEL + wr * D_MODEL))
    bf16 = lambda shape: jax.ShapeDtypeStruct(shape, jnp.bfloat16)
    return pl.pallas_call(
        _ffn_proj_kernel,
        out_shape=(
            jax.ShapeDtypeStruct((m, D_MODEL), jnp.float32),
            bf16((m, D_IN)),
            bf16((D_MODEL, D_FF)), bf16((D_MODEL, D_FF)), bf16((D_FF, D_MODEL)),
            bf16((D_MODEL, D_MODEL)),
        ),
        grid=(n_steps,),
        in_specs=[
            rows(D_MODEL),
            resident((1, D_MODEL), lambda i: (0, 0)),
            hbm, hbm, hbm,
            resident((1, D_MODEL), lambda i: (0, 0)),
            resident((D_MODEL, D_IN), lambda i: (0, 0)),
            resident((1, D_ATTN), lambda i: (0, 0)),
            resident((1, D_KV), lambda i: (0, 0)),
            resident((V7X_MXU_DIM, V7X_MXU_DIM), lambda i: (0, 0)),
            wg_rows, wu_rows, wd_rows, wo_rows,
        ],
        out_specs=(rows(D_MODEL), rows(D_IN), wg_rows, wu_rows, wd_rows, wo_rows),
        scratch_shapes=_ffn_base_scratch(tm) + _ffn_round_scratch(),
        compiler_params=pltpu.CompilerParams(
            dimension_semantics=("arbitrary",),
            vmem_limit_bytes=_vmem_limit(est + (8 << 20))),
        name="ffn_proj",
    )(x2d, ffn_gain.reshape(1, D_MODEL), w_gate, w_up, w_down, mix_gain.reshape(1, D_MODEL),
      w_in.astype(jnp.bfloat16), qg, kg, _head_mean_matrix(V7X_MXU_DIM),
      next_w_gate, next_w_up, next_w_down, w_out)


def _t5_bucket(dist):
    n = np.maximum(dist, 0)
    max_exact = N_BUCKETS // 2
    large = max_exact + (np.log(np.maximum(n, 1) / max_exact)
                         / np.log(MAX_DISTANCE / max_exact)
                         * (N_BUCKETS - max_exact)).astype(np.int32)
    large = np.minimum(large, N_BUCKETS - 1)
    return np.where(n < max_exact, n, large).astype(np.int32)


def _band_bucket_table():
    ql = np.arange(BLOCK)[:, None]
    kl = np.arange(2 * BLOCK)[None, :]
    dist = ql + BLOCK - kl
    in_band = (dist >= 0) & (dist < WINDOW)
    return np.where(in_band, _t5_bucket(dist), -1).astype(np.int32)


def _bias_kernel(rel_ref, bucket_ref, o_ref):
    bucket = bucket_ref[...]
    for slot, head in enumerate(HEADS_PLAIN + HEADS_ROLLED):
        acc = jnp.full((BLOCK, 2 * BLOCK), NEG, jnp.float32)
        for b in range(N_BUCKETS):
            acc = jnp.where(bucket == b, rel_ref[b, head], acc)
        o_ref[slot] = acc


def _band_bias(rel_bias):
    return pl.pallas_call(
        _bias_kernel,
        out_shape=jax.ShapeDtypeStruct((N_HEADS, BLOCK, 2 * BLOCK), jnp.float32),
        in_specs=[
            pl.BlockSpec(memory_space=pltpu.SMEM),
            pl.BlockSpec(memory_space=pltpu.VMEM),
        ],
        out_specs=pl.BlockSpec(memory_space=pltpu.VMEM),
        name="band_bias",
    )(rel_bias, jnp.asarray(_band_bucket_table()))


def _window_sum_matrices():
    ql = np.arange(BLOCK)[:, None]
    kl = np.arange(2 * BLOCK)[None, :]
    dist = ql + BLOCK - kl
    mats = [((dist >= 0) & (dist < w)).astype(np.float32) for w in POOL_WINDOWS]
    return jnp.asarray(np.stack(mats), jnp.bfloat16)


def _block_diagonal(w):
    g, c, _ = w.shape
    on_diagonal = jnp.eye(g, dtype=bool)[:, None, :, None]
    return jnp.where(on_diagonal, w[:, :, None, :], jnp.zeros((), w.dtype)).reshape(g * c, g * c)


def _mix_stages(seq_first, t_base, sink_ref, x_ref, z_ref, zh_ref,
                bias_ref, wsum_ref, pw_ref, ps_ref, wo_ref, x2_ref, mix_scratch, g_ref, h_ref):
    kx_ref, kxr_ref, vx_ref, vxr_ref, ux_ref, pooled_ref, y_ref = mix_scratch
    tq = z_ref.shape[1]
    k_cols = slice(D_ATTN, D_ATTN + D_KV)
    v_cols = slice(D_ATTN + D_KV, D_ATTN + 2 * D_KV)
    u_cols = slice(D_ATTN + 2 * D_KV, D_IN)
    n_blocks = tq // BLOCK
    half = HEAD_DIM
    groups = ((HEADS_PLAIN, kx_ref, vx_ref, 0), (HEADS_ROLLED, kxr_ref, vxr_ref, len(HEADS_PLAIN)))
    lane = lax.broadcasted_iota(jnp.int32, (BLOCK, 2 * half), 1)
    low = lane < half
    state = {}

    def rows_of(j):
        return slice(j * BLOCK, (j + 1) * BLOCK)

    def keys_of(j):
        return slice(j * BLOCK, (j + 2) * BLOCK)

    def build_slabs():
        halo = jnp.where(seq_first, jnp.zeros_like(zh_ref[0, :, D_ATTN:]), zh_ref[0, :, D_ATTN:])
        kx_ref[:BLOCK] = halo[:, :D_KV]
        kx_ref[BLOCK:] = z_ref[0, :, k_cols]
        kxr_ref[...] = pltpu.roll(kx_ref[...].astype(jnp.float32), half, 1).astype(jnp.bfloat16)
        ones = jnp.ones((tq + BLOCK, D_KV), jnp.bfloat16)
        vx_ref[:BLOCK, :D_KV] = halo[:, D_KV:2 * D_KV]
        vx_ref[BLOCK:, :D_KV] = z_ref[0, :, v_cols]
        vx_ref[:, D_KV:] = ones
        vxr_ref[:, :D_KV] = pltpu.roll(
            vx_ref[:, :D_KV].astype(jnp.float32), half, 1).astype(jnp.bfloat16)
        vxr_ref[:, D_KV:] = ones
        ux_ref[:BLOCK] = halo[:, 2 * D_KV:]
        ux_ref[BLOCK:] = z_ref[0, :, u_cols]

    def scores(j, grp):
        heads, k_src, _, _ = groups[grp]
        q = z_ref[0, rows_of(j), :D_ATTN]
        zero = jnp.zeros((BLOCK, 2 * half), q.dtype)
        pieces = []
        for hd in heads:
            pair = q[:, (hd // 2) * 2 * half:(hd // 2 + 1) * 2 * half]
            pieces.append(jnp.where(low if hd % 2 == 0 else ~low, pair, zero))
        qs = jnp.concatenate(pieces, axis=0)
        state["logits", j, grp] = lax.dot_general(
            qs, k_src[keys_of(j), :], (((1,), (1,)), ((), ())),
            preferred_element_type=jnp.float32)

    def softmax_numerators(j, grp):
        heads, _, _, slot0 = groups[grp]
        logits = state.pop(("logits", j, grp))
        es, sink_terms = [], []
        for s, hd in enumerate(heads):
            lg = logits[s * BLOCK:(s + 1) * BLOCK] + bias_ref[slot0 + s]
            if j == 0:
                col = lax.broadcasted_iota(jnp.int32, (1, 2 * BLOCK), 1)
                lg = lg + jnp.where(seq_first & (col < BLOCK), NEG, 0.0).astype(jnp.float32)
            sink = sink_ref[hd]
            m = jnp.maximum(jnp.max(lg, axis=-1, keepdims=True), sink)
            es.append(jnp.exp(lg - m).astype(jnp.bfloat16))
            sink_terms.append(jnp.exp(sink - m))
        state["e", j, grp] = (jnp.concatenate(es, axis=0), sink_terms)

    def attend(j, grp):
        heads, _, v_src, _ = groups[grp]
        e, sink_terms = state.pop(("e", j, grp))
        pv = jnp.dot(e, v_src[keys_of(j), :], preferred_element_type=jnp.float32)
        for s, hd in enumerate(heads):
            blk = pv[s * BLOCK:(s + 1) * BLOCK]
            state["out", j, hd] = blk[:, :D_KV] / (blk[:, D_KV:] + sink_terms[s])

    def store_attention(j):
        for p in range(N_HEADS // 2):
            y_ref[rows_of(j), p * 2 * half:(p + 1) * 2 * half] = jnp.where(
                low, state.pop(("out", j, 2 * p)), state.pop(("out", j, 2 * p + 1))
            ).astype(jnp.bfloat16)

    def pool_means(j):
        row = lax.broadcasted_iota(jnp.int32, (BLOCK, 1), 0)
        t_glob = t_base + j * BLOCK + row
        for g, w in enumerate(POOL_WINDOWS):
            gc = slice(g * POOL_GROUP_DIM, (g + 1) * POOL_GROUP_DIM)
            wsum = jnp.dot(wsum_ref[g], ux_ref[keys_of(j), gc], preferred_element_type=jnp.float32)
            cnt = jnp.minimum(t_glob + 1, w).astype(jnp.float32)
            token = z_ref[0, rows_of(j), D_ATTN + 2 * D_KV + g * POOL_GROUP_DIM:
                          D_ATTN + 2 * D_KV + (g + 1) * POOL_GROUP_DIM]
            pooled = wsum * (1.0 / cnt) - token.astype(jnp.float32)
            pooled_ref[rows_of(j), gc] = pooled.astype(jnp.bfloat16)

    def pool_map():
        state["y_pool"] = jnp.dot(pooled_ref[...], pw_ref[...],
                                  preferred_element_type=jnp.float32) * ps_ref[...]

    def project_pool():
        x2_ref[...] = x_ref[0] + jnp.dot(state.pop("y_pool").astype(jnp.bfloat16), wo_ref[D_ATTN:, :],
                                         preferred_element_type=jnp.float32)

    def project_attention(j):
        x2 = x2_ref[rows_of(j), :] + jnp.dot(
            y_ref[rows_of(j), :], wo_ref[:D_ATTN, :], preferred_element_type=jnp.float32)
        x2_ref[rows_of(j), :] = x2
        h_ref[rows_of(j), :] = _rmsnorm_rows(x2, g_ref[...]).astype(jnp.bfloat16)

    def seq(*calls):
        def run():
            for fn, *args in calls:
                fn(*args)
        return run

    early = [
        seq((build_slabs,)),
        seq(*[(pool_means, j) for j in range(n_blocks // 2)]),
        seq(*[(pool_means, j) for j in range(n_blocks // 2, n_blocks)]),
        seq((scores, 0, 0), (scores, 0, 1), (softmax_numerators, 0, 0)),
        seq((pool_map,), (attend, 0, 0), (softmax_numerators, 0, 1)),
        seq((project_pool,), (scores, 1, 0), (attend, 0, 1), (store_attention, 0)),
    ]
    for j in range(1, n_blocks):
        early.append(seq((scores, j, 1), (softmax_numerators, j, 0), (attend, j, 0)))
        tail = [(softmax_numerators, j, 1)]
        if j + 1 < n_blocks:
            tail.append((scores, j + 1, 0))
        tail += [(attend, j, 1), (store_attention, j)]
        early.append(seq(*tail))
    late = [seq((project_attention, j)) for j in range(n_blocks)]
    return early, late


N_MIX_SCRATCH = 8


def _mix_ffn_kernel(sink_ref, x_ref, z_ref, zh_ref,
                    bias_ref, wsum_ref, pw_ref, ps_ref, wo_ref, g2_ref, wg_hbm, wu_hbm, wd_hbm,
                    o_ref, *scratch, tiles_per_seq):
    mix_scratch, x2_ref = scratch[:N_MIX_SCRATCH - 1], scratch[N_MIX_SCRATCH - 1]
    base_scratch = scratch[N_MIX_SCRATCH:N_MIX_SCRATCH + N_FFN_BASE_SCRATCH]
    copy_scratch = scratch[N_MIX_SCRATCH + N_FFN_BASE_SCRATCH:]
    h_ref = base_scratch[3]
    t = pl.program_id(0)
    tq = z_ref.shape[1]
    t_mix = jnp.minimum(t, pl.num_programs(0) - 2)
    i_seq = lax.rem(t_mix, tiles_per_seq)
    stages = functools.partial(
        _mix_stages, i_seq == 0, i_seq * tq, sink_ref, x_ref, z_ref, zh_ref,
        bias_ref, wsum_ref, pw_ref, ps_ref, wo_ref, x2_ref, mix_scratch, g2_ref, h_ref)
    w_hbm = (wg_hbm, wu_hbm, wd_hbm)

    def first_step():
        _ffn_start_weight_copies(w_hbm, base_scratch, copy_scratch)
        early, late = stages()
        for stage in early + late:
            stage()

    def later_step(wait_weights):
        early, late = stages()
        o_ref[0] = x2_ref[...]
        y = _swiglu_tile(None, g2_ref, w_hbm, base_scratch, "wait" if wait_weights else None,
                         copy_scratch, between=early, before_down=late)
        o_ref[0] += 0.5 * y

    pl.when(t == 0)(first_step)
    pl.when(t == 1)(functools.partial(later_step, True))
    pl.when(t > 1)(functools.partial(later_step, False))


def _mix_ffn(x3d, z3d, sinks, bias, pool_w, pool_scale, w_out_bf16, ffn_gain,
             w_gate_bf16, w_up_bf16, w_down_bf16):
    b, s, _ = x3d.shape
    tq = TILE_M
    nblk = tq // BLOCK
    tiles_per_seq = s // tq
    n_tiles = b * tiles_per_seq
    resident = functools.partial(pl.BlockSpec, pipeline_mode=pl.Buffered(1))
    hbm = pl.BlockSpec(memory_space=pltpu.HBM)

    def mixed(t):
        tm = jnp.minimum(t, n_tiles - 1)
        return tm // tiles_per_seq, lax.rem(tm, tiles_per_seq)

    def tile(width):
        return pl.BlockSpec((1, tq, width), lambda t: (*mixed(t), 0))

    def halo(width):
        def index(t):
            bi, i = mixed(t)
            return bi, jnp.maximum(i * nblk - 1, 0), 0
        return pl.BlockSpec((1, BLOCK, width), index)

    def out_index(t):
        tf = jnp.maximum(t - 1, 0)
        return tf // tiles_per_seq, lax.rem(tf, tiles_per_seq), 0

    est = (_ffn_vmem_bytes(tq)
           + 2 * 2 * tq * D_MODEL * 4 + 2 * (tq + BLOCK) * D_IN * 2
           + N_HEADS * BLOCK * 2 * BLOCK * 4 + 2 * D_MODEL * D_MODEL + 2 * D_POOL * D_POOL
           + 6 * (tq + BLOCK) * D_KV * 2 + (tq + BLOCK) * D_POOL * 2 + tq * D_MODEL * 2
           + tq * D_MODEL * 4 + 16 * BLOCK * 2 * BLOCK * 4 * 4)
    return pl.pallas_call(
        functools.partial(_mix_ffn_kernel, tiles_per_seq=tiles_per_seq),
        out_shape=jax.ShapeDtypeStruct((b, s, D_MODEL), jnp.float32),
        grid=(n_tiles + 1,),
        in_specs=[
            pl.BlockSpec(memory_space=pltpu.SMEM),
            tile(D_MODEL),
            tile(D_IN), halo(D_IN),
            resident((N_HEADS, BLOCK, 2 * BLOCK), lambda t: (0, 0, 0)),
            resident((len(POOL_WINDOWS), BLOCK, 2 * BLOCK), lambda t: (0, 0, 0)),
            resident((D_POOL, D_POOL), lambda t: (0, 0)),
            resident((1, D_POOL), lambda t: (0, 0)),
            resident((D_MODEL, D_MODEL), lambda t: (0, 0)),
            resident((1, D_MODEL), lambda t: (0, 0)),
            hbm, hbm, hbm,
        ],
        out_specs=pl.BlockSpec((1, tq, D_MODEL), out_index),
        scratch_shapes=[
            pltpu.VMEM((tq + BLOCK, D_KV), jnp.bfloat16),
            pltpu.VMEM((tq + BLOCK, D_KV), jnp.bfloat16),
            pltpu.VMEM((tq + BLOCK, 2 * D_KV), jnp.bfloat16),
            pltpu.VMEM((tq + BLOCK, 2 * D_KV), jnp.bfloat16),
            pltpu.VMEM((tq + BLOCK, D_POOL), jnp.bfloat16),
            pltpu.VMEM((tq, D_POOL), jnp.bfloat16),
            pltpu.VMEM((tq, D_ATTN), jnp.bfloat16),
            pltpu.VMEM((tq, D_MODEL), jnp.float32),
        ] + _ffn_base_scratch(tq) + _ffn_copy_scratch(),
        compiler_params=pltpu.CompilerParams(
            dimension_semantics=("arbitrary",),
            vmem_limit_bytes=_vmem_limit(est + (8 << 20))),
        name="mix_ffn",
    )(sinks, x3d, z3d, z3d, bias, _window_sum_matrices(),
      _block_diagonal(pool_w.astype(jnp.bfloat16)), pool_scale.reshape(1, D_POOL),
      w_out_bf16, ffn_gain.reshape(1, D_MODEL), w_gate_bf16, w_up_bf16, w_down_bf16)


def kernel(x, ffn1_norm, ffn1_w_gate, ffn1_w_up, ffn1_w_down, mix_norm, w_in, q_norm, k_norm,
           attn_sinks, rel_bias, pool_w, pool_scale, w_out, ffn2_norm, ffn2_w_gate, ffn2_w_up,
           ffn2_w_down):
    b, s, d = x.shape
    assert (d, s % TILE_M) == (D_MODEL, 0)
    bias = _band_bias(rel_bias)
    for l in range(ffn1_norm.shape[0]):
        x1, z, wg2, wu2, wd2, wo = _ffn_proj(
            x.reshape(b * s, d), ffn1_norm[l], ffn1_w_gate[l], ffn1_w_up[l], ffn1_w_down[l],
            mix_norm[l], w_in[l], q_norm[l], k_norm[l],
            ffn2_w_gate[l], ffn2_w_up[l], ffn2_w_down[l], w_out[l])
        x = _mix_ffn(x1.reshape(b, s, d), z.reshape(b, s, D_IN), attn_sinks[l], bias,
                     pool_w[l], pool_scale[l], wo, ffn2_norm[l], wg2, wu2, wd2)
    return x
```

```python
import functools

import numpy as np
import jax
import jax.numpy as jnp
from jax import lax
from jax.experimental import pallas as pl
from jax.experimental.pallas import tpu as pltpu

D_MODEL = 1024
HEAD_DIM = 64
N_HEADS = 8
N_KV_HEADS = 2
D_ATTN = N_HEADS * HEAD_DIM
D_KV = N_KV_HEADS * HEAD_DIM
D_POOL = D_MODEL - D_ATTN
POOL_WINDOWS = (2, 4, 8, 16)
POOL_GROUP_DIM = D_POOL // len(POOL_WINDOWS)
D_IN = D_ATTN + 2 * D_KV + D_POOL
WINDOW = 128
BLOCK = 128
N_BUCKETS = 32
MAX_DISTANCE = 128
D_FF = 2816
EPS = 1e-6
NEG = -1e30

V7X_LANES = 128
V7X_MXU_DIM = 256
V7X_VMEM_BYTES = 64 * 1024 * 1024

HEADS_PLAIN = (0, 2, 5, 7)
HEADS_ROLLED = (1, 3, 4, 6)

TILE_M = 512
FFN_CHUNK_F = 256
FFN_STAGE_SLOTS = 2


def _vmem_limit(estimate_bytes):
    return int(min(V7X_VMEM_BYTES - (4 << 20), max(estimate_bytes, 16 << 20)))


def _rmsnorm_rows(x32, gain_row):
    ms = jnp.mean(x32 * x32, axis=-1, keepdims=True)
    return x32 * lax.rsqrt(ms + EPS) * gain_row


N_FFN_CHUNKS = D_FF // FFN_CHUNK_F
N_FFN_BASE_SCRATCH = 5


def _ffn_base_scratch(tm):
    return [
        pltpu.VMEM((D_MODEL, D_FF), jnp.bfloat16),
        pltpu.VMEM((D_MODEL, D_FF), jnp.bfloat16),
        pltpu.VMEM((D_FF, D_MODEL), jnp.bfloat16),
        pltpu.VMEM((tm, D_MODEL), jnp.bfloat16),
        pltpu.VMEM((tm, D_FF), jnp.bfloat16),
    ]


def _ffn_round_scratch():
    return [
        pltpu.VMEM((2 * FFN_STAGE_SLOTS, D_MODEL, FFN_CHUNK_F), jnp.float32),
        pltpu.VMEM((FFN_STAGE_SLOTS, FFN_CHUNK_F, D_MODEL), jnp.float32),
        pltpu.SemaphoreType.DMA((3 * FFN_STAGE_SLOTS,)),
    ]


def _ffn_copy_scratch():
    return [pltpu.SemaphoreType.DMA((3, N_FFN_CHUNKS))]


def _ffn_vmem_bytes(tm):
    return (2 * 3 * D_MODEL * D_FF + 3 * FFN_STAGE_SLOTS * D_MODEL * FFN_CHUNK_F * 4
            + tm * D_MODEL * 2 + tm * D_FF * 2 + 6 * tm * FFN_CHUNK_F * 4 + tm * D_MODEL * 4)


def _ffn_window(c):
    return pl.ds(c * FFN_CHUNK_F, FFN_CHUNK_F)


def _ffn_weight_copy(which, c, w_hbm, base_scratch, copy_scratch):
    (sem_ref,) = copy_scratch
    src, dst = w_hbm[which], base_scratch[which]
    if which == 2:
        return pltpu.make_async_copy(src.at[_ffn_window(c), :], dst.at[_ffn_window(c), :],
                                     sem_ref.at[which, c])
    return pltpu.make_async_copy(src.at[:, _ffn_window(c)], dst.at[:, _ffn_window(c)],
                                 sem_ref.at[which, c])


def _ffn_start_weight_copies(w_hbm, base_scratch, copy_scratch):
    for c in range(N_FFN_CHUNKS):
        for which in range(3):
            _ffn_weight_copy(which, c, w_hbm, base_scratch, copy_scratch).start()


def _swiglu_tile(x32, g_ref, w_hbm, base_scratch, stream, stream_scratch, between=(), before_down=()):
    wg_hbm, wu_hbm, wd_hbm = w_hbm
    wg_ref, wu_ref, wd_ref, h_ref, a_ref = base_scratch

    def staged_copy(which, c):
        stage_in_ref, stage_out_ref, sem_ref = stream_scratch
        slot = c % FFN_STAGE_SLOTS
        sem = sem_ref.at[which * FFN_STAGE_SLOTS + slot]
        if which == 2:
            return pltpu.make_async_copy(wd_hbm.at[_ffn_window(c), :], stage_out_ref.at[slot], sem)
        src = (wg_hbm, wu_hbm)[which]
        return pltpu.make_async_copy(src.at[:, _ffn_window(c)],
                                     stage_in_ref.at[which * FFN_STAGE_SLOTS + slot], sem)

    def land(which, c):
        if stream == "wait":
            _ffn_weight_copy(which, c, w_hbm, base_scratch, stream_scratch).wait()
            return
        staged_copy(which, c).wait()
        stage_in_ref, stage_out_ref, _ = stream_scratch
        slot = c % FFN_STAGE_SLOTS
        rows_or_cols = slice(c * FFN_CHUNK_F, (c + 1) * FFN_CHUNK_F)
        if which == 2:
            wd_ref[rows_or_cols, :] = stage_out_ref[slot].astype(jnp.bfloat16)
        else:
            dst = (wg_ref, wu_ref)[which]
            dst[:, rows_or_cols] = stage_in_ref[which * FFN_STAGE_SLOTS + slot].astype(jnp.bfloat16)
        if c + FFN_STAGE_SLOTS < N_FFN_CHUNKS:
            staged_copy(which, c + FFN_STAGE_SLOTS).start()

    if stream == "round":
        for c in range(FFN_STAGE_SLOTS):
            for which in range(3):
                staged_copy(which, c).start()
    if x32 is not None:
        h_ref[...] = _rmsnorm_rows(x32, g_ref[...]).astype(jnp.bfloat16)
    for c in range(N_FFN_CHUNKS):
        cols = slice(c * FFN_CHUNK_F, (c + 1) * FFN_CHUNK_F)
        if stream is not None:
            for which in range(3 if stream == "round" else 2):
                land(which, c)
        h = h_ref[...]
        gate = jnp.dot(h, wg_ref[:, cols], preferred_element_type=jnp.float32)
        up = jnp.dot(h, wu_ref[:, cols], preferred_element_type=jnp.float32)
        act = gate * (1.0 / (1.0 + jnp.exp(-gate))) * up
        a_ref[:, cols] = act.astype(jnp.bfloat16)
        if c < len(between):
            between[c]()
    for stage in tuple(between[N_FFN_CHUNKS:]) + tuple(before_down):
        stage()
    if stream == "wait":
        for c in range(N_FFN_CHUNKS):
            land(2, c)
    return jnp.dot(a_ref[...], wd_ref[...], preferred_element_type=jnp.float32)


def _ffn_proj_kernel(x_ref, g1_ref, wg_hbm, wu_hbm, wd_hbm, gm_ref, win_ref, qg_ref, kg_ref, dm_ref,
                     wg2_ref, wu2_ref, wd2_ref, wo_ref,
                     x1_ref, z_ref, wg2_bf_ref, wu2_bf_ref, wd2_bf_ref, wo_bf_ref,
                     *scratch):
    tm = x_ref.shape[0]
    base_scratch, round_scratch = scratch[:N_FFN_BASE_SCRATCH], scratch[N_FFN_BASE_SCRATCH:]

    first_half = pl.program_id(0) < pl.num_programs(0) // 2

    @pl.when(first_half)
    def _():
        wg2_bf_ref[...] = wg2_ref[...].astype(jnp.bfloat16)
        wd2_bf_ref[...] = wd2_ref[...].astype(jnp.bfloat16)

    @pl.when(jnp.logical_not(first_half))
    def _():
        wu2_bf_ref[...] = wu2_ref[...].astype(jnp.bfloat16)
        wo_bf_ref[...] = wo_ref[...].astype(jnp.bfloat16)

    def body(stream):
        y = _swiglu_tile(x_ref[...], g1_ref, (wg_hbm, wu_hbm, wd_hbm), base_scratch, stream,
                         round_scratch)
        x1 = x_ref[...] + 0.5 * y
        x1_ref[...] = x1
        h = _rmsnorm_rows(x1, gm_ref[...]).astype(jnp.bfloat16)
        z = jnp.dot(h, win_ref[...], preferred_element_type=jnp.float32)
        n_slab = (D_ATTN + 2 * D_KV) // V7X_MXU_DIM
        sq = jnp.concatenate(
            [z[:, i * V7X_MXU_DIM:(i + 1) * V7X_MXU_DIM] for i in range(n_slab)], axis=0)
        ms = jnp.dot((sq * sq).astype(jnp.bfloat16), dm_ref[...], preferred_element_type=jnp.float32)
        q_ms = jnp.concatenate([ms[i * tm:(i + 1) * tm] for i in range(n_slab - 1)], axis=1)
        k_ms = ms[(n_slab - 1) * tm:, :D_KV]
        q = z[:, :D_ATTN]
        k = z[:, D_ATTN:D_ATTN + D_KV]
        z_ref[:, :D_ATTN] = (q * lax.rsqrt(q_ms + EPS) * qg_ref[...]).astype(jnp.bfloat16)
        z_ref[:, D_ATTN:D_ATTN + D_KV] = (k * lax.rsqrt(k_ms + EPS) * kg_ref[...]).astype(jnp.bfloat16)
        z_ref[:, D_ATTN + D_KV:] = z[:, D_ATTN + D_KV:].astype(jnp.bfloat16)

    first = pl.program_id(0) == 0
    pl.when(first)(functools.partial(body, "round"))
    pl.when(jnp.logical_not(first))(functools.partial(body, None))


def _head_mean_matrix(width):
    idx = np.arange(width) // HEAD_DIM
    return jnp.asarray((idx[:, None] == idx[None, :]).astype(np.float32) / HEAD_DIM, jnp.bfloat16)


def _ffn_proj(x2d, ffn_gain, w_gate, w_up, w_down, mix_gain, w_in, q_gain, k_gain,
              next_w_gate, next_w_up, next_w_down, w_out):
    m = x2d.shape[0]
    tm = TILE_M
    n_steps = m // tm
    resident = functools.partial(pl.BlockSpec, pipeline_mode=pl.Buffered(1))
    rows = lambda width: pl.BlockSpec((tm, width), lambda i: (i, 0))
    hbm = pl.BlockSpec(memory_space=pltpu.HBM)
    half_steps = n_steps // 2
    wr = D_MODEL // half_steps
    wdr = D_FF // half_steps
    assert 2 * half_steps == n_steps and wr * half_steps == D_MODEL and wdr * half_steps == D_FF
    assert wr % 16 == 0 and wdr % 16 == 0
    early_slice = lambda i: (jnp.minimum(i, half_steps - 1), 0)
    late_slice = lambda i: (jnp.maximum(i - half_steps, 0), 0)
    wg_rows = pl.BlockSpec((wr, D_FF), early_slice)
    wd_rows = pl.BlockSpec((wdr, D_MODEL), early_slice)
    wu_rows = pl.BlockSpec((wr, D_FF), late_slice)
    wo_rows = pl.BlockSpec((wr, D_MODEL), late_slice)
    qg = (jnp.tile(q_gain, N_HEADS) * (HEAD_DIM ** -0.5)).reshape(1, D_ATTN)
    kg = jnp.tile(k_gain, N_KV_HEADS).reshape(1, D_KV)
    est = (_ffn_vmem_bytes(tm) + 2 * D_MODEL * D_IN
           + 2 * tm * D_MODEL * 4 * 2
           + 2 * tm * D_IN * 2
           + 4 * tm * D_IN * 4
           + 2 * 6 * (2 * wr * D_FF + wdr * D_MODEL + wr * D_MODEL))
    bf16 = lambda shape: jax.ShapeDtypeStruct(shape, jnp.bfloat16)
    return pl.pallas_call(
        _ffn_proj_kernel,
        out_shape=(
            jax.ShapeDtypeStruct((m, D_MODEL), jnp.float32),
            bf16((m, D_IN)),
            bf16((D_MODEL, D_FF)), bf16((D_MODEL, D_FF)), bf16((D_FF, D_MODEL)),
            bf16((D_MODEL, D_MODEL)),
        ),
        grid=(n_steps,),
        in_specs=[
            rows(D_MODEL),
            resident((1, D_MODEL), lambda i: (0, 0)),
            hbm, hbm, hbm,
            resident((1, D_MODEL), lambda i: (0, 0)),
            resident((D_MODEL, D_IN), lambda i: (0, 0)),
            resident((1, D_ATTN), lambda i: (0, 0)),
            resident((1, D_KV), lambda i: (0, 0)),
            resident((V7X_MXU_DIM, V7X_MXU_DIM), lambda i: (0, 0)),
            wg_rows, wu_rows, wd_rows, wo_rows,
        ],
        out_specs=(rows(D_MODEL), rows(D_IN), wg_rows, wu_rows, wd_rows, wo_rows),
        scratch_shapes=_ffn_base_scratch(tm) + _ffn_round_scratch(),
        compiler_params=pltpu.CompilerParams(
            dimension_semantics=("arbitrary",),
            vmem_limit_bytes=_vmem_limit(est + (8 << 20))),
        name="ffn_proj",
    )(x2d, ffn_gain.reshape(1, D_MODEL), w_gate, w_up, w_down, mix_gain.reshape(1, D_MODEL),
      w_in.astype(jnp.bfloat16), qg, kg, _head_mean_matrix(V7X_MXU_DIM),
      next_w_gate, next_w_up, next_w_down, w_out)


def _t5_bucket(dist):
    n = np.maximum(dist, 0)
    max_exact = N_BUCKETS // 2
    large = max_exact + (np.log(np.maximum(n, 1) / max_exact)
                         / np.log(MAX_DISTANCE / max_exact)
                         * (N_BUCKETS - max_exact)).astype(np.int32)
    large = np.minimum(large, N_BUCKETS - 1)
    return np.where(n < max_exact, n, large).astype(np.int32)


def _band_bucket_table():
    ql = np.arange(BLOCK)[:, None]
    kl = np.arange(2 * BLOCK)[None, :]
    dist = ql + BLOCK - kl
    in_band = (dist >= 0) & (dist < WINDOW)
    return np.where(in_band, _t5_bucket(dist), -1).astype(np.int32)


def _bias_kernel(rel_ref, bucket_ref, o_ref):
    bucket = bucket_ref[...]
    for slot, head in enumerate(HEADS_PLAIN + HEADS_ROLLED):
        acc = jnp.full((BLOCK, 2 * BLOCK), NEG, jnp.float32)
        for b in range(N_BUCKETS):
            acc = jnp.where(bucket == b, rel_ref[b, head], acc)
        o_ref[slot] = acc


def _band_bias(rel_bias):
    return pl.pallas_call(
        _bias_kernel,
        out_shape=jax.ShapeDtypeStruct((N_HEADS, BLOCK, 2 * BLOCK), jnp.float32),
        in_specs=[
            pl.BlockSpec(memory_space=pltpu.SMEM),
            pl.BlockSpec(memory_space=pltpu.VMEM),
        ],
        out_specs=pl.BlockSpec(memory_space=pltpu.VMEM),
        name="band_bias",
    )(rel_bias, jnp.asarray(_band_bucket_table()))


def _window_sum_matrices():
    ql = np.arange(BLOCK)[:, None]
    kl = np.arange(2 * BLOCK)[None, :]
    dist = ql + BLOCK - kl
    mats = [((dist >= 0) & (dist < w)).astype(np.float32) for w in POOL_WINDOWS]
    return jnp.asarray(np.stack(mats), jnp.bfloat16)


def _block_diagonal(w):
    g, c, _ = w.shape
    on_diagonal = jnp.eye(g, dtype=bool)[:, None, :, None]
    return jnp.where(on_diagonal, w[:, :, None, :], jnp.zeros((), w.dtype)).reshape(g * c, g * c)


def _mix_stages(seq_first, t_base, sink_ref, x_ref, z_ref, zh_ref,
                bias_ref, wsum_ref, pw_ref, ps_ref, wo_ref, x2_ref, mix_scratch, g_ref, h_ref):
    kx_ref, kxr_ref, vx_ref, vxr_ref, ux_ref, pooled_ref, y_ref = mix_scratch
    tq = z_ref.shape[1]
    k_cols = slice(D_ATTN, D_ATTN + D_KV)
    v_cols = slice(D_ATTN + D_KV, D_ATTN + 2 * D_KV)
    u_cols = slice(D_ATTN + 2 * D_KV, D_IN)
    n_blocks = tq // BLOCK
    half = HEAD_DIM
    groups = ((HEADS_PLAIN, kx_ref, vx_ref, 0), (HEADS_ROLLED, kxr_ref, vxr_ref, len(HEADS_PLAIN)))
    lane = lax.broadcasted_iota(jnp.int32, (BLOCK, 2 * half), 1)
    low = lane < half
    state = {}

    def rows_of(j):
        return slice(j * BLOCK, (j + 1) * BLOCK)

    def keys_of(j):
        return slice(j * BLOCK, (j + 2) * BLOCK)

    def build_slabs():
        halo = jnp.where(seq_first, jnp.zeros_like(zh_ref[0, :, D_ATTN:]), zh_ref[0, :, D_ATTN:])
        kx_ref[:BLOCK] = halo[:, :D_KV]
        kx_ref[BLOCK:] = z_ref[0, :, k_cols]
        kxr_ref[...] = pltpu.roll(kx_ref[...].astype(jnp.float32), half, 1).astype(jnp.bfloat16)
        ones = jnp.ones((tq + BLOCK, D_KV), jnp.bfloat16)
        vx_ref[:BLOCK, :D_KV] = halo[:, D_KV:2 * D_KV]
        vx_ref[BLOCK:, :D_KV] = z_ref[0, :, v_cols]
        vx_ref[:, D_KV:] = ones
        vxr_ref[:, :D_KV] = pltpu.roll(
            vx_ref[:, :D_KV].astype(jnp.float32), half, 1).astype(jnp.bfloat16)
        vxr_ref[:, D_KV:] = ones
        ux_ref[:BLOCK] = halo[:, 2 * D_KV:]
        ux_ref[BLOCK:] = z_ref[0, :, u_cols]

    def scores(j, grp):
        heads, k_src, _, _ = groups[grp]
        q = z_ref[0, rows_of(j), :D_ATTN]
        zero = jnp.zeros((BLOCK, 2 * half), q.dtype)
        pieces = []
        for hd in heads:
            pair = q[:, (hd // 2) * 2 * half:(hd // 2 + 1) * 2 * half]
            pieces.append(jnp.where(low if hd % 2 == 0 else ~low, pair, zero))
        qs = jnp.concatenate(pieces, axis=0)
        state["logits", j, grp] = lax.dot_general(
            qs, k_src[keys_of(j), :], (((1,), (1,)), ((), ())),
            preferred_element_type=jnp.float32)

    def softmax_numerators(j, grp):
        heads, _, _, slot0 = groups[grp]
        logits = state.pop(("logits", j, grp))
        es, sink_terms = [], []
        for s, hd in enumerate(heads):
            lg = logits[s * BLOCK:(s + 1) * BLOCK] + bias_ref[slot0 + s]
            if j == 0:
                col = lax.broadcasted_iota(jnp.int32, (1, 2 * BLOCK), 1)
                lg = lg + jnp.where(seq_first & (col < BLOCK), NEG, 0.0).astype(jnp.float32)
            sink = sink_ref[hd]
            m = jnp.maximum(jnp.max(lg, axis=-1, keepdims=True), sink)
            es.append(jnp.exp(lg - m).astype(jnp.bfloat16))
            sink_terms.append(jnp.exp(sink - m))
        state["e", j, grp] = (jnp.concatenate(es, axis=0), sink_terms)

    def attend(j, grp):
        heads, _, v_src, _ = groups[grp]
        e, sink_terms = state.pop(("e", j, grp))
        pv = jnp.dot(e, v_src[keys_of(j), :], preferred_element_type=jnp.float32)
        for s, hd in enumerate(heads):
            blk = pv[s * BLOCK:(s + 1) * BLOCK]
            state["out", j, hd] = blk[:, :D_KV] / (blk[:, D_KV:] + sink_terms[s])

    def store_attention(j):
        for p in range(N_HEADS // 2):
            y_ref[rows_of(j), p * 2 * half:(p + 1) * 2 * half] = jnp.where(
                low, state.pop(("out", j, 2 * p)), state.pop(("out", j, 2 * p + 1))
            ).astype(jnp.bfloat16)

    def pool_means(j):
        row = lax.broadcasted_iota(jnp.int32, (BLOCK, 1), 0)
        t_glob = t_base + j * BLOCK + row
        for g, w in enumerate(POOL_WINDOWS):
            gc = slice(g * POOL_GROUP_DIM, (g + 1) * POOL_GROUP_DIM)
            wsum = jnp.dot(wsum_ref[g], ux_ref[keys_of(j), gc], preferred_element_type=jnp.float32)
            cnt = jnp.minimum(t_glob + 1, w).astype(jnp.float32)
            token = z_ref[0, rows_of(j), D_ATTN + 2 * D_KV + g * POOL_GROUP_DIM:
                          D_ATTN + 2 * D_KV + (g + 1) * POOL_GROUP_DIM]
            pooled = wsum * (1.0 / cnt) - token.astype(jnp.float32)
            pooled_ref[rows_of(j), gc] = pooled.astype(jnp.bfloat16)

    def pool_map():
        state["y_pool"] = jnp.dot(pooled_ref[...], pw_ref[...],
                                  preferred_element_type=jnp.float32) * ps_ref[...]

    def project_pool():
        x2_ref[...] = x_ref[0] + jnp.dot(state.pop("y_pool").astype(jnp.bfloat16), wo_ref[D_ATTN:, :],
                                         preferred_element_type=jnp.float32)

    def project_attention():
        x2 = x2_ref[...] + jnp.dot(y_ref[...], wo_ref[:D_ATTN, :], preferred_element_type=jnp.float32)
        x2_ref[...] = x2
        h_ref[...] = _rmsnorm_rows(x2, g_ref[...]).astype(jnp.bfloat16)

    def seq(*calls):
        def run():
            for fn, *args in calls:
                fn(*args)
        return run

    early = [
        seq((build_slabs,)),
        seq(*[(pool_means, j) for j in range(n_blocks // 2)]),
        seq(*[(pool_means, j) for j in range(n_blocks // 2, n_blocks)]),
        seq((scores, 0, 0), (scores, 0, 1), (softmax_numerators, 0, 0)),
        seq((pool_map,), (attend, 0, 0), (softmax_numerators, 0, 1)),
        seq((project_pool,), (scores, 1, 0), (attend, 0, 1), (store_attention, 0)),
    ]
    for j in range(1, n_blocks):
        early.append(seq((scores, j, 1), (softmax_numerators, j, 0), (attend, j, 0)))
        tail = [(softmax_numerators, j, 1)]
        if j + 1 < n_blocks:
            tail.append((scores, j + 1, 0))
        tail += [(attend, j, 1), (store_attention, j)]
        early.append(seq(*tail))
    late = [project_attention]
    return early, late


N_MIX_SCRATCH = 8


def _mix_ffn_kernel(sink_ref, x_ref, z_ref, zh_ref,
                    bias_ref, wsum_ref, pw_ref, ps_ref, wo_ref, g2_ref, wg_hbm, wu_hbm, wd_hbm,
                    o_ref, *scratch, tiles_per_seq):
    mix_scratch, x2_ref = scratch[:N_MIX_SCRATCH - 1], scratch[N_MIX_SCRATCH - 1]
    base_scratch = scratch[N_MIX_SCRATCH:N_MIX_SCRATCH + N_FFN_BASE_SCRATCH]
    copy_scratch = scratch[N_MIX_SCRATCH + N_FFN_BASE_SCRATCH:]
    h_ref = base_scratch[3]
    t = pl.program_id(0)
    tq = z_ref.shape[1]
    t_mix = jnp.minimum(t, pl.num_programs(0) - 2)
    i_seq = lax.rem(t_mix, tiles_per_seq)
    stages = functools.partial(
        _mix_stages, i_seq == 0, i_seq * tq, sink_ref, x_ref, z_ref, zh_ref,
        bias_ref, wsum_ref, pw_ref, ps_ref, wo_ref, x2_ref, mix_scratch, g2_ref, h_ref)
    w_hbm = (wg_hbm, wu_hbm, wd_hbm)

    def first_step():
        _ffn_start_weight_copies(w_hbm, base_scratch, copy_scratch)
        early, late = stages()
        for stage in early + late:
            stage()

    def later_step(wait_weights):
        early, late = stages()
        o_ref[0] = x2_ref[...]
        y = _swiglu_tile(None, g2_ref, w_hbm, base_scratch, "wait" if wait_weights else None,
                         copy_scratch, between=early, before_down=late)
        o_ref[0] += 0.5 * y

    pl.when(t == 0)(first_step)
    pl.when(t == 1)(functools.partial(later_step, True))
    pl.when(t > 1)(functools.partial(later_step, False))


def _mix_ffn(x3d, z3d, sinks, bias, pool_w, pool_scale, w_out_bf16, ffn_gain,
             w_gate_bf16, w_up_bf16, w_down_bf16):
    b, s, _ = x3d.shape
    tq = TILE_M
    nblk = tq // BLOCK
    tiles_per_seq = s // tq
    n_tiles = b * tiles_per_seq
    resident = functools.partial(pl.BlockSpec, pipeline_mode=pl.Buffered(1))
    hbm = pl.BlockSpec(memory_space=pltpu.HBM)

    def mixed(t):
        tm = jnp.minimum(t, n_tiles - 1)
        return tm // tiles_per_seq, lax.rem(tm, tiles_per_seq)

    def tile(width):
        return pl.BlockSpec((1, tq, width), lambda t: (*mixed(t), 0))

    def halo(width):
        def index(t):
            bi, i = mixed(t)
            return bi, jnp.maximum(i * nblk - 1, 0), 0
        return pl.BlockSpec((1, BLOCK, width), index)

    def out_index(t):
        tf = jnp.maximum(t - 1, 0)
        return tf // tiles_per_seq, lax.rem(tf, tiles_per_seq), 0

    est = (_ffn_vmem_bytes(tq)
           + 2 * 2 * tq * D_MODEL * 4 + 2 * (tq + BLOCK) * D_IN * 2
           + N_HEADS * BLOCK * 2 * BLOCK * 4 + 2 * D_MODEL * D_MODEL + 2 * D_POOL * D_POOL
           + 6 * (tq + BLOCK) * D_KV * 2 + (tq + BLOCK) * D_POOL * 2 + tq * D_MODEL * 2
           + tq * D_MODEL * 4 + 16 * BLOCK * 2 * BLOCK * 4 * 4)
    return pl.pallas_call(
        functools.partial(_mix_ffn_kernel, tiles_per_seq=tiles_per_seq),
        out_shape=jax.ShapeDtypeStruct((b, s, D_MODEL), jnp.float32),
        grid=(n_tiles + 1,),
        in_specs=[
            pl.BlockSpec(memory_space=pltpu.SMEM),
            tile(D_MODEL),
            tile(D_IN), halo(D_IN),
            resident((N_HEADS, BLOCK, 2 * BLOCK), lambda t: (0, 0, 0)),
            resident((len(POOL_WINDOWS), BLOCK, 2 * BLOCK), lambda t: (0, 0, 0)),
            resident((D_POOL, D_POOL), lambda t: (0, 0)),
            resident((1, D_POOL), lambda t: (0, 0)),
            resident((D_MODEL, D_MODEL), lambda t: (0, 0)),
            resident((1, D_MODEL), lambda t: (0, 0)),
            hbm, hbm, hbm,
        ],
        out_specs=pl.BlockSpec((1, tq, D_MODEL), out_index),
        scratch_shapes=[
            pltpu.VMEM((tq + BLOCK, D_KV), jnp.bfloat16),
            pltpu.VMEM((tq + BLOCK, D_KV), jnp.bfloat16),
            pltpu.VMEM((tq + BLOCK, 2 * D_KV), jnp.bfloat16),
            pltpu.VMEM((tq + BLOCK, 2 * D_KV), jnp.bfloat16),
            pltpu.VMEM((tq + BLOCK, D_POOL), jnp.bfloat16),
            pltpu.VMEM((tq, D_POOL), jnp.bfloat16),
            pltpu.VMEM((tq, D_ATTN), jnp.bfloat16),
            pltpu.VMEM((tq, D_MODEL), jnp.float32),
        ] + _ffn_base_scratch(tq) + _ffn_copy_scratch(),
        compiler_params=pltpu.CompilerParams(
            dimension_semantics=("arbitrary",),
            vmem_limit_bytes=_vmem_limit(est + (8 << 20))),
        name="mix_ffn",
    )(sinks, x3d, z3d, z3d, bias, _window_sum_matrices(),
      _block_diagonal(pool_w.astype(jnp.bfloat16)), pool_scale.reshape(1, D_POOL),
      w_out_bf16, ffn_gain.reshape(1, D_MODEL), w_gate_bf16, w_up_bf16, w_down_bf16)


def kernel(x, ffn1_norm, ffn1_w_gate, ffn1_w_up, ffn1_w_down, mix_norm, w_in, q_norm, k_norm,
           attn_sinks, rel_bias, pool_w, pool_scale, w_out, ffn2_norm, ffn2_w_gate, ffn2_w_up,
           ffn2_w_down):
    b, s, d = x.shape
    assert (d, s % TILE_M) == (D_MODEL, 0)
    bias = _band_bias(rel_bias)
    for l in range(ffn1_norm.shape[0]):
        x1, z, wg2, wu2, wd2, wo = _ffn_proj(
            x.reshape(b * s, d), ffn1_norm[l], ffn1_w_gate[l], ffn1_w_up[l], ffn1_w_down[l],
            mix_norm[l], w_in[l], q_norm[l], k_norm[l],
            ffn2_w_gate[l], ffn2_w_up[l], ffn2_w_down[l], w_out[l])
        x = _mix_ffn(x1.reshape(b, s, d), z.reshape(b, s, D_IN), attn_sinks[l], bias,
                     pool_w[l], pool_scale[l], wo, ffn2_norm[l], wg2, wu2, wd2)
    return x
```

```python
import functools

import numpy as np
import jax
import jax.numpy as jnp
from jax import lax
from jax.experimental import pallas as pl
from jax.experimental.pallas import tpu as pltpu

D_MODEL = 1024
HEAD_DIM = 64
N_HEADS = 8
N_KV_HEADS = 2
D_ATTN = N_HEADS * HEAD_DIM
D_KV = N_KV_HEADS * HEAD_DIM
D_POOL = D_MODEL - D_ATTN
POOL_WINDOWS = (2, 4, 8, 16)
POOL_GROUP_DIM = D_POOL // len(POOL_WINDOWS)
D_IN = D_ATTN + 2 * D_KV + D_POOL
WINDOW = 128
BLOCK = 128
N_BUCKETS = 32
MAX_DISTANCE = 128
D_FF = 2816
EPS = 1e-6
NEG = -1e30

V7X_LANES = 128
V7X_MXU_DIM = 256
V7X_VMEM_BYTES = 64 * 1024 * 1024

HEADS_PLAIN = (0, 2, 5, 7)
HEADS_ROLLED = (1, 3, 4, 6)

TILE_M = 512
FFN_CHUNK_F = 256
FFN_STAGE_SLOTS = 2


def _vmem_limit(estimate_bytes):
    return int(min(V7X_VMEM_BYTES - (4 << 20), max(estimate_bytes, 16 << 20)))


def _rmsnorm_rows(x32, gain_row):
    ms = jnp.mean(x32 * x32, axis=-1, keepdims=True)
    return x32 * lax.rsqrt(ms + EPS) * gain_row


N_FFN_CHUNKS = D_FF // FFN_CHUNK_F
N_FFN_BASE_SCRATCH = 5


def _ffn_base_scratch(tm):
    return [
        pltpu.VMEM((D_MODEL, D_FF), jnp.bfloat16),
        pltpu.VMEM((D_MODEL, D_FF), jnp.bfloat16),
        pltpu.VMEM((D_FF, D_MODEL), jnp.bfloat16),
        pltpu.VMEM((tm, D_MODEL), jnp.bfloat16),
        pltpu.VMEM((tm, D_FF), jnp.bfloat16),
    ]


def _ffn_round_scratch():
    return [
        pltpu.VMEM((2 * FFN_STAGE_SLOTS, D_MODEL, FFN_CHUNK_F), jnp.float32),
        pltpu.VMEM((FFN_STAGE_SLOTS, FFN_CHUNK_F, D_MODEL), jnp.float32),
        pltpu.SemaphoreType.DMA((3 * FFN_STAGE_SLOTS,)),
    ]


def _ffn_copy_scratch():
    return [pltpu.SemaphoreType.DMA((3, N_FFN_CHUNKS))]


def _ffn_vmem_bytes(tm):
    return (2 * 3 * D_MODEL * D_FF + 3 * FFN_STAGE_SLOTS * D_MODEL * FFN_CHUNK_F * 4
            + tm * D_MODEL * 2 + tm * D_FF * 2 + 6 * tm * FFN_CHUNK_F * 4 + tm * D_MODEL * 4)


def _ffn_window(c):
    return pl.ds(c * FFN_CHUNK_F, FFN_CHUNK_F)


def _ffn_weight_copy(which, c, w_hbm, base_scratch, copy_scratch):
    (sem_ref,) = copy_scratch
    src, dst = w_hbm[which], base_scratch[which]
    if which == 2:
        return pltpu.make_async_copy(src.at[_ffn_window(c), :], dst.at[_ffn_window(c), :],
                                     sem_ref.at[which, c])
    return pltpu.make_async_copy(src.at[:, _ffn_window(c)], dst.at[:, _ffn_window(c)],
                                 sem_ref.at[which, c])


def _ffn_start_weight_copies(w_hbm, base_scratch, copy_scratch):
    for c in range(N_FFN_CHUNKS):
        for which in range(3):
            _ffn_weight_copy(which, c, w_hbm, base_scratch, copy_scratch).start()


def _swiglu_tile(x32, g_ref, w_hbm, base_scratch, stream, stream_scratch, between=(), before_down=()):
    wg_hbm, wu_hbm, wd_hbm = w_hbm
    wg_ref, wu_ref, wd_ref, h_ref, a_ref = base_scratch

    def staged_copy(which, c):
        stage_in_ref, stage_out_ref, sem_ref = stream_scratch
        slot = c % FFN_STAGE_SLOTS
        sem = sem_ref.at[which * FFN_STAGE_SLOTS + slot]
        if which == 2:
            return pltpu.make_async_copy(wd_hbm.at[_ffn_window(c), :], stage_out_ref.at[slot], sem)
        src = (wg_hbm, wu_hbm)[which]
        return pltpu.make_async_copy(src.at[:, _ffn_window(c)],
                                     stage_in_ref.at[which * FFN_STAGE_SLOTS + slot], sem)

    def land(which, c):
        if stream == "wait":
            _ffn_weight_copy(which, c, w_hbm, base_scratch, stream_scratch).wait()
            return
        staged_copy(which, c).wait()
        stage_in_ref, stage_out_ref, _ = stream_scratch
        slot = c % FFN_STAGE_SLOTS
        rows_or_cols = slice(c * FFN_CHUNK_F, (c + 1) * FFN_CHUNK_F)
        if which == 2:
            wd_ref[rows_or_cols, :] = stage_out_ref[slot].astype(jnp.bfloat16)
        else:
            dst = (wg_ref, wu_ref)[which]
            dst[:, rows_or_cols] = stage_in_ref[which * FFN_STAGE_SLOTS + slot].astype(jnp.bfloat16)
        if c + FFN_STAGE_SLOTS < N_FFN_CHUNKS:
            staged_copy(which, c + FFN_STAGE_SLOTS).start()

    if stream == "round":
        for c in range(FFN_STAGE_SLOTS):
            for which in range(3):
                staged_copy(which, c).start()
    if x32 is not None:
        h_ref[...] = _rmsnorm_rows(x32, g_ref[...]).astype(jnp.bfloat16)
    for c in range(N_FFN_CHUNKS):
        cols = slice(c * FFN_CHUNK_F, (c + 1) * FFN_CHUNK_F)
        if stream is not None:
            for which in range(3 if stream == "round" else 2):
                land(which, c)
        h = h_ref[...]
        gate = jnp.dot(h, wg_ref[:, cols], preferred_element_type=jnp.float32)
        up = jnp.dot(h, wu_ref[:, cols], preferred_element_type=jnp.float32)
        act = gate * (1.0 / (1.0 + jnp.exp(-gate))) * up
        a_ref[:, cols] = act.astype(jnp.bfloat16)
        if c < len(between):
            between[c]()
    for stage in tuple(between[N_FFN_CHUNKS:]) + tuple(before_down):
        stage()
    if stream == "wait":
        for c in range(N_FFN_CHUNKS):
            land(2, c)
    return jnp.dot(a_ref[...], wd_ref[...], preferred_element_type=jnp.float32)


def _ffn_proj_kernel(x_ref, g1_ref, wg_hbm, wu_hbm, wd_hbm, gm_ref, win_ref, qg_ref, kg_ref, dm_ref,
                     wg2_ref, wu2_ref, wd2_ref, wo_ref,
                     x1_ref, z_ref, wg2_bf_ref, wu2_bf_ref, wd2_bf_ref, wo_bf_ref,
                     *scratch):
    tm = x_ref.shape[0]
    base_scratch, round_scratch = scratch[:N_FFN_BASE_SCRATCH], scratch[N_FFN_BASE_SCRATCH:]

    first_half = pl.program_id(0) < pl.num_programs(0) // 2

    @pl.when(first_half)
    def _():
        wg2_bf_ref[...] = wg2_ref[...].astype(jnp.bfloat16)
        wd2_bf_ref[...] = wd2_ref[...].astype(jnp.bfloat16)

    @pl.when(jnp.logical_not(first_half))
    def _():
        wu2_bf_ref[...] = wu2_ref[...].astype(jnp.bfloat16)
        wo_bf_ref[...] = wo_ref[...].astype(jnp.bfloat16)

    def body(stream):
        y = _swiglu_tile(x_ref[...], g1_ref, (wg_hbm, wu_hbm, wd_hbm), base_scratch, stream,
                         round_scratch)
        x1 = x_ref[...] + 0.5 * y
        x1_ref[...] = x1
        h = _rmsnorm_rows(x1, gm_ref[...]).astype(jnp.bfloat16)
        z = jnp.dot(h, win_ref[...], preferred_element_type=jnp.float32)
        n_slab = (D_ATTN + 2 * D_KV) // V7X_MXU_DIM
        sq = jnp.concatenate(
            [z[:, i * V7X_MXU_DIM:(i + 1) * V7X_MXU_DIM] for i in range(n_slab)], axis=0)
        ms = jnp.dot((sq * sq).astype(jnp.bfloat16), dm_ref[...], preferred_element_type=jnp.float32)
        q_ms = jnp.concatenate([ms[i * tm:(i + 1) * tm] for i in range(n_slab - 1)], axis=1)
        k_ms = ms[(n_slab - 1) * tm:, :D_KV]
        q = z[:, :D_ATTN]
        k = z[:, D_ATTN:D_ATTN + D_KV]
        z_ref[:, :D_ATTN] = (q * lax.rsqrt(q_ms + EPS) * qg_ref[...]).astype(jnp.bfloat16)
        z_ref[:, D_ATTN:D_ATTN + D_KV] = (k * lax.rsqrt(k_ms + EPS) * kg_ref[...]).astype(jnp.bfloat16)
        z_ref[:, D_ATTN + D_KV:] = z[:, D_ATTN + D_KV:].astype(jnp.bfloat16)

    first = pl.program_id(0) == 0
    pl.when(first)(functools.partial(body, "round"))
    pl.when(jnp.logical_not(first))(functools.partial(body, None))


def _head_mean_matrix(width):
    idx = np.arange(width) // HEAD_DIM
    return jnp.asarray((idx[:, None] == idx[None, :]).astype(np.float32) / HEAD_DIM, jnp.bfloat16)


def _ffn_proj(x2d, ffn_gain, w_gate, w_up, w_down, mix_gain, w_in, q_gain, k_gain,
              next_w_gate, next_w_up, next_w_down, w_out):
    m = x2d.shape[0]
    tm = TILE_M
    n_steps = m // tm
    resident = functools.partial(pl.BlockSpec, pipeline_mode=pl.Buffered(1))
    rows = lambda width: pl.BlockSpec((tm, width), lambda i: (i, 0))
    hbm = pl.BlockSpec(memory_space=pltpu.HBM)
    half_steps = n_steps // 2
    wr = D_MODEL // half_steps
    wdr = D_FF // half_steps
    assert 2 * half_steps == n_steps and wr * half_steps == D_MODEL and wdr * half_steps == D_FF
    assert wr % 16 == 0 and wdr % 16 == 0
    early_slice = lambda i: (jnp.minimum(i, half_steps - 1), 0)
    late_slice = lambda i: (jnp.maximum(i - half_steps, 0), 0)
    wg_rows = pl.BlockSpec((wr, D_FF), early_slice)
    wd_rows = pl.BlockSpec((wdr, D_MODEL), early_slice)
    wu_rows = pl.BlockSpec((wr, D_FF), late_slice)
    wo_rows = pl.BlockSpec((wr, D_MODEL), late_slice)
    qg = (jnp.tile(q_gain, N_HEADS) * (HEAD_DIM ** -0.5)).reshape(1, D_ATTN)
    kg = jnp.tile(k_gain, N_KV_HEADS).reshape(1, D_KV)
    est = (_ffn_vmem_bytes(tm) + 2 * D_MODEL * D_IN
           + 2 * tm * D_MODEL * 4 * 2
           + 2 * tm * D_IN * 2
           + 4 * tm * D_IN * 4
           + 2 * 6 * (2 * wr * D_FF + wdr * D_MODEL + wr * D_MODEL))
    bf16 = lambda shape: jax.ShapeDtypeStruct(shape, jnp.bfloat16)
    return pl.pallas_call(
        _ffn_proj_kernel,
        out_shape=(
            jax.ShapeDtypeStruct((m, D_MODEL), jnp.float32),
            bf16((m, D_IN)),
            bf16((D_MODEL, D_FF)), bf16((D_MODEL, D_FF)), bf16((D_FF, D_MODEL)),
            bf16((D_MODEL, D_MODEL)),
        ),
        grid=(n_steps,),
        in_specs=[
            rows(D_MODEL),
            resident((1, D_MODEL), lambda i: (0, 0)),
            hbm, hbm, hbm,
            resident((1, D_MODEL), lambda i: (0, 0)),
            resident((D_MODEL, D_IN), lambda i: (0, 0)),
            resident((1, D_ATTN), lambda i: (0, 0)),
            resident((1, D_KV), lambda i: (0, 0)),
            resident((V7X_MXU_DIM, V7X_MXU_DIM), lambda i: (0, 0)),
            wg_rows, wu_rows, wd_rows, wo_rows,
        ],
        out_specs=(rows(D_MODEL), rows(D_IN), wg_rows, wu_rows, wd_rows, wo_rows),
        scratch_shapes=_ffn_base_scratch(tm) + _ffn_round_scratch(),
        compiler_params=pltpu.CompilerParams(
            dimension_semantics=("arbitrary",),
            vmem_limit_bytes=_vmem_limit(est + (8 << 20))),
        name="ffn_proj",
    )(x2d, ffn_gain.reshape(1, D_MODEL), w_gate, w_up, w_down, mix_gain.reshape(1, D_MODEL),
      w_in.astype(jnp.bfloat16), qg, kg, _head_mean_matrix(V7X_MXU_DIM),
      next_w_gate, next_w_up, next_w_down, w_out)


def _t5_bucket(dist):
    n = np.maximum(dist, 0)
    max_exact = N_BUCKETS // 2
    large = max_exact + (np.log(np.maximum(n, 1) / max_exact)
                         / np.log(MAX_DISTANCE / max_exact)
                         * (N_BUCKETS - max_exact)).astype(np.int32)
    large = np.minimum(large, N_BUCKETS - 1)
    return np.where(n < max_exact, n, large).astype(np.int32)


def _band_bucket_table():
    ql = np.arange(BLOCK)[:, None]
    kl = np.arange(2 * BLOCK)[None, :]
    dist = ql + BLOCK - kl
    in_band = (dist >= 0) & (dist < WINDOW)
    return np.where(in_band, _t5_bucket(dist), -1).astype(np.int32)


def _bias_kernel(rel_ref, bucket_ref, o_ref):
    bucket = bucket_ref[...]
    for slot, head in enumerate(HEADS_PLAIN + HEADS_ROLLED):
        acc = jnp.full((BLOCK, 2 * BLOCK), NEG, jnp.float32)
        for b in range(N_BUCKETS):
            acc = jnp.where(bucket == b, rel_ref[b, head], acc)
        o_ref[slot] = acc


def _band_bias(rel_bias):
    return pl.pallas_call(
        _bias_kernel,
        out_shape=jax.ShapeDtypeStruct((N_HEADS, BLOCK, 2 * BLOCK), jnp.float32),
        in_specs=[
            pl.BlockSpec(memory_space=pltpu.SMEM),
            pl.BlockSpec(memory_space=pltpu.VMEM),
        ],
        out_specs=pl.BlockSpec(memory_space=pltpu.VMEM),
        name="band_bias",
    )(rel_bias, jnp.asarray(_band_bucket_table()))


def _window_sum_matrices():
    ql = np.arange(BLOCK)[:, None]
    kl = np.arange(2 * BLOCK)[None, :]
    dist = ql + BLOCK - kl
    mats = [((dist >= 0) & (dist < w)).astype(np.float32) for w in POOL_WINDOWS]
    return jnp.asarray(np.stack(mats), jnp.bfloat16)


def _block_diagonal(w):
    g, c, _ = w.shape
    on_diagonal = jnp.eye(g, dtype=bool)[:, None, :, None]
    return jnp.where(on_diagonal, w[:, :, None, :], jnp.zeros((), w.dtype)).reshape(g * c, g * c)


def _mix_stages(seq_first, t_base, sink_ref, x_ref, z_ref, zh_ref,
                bias_ref, wsum_ref, wo_ref, x2_ref, mix_scratch, g_ref, h_ref):
    kx_ref, kxr_ref, vx_ref, vxr_ref, ux_ref, pooled_ref, y_ref, wc_ref = mix_scratch
    tq = z_ref.shape[1]
    k_cols = slice(D_ATTN, D_ATTN + D_KV)
    v_cols = slice(D_ATTN + D_KV, D_ATTN + 2 * D_KV)
    u_cols = slice(D_ATTN + 2 * D_KV, D_IN)
    n_blocks = tq // BLOCK
    half = HEAD_DIM
    groups = ((HEADS_PLAIN, kx_ref, vx_ref, 0), (HEADS_ROLLED, kxr_ref, vxr_ref, len(HEADS_PLAIN)))
    lane = lax.broadcasted_iota(jnp.int32, (BLOCK, 2 * half), 1)
    low = lane < half
    state = {}

    def rows_of(j):
        return slice(j * BLOCK, (j + 1) * BLOCK)

    def keys_of(j):
        return slice(j * BLOCK, (j + 2) * BLOCK)

    def build_slabs():
        halo = jnp.where(seq_first, jnp.zeros_like(zh_ref[0, :, D_ATTN:]), zh_ref[0, :, D_ATTN:])
        kx_ref[:BLOCK] = halo[:, :D_KV]
        kx_ref[BLOCK:] = z_ref[0, :, k_cols]
        kxr_ref[...] = pltpu.roll(kx_ref[...].astype(jnp.float32), half, 1).astype(jnp.bfloat16)
        ones = jnp.ones((tq + BLOCK, D_KV), jnp.bfloat16)
        vx_ref[:BLOCK, :D_KV] = halo[:, D_KV:2 * D_KV]
        vx_ref[BLOCK:, :D_KV] = z_ref[0, :, v_cols]
        vx_ref[:, D_KV:] = ones
        vxr_ref[:, :D_KV] = pltpu.roll(
            vx_ref[:, :D_KV].astype(jnp.float32), half, 1).astype(jnp.bfloat16)
        vxr_ref[:, D_KV:] = ones
        ux_ref[:BLOCK] = halo[:, 2 * D_KV:]
        ux_ref[BLOCK:] = z_ref[0, :, u_cols]

    def scores(j, grp):
        heads, k_src, _, _ = groups[grp]
        q = z_ref[0, rows_of(j), :D_ATTN]
        zero = jnp.zeros((BLOCK, 2 * half), q.dtype)
        pieces = []
        for hd in heads:
            pair = q[:, (hd // 2) * 2 * half:(hd // 2 + 1) * 2 * half]
            pieces.append(jnp.where(low if hd % 2 == 0 else ~low, pair, zero))
        qs = jnp.concatenate(pieces, axis=0)
        state["logits", j, grp] = lax.dot_general(
            qs, k_src[keys_of(j), :], (((1,), (1,)), ((), ())),
            preferred_element_type=jnp.float32)

    def softmax_numerators(j, grp):
        heads, _, _, slot0 = groups[grp]
        logits = state.pop(("logits", j, grp))
        es, sink_terms = [], []
        for s, hd in enumerate(heads):
            lg = logits[s * BLOCK:(s + 1) * BLOCK] + bias_ref[slot0 + s]
            if j == 0:
                col = lax.broadcasted_iota(jnp.int32, (1, 2 * BLOCK), 1)
                lg = lg + jnp.where(seq_first & (col < BLOCK), NEG, 0.0).astype(jnp.float32)
            sink = sink_ref[hd]
            m = jnp.maximum(jnp.max(lg, axis=-1, keepdims=True), sink)
            es.append(jnp.exp(lg - m).astype(jnp.bfloat16))
            sink_terms.append(jnp.exp(sink - m))
        state["e", j, grp] = (jnp.concatenate(es, axis=0), sink_terms)

    def attend(j, grp):
        heads, _, v_src, _ = groups[grp]
        e, sink_terms = state.pop(("e", j, grp))
        pv = jnp.dot(e, v_src[keys_of(j), :], preferred_element_type=jnp.float32)
        for s, hd in enumerate(heads):
            blk = pv[s * BLOCK:(s + 1) * BLOCK]
            state["out", j, hd] = blk[:, :D_KV] / (blk[:, D_KV:] + sink_terms[s])

    def store_attention(j):
        for p in range(N_HEADS // 2):
            y_ref[rows_of(j), p * 2 * half:(p + 1) * 2 * half] = jnp.where(
                low, state.pop(("out", j, 2 * p)), state.pop(("out", j, 2 * p + 1))
            ).astype(jnp.bfloat16)

    def pool_means(j):
        row = lax.broadcasted_iota(jnp.int32, (BLOCK, 1), 0)
        t_glob = t_base + j * BLOCK + row
        for g, w in enumerate(POOL_WINDOWS):
            gc = slice(g * POOL_GROUP_DIM, (g + 1) * POOL_GROUP_DIM)
            wsum = jnp.dot(wsum_ref[g], ux_ref[keys_of(j), gc], preferred_element_type=jnp.float32)
            cnt = jnp.minimum(t_glob + 1, w).astype(jnp.float32)
            token = z_ref[0, rows_of(j), D_ATTN + 2 * D_KV + g * POOL_GROUP_DIM:
                          D_ATTN + 2 * D_KV + (g + 1) * POOL_GROUP_DIM]
            pooled = wsum * (1.0 / cnt) - token.astype(jnp.float32)
            pooled_ref[rows_of(j), gc] = pooled.astype(jnp.bfloat16)

    def project_pool():
        x2_ref[...] = x_ref[0] + jnp.dot(pooled_ref[...], wc_ref[...],
                                         preferred_element_type=jnp.float32)

    def project_attention():
        x2 = x2_ref[...] + jnp.dot(y_ref[...], wo_ref[:D_ATTN, :], preferred_element_type=jnp.float32)
        x2_ref[...] = x2
        h_ref[...] = _rmsnorm_rows(x2, g_ref[...]).astype(jnp.bfloat16)

    def seq(*calls):
        def run():
            for fn, *args in calls:
                fn(*args)
        return run

    early = [
        seq((build_slabs,)),
        seq(*[(pool_means, j) for j in range(n_blocks // 2)]),
        seq(*[(pool_means, j) for j in range(n_blocks // 2, n_blocks)]),
        seq((scores, 0, 0), (scores, 0, 1), (softmax_numerators, 0, 0)),
        seq((attend, 0, 0), (softmax_numerators, 0, 1)),
        seq((project_pool,), (scores, 1, 0), (attend, 0, 1), (store_attention, 0)),
    ]
    for j in range(1, n_blocks):
        early.append(seq((scores, j, 1), (softmax_numerators, j, 0), (attend, j, 0)))
        tail = [(softmax_numerators, j, 1)]
        if j + 1 < n_blocks:
            tail.append((scores, j + 1, 0))
        tail += [(attend, j, 1), (store_attention, j)]
        early.append(seq(*tail))
    late = [project_attention]
    return early, late


N_MIX_SCRATCH = 9


def _mix_ffn_kernel(sink_ref, x_ref, z_ref, zh_ref,
                    bias_ref, wsum_ref, pw_ref, ps_ref, wo_ref, g2_ref, wg_hbm, wu_hbm, wd_hbm,
                    o_ref, *scratch, tiles_per_seq):
    mix_scratch, x2_ref = scratch[:N_MIX_SCRATCH - 1], scratch[N_MIX_SCRATCH - 1]
    base_scratch = scratch[N_MIX_SCRATCH:N_MIX_SCRATCH + N_FFN_BASE_SCRATCH]
    copy_scratch = scratch[N_MIX_SCRATCH + N_FFN_BASE_SCRATCH:]
    h_ref = base_scratch[3]
    t = pl.program_id(0)
    tq = z_ref.shape[1]
    t_mix = jnp.minimum(t, pl.num_programs(0) - 2)
    i_seq = lax.rem(t_mix, tiles_per_seq)
    stages = functools.partial(
        _mix_stages, i_seq == 0, i_seq * tq, sink_ref, x_ref, z_ref, zh_ref,
        bias_ref, wsum_ref, wo_ref, x2_ref, mix_scratch, g2_ref, h_ref)
    w_hbm = (wg_hbm, wu_hbm, wd_hbm)

    def first_step():
        _ffn_start_weight_copies(w_hbm, base_scratch, copy_scratch)
        pool_map = (pw_ref[...] * ps_ref[...]).astype(jnp.bfloat16)
        mix_scratch[-1][...] = jnp.dot(pool_map, wo_ref[D_ATTN:, :],
                                       preferred_element_type=jnp.float32).astype(jnp.bfloat16)
        early, late = stages()
        for stage in early + late:
            stage()

    def later_step(wait_weights):
        early, late = stages()
        o_ref[0] = x2_ref[...]
        y = _swiglu_tile(None, g2_ref, w_hbm, base_scratch, "wait" if wait_weights else None,
                         copy_scratch, between=early, before_down=late)
        o_ref[0] += 0.5 * y

    pl.when(t == 0)(first_step)
    pl.when(t == 1)(functools.partial(later_step, True))
    pl.when(t > 1)(functools.partial(later_step, False))


def _mix_ffn(x3d, z3d, sinks, bias, pool_w, pool_scale, w_out_bf16, ffn_gain,
             w_gate_bf16, w_up_bf16, w_down_bf16):
    b, s, _ = x3d.shape
    tq = TILE_M
    nblk = tq // BLOCK
    tiles_per_seq = s // tq
    n_tiles = b * tiles_per_seq
    resident = functools.partial(pl.BlockSpec, pipeline_mode=pl.Buffered(1))
    hbm = pl.BlockSpec(memory_space=pltpu.HBM)

    def mixed(t):
        tm = jnp.minimum(t, n_tiles - 1)
        return tm // tiles_per_seq, lax.rem(tm, tiles_per_seq)

    def tile(width):
        return pl.BlockSpec((1, tq, width), lambda t: (*mixed(t), 0))

    def halo(width):
        def index(t):
            bi, i = mixed(t)
            return bi, jnp.maximum(i * nblk - 1, 0), 0
        return pl.BlockSpec((1, BLOCK, width), index)

    def out_index(t):
        tf = jnp.maximum(t - 1, 0)
        return tf // tiles_per_seq, lax.rem(tf, tiles_per_seq), 0

    est = (_ffn_vmem_bytes(tq)
           + 2 * 2 * tq * D_MODEL * 4 + 2 * (tq + BLOCK) * D_IN * 2
           + N_HEADS * BLOCK * 2 * BLOCK * 4 + 2 * D_MODEL * D_MODEL + 4 * D_POOL * D_POOL
           + 2 * D_POOL * D_MODEL
           + 6 * (tq + BLOCK) * D_KV * 2 + (tq + BLOCK) * D_POOL * 2 + tq * D_MODEL * 2
           + tq * D_MODEL * 4 + 16 * BLOCK * 2 * BLOCK * 4 * 4)
    return pl.pallas_call(
        functools.partial(_mix_ffn_kernel, tiles_per_seq=tiles_per_seq),
        out_shape=jax.ShapeDtypeStruct((b, s, D_MODEL), jnp.float32),
        grid=(n_tiles + 1,),
        in_specs=[
            pl.BlockSpec(memory_space=pltpu.SMEM),
            tile(D_MODEL),
            tile(D_IN), halo(D_IN),
            resident((N_HEADS, BLOCK, 2 * BLOCK), lambda t: (0, 0, 0)),
            resident((len(POOL_WINDOWS), BLOCK, 2 * BLOCK), lambda t: (0, 0, 0)),
            resident((D_POOL, D_POOL), lambda t: (0, 0)),
            resident((1, D_POOL), lambda t: (0, 0)),
            resident((D_MODEL, D_MODEL), lambda t: (0, 0)),
            resident((1, D_MODEL), lambda t: (0, 0)),
            hbm, hbm, hbm,
        ],
        out_specs=pl.BlockSpec((1, tq, D_MODEL), out_index),
        scratch_shapes=[
            pltpu.VMEM((tq + BLOCK, D_KV), jnp.bfloat16),
            pltpu.VMEM((tq + BLOCK, D_KV), jnp.bfloat16),
            pltpu.VMEM((tq + BLOCK, 2 * D_KV), jnp.bfloat16),
            pltpu.VMEM((tq + BLOCK, 2 * D_KV), jnp.bfloat16),
            pltpu.VMEM((tq + BLOCK, D_POOL), jnp.bfloat16),
            pltpu.VMEM((tq, D_POOL), jnp.bfloat16),
            pltpu.VMEM((tq, D_ATTN), jnp.bfloat16),
            pltpu.VMEM((D_POOL, D_MODEL), jnp.bfloat16),
            pltpu.VMEM((tq, D_MODEL), jnp.float32),
        ] + _ffn_base_scratch(tq) + _ffn_copy_scratch(),
        compiler_params=pltpu.CompilerParams(
            dimension_semantics=("arbitrary",),
            vmem_limit_bytes=_vmem_limit(est + (8 << 20))),
        name="mix_ffn",
    )(sinks, x3d, z3d, z3d, bias, _window_sum_matrices(),
      _block_diagonal(pool_w), pool_scale.reshape(1, D_POOL),
      w_out_bf16, ffn_gain.reshape(1, D_MODEL), w_gate_bf16, w_up_bf16, w_down_bf16)


def kernel(x, ffn1_norm, ffn1_w_gate, ffn1_w_up, ffn1_w_down, mix_norm, w_in, q_norm, k_norm,
           attn_sinks, rel_bias, pool_w, pool_scale, w_out, ffn2_norm, ffn2_w_gate, ffn2_w_up,
           ffn2_w_down):
    b, s, d = x.shape
    assert (d, s % TILE_M) == (D_MODEL, 0)
    bias = _band_bias(rel_bias)
    for l in range(ffn1_norm.shape[0]):
        x1, z, wg2, wu2, wd2, wo = _ffn_proj(
            x.reshape(b * s, d), ffn1_norm[l], ffn1_w_gate[l], ffn1_w_up[l], ffn1_w_down[l],
            mix_norm[l], w_in[l], q_norm[l], k_norm[l],
            ffn2_w_gate[l], ffn2_w_up[l], ffn2_w_down[l], w_out[l])
        x = _mix_ffn(x1.reshape(b, s, d), z.reshape(b, s, D_IN), attn_sinks[l], bias,
                     pool_w[l], pool_scale[l], wo, ffn2_norm[l], wg2, wu2, wd2)
    return x
```

```python
import functools

import numpy as np
import jax
import jax.numpy as jnp
from jax import lax
from jax.experimental import pallas as pl
from jax.experimental.pallas import tpu as pltpu

D_MODEL = 1024
HEAD_DIM = 64
N_HEADS = 8
N_KV_HEADS = 2
D_ATTN = N_HEADS * HEAD_DIM
D_KV = N_KV_HEADS * HEAD_DIM
D_POOL = D_MODEL - D_ATTN
POOL_WINDOWS = (2, 4, 8, 16)
POOL_GROUP_DIM = D_POOL // len(POOL_WINDOWS)
POOL_HALO = 16
D_IN = D_ATTN + 2 * D_KV + D_POOL
WINDOW = 128
BLOCK = 128
N_BUCKETS = 32
MAX_DISTANCE = 128
D_FF = 2816
EPS = 1e-6
NEG = -1e30

V7X_LANES = 128
V7X_MXU_DIM = 256
V7X_VMEM_BYTES = 64 * 1024 * 1024

HEADS_PLAIN = (0, 2, 5, 7)
HEADS_ROLLED = (1, 3, 4, 6)

TILE_M = 512
FFN_CHUNK_F = 256
FFN_STAGE_SLOTS = 2


def _vmem_limit(estimate_bytes):
    return int(min(V7X_VMEM_BYTES - (4 << 20), max(estimate_bytes, 16 << 20)))


def _rmsnorm_rows(x32, gain_row):
    ms = jnp.mean(x32 * x32, axis=-1, keepdims=True)
    return x32 * lax.rsqrt(ms + EPS) * gain_row


N_FFN_CHUNKS = D_FF // FFN_CHUNK_F
N_FFN_BASE_SCRATCH = 5


def _ffn_base_scratch(tm):
    return [
        pltpu.VMEM((D_MODEL, D_FF), jnp.bfloat16),
        pltpu.VMEM((D_MODEL, D_FF), jnp.bfloat16),
        pltpu.VMEM((D_FF, D_MODEL), jnp.bfloat16),
        pltpu.VMEM((tm, D_MODEL), jnp.bfloat16),
        pltpu.VMEM((tm, D_FF), jnp.bfloat16),
    ]


def _ffn_round_scratch():
    return [
        pltpu.VMEM((2 * FFN_STAGE_SLOTS, D_MODEL, FFN_CHUNK_F), jnp.float32),
        pltpu.VMEM((FFN_STAGE_SLOTS, FFN_CHUNK_F, D_MODEL), jnp.float32),
        pltpu.SemaphoreType.DMA((3 * FFN_STAGE_SLOTS,)),
    ]


def _ffn_copy_scratch():
    return [pltpu.SemaphoreType.DMA((3, N_FFN_CHUNKS))]


def _ffn_vmem_bytes(tm):
    return (2 * 3 * D_MODEL * D_FF + 3 * FFN_STAGE_SLOTS * D_MODEL * FFN_CHUNK_F * 4
            + tm * D_MODEL * 2 + tm * D_FF * 2 + 6 * tm * FFN_CHUNK_F * 4 + tm * D_MODEL * 4)


def _ffn_window(c):
    return pl.ds(c * FFN_CHUNK_F, FFN_CHUNK_F)


def _ffn_weight_copy(which, c, w_hbm, base_scratch, copy_scratch):
    (sem_ref,) = copy_scratch
    src, dst = w_hbm[which], base_scratch[which]
    if which == 2:
        return pltpu.make_async_copy(src.at[_ffn_window(c), :], dst.at[_ffn_window(c), :],
                                     sem_ref.at[which, c])
    return pltpu.make_async_copy(src.at[:, _ffn_window(c)], dst.at[:, _ffn_window(c)],
                                 sem_ref.at[which, c])


def _ffn_start_weight_copies(w_hbm, base_scratch, copy_scratch):
    for c in range(N_FFN_CHUNKS):
        for which in range(3):
            _ffn_weight_copy(which, c, w_hbm, base_scratch, copy_scratch).start()


def _swiglu_tile(x32, g_ref, w_hbm, base_scratch, stream, stream_scratch, between=(), before_down=()):
    wg_hbm, wu_hbm, wd_hbm = w_hbm
    wg_ref, wu_ref, wd_ref, h_ref, a_ref = base_scratch

    def staged_copy(which, c):
        stage_in_ref, stage_out_ref, sem_ref = stream_scratch
        slot = c % FFN_STAGE_SLOTS
        sem = sem_ref.at[which * FFN_STAGE_SLOTS + slot]
        if which == 2:
            return pltpu.make_async_copy(wd_hbm.at[_ffn_window(c), :], stage_out_ref.at[slot], sem)
        src = (wg_hbm, wu_hbm)[which]
        return pltpu.make_async_copy(src.at[:, _ffn_window(c)],
                                     stage_in_ref.at[which * FFN_STAGE_SLOTS + slot], sem)

    def land(which, c):
        if stream == "wait":
            _ffn_weight_copy(which, c, w_hbm, base_scratch, stream_scratch).wait()
            return
        staged_copy(which, c).wait()
        stage_in_ref, stage_out_ref, _ = stream_scratch
        slot = c % FFN_STAGE_SLOTS
        rows_or_cols = slice(c * FFN_CHUNK_F, (c + 1) * FFN_CHUNK_F)
        if which == 2:
            wd_ref[rows_or_cols, :] = stage_out_ref[slot].astype(jnp.bfloat16)
        else:
            dst = (wg_ref, wu_ref)[which]
            dst[:, rows_or_cols] = stage_in_ref[which * FFN_STAGE_SLOTS + slot].astype(jnp.bfloat16)
        if c + FFN_STAGE_SLOTS < N_FFN_CHUNKS:
            staged_copy(which, c + FFN_STAGE_SLOTS).start()

    if stream == "round":
        for c in range(FFN_STAGE_SLOTS):
            for which in range(3):
                staged_copy(which, c).start()
    if x32 is not None:
        h_ref[...] = _rmsnorm_rows(x32, g_ref[...]).astype(jnp.bfloat16)
    for c in range(N_FFN_CHUNKS):
        cols = slice(c * FFN_CHUNK_F, (c + 1) * FFN_CHUNK_F)
        if stream is not None:
            for which in range(3 if stream == "round" else 2):
                land(which, c)
        h = h_ref[...]
        gate = jnp.dot(h, wg_ref[:, cols], preferred_element_type=jnp.float32)
        up = jnp.dot(h, wu_ref[:, cols], preferred_element_type=jnp.float32)
        act = gate * (1.0 / (1.0 + jnp.exp(-gate))) * up
        a_ref[:, cols] = act.astype(jnp.bfloat16)
        if c < len(between):
            between[c]()
    for stage in tuple(between[N_FFN_CHUNKS:]) + tuple(before_down):
        stage()
    if stream == "wait":
        for c in range(N_FFN_CHUNKS):
            land(2, c)
    return jnp.dot(a_ref[...], wd_ref[...], preferred_element_type=jnp.float32)


def _ffn_proj_kernel(x_ref, g1_ref, wg_hbm, wu_hbm, wd_hbm, gm_ref, win_ref, qg_ref, kg_ref, dm_ref,
                     wg2_ref, wu2_ref, wd2_ref, wo_ref,
                     x1_ref, z_ref, wg2_bf_ref, wu2_bf_ref, wd2_bf_ref, wo_bf_ref,
                     *scratch):
    tm = x_ref.shape[0]
    base_scratch, round_scratch = scratch[:N_FFN_BASE_SCRATCH], scratch[N_FFN_BASE_SCRATCH:]

    first_half = pl.program_id(0) < pl.num_programs(0) // 2

    @pl.when(first_half)
    def _():
        wg2_bf_ref[...] = wg2_ref[...].astype(jnp.bfloat16)
        wd2_bf_ref[...] = wd2_ref[...].astype(jnp.bfloat16)

    @pl.when(jnp.logical_not(first_half))
    def _():
        wu2_bf_ref[...] = wu2_ref[...].astype(jnp.bfloat16)
        wo_bf_ref[...] = wo_ref[...].astype(jnp.bfloat16)

    def body(stream):
        y = _swiglu_tile(x_ref[...], g1_ref, (wg_hbm, wu_hbm, wd_hbm), base_scratch, stream,
                         round_scratch)
        x1 = x_ref[...] + 0.5 * y
        x1_ref[...] = x1
        h = _rmsnorm_rows(x1, gm_ref[...]).astype(jnp.bfloat16)
        z = jnp.dot(h, win_ref[...], preferred_element_type=jnp.float32)
        n_slab = (D_ATTN + 2 * D_KV) // V7X_MXU_DIM
        sq = jnp.concatenate(
            [z[:, i * V7X_MXU_DIM:(i + 1) * V7X_MXU_DIM] for i in range(n_slab)], axis=0)
        ms = jnp.dot((sq * sq).astype(jnp.bfloat16), dm_ref[...], preferred_element_type=jnp.float32)
        q_ms = jnp.concatenate([ms[i * tm:(i + 1) * tm] for i in range(n_slab - 1)], axis=1)
        k_ms = ms[(n_slab - 1) * tm:, :D_KV]
        q = z[:, :D_ATTN]
        k = z[:, D_ATTN:D_ATTN + D_KV]
        z_ref[:, :D_ATTN] = (q * lax.rsqrt(q_ms + EPS) * qg_ref[...]).astype(jnp.bfloat16)
        z_ref[:, D_ATTN:D_ATTN + D_KV] = (k * lax.rsqrt(k_ms + EPS) * kg_ref[...]).astype(jnp.bfloat16)
        z_ref[:, D_ATTN + D_KV:] = z[:, D_ATTN + D_KV:].astype(jnp.bfloat16)

    first = pl.program_id(0) == 0
    pl.when(first)(functools.partial(body, "round"))
    pl.when(jnp.logical_not(first))(functools.partial(body, None))


def _head_mean_matrix(width):
    idx = np.arange(width) // HEAD_DIM
    return jnp.asarray((idx[:, None] == idx[None, :]).astype(np.float32) / HEAD_DIM, jnp.bfloat16)


def _ffn_proj(x2d, ffn_gain, w_gate, w_up, w_down, mix_gain, w_in, q_gain, k_gain,
              next_w_gate, next_w_up, next_w_down, w_out):
    m = x2d.shape[0]
    tm = TILE_M
    n_steps = m // tm
    resident = functools.partial(pl.BlockSpec, pipeline_mode=pl.Buffered(1))
    rows = lambda width: pl.BlockSpec((tm, width), lambda i: (i, 0))
    hbm = pl.BlockSpec(memory_space=pltpu.HBM)
    half_steps = n_steps // 2
    wr = D_MODEL // half_steps
    wdr = D_FF // half_steps
    assert 2 * half_steps == n_steps and wr * half_steps == D_MODEL and wdr * half_steps == D_FF
    assert wr % 16 == 0 and wdr % 16 == 0
    early_slice = lambda i: (jnp.minimum(i, half_steps - 1), 0)
    late_slice = lambda i: (jnp.maximum(i - half_steps, 0), 0)
    wg_rows = pl.BlockSpec((wr, D_FF), early_slice)
    wd_rows = pl.BlockSpec((wdr, D_MODEL), early_slice)
    wu_rows = pl.BlockSpec((wr, D_FF), late_slice)
    wo_rows = pl.BlockSpec((wr, D_MODEL), late_slice)
    qg = (jnp.tile(q_gain, N_HEADS) * (HEAD_DIM ** -0.5)).reshape(1, D_ATTN)
    kg = jnp.tile(k_gain, N_KV_HEADS).reshape(1, D_KV)
    est = (_ffn_vmem_bytes(tm) + 2 * D_MODEL * D_IN
           + 2 * tm * D_MODEL * 4 * 2
           + 2 * tm * D_IN * 2
           + 4 * tm * D_IN * 4
           + 2 * 6 * (2 * wr * D_FF + wdr * D_MODEL + wr * D_MODEL))
    bf16 = lambda shape: jax.ShapeDtypeStruct(shape, jnp.bfloat16)
    return pl.pallas_call(
        _ffn_proj_kernel,
        out_shape=(
            jax.ShapeDtypeStruct((m, D_MODEL), jnp.float32),
            bf16((m, D_IN)),
            bf16((D_MODEL, D_FF)), bf16((D_MODEL, D_FF)), bf16((D_FF, D_MODEL)),
            bf16((D_MODEL, D_MODEL)),
        ),
        grid=(n_steps,),
        in_specs=[
            rows(D_MODEL),
            resident((1, D_MODEL), lambda i: (0, 0)),
            hbm, hbm, hbm,
            resident((1, D_MODEL), lambda i: (0, 0)),
            resident((D_MODEL, D_IN), lambda i: (0, 0)),
            resident((1, D_ATTN), lambda i: (0, 0)),
            resident((1, D_KV), lambda i: (0, 0)),
            resident((V7X_MXU_DIM, V7X_MXU_DIM), lambda i: (0, 0)),
            wg_rows, wu_rows, wd_rows, wo_rows,
        ],
        out_specs=(rows(D_MODEL), rows(D_IN), wg_rows, wu_rows, wd_rows, wo_rows),
        scratch_shapes=_ffn_base_scratch(tm) + _ffn_round_scratch(),
        compiler_params=pltpu.CompilerParams(
            dimension_semantics=("arbitrary",),
            vmem_limit_bytes=_vmem_limit(est + (8 << 20))),
        name="ffn_proj",
    )(x2d, ffn_gain.reshape(1, D_MODEL), w_gate, w_up, w_down, mix_gain.reshape(1, D_MODEL),
      w_in.astype(jnp.bfloat16), qg, kg, _head_mean_matrix(V7X_MXU_DIM),
      next_w_gate, next_w_up, next_w_down, w_out)


def _t5_bucket(dist):
    n = np.maximum(dist, 0)
    max_exact = N_BUCKETS // 2
    large = max_exact + (np.log(np.maximum(n, 1) / max_exact)
                         / np.log(MAX_DISTANCE / max_exact)
                         * (N_BUCKETS - max_exact)).astype(np.int32)
    large = np.minimum(large, N_BUCKETS - 1)
    return np.where(n < max_exact, n, large).astype(np.int32)


def _band_bucket_table():
    ql = np.arange(BLOCK)[:, None]
    kl = np.arange(2 * BLOCK)[None, :]
    dist = ql + BLOCK - kl
    in_band = (dist >= 0) & (dist < WINDOW)
    return np.where(in_band, _t5_bucket(dist), -1).astype(np.int32)


def _bias_kernel(rel_ref, bucket_ref, o_ref):
    bucket = bucket_ref[...]
    for slot, head in enumerate(HEADS_PLAIN + HEADS_ROLLED):
        acc = jnp.full((BLOCK, 2 * BLOCK), NEG, jnp.float32)
        for b in range(N_BUCKETS):
            acc = jnp.where(bucket == b, rel_ref[b, head], acc)
        o_ref[slot] = acc


def _band_bias(rel_bias):
    return pl.pallas_call(
        _bias_kernel,
        out_shape=jax.ShapeDtypeStruct((N_HEADS, BLOCK, 2 * BLOCK), jnp.float32),
        in_specs=[
            pl.BlockSpec(memory_space=pltpu.SMEM),
            pl.BlockSpec(memory_space=pltpu.VMEM),
        ],
        out_specs=pl.BlockSpec(memory_space=pltpu.VMEM),
        name="band_bias",
    )(rel_bias, jnp.asarray(_band_bucket_table()))


def _block_diagonal(w):
    g, c, _ = w.shape
    on_diagonal = jnp.eye(g, dtype=bool)[:, None, :, None]
    return jnp.where(on_diagonal, w[:, :, None, :], jnp.zeros((), w.dtype)).reshape(g * c, g * c)


def _mix_stages(seq_first, t_base, sink_ref, x_ref, z_ref, zh_ref,
                bias_ref, wo_ref, x2_ref, mix_scratch, g_ref, h_ref):
    kx_ref, kxr_ref, vx_ref, vxr_ref, pooled_ref, y_ref, wc_ref = mix_scratch
    tq = z_ref.shape[1]
    k_cols = slice(D_ATTN, D_ATTN + D_KV)
    v_cols = slice(D_ATTN + D_KV, D_ATTN + 2 * D_KV)
    u_cols = slice(D_ATTN + 2 * D_KV, D_IN)
    n_blocks = tq // BLOCK
    half = HEAD_DIM
    groups = ((HEADS_PLAIN, kx_ref, vx_ref, 0), (HEADS_ROLLED, kxr_ref, vxr_ref, len(HEADS_PLAIN)))
    lane = lax.broadcasted_iota(jnp.int32, (BLOCK, 2 * half), 1)
    low = lane < half
    state = {}

    def rows_of(j):
        return slice(j * BLOCK, (j + 1) * BLOCK)

    def keys_of(j):
        return slice(j * BLOCK, (j + 2) * BLOCK)

    def build_slabs():
        halo = jnp.where(seq_first, jnp.zeros_like(zh_ref[0, :, D_ATTN:D_ATTN + 2 * D_KV]),
                         zh_ref[0, :, D_ATTN:D_ATTN + 2 * D_KV])
        kx_ref[:BLOCK] = halo[:, :D_KV]
        kx_ref[BLOCK:] = z_ref[0, :, k_cols]
        kxr_ref[...] = pltpu.roll(kx_ref[...].astype(jnp.float32), half, 1).astype(jnp.bfloat16)
        ones = jnp.ones((tq + BLOCK, D_KV), jnp.bfloat16)
        vx_ref[:BLOCK, :D_KV] = halo[:, D_KV:2 * D_KV]
        vx_ref[BLOCK:, :D_KV] = z_ref[0, :, v_cols]
        vx_ref[:, D_KV:] = ones
        vxr_ref[:, :D_KV] = pltpu.roll(
            vx_ref[:, :D_KV].astype(jnp.float32), half, 1).astype(jnp.bfloat16)
        vxr_ref[:, D_KV:] = ones

    def scores(j, grp):
        heads, k_src, _, _ = groups[grp]
        q = z_ref[0, rows_of(j), :D_ATTN]
        zero = jnp.zeros((BLOCK, 2 * half), q.dtype)
        pieces = []
        for hd in heads:
            pair = q[:, (hd // 2) * 2 * half:(hd // 2 + 1) * 2 * half]
            pieces.append(jnp.where(low if hd % 2 == 0 else ~low, pair, zero))
        qs = jnp.concatenate(pieces, axis=0)
        state["logits", j, grp] = lax.dot_general(
            qs, k_src[keys_of(j), :], (((1,), (1,)), ((), ())),
            preferred_element_type=jnp.float32)

    def softmax_numerators(j, grp):
        heads, _, _, slot0 = groups[grp]
        logits = state.pop(("logits", j, grp))
        es, sink_terms = [], []
        for s, hd in enumerate(heads):
            lg = logits[s * BLOCK:(s + 1) * BLOCK] + bias_ref[slot0 + s]
            if j == 0:
                col = lax.broadcasted_iota(jnp.int32, (1, 2 * BLOCK), 1)
                lg = lg + jnp.where(seq_first & (col < BLOCK), NEG, 0.0).astype(jnp.float32)
            sink = sink_ref[hd]
            m = jnp.maximum(jnp.max(lg, axis=-1, keepdims=True), sink)
            es.append(jnp.exp(lg - m).astype(jnp.bfloat16))
            sink_terms.append(jnp.exp(sink - m))
        state["e", j, grp] = (jnp.concatenate(es, axis=0), sink_terms)

    def attend(j, grp):
        heads, _, v_src, _ = groups[grp]
        e, sink_terms = state.pop(("e", j, grp))
        pv = jnp.dot(e, v_src[keys_of(j), :], preferred_element_type=jnp.float32)
        for s, hd in enumerate(heads):
            blk = pv[s * BLOCK:(s + 1) * BLOCK]
            state["out", j, hd] = blk[:, :D_KV] / (blk[:, D_KV:] + sink_terms[s])

    def store_attention(j):
        for p in range(N_HEADS // 2):
            y_ref[rows_of(j), p * 2 * half:(p + 1) * 2 * half] = jnp.where(
                low, state.pop(("out", j, 2 * p)), state.pop(("out", j, 2 * p + 1))
            ).astype(jnp.bfloat16)

    def pool_means(groups_of_channels):
        row = lax.broadcasted_iota(jnp.int32, (tq, 1), 0)
        for g in groups_of_channels:
            w = POOL_WINDOWS[g]
            gc = slice(u_cols.start + g * POOL_GROUP_DIM, u_cols.start + (g + 1) * POOL_GROUP_DIM)
            before = zh_ref[0, BLOCK - POOL_HALO:, gc].astype(jnp.float32)
            before = jnp.where(seq_first, jnp.zeros_like(before), before)
            token = z_ref[0, :, gc].astype(jnp.float32)
            total = jnp.concatenate([before, token], axis=0)
            span = 1
            while span < w:
                total = total + pltpu.roll(total, span, 0)
                span *= 2
            cnt = jnp.minimum(t_base + row + 1, w).astype(jnp.float32)
            pooled = total[POOL_HALO:] * (1.0 / cnt) - token
            pooled_ref[:, g * POOL_GROUP_DIM:(g + 1) * POOL_GROUP_DIM] = pooled.astype(jnp.bfloat16)

    def project_pool():
        x2_ref[...] = x_ref[0] + jnp.dot(pooled_ref[...], wc_ref[...],
                                         preferred_element_type=jnp.float32)

    def project_attention():
        x2 = x2_ref[...] + jnp.dot(y_ref[...], wo_ref[:D_ATTN, :], preferred_element_type=jnp.float32)
        x2_ref[...] = x2
        h_ref[...] = _rmsnorm_rows(x2, g_ref[...]).astype(jnp.bfloat16)

    def seq(*calls):
        def run():
            for fn, *args in calls:
                fn(*args)
        return run

    early = [
        seq((build_slabs,)),
        seq((pool_means, (0, 1))),
        seq((pool_means, (2, 3))),
        seq((scores, 0, 0), (scores, 0, 1), (softmax_numerators, 0, 0)),
        seq((attend, 0, 0), (softmax_numerators, 0, 1)),
        seq((project_pool,), (scores, 1, 0), (attend, 0, 1), (store_attention, 0)),
    ]
    for j in range(1, n_blocks):
        early.append(seq((scores, j, 1), (softmax_numerators, j, 0), (attend, j, 0)))
        tail = [(softmax_numerators, j, 1)]
        if j + 1 < n_blocks:
            tail.append((scores, j + 1, 0))
        tail += [(attend, j, 1), (store_attention, j)]
        early.append(seq(*tail))
    late = [project_attention]
    return early, late


N_MIX_SCRATCH = 8


def _mix_ffn_kernel(sink_ref, x_ref, z_ref, zh_ref,
                    bias_ref, pw_ref, ps_ref, wo_ref, g2_ref, wg_hbm, wu_hbm, wd_hbm,
                    o_ref, *scratch, tiles_per_seq):
    mix_scratch, x2_ref = scratch[:N_MIX_SCRATCH - 1], scratch[N_MIX_SCRATCH - 1]
    base_scratch = scratch[N_MIX_SCRATCH:N_MIX_SCRATCH + N_FFN_BASE_SCRATCH]
    copy_scratch = scratch[N_MIX_SCRATCH + N_FFN_BASE_SCRATCH:]
    h_ref = base_scratch[3]
    t = pl.program_id(0)
    tq = z_ref.shape[1]
    t_mix = jnp.minimum(t, pl.num_programs(0) - 2)
    i_seq = lax.rem(t_mix, tiles_per_seq)
    stages = functools.partial(
        _mix_stages, i_seq == 0, i_seq * tq, sink_ref, x_ref, z_ref, zh_ref,
        bias_ref, wo_ref, x2_ref, mix_scratch, g2_ref, h_ref)
    w_hbm = (wg_hbm, wu_hbm, wd_hbm)

    def first_step():
        _ffn_start_weight_copies(w_hbm, base_scratch, copy_scratch)
        pool_map = (pw_ref[...] * ps_ref[...]).astype(jnp.bfloat16)
        mix_scratch[-1][...] = jnp.dot(pool_map, wo_ref[D_ATTN:, :],
                                       preferred_element_type=jnp.float32).astype(jnp.bfloat16)
        early, late = stages()
        for stage in early + late:
            stage()

    def later_step(wait_weights):
        early, late = stages()
        o_ref[0] = x2_ref[...]
        y = _swiglu_tile(None, g2_ref, w_hbm, base_scratch, "wait" if wait_weights else None,
                         copy_scratch, between=early, before_down=late)
        o_ref[0] += 0.5 * y

    pl.when(t == 0)(first_step)
    pl.when(t == 1)(functools.partial(later_step, True))
    pl.when(t > 1)(functools.partial(later_step, False))


def _mix_ffn(x3d, z3d, sinks, bias, pool_w, pool_scale, w_out_bf16, ffn_gain,
             w_gate_bf16, w_up_bf16, w_down_bf16):
    b, s, _ = x3d.shape
    tq = TILE_M
    nblk = tq // BLOCK
    tiles_per_seq = s // tq
    n_tiles = b * tiles_per_seq
    resident = functools.partial(pl.BlockSpec, pipeline_mode=pl.Buffered(1))
    hbm = pl.BlockSpec(memory_space=pltpu.HBM)

    def mixed(t):
        tm = jnp.minimum(t, n_tiles - 1)
        return tm // tiles_per_seq, lax.rem(tm, tiles_per_seq)

    def tile(width):
        return pl.BlockSpec((1, tq, width), lambda t: (*mixed(t), 0))

    def halo(width):
        def index(t):
            bi, i = mixed(t)
            return bi, jnp.maximum(i * nblk - 1, 0), 0
        return pl.BlockSpec((1, BLOCK, width), index)

    def out_index(t):
        tf = jnp.maximum(t - 1, 0)
        return tf // tiles_per_seq, lax.rem(tf, tiles_per_seq), 0

    est = (_ffn_vmem_bytes(tq)
           + 2 * 2 * tq * D_MODEL * 4 + 2 * (tq + BLOCK) * D_IN * 2
           + N_HEADS * BLOCK * 2 * BLOCK * 4 + 2 * D_MODEL * D_MODEL + 4 * D_POOL * D_POOL
           + 2 * D_POOL * D_MODEL
           + 6 * (tq + BLOCK) * D_KV * 2 + (tq + BLOCK) * D_POOL * 2 + tq * D_MODEL * 2
           + tq * D_MODEL * 4 + 16 * BLOCK * 2 * BLOCK * 4 * 4)
    return pl.pallas_call(
        functools.partial(_mix_ffn_kernel, tiles_per_seq=tiles_per_seq),
        out_shape=jax.ShapeDtypeStruct((b, s, D_MODEL), jnp.float32),
        grid=(n_tiles + 1,),
        in_specs=[
            pl.BlockSpec(memory_space=pltpu.SMEM),
            tile(D_MODEL),
            tile(D_IN), halo(D_IN),
            resident((N_HEADS, BLOCK, 2 * BLOCK), lambda t: (0, 0, 0)),
            resident((D_POOL, D_POOL), lambda t: (0, 0)),
            resident((1, D_POOL), lambda t: (0, 0)),
            resident((D_MODEL, D_MODEL), lambda t: (0, 0)),
            resident((1, D_MODEL), lambda t: (0, 0)),
            hbm, hbm, hbm,
        ],
        out_specs=pl.BlockSpec((1, tq, D_MODEL), out_index),
        scratch_shapes=[
            pltpu.VMEM((tq + BLOCK, D_KV), jnp.bfloat16),
            pltpu.VMEM((tq + BLOCK, D_KV), jnp.bfloat16),
            pltpu.VMEM((tq + BLOCK, 2 * D_KV), jnp.bfloat16),
            pltpu.VMEM((tq + BLOCK, 2 * D_KV), jnp.bfloat16),
            pltpu.VMEM((tq, D_POOL), jnp.bfloat16),
            pltpu.VMEM((tq, D_ATTN), jnp.bfloat16),
            pltpu.VMEM((D_POOL, D_MODEL), jnp.bfloat16),
            pltpu.VMEM((tq, D_MODEL), jnp.float32),
        ] + _ffn_base_scratch(tq) + _ffn_copy_scratch(),
        compiler_params=pltpu.CompilerParams(
            dimension_semantics=("arbitrary",),
            vmem_limit_bytes=_vmem_limit(est + (8 << 20))),
        name="mix_ffn",
    )(sinks, x3d, z3d, z3d, bias,
      _block_diagonal(pool_w), pool_scale.reshape(1, D_POOL),
      w_out_bf16, ffn_gain.reshape(1, D_MODEL), w_gate_bf16, w_up_bf16, w_down_bf16)


def kernel(x, ffn1_norm, ffn1_w_gate, ffn1_w_up, ffn1_w_down, mix_norm, w_in, q_norm, k_norm,
           attn_sinks, rel_bias, pool_w, pool_scale, w_out, ffn2_norm, ffn2_w_gate, ffn2_w_up,
           ffn2_w_down):
    b, s, d = x.shape
    assert (d, s % TILE_M) == (D_MODEL, 0)
    bias = _band_bias(rel_bias)
    for l in range(ffn1_norm.shape[0]):
        x1, z, wg2, wu2, wd2, wo = _ffn_proj(
            x.reshape(b * s, d), ffn1_norm[l], ffn1_w_gate[l], ffn1_w_up[l], ffn1_w_down[l],
            mix_norm[l], w_in[l], q_norm[l], k_norm[l],
            ffn2_w_gate[l], ffn2_w_up[l], ffn2_w_down[l], w_out[l])
        x = _mix_ffn(x1.reshape(b, s, d), z.reshape(b, s, D_IN), attn_sinks[l], bias,
                     pool_w[l], pool_scale[l], wo, ffn2_norm[l], wg2, wu2, wd2)
    return x
```

```python
import functools

import numpy as np
import jax
import jax.numpy as jnp
from jax import lax
from jax.experimental import pallas as pl
from jax.experimental.pallas import tpu as pltpu

D_MODEL = 1024
HEAD_DIM = 64
N_HEADS = 8
N_KV_HEADS = 2
D_ATTN = N_HEADS * HEAD_DIM
D_KV = N_KV_HEADS * HEAD_DIM
D_POOL = D_MODEL - D_ATTN
POOL_WINDOWS = (2, 4, 8, 16)
POOL_GROUP_DIM = D_POOL // len(POOL_WINDOWS)
D_IN = D_ATTN + 2 * D_KV + D_POOL
WINDOW = 128
BLOCK = 128
N_BUCKETS = 32
MAX_DISTANCE = 128
D_FF = 2816
EPS = 1e-6
NEG = -1e30

V7X_LANES = 128
V7X_MXU_DIM = 256
V7X_VMEM_BYTES = 64 * 1024 * 1024

HEADS_PLAIN = (0, 2, 5, 7)
HEADS_ROLLED = (1, 3, 4, 6)

TILE_M = 512
FFN_CHUNK_F = 256
FFN_STAGE_SLOTS = 2


def _vmem_limit(estimate_bytes):
    return int(min(V7X_VMEM_BYTES - (4 << 20), max(estimate_bytes, 16 << 20)))


def _rmsnorm_rows(x32, gain_row):
    ms = jnp.mean(x32 * x32, axis=-1, keepdims=True)
    return x32 * lax.rsqrt(ms + EPS) * gain_row


N_FFN_CHUNKS = D_FF // FFN_CHUNK_F
N_FFN_BASE_SCRATCH = 5


def _ffn_base_scratch(tm):
    return [
        pltpu.VMEM((D_MODEL, D_FF), jnp.bfloat16),
        pltpu.VMEM((D_MODEL, D_FF), jnp.bfloat16),
        pltpu.VMEM((D_FF, D_MODEL), jnp.bfloat16),
        pltpu.VMEM((tm, D_MODEL), jnp.bfloat16),
        pltpu.VMEM((tm, D_FF), jnp.bfloat16),
    ]


def _ffn_round_scratch():
    return [
        pltpu.VMEM((2 * FFN_STAGE_SLOTS, D_MODEL, FFN_CHUNK_F), jnp.float32),
        pltpu.VMEM((FFN_STAGE_SLOTS, FFN_CHUNK_F, D_MODEL), jnp.float32),
        pltpu.SemaphoreType.DMA((3 * FFN_STAGE_SLOTS,)),
    ]


def _ffn_copy_scratch():
    return [pltpu.SemaphoreType.DMA((3, N_FFN_CHUNKS))]


def _ffn_vmem_bytes(tm):
    return (2 * 3 * D_MODEL * D_FF + 3 * FFN_STAGE_SLOTS * D_MODEL * FFN_CHUNK_F * 4
            + tm * D_MODEL * 2 + tm * D_FF * 2 + 6 * tm * FFN_CHUNK_F * 4 + tm * D_MODEL * 4)


def _ffn_window(c):
    return pl.ds(c * FFN_CHUNK_F, FFN_CHUNK_F)


def _ffn_weight_copy(which, c, w_hbm, base_scratch, copy_scratch):
    (sem_ref,) = copy_scratch
    src, dst = w_hbm[which], base_scratch[which]
    if which == 2:
        return pltpu.make_async_copy(src.at[_ffn_window(c), :], dst.at[_ffn_window(c), :],
                                     sem_ref.at[which, c])
    return pltpu.make_async_copy(src.at[:, _ffn_window(c)], dst.at[:, _ffn_window(c)],
                                 sem_ref.at[which, c])


def _ffn_start_weight_copies(w_hbm, base_scratch, copy_scratch):
    for c in range(N_FFN_CHUNKS):
        for which in range(3):
            _ffn_weight_copy(which, c, w_hbm, base_scratch, copy_scratch).start()


def _swiglu_tile(x32, g_ref, w_hbm, base_scratch, stream, stream_scratch, between=(), before_down=()):
    wg_hbm, wu_hbm, wd_hbm = w_hbm
    wg_ref, wu_ref, wd_ref, h_ref, a_ref = base_scratch

    def staged_copy(which, c):
        stage_in_ref, stage_out_ref, sem_ref = stream_scratch
        slot = c % FFN_STAGE_SLOTS
        sem = sem_ref.at[which * FFN_STAGE_SLOTS + slot]
        if which == 2:
            return pltpu.make_async_copy(wd_hbm.at[_ffn_window(c), :], stage_out_ref.at[slot], sem)
        src = (wg_hbm, wu_hbm)[which]
        return pltpu.make_async_copy(src.at[:, _ffn_window(c)],
                                     stage_in_ref.at[which * FFN_STAGE_SLOTS + slot], sem)

    def land(which, c):
        if stream == "wait":
            _ffn_weight_copy(which, c, w_hbm, base_scratch, stream_scratch).wait()
            return
        staged_copy(which, c).wait()
        stage_in_ref, stage_out_ref, _ = stream_scratch
        slot = c % FFN_STAGE_SLOTS
        rows_or_cols = slice(c * FFN_CHUNK_F, (c + 1) * FFN_CHUNK_F)
        if which == 2:
            wd_ref[rows_or_cols, :] = stage_out_ref[slot].astype(jnp.bfloat16)
        else:
            dst = (wg_ref, wu_ref)[which]
            dst[:, rows_or_cols] = stage_in_ref[which * FFN_STAGE_SLOTS + slot].astype(jnp.bfloat16)
        if c + FFN_STAGE_SLOTS < N_FFN_CHUNKS:
            staged_copy(which, c + FFN_STAGE_SLOTS).start()

    if stream == "round":
        for c in range(FFN_STAGE_SLOTS):
            for which in range(3):
                staged_copy(which, c).start()
    if x32 is not None:
        h_ref[...] = _rmsnorm_rows(x32, g_ref[...]).astype(jnp.bfloat16)
    for c in range(N_FFN_CHUNKS):
        cols = slice(c * FFN_CHUNK_F, (c + 1) * FFN_CHUNK_F)
        if stream is not None:
            for which in range(3 if stream == "round" else 2):
                land(which, c)
        h = h_ref[...]
        gate = jnp.dot(h, wg_ref[:, cols], preferred_element_type=jnp.float32)
        up = jnp.dot(h, wu_ref[:, cols], preferred_element_type=jnp.float32)
        act = gate * (1.0 / (1.0 + jnp.exp(-gate))) * up
        a_ref[:, cols] = act.astype(jnp.bfloat16)
        if c < len(between):
            between[c]()
    for stage in tuple(between[N_FFN_CHUNKS:]) + tuple(before_down):
        stage()
    if stream == "wait":
        for c in range(N_FFN_CHUNKS):
            land(2, c)
    return jnp.dot(a_ref[...], wd_ref[...], preferred_element_type=jnp.float32)


def _ffn_proj_kernel(x_ref, g1_ref, wg_hbm, wu_hbm, wd_hbm, gm_ref, win_ref, qg_ref, kg_ref, dm_ref,
                     wg2_ref, wu2_ref, wd2_ref, wo_ref, rel_ref, bucket_ref,
                     x1_ref, z_ref, wg2_bf_ref, wu2_bf_ref, wd2_bf_ref, wo_bf_ref, bias_ref,
                     *scratch):
    tm = x_ref.shape[0]
    base_scratch, round_scratch = scratch[:N_FFN_BASE_SCRATCH], scratch[N_FFN_BASE_SCRATCH:]

    first_half = pl.program_id(0) < pl.num_programs(0) // 2

    @pl.when(first_half)
    def _():
        wg2_bf_ref[...] = wg2_ref[...].astype(jnp.bfloat16)
        wd2_bf_ref[...] = wd2_ref[...].astype(jnp.bfloat16)

    @pl.when(jnp.logical_not(first_half))
    def _():
        wu2_bf_ref[...] = wu2_ref[...].astype(jnp.bfloat16)
        wo_bf_ref[...] = wo_ref[...].astype(jnp.bfloat16)

    def body(stream):
        if stream is not None:
            _write_band_bias(rel_ref, bucket_ref, bias_ref)
        y = _swiglu_tile(x_ref[...], g1_ref, (wg_hbm, wu_hbm, wd_hbm), base_scratch, stream,
                         round_scratch)
        x1 = x_ref[...] + 0.5 * y
        x1_ref[...] = x1
        h = _rmsnorm_rows(x1, gm_ref[...]).astype(jnp.bfloat16)
        z = jnp.dot(h, win_ref[...], preferred_element_type=jnp.float32)
        n_slab = (D_ATTN + 2 * D_KV) // V7X_MXU_DIM
        sq = jnp.concatenate(
            [z[:, i * V7X_MXU_DIM:(i + 1) * V7X_MXU_DIM] for i in range(n_slab)], axis=0)
        ms = jnp.dot((sq * sq).astype(jnp.bfloat16), dm_ref[...], preferred_element_type=jnp.float32)
        q_ms = jnp.concatenate([ms[i * tm:(i + 1) * tm] for i in range(n_slab - 1)], axis=1)
        k_ms = ms[(n_slab - 1) * tm:, :D_KV]
        q = z[:, :D_ATTN]
        k = z[:, D_ATTN:D_ATTN + D_KV]
        z_ref[:, :D_ATTN] = (q * lax.rsqrt(q_ms + EPS) * qg_ref[...]).astype(jnp.bfloat16)
        z_ref[:, D_ATTN:D_ATTN + D_KV] = (k * lax.rsqrt(k_ms + EPS) * kg_ref[...]).astype(jnp.bfloat16)
        z_ref[:, D_ATTN + D_KV:] = z[:, D_ATTN + D_KV:].astype(jnp.bfloat16)

    first = pl.program_id(0) == 0
    pl.when(first)(functools.partial(body, "round"))
    pl.when(jnp.logical_not(first))(functools.partial(body, None))


def _head_mean_matrix(width):
    idx = np.arange(width) // HEAD_DIM
    return jnp.asarray((idx[:, None] == idx[None, :]).astype(np.float32) / HEAD_DIM, jnp.bfloat16)


def _ffn_proj(x2d, ffn_gain, w_gate, w_up, w_down, mix_gain, w_in, q_gain, k_gain,
              next_w_gate, next_w_up, next_w_down, w_out, rel_bias):
    m = x2d.shape[0]
    tm = TILE_M
    n_steps = m // tm
    resident = functools.partial(pl.BlockSpec, pipeline_mode=pl.Buffered(1))
    rows = lambda width: pl.BlockSpec((tm, width), lambda i: (i, 0))
    hbm = pl.BlockSpec(memory_space=pltpu.HBM)
    half_steps = n_steps // 2
    wr = D_MODEL // half_steps
    wdr = D_FF // half_steps
    assert 2 * half_steps == n_steps and wr * half_steps == D_MODEL and wdr * half_steps == D_FF
    assert wr % 16 == 0 and wdr % 16 == 0
    early_slice = lambda i: (jnp.minimum(i, half_steps - 1), 0)
    late_slice = lambda i: (jnp.maximum(i - half_steps, 0), 0)
    wg_rows = pl.BlockSpec((wr, D_FF), early_slice)
    wd_rows = pl.BlockSpec((wdr, D_MODEL), early_slice)
    wu_rows = pl.BlockSpec((wr, D_FF), late_slice)
    wo_rows = pl.BlockSpec((wr, D_MODEL), late_slice)
    qg = (jnp.tile(q_gain, N_HEADS) * (HEAD_DIM ** -0.5)).reshape(1, D_ATTN)
    kg = jnp.tile(k_gain, N_KV_HEADS).reshape(1, D_KV)
    est = (_ffn_vmem_bytes(tm) + 2 * D_MODEL * D_IN
           + 2 * tm * D_MODEL * 4 * 2
           + 2 * tm * D_IN * 2
           + 4 * tm * D_IN * 4
           + 2 * 6 * (2 * wr * D_FF + wdr * D_MODEL + wr * D_MODEL)
           + 3 * N_HEADS * BLOCK * 2 * BLOCK * 4)
    bf16 = lambda shape: jax.ShapeDtypeStruct(shape, jnp.bfloat16)
    return pl.pallas_call(
        _ffn_proj_kernel,
        out_shape=(
            jax.ShapeDtypeStruct((m, D_MODEL), jnp.float32),
            bf16((m, D_IN)),
            bf16((D_MODEL, D_FF)), bf16((D_MODEL, D_FF)), bf16((D_FF, D_MODEL)),
            bf16((D_MODEL, D_MODEL)),
            jax.ShapeDtypeStruct((N_HEADS, BLOCK, 2 * BLOCK), jnp.float32),
        ),
        grid=(n_steps,),
        in_specs=[
            rows(D_MODEL),
            resident((1, D_MODEL), lambda i: (0, 0)),
            hbm, hbm, hbm,
            resident((1, D_MODEL), lambda i: (0, 0)),
            resident((D_MODEL, D_IN), lambda i: (0, 0)),
            resident((1, D_ATTN), lambda i: (0, 0)),
            resident((1, D_KV), lambda i: (0, 0)),
            resident((V7X_MXU_DIM, V7X_MXU_DIM), lambda i: (0, 0)),
            wg_rows, wu_rows, wd_rows, wo_rows,
            pl.BlockSpec(memory_space=pltpu.SMEM),
            resident((BLOCK, 2 * BLOCK), lambda i: (0, 0)),
        ],
        out_specs=(rows(D_MODEL), rows(D_IN), wg_rows, wu_rows, wd_rows, wo_rows,
                   pl.BlockSpec((N_HEADS, BLOCK, 2 * BLOCK), lambda i: (0, 0, 0))),
        scratch_shapes=_ffn_base_scratch(tm) + _ffn_round_scratch(),
        compiler_params=pltpu.CompilerParams(
            dimension_semantics=("arbitrary",),
            vmem_limit_bytes=_vmem_limit(est + (8 << 20))),
        name="ffn_proj",
    )(x2d, ffn_gain.reshape(1, D_MODEL), w_gate, w_up, w_down, mix_gain.reshape(1, D_MODEL),
      w_in.astype(jnp.bfloat16), qg, kg, _head_mean_matrix(V7X_MXU_DIM),
      next_w_gate, next_w_up, next_w_down, w_out, rel_bias, jnp.asarray(_band_bucket_table()))


def _t5_bucket(dist):
    n = np.maximum(dist, 0)
    max_exact = N_BUCKETS // 2
    large = max_exact + (np.log(np.maximum(n, 1) / max_exact)
                         / np.log(MAX_DISTANCE / max_exact)
                         * (N_BUCKETS - max_exact)).astype(np.int32)
    large = np.minimum(large, N_BUCKETS - 1)
    return np.where(n < max_exact, n, large).astype(np.int32)


def _band_bucket_table():
    ql = np.arange(BLOCK)[:, None]
    kl = np.arange(2 * BLOCK)[None, :]
    dist = ql + BLOCK - kl
    in_band = (dist >= 0) & (dist < WINDOW)
    return np.where(in_band, _t5_bucket(dist), -1).astype(np.int32)


def _write_band_bias(rel_ref, bucket_ref, o_ref):
    bucket = bucket_ref[...]
    for slot, head in enumerate(HEADS_PLAIN + HEADS_ROLLED):
        acc = jnp.full((BLOCK, 2 * BLOCK), NEG, jnp.float32)
        for b in range(N_BUCKETS):
            acc = jnp.where(bucket == b, rel_ref[b, head], acc)
        o_ref[slot] = acc


def _window_sum_matrices():
    ql = np.arange(BLOCK)[:, None]
    kl = np.arange(2 * BLOCK)[None, :]
    dist = ql + BLOCK - kl
    mats = [((dist >= 0) & (dist < w)).astype(np.float32) for w in POOL_WINDOWS]
    return jnp.asarray(np.stack(mats), jnp.bfloat16)


def _mix_stages(seq_first, t_base, sink_ref, x_ref, z_ref, zh_ref,
                bias_ref, wsum_ref, wo_ref, x2_ref, mix_scratch, g_ref, h_ref):
    kx_ref, kxr_ref, vx_ref, vxr_ref, ux_ref, pooled_ref, y_ref, wc_ref = mix_scratch
    tq = z_ref.shape[1]
    k_cols = slice(D_ATTN, D_ATTN + D_KV)
    v_cols = slice(D_ATTN + D_KV, D_ATTN + 2 * D_KV)
    u_cols = slice(D_ATTN + 2 * D_KV, D_IN)
    n_blocks = tq // BLOCK
    half = HEAD_DIM
    groups = ((HEADS_PLAIN, kx_ref, vx_ref, 0), (HEADS_ROLLED, kxr_ref, vxr_ref, len(HEADS_PLAIN)))
    lane = lax.broadcasted_iota(jnp.int32, (BLOCK, 2 * half), 1)
    low = lane < half
    state = {}

    def rows_of(j):
        return slice(j * BLOCK, (j + 1) * BLOCK)

    def keys_of(j):
        return slice(j * BLOCK, (j + 2) * BLOCK)

    def build_slabs():
        halo = jnp.where(seq_first, jnp.zeros_like(zh_ref[0, :, D_ATTN:]), zh_ref[0, :, D_ATTN:])
        kx_ref[:BLOCK] = halo[:, :D_KV]
        kx_ref[BLOCK:] = z_ref[0, :, k_cols]
        kxr_ref[...] = pltpu.roll(kx_ref[...].astype(jnp.float32), half, 1).astype(jnp.bfloat16)
        ones = jnp.ones((tq + BLOCK, D_KV), jnp.bfloat16)
        vx_ref[:BLOCK, :D_KV] = halo[:, D_KV:2 * D_KV]
        vx_ref[BLOCK:, :D_KV] = z_ref[0, :, v_cols]
        vx_ref[:, D_KV:] = ones
        vxr_ref[:, :D_KV] = pltpu.roll(
            vx_ref[:, :D_KV].astype(jnp.float32), half, 1).astype(jnp.bfloat16)
        vxr_ref[:, D_KV:] = ones
        ux_ref[:BLOCK] = halo[:, 2 * D_KV:]
        ux_ref[BLOCK:] = z_ref[0, :, u_cols]

    def scores(j, grp):
        heads, k_src, _, _ = groups[grp]
        q = z_ref[0, rows_of(j), :D_ATTN]
        zero = jnp.zeros((BLOCK, 2 * half), q.dtype)
        pieces = []
        for hd in heads:
            pair = q[:, (hd // 2) * 2 * half:(hd // 2 + 1) * 2 * half]
            pieces.append(jnp.where(low if hd % 2 == 0 else ~low, pair, zero))
        qs = jnp.concatenate(pieces, axis=0)
        state["logits", j, grp] = lax.dot_general(
            qs, k_src[keys_of(j), :], (((1,), (1,)), ((), ())),
            preferred_element_type=jnp.float32)

    def softmax_numerators(j, grp):
        heads, _, _, slot0 = groups[grp]
        logits = state.pop(("logits", j, grp))
        es, sink_terms = [], []
        for s, hd in enumerate(heads):
            lg = logits[s * BLOCK:(s + 1) * BLOCK] + bias_ref[slot0 + s]
            if j == 0:
                col = lax.broadcasted_iota(jnp.int32, (1, 2 * BLOCK), 1)
                lg = lg + jnp.where(seq_first & (col < BLOCK), NEG, 0.0).astype(jnp.float32)
            sink = sink_ref[hd]
            m = jnp.maximum(jnp.max(lg, axis=-1, keepdims=True), sink)
            es.append(jnp.exp(lg - m).astype(jnp.bfloat16))
            sink_terms.append(jnp.exp(sink - m))
        state["e", j, grp] = (jnp.concatenate(es, axis=0), sink_terms)

    def attend(j, grp):
        heads, _, v_src, _ = groups[grp]
        e, sink_terms = state.pop(("e", j, grp))
        pv = jnp.dot(e, v_src[keys_of(j), :], preferred_element_type=jnp.float32)
        for s, hd in enumerate(heads):
            blk = pv[s * BLOCK:(s + 1) * BLOCK]
            state["out", j, hd] = blk[:, :D_KV] / (blk[:, D_KV:] + sink_terms[s])

    def store_attention(j):
        for p in range(N_HEADS // 2):
            y_ref[rows_of(j), p * 2 * half:(p + 1) * 2 * half] = jnp.where(
                low, state.pop(("out", j, 2 * p)), state.pop(("out", j, 2 * p + 1))
            ).astype(jnp.bfloat16)

    def pool_means(j):
        row = lax.broadcasted_iota(jnp.int32, (BLOCK, 1), 0)
        t_glob = t_base + j * BLOCK + row
        for g, w in enumerate(POOL_WINDOWS):
            gc = slice(g * POOL_GROUP_DIM, (g + 1) * POOL_GROUP_DIM)
            wsum = jnp.dot(wsum_ref[g], ux_ref[keys_of(j), gc], preferred_element_type=jnp.float32)
            cnt = jnp.minimum(t_glob + 1, w).astype(jnp.float32)
            token = z_ref[0, rows_of(j), D_ATTN + 2 * D_KV + g * POOL_GROUP_DIM:
                          D_ATTN + 2 * D_KV + (g + 1) * POOL_GROUP_DIM]
            pooled = wsum * (1.0 / cnt) - token.astype(jnp.float32)
            pooled_ref[rows_of(j), gc] = pooled.astype(jnp.bfloat16)

    def project_pool():
        x2_ref[...] = x_ref[0] + jnp.dot(pooled_ref[...], wc_ref[...],
                                         preferred_element_type=jnp.float32)

    def project_attention():
        x2 = x2_ref[...] + jnp.dot(y_ref[...], wo_ref[:D_ATTN, :], preferred_element_type=jnp.float32)
        x2_ref[...] = x2
        h_ref[...] = _rmsnorm_rows(x2, g_ref[...]).astype(jnp.bfloat16)

    def seq(*calls):
        def run():
            for fn, *args in calls:
                fn(*args)
        return run

    early = [
        seq((build_slabs,)),
        seq(*[(pool_means, j) for j in range(n_blocks // 2)]),
        seq(*[(pool_means, j) for j in range(n_blocks // 2, n_blocks)]),
        seq((scores, 0, 0), (scores, 0, 1), (softmax_numerators, 0, 0)),
        seq((attend, 0, 0), (softmax_numerators, 0, 1)),
        seq((project_pool,), (scores, 1, 0), (attend, 0, 1), (store_attention, 0)),
    ]
    for j in range(1, n_blocks):
        early.append(seq((scores, j, 1), (softmax_numerators, j, 0), (attend, j, 0)))
        tail = [(softmax_numerators, j, 1)]
        if j + 1 < n_blocks:
            tail.append((scores, j + 1, 0))
        tail += [(attend, j, 1), (store_attention, j)]
        early.append(seq(*tail))
    late = [project_attention]
    return early, late


N_MIX_SCRATCH = 9


def _mix_ffn_kernel(sink_ref, x_ref, z_ref, zh_ref,
                    bias_ref, wsum_ref, pw_ref, ps_ref, wo_ref, g2_ref, wg_hbm, wu_hbm, wd_hbm,
                    o_ref, *scratch, tiles_per_seq):
    mix_scratch, x2_ref = scratch[:N_MIX_SCRATCH - 1], scratch[N_MIX_SCRATCH - 1]
    base_scratch = scratch[N_MIX_SCRATCH:N_MIX_SCRATCH + N_FFN_BASE_SCRATCH]
    copy_scratch = scratch[N_MIX_SCRATCH + N_FFN_BASE_SCRATCH:]
    h_ref = base_scratch[3]
    t = pl.program_id(0)
    tq = z_ref.shape[1]
    t_mix = jnp.minimum(t, pl.num_programs(0) - 2)
    i_seq = lax.rem(t_mix, tiles_per_seq)
    stages = functools.partial(
        _mix_stages, i_seq == 0, i_seq * tq, sink_ref, x_ref, z_ref, zh_ref,
        bias_ref, wsum_ref, wo_ref, x2_ref, mix_scratch, g2_ref, h_ref)
    w_hbm = (wg_hbm, wu_hbm, wd_hbm)

    def first_step():
        _ffn_start_weight_copies(w_hbm, base_scratch, copy_scratch)
        for g in range(len(POOL_WINDOWS)):
            gc = slice(g * POOL_GROUP_DIM, (g + 1) * POOL_GROUP_DIM)
            pool_map = (pw_ref[g] * ps_ref[:, gc]).astype(jnp.bfloat16)
            mix_scratch[-1][gc, :] = jnp.dot(
                pool_map, wo_ref[D_ATTN + g * POOL_GROUP_DIM:D_ATTN + (g + 1) * POOL_GROUP_DIM, :],
                preferred_element_type=jnp.float32).astype(jnp.bfloat16)
        early, late = stages()
        for stage in early + late:
            stage()

    def later_step(wait_weights, mix_next):
        early, late = stages() if mix_next else ((), ())
        o_ref[0] = x2_ref[...]
        y = _swiglu_tile(None, g2_ref, w_hbm, base_scratch, "wait" if wait_weights else None,
                         copy_scratch, between=early, before_down=late)
        o_ref[0] += 0.5 * y

    last = pl.num_programs(0) - 1
    pl.when(t == 0)(first_step)
    pl.when(t == 1)(functools.partial(later_step, True, True))
    pl.when(jnp.logical_and(t > 1, t < last))(functools.partial(later_step, False, True))
    pl.when(t == last)(functools.partial(later_step, False, False))


def _mix_ffn(x3d, z3d, sinks, bias, pool_w, pool_scale, w_out_bf16, ffn_gain,
             w_gate_bf16, w_up_bf16, w_down_bf16):
    b, s, _ = x3d.shape
    tq = TILE_M
    nblk = tq // BLOCK
    tiles_per_seq = s // tq
    n_tiles = b * tiles_per_seq
    resident = functools.partial(pl.BlockSpec, pipeline_mode=pl.Buffered(1))
    hbm = pl.BlockSpec(memory_space=pltpu.HBM)

    def mixed(t):
        tm = jnp.minimum(t, n_tiles - 1)
        return tm // tiles_per_seq, lax.rem(tm, tiles_per_seq)

    def tile(width):
        return pl.BlockSpec((1, tq, width), lambda t: (*mixed(t), 0))

    def halo(width):
        def index(t):
            bi, i = mixed(t)
            return bi, jnp.maximum(i * nblk - 1, 0), 0
        return pl.BlockSpec((1, BLOCK, width), index)

    def out_index(t):
        tf = jnp.maximum(t - 1, 0)
        return tf // tiles_per_seq, lax.rem(tf, tiles_per_seq), 0

    est = (_ffn_vmem_bytes(tq)
           + 2 * 2 * tq * D_MODEL * 4 + 2 * (tq + BLOCK) * D_IN * 2
           + N_HEADS * BLOCK * 2 * BLOCK * 4 + 2 * D_MODEL * D_MODEL + 4 * D_POOL * D_POOL
           + 2 * D_POOL * D_MODEL
           + 6 * (tq + BLOCK) * D_KV * 2 + (tq + BLOCK) * D_POOL * 2 + tq * D_MODEL * 2
           + tq * D_MODEL * 4 + 16 * BLOCK * 2 * BLOCK * 4 * 4)
    return pl.pallas_call(
        functools.partial(_mix_ffn_kernel, tiles_per_seq=tiles_per_seq),
        out_shape=jax.ShapeDtypeStruct((b, s, D_MODEL), jnp.float32),
        grid=(n_tiles + 1,),
        in_specs=[
            pl.BlockSpec(memory_space=pltpu.SMEM),
            tile(D_MODEL),
            tile(D_IN), halo(D_IN),
            resident((N_HEADS, BLOCK, 2 * BLOCK), lambda t: (0, 0, 0)),
            resident((len(POOL_WINDOWS), BLOCK, 2 * BLOCK), lambda t: (0, 0, 0)),
            resident((len(POOL_WINDOWS), POOL_GROUP_DIM, POOL_GROUP_DIM), lambda t: (0, 0, 0)),
            resident((1, D_POOL), lambda t: (0, 0)),
            resident((D_MODEL, D_MODEL), lambda t: (0, 0)),
            resident((1, D_MODEL), lambda t: (0, 0)),
            hbm, hbm, hbm,
        ],
        out_specs=pl.BlockSpec((1, tq, D_MODEL), out_index),
        scratch_shapes=[
            pltpu.VMEM((tq + BLOCK, D_KV), jnp.bfloat16),
            pltpu.VMEM((tq + BLOCK, D_KV), jnp.bfloat16),
            pltpu.VMEM((tq + BLOCK, 2 * D_KV), jnp.bfloat16),
            pltpu.VMEM((tq + BLOCK, 2 * D_KV), jnp.bfloat16),
            pltpu.VMEM((tq + BLOCK, D_POOL), jnp.bfloat16),
            pltpu.VMEM((tq, D_POOL), jnp.bfloat16),
            pltpu.VMEM((tq, D_ATTN), jnp.bfloat16),
            pltpu.VMEM((D_POOL, D_MODEL), jnp.bfloat16),
            pltpu.VMEM((tq, D_MODEL), jnp.float32),
        ] + _ffn_base_scratch(tq) + _ffn_copy_scratch(),
        compiler_params=pltpu.CompilerParams(
            dimension_semantics=("arbitrary",),
            vmem_limit_bytes=_vmem_limit(est + (8 << 20))),
        name="mix_ffn",
    )(sinks, x3d, z3d, z3d, bias, _window_sum_matrices(),
      pool_w, pool_scale.reshape(1, D_POOL),
      w_out_bf16, ffn_gain.reshape(1, D_MODEL), w_gate_bf16, w_up_bf16, w_down_bf16)


def kernel(x, ffn1_norm, ffn1_w_gate, ffn1_w_up, ffn1_w_down, mix_norm, w_in, q_norm, k_norm,
           attn_sinks, rel_bias, pool_w, pool_scale, w_out, ffn2_norm, ffn2_w_gate, ffn2_w_up,
           ffn2_w_down):
    b, s, d = x.shape
    assert (d, s % TILE_M) == (D_MODEL, 0)
    for l in range(ffn1_norm.shape[0]):
        x1, z, wg2, wu2, wd2, wo, bias = _ffn_proj(
            x.reshape(b * s, d), ffn1_norm[l], ffn1_w_gate[l], ffn1_w_up[l], ffn1_w_down[l],
            mix_norm[l], w_in[l], q_norm[l], k_norm[l],
            ffn2_w_gate[l], ffn2_w_up[l], ffn2_w_down[l], w_out[l], rel_bias)
        x = _mix_ffn(x1.reshape(b, s, d), z.reshape(b, s, D_IN), attn_sinks[l], bias,
                     pool_w[l], pool_scale[l], wo, ffn2_norm[l], wg2, wu2, wd2)
    return x
```

```python
import functools

import numpy as np
import jax
import jax.numpy as jnp
from jax import lax
from jax.experimental import pallas as pl
from jax.experimental.pallas import tpu as pltpu

D_MODEL = 1024
HEAD_DIM = 64
N_HEADS = 8
N_KV_HEADS = 2
D_ATTN = N_HEADS * HEAD_DIM
D_KV = N_KV_HEADS * HEAD_DIM
D_POOL = D_MODEL - D_ATTN
POOL_WINDOWS = (2, 4, 8, 16)
POOL_GROUP_DIM = D_POOL // len(POOL_WINDOWS)
D_IN = D_ATTN + 2 * D_KV + D_POOL
WINDOW = 128
BLOCK = 128
N_BUCKETS = 32
MAX_DISTANCE = 128
D_FF = 2816
EPS = 1e-6
NEG = -1e30

V7X_LANES = 128
V7X_MXU_DIM = 256
V7X_VMEM_BYTES = 64 * 1024 * 1024

HEADS_PLAIN = (0, 2, 5, 7)
HEADS_ROLLED = (1, 3, 4, 6)

TILE_M = 512
FFN_CHUNK_F = 256
FFN_STAGE_SLOTS = 2


def _vmem_limit(estimate_bytes):
    return int(min(V7X_VMEM_BYTES - (4 << 20), max(estimate_bytes, 16 << 20)))


def _rmsnorm_rows(x32, gain_row):
    ms = jnp.mean(x32 * x32, axis=-1, keepdims=True)
    return x32 * lax.rsqrt(ms + EPS) * gain_row


N_FFN_CHUNKS = D_FF // FFN_CHUNK_F
N_FFN_BASE_SCRATCH = 5


def _ffn_base_scratch(tm):
    return [
        pltpu.VMEM((D_MODEL, D_FF), jnp.bfloat16),
        pltpu.VMEM((D_MODEL, D_FF), jnp.bfloat16),
        pltpu.VMEM((D_FF, D_MODEL), jnp.bfloat16),
        pltpu.VMEM((tm, D_MODEL), jnp.bfloat16),
        pltpu.VMEM((tm, D_FF), jnp.bfloat16),
    ]


def _ffn_round_scratch():
    return [
        pltpu.VMEM((2 * FFN_STAGE_SLOTS, D_MODEL, FFN_CHUNK_F), jnp.float32),
        pltpu.VMEM((FFN_STAGE_SLOTS, FFN_CHUNK_F, D_MODEL), jnp.float32),
        pltpu.SemaphoreType.DMA((3 * FFN_STAGE_SLOTS,)),
    ]


def _ffn_copy_scratch():
    return [pltpu.SemaphoreType.DMA((3, N_FFN_CHUNKS))]


def _ffn_vmem_bytes(tm):
    return (2 * 3 * D_MODEL * D_FF + 3 * FFN_STAGE_SLOTS * D_MODEL * FFN_CHUNK_F * 4
            + tm * D_MODEL * 2 + tm * D_FF * 2 + 6 * tm * FFN_CHUNK_F * 4 + tm * D_MODEL * 4)


def _ffn_window(c):
    return pl.ds(c * FFN_CHUNK_F, FFN_CHUNK_F)


def _ffn_weight_copy(which, c, w_hbm, base_scratch, copy_scratch):
    (sem_ref,) = copy_scratch
    src, dst = w_hbm[which], base_scratch[which]
    if which == 2:
        return pltpu.make_async_copy(src.at[_ffn_window(c), :], dst.at[_ffn_window(c), :],
                                     sem_ref.at[which, c])
    return pltpu.make_async_copy(src.at[:, _ffn_window(c)], dst.at[:, _ffn_window(c)],
                                 sem_ref.at[which, c])


def _ffn_start_weight_copies(w_hbm, base_scratch, copy_scratch):
    for c in range(N_FFN_CHUNKS):
        for which in range(3):
            _ffn_weight_copy(which, c, w_hbm, base_scratch, copy_scratch).start()


def _swiglu_tile(x32, g_ref, w_hbm, base_scratch, stream, stream_scratch, between=(), before_down=()):
    wg_hbm, wu_hbm, wd_hbm = w_hbm
    wg_ref, wu_ref, wd_ref, h_ref, a_ref = base_scratch

    def staged_copy(which, c):
        stage_in_ref, stage_out_ref, sem_ref = stream_scratch
        slot = c % FFN_STAGE_SLOTS
        sem = sem_ref.at[which * FFN_STAGE_SLOTS + slot]
        if which == 2:
            return pltpu.make_async_copy(wd_hbm.at[_ffn_window(c), :], stage_out_ref.at[slot], sem)
        src = (wg_hbm, wu_hbm)[which]
        return pltpu.make_async_copy(src.at[:, _ffn_window(c)],
                                     stage_in_ref.at[which * FFN_STAGE_SLOTS + slot], sem)

    def land(which, c):
        if stream == "wait":
            _ffn_weight_copy(which, c, w_hbm, base_scratch, stream_scratch).wait()
            return
        staged_copy(which, c).wait()
        stage_in_ref, stage_out_ref, _ = stream_scratch
        slot = c % FFN_STAGE_SLOTS
        rows_or_cols = slice(c * FFN_CHUNK_F, (c + 1) * FFN_CHUNK_F)
        if which == 2:
            wd_ref[rows_or_cols, :] = stage_out_ref[slot].astype(jnp.bfloat16)
        else:
            dst = (wg_ref, wu_ref)[which]
            dst[:, rows_or_cols] = stage_in_ref[which * FFN_STAGE_SLOTS + slot].astype(jnp.bfloat16)
        if c + FFN_STAGE_SLOTS < N_FFN_CHUNKS:
            staged_copy(which, c + FFN_STAGE_SLOTS).start(priority=(which + c) % 2)

    if stream == "round":
        for c in range(FFN_STAGE_SLOTS):
            for which in range(3):
                staged_copy(which, c).start(priority=(which + c) % 2)
    if x32 is not None:
        h_ref[...] = _rmsnorm_rows(x32, g_ref[...]).astype(jnp.bfloat16)
    for c in range(N_FFN_CHUNKS):
        cols = slice(c * FFN_CHUNK_F, (c + 1) * FFN_CHUNK_F)
        if stream is not None:
            for which in range(3 if stream == "round" else 2):
                land(which, c)
        h = h_ref[...]
        gate = jnp.dot(h, wg_ref[:, cols], preferred_element_type=jnp.float32)
        up = jnp.dot(h, wu_ref[:, cols], preferred_element_type=jnp.float32)
        act = gate * (1.0 / (1.0 + jnp.exp(-gate))) * up
        a_ref[:, cols] = act.astype(jnp.bfloat16)
        if c < len(between):
            between[c]()
    for stage in tuple(between[N_FFN_CHUNKS:]) + tuple(before_down):
        stage()
    if stream == "wait":
        for c in range(N_FFN_CHUNKS):
            land(2, c)
    return jnp.dot(a_ref[...], wd_ref[...], preferred_element_type=jnp.float32)


def _ffn_proj_kernel(x_ref, g1_ref, wg_hbm, wu_hbm, wd_hbm, gm_ref, win_ref, qg_ref, kg_ref, dm_ref,
                     wg2_ref, wu2_ref, wd2_ref, wo_ref, rel_ref, bucket_ref,
                     x1_ref, z_ref, wg2_bf_ref, wu2_bf_ref, wd2_bf_ref, wo_bf_ref, bias_ref,
                     *scratch):
    tm = x_ref.shape[0]
    base_scratch, round_scratch = scratch[:N_FFN_BASE_SCRATCH], scratch[N_FFN_BASE_SCRATCH:]

    first_half = pl.program_id(0) < pl.num_programs(0) // 2

    @pl.when(first_half)
    def _():
        wg2_bf_ref[...] = wg2_ref[...].astype(jnp.bfloat16)
        wd2_bf_ref[...] = wd2_ref[...].astype(jnp.bfloat16)

    @pl.when(jnp.logical_not(first_half))
    def _():
        wu2_bf_ref[...] = wu2_ref[...].astype(jnp.bfloat16)
        wo_bf_ref[...] = wo_ref[...].astype(jnp.bfloat16)

    def body(stream):
        if stream is not None:
            _write_band_bias(rel_ref, bucket_ref, bias_ref)
        y = _swiglu_tile(x_ref[...], g1_ref, (wg_hbm, wu_hbm, wd_hbm), base_scratch, stream,
                         round_scratch)
        x1 = x_ref[...] + 0.5 * y
        x1_ref[...] = x1
        h = _rmsnorm_rows(x1, gm_ref[...]).astype(jnp.bfloat16)
        z = jnp.dot(h, win_ref[...], preferred_element_type=jnp.float32)
        n_slab = (D_ATTN + 2 * D_KV) // V7X_MXU_DIM
        sq = jnp.concatenate(
            [z[:, i * V7X_MXU_DIM:(i + 1) * V7X_MXU_DIM] for i in range(n_slab)], axis=0)
        ms = jnp.dot((sq * sq).astype(jnp.bfloat16), dm_ref[...], preferred_element_type=jnp.float32)
        q_ms = jnp.concatenate([ms[i * tm:(i + 1) * tm] for i in range(n_slab - 1)], axis=1)
        k_ms = ms[(n_slab - 1) * tm:, :D_KV]
        q = z[:, :D_ATTN]
        k = z[:, D_ATTN:D_ATTN + D_KV]
        z_ref[:, :D_ATTN] = (q * lax.rsqrt(q_ms + EPS) * qg_ref[...]).astype(jnp.bfloat16)
        z_ref[:, D_ATTN:D_ATTN + D_KV] = (k * lax.rsqrt(k_ms + EPS) * kg_ref[...]).astype(jnp.bfloat16)
        z_ref[:, D_ATTN + D_KV:] = z[:, D_ATTN + D_KV:].astype(jnp.bfloat16)

    first = pl.program_id(0) == 0
    pl.when(first)(functools.partial(body, "round"))
    pl.when(jnp.logical_not(first))(functools.partial(body, None))


def _head_mean_matrix(width):
    idx = np.arange(width) // HEAD_DIM
    return jnp.asarray((idx[:, None] == idx[None, :]).astype(np.float32) / HEAD_DIM, jnp.bfloat16)


def _ffn_proj(x2d, ffn_gain, w_gate, w_up, w_down, mix_gain, w_in, q_gain, k_gain,
              next_w_gate, next_w_up, next_w_down, w_out, rel_bias):
    m = x2d.shape[0]
    tm = TILE_M
    n_steps = m // tm
    resident = functools.partial(pl.BlockSpec, pipeline_mode=pl.Buffered(1))
    rows = lambda width: pl.BlockSpec((tm, width), lambda i: (i, 0))
    hbm = pl.BlockSpec(memory_space=pltpu.HBM)
    half_steps = n_steps // 2
    wr = D_MODEL // half_steps
    wdr = D_FF // half_steps
    assert 2 * half_steps == n_steps and wr * half_steps == D_MODEL and wdr * half_steps == D_FF
    assert wr % 16 == 0 and wdr % 16 == 0
    early_slice = lambda i: (jnp.minimum(i, half_steps - 1), 0)
    late_slice = lambda i: (jnp.maximum(i - half_steps, 0), 0)
    wg_rows = pl.BlockSpec((wr, D_FF), early_slice)
    wd_rows = pl.BlockSpec((wdr, D_MODEL), early_slice)
    wu_rows = pl.BlockSpec((wr, D_FF), late_slice)
    wo_rows = pl.BlockSpec((wr, D_MODEL), late_slice)
    qg = (jnp.tile(q_gain, N_HEADS) * (HEAD_DIM ** -0.5)).reshape(1, D_ATTN)
    kg = jnp.tile(k_gain, N_KV_HEADS).reshape(1, D_KV)
    est = (_ffn_vmem_bytes(tm) + 2 * D_MODEL * D_IN
           + 2 * tm * D_MODEL * 4 * 2
           + 2 * tm * D_IN * 2
           + 4 * tm * D_IN * 4
           + 2 * 6 * (2 * wr * D_FF + wdr * D_MODEL + wr * D_MODEL)
           + 3 * N_HEADS * BLOCK * 2 * BLOCK * 4)
    bf16 = lambda shape: jax.ShapeDtypeStruct(shape, jnp.bfloat16)
    return pl.pallas_call(
        _ffn_proj_kernel,
        out_shape=(
            jax.ShapeDtypeStruct((m, D_MODEL), jnp.float32),
            bf16((m, D_IN)),
            bf16((D_MODEL, D_FF)), bf16((D_MODEL, D_FF)), bf16((D_FF, D_MODEL)),
            bf16((D_MODEL, D_MODEL)),
            jax.ShapeDtypeStruct((N_HEADS, BLOCK, 2 * BLOCK), jnp.float32),
        ),
        grid=(n_steps,),
        in_specs=[
            rows(D_MODEL),
            resident((1, D_MODEL), lambda i: (0, 0)),
            hbm, hbm, hbm,
            resident((1, D_MODEL), lambda i: (0, 0)),
            resident((D_MODEL, D_IN), lambda i: (0, 0)),
            resident((1, D_ATTN), lambda i: (0, 0)),
            resident((1, D_KV), lambda i: (0, 0)),
            resident((V7X_MXU_DIM, V7X_MXU_DIM), lambda i: (0, 0)),
            wg_rows, wu_rows, wd_rows, wo_rows,
            pl.BlockSpec(memory_space=pltpu.SMEM),
            resident((BLOCK, 2 * BLOCK), lambda i: (0, 0)),
        ],
        out_specs=(rows(D_MODEL), rows(D_IN), wg_rows, wu_rows, wd_rows, wo_rows,
                   pl.BlockSpec((N_HEADS, BLOCK, 2 * BLOCK), lambda i: (0, 0, 0))),
        scratch_shapes=_ffn_base_scratch(tm) + _ffn_round_scratch(),
        compiler_params=pltpu.CompilerParams(
            dimension_semantics=("arbitrary",),
            vmem_limit_bytes=_vmem_limit(est + (8 << 20))),
        name="ffn_proj",
    )(x2d, ffn_gain.reshape(1, D_MODEL), w_gate, w_up, w_down, mix_gain.reshape(1, D_MODEL),
      w_in.astype(jnp.bfloat16), qg, kg, _head_mean_matrix(V7X_MXU_DIM),
      next_w_gate, next_w_up, next_w_down, w_out, rel_bias, jnp.asarray(_band_bucket_table()))


def _t5_bucket(dist):
    n = np.maximum(dist, 0)
    max_exact = N_BUCKETS // 2
    large = max_exact + (np.log(np.maximum(n, 1) / max_exact)
                         / np.log(MAX_DISTANCE / max_exact)
                         * (N_BUCKETS - max_exact)).astype(np.int32)
    large = np.minimum(large, N_BUCKETS - 1)
    return np.where(n < max_exact, n, large).astype(np.int32)


def _band_bucket_table():
    ql = np.arange(BLOCK)[:, None]
    kl = np.arange(2 * BLOCK)[None, :]
    dist = ql + BLOCK - kl
    in_band = (dist >= 0) & (dist < WINDOW)
    return np.where(in_band, _t5_bucket(dist), -1).astype(np.int32)


def _write_band_bias(rel_ref, bucket_ref, o_ref):
    bucket = bucket_ref[...]
    for slot, head in enumerate(HEADS_PLAIN + HEADS_ROLLED):
        acc = jnp.full((BLOCK, 2 * BLOCK), NEG, jnp.float32)
        for b in range(N_BUCKETS):
            acc = jnp.where(bucket == b, rel_ref[b, head], acc)
        o_ref[slot] = acc


def _window_sum_matrices():
    ql = np.arange(BLOCK)[:, None]
    kl = np.arange(2 * BLOCK)[None, :]
    dist = ql + BLOCK - kl
    mats = [((dist >= 0) & (dist < w)).astype(np.float32) for w in POOL_WINDOWS]
    return jnp.asarray(np.stack(mats), jnp.bfloat16)


def _mix_stages(seq_first, t_base, sink_ref, x_ref, z_ref, zh_ref,
                bias_ref, wsum_ref, wo_ref, x2_ref, mix_scratch, g_ref, h_ref):
    kx_ref, kxr_ref, vx_ref, vxr_ref, ux_ref, pooled_ref, y_ref, wc_ref = mix_scratch
    tq = z_ref.shape[1]
    k_cols = slice(D_ATTN, D_ATTN + D_KV)
    v_cols = slice(D_ATTN + D_KV, D_ATTN + 2 * D_KV)
    u_cols = slice(D_ATTN + 2 * D_KV, D_IN)
    n_blocks = tq // BLOCK
    half = HEAD_DIM
    groups = ((HEADS_PLAIN, kx_ref, vx_ref, 0), (HEADS_ROLLED, kxr_ref, vxr_ref, len(HEADS_PLAIN)))
    lane = lax.broadcasted_iota(jnp.int32, (BLOCK, 2 * half), 1)
    low = lane < half
    state = {}

    def rows_of(j):
        return slice(j * BLOCK, (j + 1) * BLOCK)

    def keys_of(j):
        return slice(j * BLOCK, (j + 2) * BLOCK)

    def build_slabs():
        halo = jnp.where(seq_first, jnp.zeros_like(zh_ref[0, :, D_ATTN:]), zh_ref[0, :, D_ATTN:])
        kx_ref[:BLOCK] = halo[:, :D_KV]
        kx_ref[BLOCK:] = z_ref[0, :, k_cols]
        kxr_ref[...] = pltpu.roll(kx_ref[...].astype(jnp.float32), half, 1).astype(jnp.bfloat16)
        ones = jnp.ones((tq + BLOCK, D_KV), jnp.bfloat16)
        vx_ref[:BLOCK, :D_KV] = halo[:, D_KV:2 * D_KV]
        vx_ref[BLOCK:, :D_KV] = z_ref[0, :, v_cols]
        vx_ref[:, D_KV:] = ones
        vxr_ref[:, :D_KV] = pltpu.roll(
            vx_ref[:, :D_KV].astype(jnp.float32), half, 1).astype(jnp.bfloat16)
        vxr_ref[:, D_KV:] = ones
        ux_ref[:BLOCK] = halo[:, 2 * D_KV:]
        ux_ref[BLOCK:] = z_ref[0, :, u_cols]

    def scores(j, grp):
        heads, k_src, _, _ = groups[grp]
        q = z_ref[0, rows_of(j), :D_ATTN]
        zero = jnp.zeros((BLOCK, 2 * half), q.dtype)
        pieces = []
        for hd in heads:
            pair = q[:, (hd // 2) * 2 * half:(hd // 2 + 1) * 2 * half]
            pieces.append(jnp.where(low if hd % 2 == 0 else ~low, pair, zero))
        qs = jnp.concatenate(pieces, axis=0)
        state["logits", j, grp] = lax.dot_general(
            qs, k_src[keys_of(j), :], (((1,), (1,)), ((), ())),
            preferred_element_type=jnp.float32)

    def softmax_numerators(j, grp):
        heads, _, _, slot0 = groups[grp]
        logits = state.pop(("logits", j, grp))
        es, sink_terms = [], []
        for s, hd in enumerate(heads):
            lg = logits[s * BLOCK:(s + 1) * BLOCK] + bias_ref[slot0 + s]
            if j == 0:
                col = lax.broadcasted_iota(jnp.int32, (1, 2 * BLOCK), 1)
                lg = lg + jnp.where(seq_first & (col < BLOCK), NEG, 0.0).astype(jnp.float32)
            sink = sink_ref[hd]
            m = jnp.maximum(jnp.max(lg, axis=-1, keepdims=True), sink)
            es.append(jnp.exp(lg - m).astype(jnp.bfloat16))
            sink_terms.append(jnp.exp(sink - m))
        state["e", j, grp] = (jnp.concatenate(es, axis=0), sink_terms)

    def attend(j, grp):
        heads, _, v_src, _ = groups[grp]
        e, sink_terms = state.pop(("e", j, grp))
        pv = jnp.dot(e, v_src[keys_of(j), :], preferred_element_type=jnp.float32)
        for s, hd in enumerate(heads):
            blk = pv[s * BLOCK:(s + 1) * BLOCK]
            state["out", j, hd] = blk[:, :D_KV] / (blk[:, D_KV:] + sink_terms[s])

    def store_attention(j):
        for p in range(N_HEADS // 2):
            y_ref[rows_of(j), p * 2 * half:(p + 1) * 2 * half] = jnp.where(
                low, state.pop(("out", j, 2 * p)), state.pop(("out", j, 2 * p + 1))
            ).astype(jnp.bfloat16)

    def pool_means(j):
        row = lax.broadcasted_iota(jnp.int32, (BLOCK, 1), 0)
        t_glob = t_base + j * BLOCK + row
        for g, w in enumerate(POOL_WINDOWS):
            gc = slice(g * POOL_GROUP_DIM, (g + 1) * POOL_GROUP_DIM)
            wsum = jnp.dot(wsum_ref[g], ux_ref[keys_of(j), gc], preferred_element_type=jnp.float32)
            cnt = jnp.minimum(t_glob + 1, w).astype(jnp.float32)
            token = z_ref[0, rows_of(j), D_ATTN + 2 * D_KV + g * POOL_GROUP_DIM:
                          D_ATTN + 2 * D_KV + (g + 1) * POOL_GROUP_DIM]
            pooled = wsum * (1.0 / cnt) - token.astype(jnp.float32)
            pooled_ref[rows_of(j), gc] = pooled.astype(jnp.bfloat16)

    def project_pool():
        x2_ref[...] = x_ref[0] + jnp.dot(pooled_ref[...], wc_ref[...],
                                         preferred_element_type=jnp.float32)

    def project_attention():
        x2 = x2_ref[...] + jnp.dot(y_ref[...], wo_ref[:D_ATTN, :], preferred_element_type=jnp.float32)
        x2_ref[...] = x2
        h_ref[...] = _rmsnorm_rows(x2, g_ref[...]).astype(jnp.bfloat16)

    def seq(*calls):
        def run():
            for fn, *args in calls:
                fn(*args)
        return run

    early = [
        seq((build_slabs,)),
        seq(*[(pool_means, j) for j in range(n_blocks // 2)]),
        seq(*[(pool_means, j) for j in range(n_blocks // 2, n_blocks)]),
        seq((scores, 0, 0), (scores, 0, 1), (softmax_numerators, 0, 0)),
        seq((attend, 0, 0), (softmax_numerators, 0, 1)),
        seq((project_pool,), (scores, 1, 0), (attend, 0, 1), (store_attention, 0)),
    ]
    for j in range(1, n_blocks):
        early.append(seq((scores, j, 1), (softmax_numerators, j, 0), (attend, j, 0)))
        tail = [(softmax_numerators, j, 1)]
        if j + 1 < n_blocks:
            tail.append((scores, j + 1, 0))
        tail += [(attend, j, 1), (store_attention, j)]
        early.append(seq(*tail))
    late = [project_attention]
    return early, late


N_MIX_SCRATCH = 9


def _mix_ffn_kernel(sink_ref, x_ref, z_ref, zh_ref,
                    bias_ref, wsum_ref, pw_ref, ps_ref, wo_ref, g2_ref, wg_hbm, wu_hbm, wd_hbm,
                    o_ref, *scratch, tiles_per_seq):
    mix_scratch, x2_ref = scratch[:N_MIX_SCRATCH - 1], scratch[N_MIX_SCRATCH - 1]
    base_scratch = scratch[N_MIX_SCRATCH:N_MIX_SCRATCH + N_FFN_BASE_SCRATCH]
    copy_scratch = scratch[N_MIX_SCRATCH + N_FFN_BASE_SCRATCH:]
    h_ref = base_scratch[3]
    t = pl.program_id(0)
    tq = z_ref.shape[1]
    t_mix = jnp.minimum(t, pl.num_programs(0) - 2)
    i_seq = lax.rem(t_mix, tiles_per_seq)
    stages = functools.partial(
        _mix_stages, i_seq == 0, i_seq * tq, sink_ref, x_ref, z_ref, zh_ref,
        bias_ref, wsum_ref, wo_ref, x2_ref, mix_scratch, g2_ref, h_ref)
    w_hbm = (wg_hbm, wu_hbm, wd_hbm)

    def first_step():
        _ffn_start_weight_copies(w_hbm, base_scratch, copy_scratch)
        for g in range(len(POOL_WINDOWS)):
            gc = slice(g * POOL_GROUP_DIM, (g + 1) * POOL_GROUP_DIM)
            pool_map = (pw_ref[g] * ps_ref[:, gc]).astype(jnp.bfloat16)
            mix_scratch[-1][gc, :] = jnp.dot(
                pool_map, wo_ref[D_ATTN + g * POOL_GROUP_DIM:D_ATTN + (g + 1) * POOL_GROUP_DIM, :],
                preferred_element_type=jnp.float32).astype(jnp.bfloat16)
        early, late = stages()
        for stage in early + late:
            stage()

    def later_step(wait_weights, mix_next):
        early, late = stages() if mix_next else ((), ())
        o_ref[0] = x2_ref[...]
        y = _swiglu_tile(None, g2_ref, w_hbm, base_scratch, "wait" if wait_weights else None,
                         copy_scratch, between=early, before_down=late)
        o_ref[0] += 0.5 * y

    last = pl.num_programs(0) - 1
    pl.when(t == 0)(first_step)
    pl.when(t == 1)(functools.partial(later_step, True, True))
    pl.when(jnp.logical_and(t > 1, t < last))(functools.partial(later_step, False, True))
    pl.when(t == last)(functools.partial(later_step, False, False))


def _mix_ffn(x3d, z3d, sinks, bias, pool_w, pool_scale, w_out_bf16, ffn_gain,
             w_gate_bf16, w_up_bf16, w_down_bf16):
    b, s, _ = x3d.shape
    tq = TILE_M
    nblk = tq // BLOCK
    tiles_per_seq = s // tq
    n_tiles = b * tiles_per_seq
    resident = functools.partial(pl.BlockSpec, pipeline_mode=pl.Buffered(1))
    hbm = pl.BlockSpec(memory_space=pltpu.HBM)

    def mixed(t):
        tm = jnp.minimum(t, n_tiles - 1)
        return tm // tiles_per_seq, lax.rem(tm, tiles_per_seq)

    def tile(width):
        return pl.BlockSpec((1, tq, width), lambda t: (*mixed(t), 0))

    def halo(width):
        def index(t):
            bi, i = mixed(t)
            return bi, jnp.maximum(i * nblk - 1, 0), 0
        return pl.BlockSpec((1, BLOCK, width), index)

    def out_index(t):
        tf = jnp.maximum(t - 1, 0)
        return tf // tiles_per_seq, lax.rem(tf, tiles_per_seq), 0

    est = (_ffn_vmem_bytes(tq)
           + 2 * 2 * tq * D_MODEL * 4 + 2 * (tq + BLOCK) * D_IN * 2
           + N_HEADS * BLOCK * 2 * BLOCK * 4 + 2 * D_MODEL * D_MODEL + 4 * D_POOL * D_POOL
           + 2 * D_POOL * D_MODEL
           + 6 * (tq + BLOCK) * D_KV * 2 + (tq + BLOCK) * D_POOL * 2 + tq * D_MODEL * 2
           + tq * D_MODEL * 4 + 16 * BLOCK * 2 * BLOCK * 4 * 4)
    return pl.pallas_call(
        functools.partial(_mix_ffn_kernel, tiles_per_seq=tiles_per_seq),
        out_shape=jax.ShapeDtypeStruct((b, s, D_MODEL), jnp.float32),
        grid=(n_tiles + 1,),
        in_specs=[
            pl.BlockSpec(memory_space=pltpu.SMEM),
            tile(D_MODEL),
            tile(D_IN), halo(D_IN),
            resident((N_HEADS, BLOCK, 2 * BLOCK), lambda t: (0, 0, 0)),
            resident((len(POOL_WINDOWS), BLOCK, 2 * BLOCK), lambda t: (0, 0, 0)),
            resident((len(POOL_WINDOWS), POOL_GROUP_DIM, POOL_GROUP_DIM), lambda t: (0, 0, 0)),
            resident((1, D_POOL), lambda t: (0, 0)),
            resident((D_MODEL, D_MODEL), lambda t: (0, 0)),
            resident((1, D_MODEL), lambda t: (0, 0)),
            hbm, hbm, hbm,
        ],
        out_specs=pl.BlockSpec((1, tq, D_MODEL), out_index),
        scratch_shapes=[
            pltpu.VMEM((tq + BLOCK, D_KV), jnp.bfloat16),
            pltpu.VMEM((tq + BLOCK, D_KV), jnp.bfloat16),
            pltpu.VMEM((tq + BLOCK, 2 * D_KV), jnp.bfloat16),
            pltpu.VMEM((tq + BLOCK, 2 * D_KV), jnp.bfloat16),
            pltpu.VMEM((tq + BLOCK, D_POOL), jnp.bfloat16),
            pltpu.VMEM((tq, D_POOL), jnp.bfloat16),
            pltpu.VMEM((tq, D_ATTN), jnp.bfloat16),
            pltpu.VMEM((D_POOL, D_MODEL), jnp.bfloat16),
            pltpu.VMEM((tq, D_MODEL), jnp.float32),
        ] + _ffn_base_scratch(tq) + _ffn_copy_scratch(),
        compiler_params=pltpu.CompilerParams(
            dimension_semantics=("arbitrary",),
            vmem_limit_bytes=_vmem_limit(est + (8 << 20))),
        name="mix_ffn",
    )(sinks, x3d, z3d, z3d, bias, _window_sum_matrices(),
      pool_w, pool_scale.reshape(1, D_POOL),
      w_out_bf16, ffn_gain.reshape(1, D_MODEL), w_gate_bf16, w_up_bf16, w_down_bf16)


def kernel(x, ffn1_norm, ffn1_w_gate, ffn1_w_up, ffn1_w_down, mix_norm, w_in, q_norm, k_norm,
           attn_sinks, rel_bias, pool_w, pool_scale, w_out, ffn2_norm, ffn2_w_gate, ffn2_w_up,
           ffn2_w_down):
    b, s, d = x.shape
    assert (d, s % TILE_M) == (D_MODEL, 0)
    for l in range(ffn1_norm.shape[0]):
        x1, z, wg2, wu2, wd2, wo, bias = _ffn_proj(
            x.reshape(b * s, d), ffn1_norm[l], ffn1_w_gate[l], ffn1_w_up[l], ffn1_w_down[l],
            mix_norm[l], w_in[l], q_norm[l], k_norm[l],
            ffn2_w_gate[l], ffn2_w_up[l], ffn2_w_down[l], w_out[l], rel_bias)
        x = _mix_ffn(x1.reshape(b, s, d), z.reshape(b, s, D_IN), attn_sinks[l], bias,
                     pool_w[l], pool_scale[l], wo, ffn2_norm[l], wg2, wu2, wd2)
    return x
```

```python
import functools

import numpy as np
import jax
import jax.numpy as jnp
from jax import lax
from jax.experimental import pallas as pl
from jax.experimental.pallas import tpu as pltpu

D_MODEL = 1024
HEAD_DIM = 64
N_HEADS = 8
N_KV_HEADS = 2
D_ATTN = N_HEADS * HEAD_DIM
D_KV = N_KV_HEADS * HEAD_DIM
D_POOL = D_MODEL - D_ATTN
POOL_WINDOWS = (2, 4, 8, 16)
POOL_GROUP_DIM = D_POOL // len(POOL_WINDOWS)
D_IN = D_ATTN + 2 * D_KV + D_POOL
WINDOW = 128
BLOCK = 128
N_BUCKETS = 32
MAX_DISTANCE = 128
D_FF = 2816
EPS = 1e-6
NEG = -1e30

V7X_LANES = 128
V7X_MXU_DIM = 256
V7X_VMEM_BYTES = 64 * 1024 * 1024

HEADS_PLAIN = (0, 2, 5, 7)
HEADS_ROLLED = (1, 3, 4, 6)

TILE_M = 512
FFN_CHUNK_F = 256
FFN_STAGE_SLOTS = 2


def _vmem_limit(estimate_bytes):
    return int(min(V7X_VMEM_BYTES - (4 << 20), max(estimate_bytes, 16 << 20)))


def _rmsnorm_rows(x32, gain_row):
    ms = jnp.mean(x32 * x32, axis=-1, keepdims=True)
    return x32 * lax.rsqrt(ms + EPS) * gain_row


N_FFN_CHUNKS = D_FF // FFN_CHUNK_F
N_FFN_BASE_SCRATCH = 5


def _ffn_base_scratch(tm):
    return [
        pltpu.VMEM((D_MODEL, D_FF), jnp.bfloat16),
        pltpu.VMEM((D_MODEL, D_FF), jnp.bfloat16),
        pltpu.VMEM((D_FF, D_MODEL), jnp.bfloat16),
        pltpu.VMEM((tm, D_MODEL), jnp.bfloat16),
        pltpu.VMEM((tm, D_FF), jnp.bfloat16),
    ]


def _ffn_round_scratch():
    return [
        pltpu.VMEM((2 * FFN_STAGE_SLOTS, D_MODEL, FFN_CHUNK_F), jnp.float32),
        pltpu.VMEM((FFN_STAGE_SLOTS, FFN_CHUNK_F, D_MODEL), jnp.float32),
        pltpu.SemaphoreType.DMA((3 * FFN_STAGE_SLOTS,)),
    ]


def _ffn_copy_scratch():
    return [pltpu.SemaphoreType.DMA((3, N_FFN_CHUNKS))]


def _ffn_vmem_bytes(tm):
    return (2 * 3 * D_MODEL * D_FF + 3 * FFN_STAGE_SLOTS * D_MODEL * FFN_CHUNK_F * 4
            + tm * D_MODEL * 2 + tm * D_FF * 2 + 6 * tm * FFN_CHUNK_F * 4 + tm * D_MODEL * 4)


def _ffn_window(c):
    return pl.ds(c * FFN_CHUNK_F, FFN_CHUNK_F)


def _ffn_weight_copy(which, c, w_hbm, base_scratch, copy_scratch):
    (sem_ref,) = copy_scratch
    src, dst = w_hbm[which], base_scratch[which]
    if which == 2:
        return pltpu.make_async_copy(src.at[_ffn_window(c), :], dst.at[_ffn_window(c), :],
                                     sem_ref.at[which, c])
    return pltpu.make_async_copy(src.at[:, _ffn_window(c)], dst.at[:, _ffn_window(c)],
                                 sem_ref.at[which, c])


def _ffn_start_weight_copies(w_hbm, base_scratch, copy_scratch):
    for c in range(N_FFN_CHUNKS):
        for which in range(3):
            _ffn_weight_copy(which, c, w_hbm, base_scratch, copy_scratch).start()


def _swiglu_tile(x32, g_ref, w_hbm, base_scratch, stream, stream_scratch, between=(), before_down=()):
    wg_hbm, wu_hbm, wd_hbm = w_hbm
    wg_ref, wu_ref, wd_ref, h_ref, a_ref = base_scratch

    def staged_copy(which, c):
        stage_in_ref, stage_out_ref, sem_ref = stream_scratch
        slot = c % FFN_STAGE_SLOTS
        sem = sem_ref.at[which * FFN_STAGE_SLOTS + slot]
        if which == 2:
            return pltpu.make_async_copy(wd_hbm.at[_ffn_window(c), :], stage_out_ref.at[slot], sem)
        src = (wg_hbm, wu_hbm)[which]
        return pltpu.make_async_copy(src.at[:, _ffn_window(c)],
                                     stage_in_ref.at[which * FFN_STAGE_SLOTS + slot], sem)

    def land(which, c):
        if stream == "wait":
            _ffn_weight_copy(which, c, w_hbm, base_scratch, stream_scratch).wait()
            return
        staged_copy(which, c).wait()
        stage_in_ref, stage_out_ref, _ = stream_scratch
        slot = c % FFN_STAGE_SLOTS
        rows_or_cols = slice(c * FFN_CHUNK_F, (c + 1) * FFN_CHUNK_F)
        if which == 2:
            wd_ref[rows_or_cols, :] = stage_out_ref[slot].astype(jnp.bfloat16)
        else:
            dst = (wg_ref, wu_ref)[which]
            dst[:, rows_or_cols] = stage_in_ref[which * FFN_STAGE_SLOTS + slot].astype(jnp.bfloat16)
        if c + FFN_STAGE_SLOTS < N_FFN_CHUNKS:
            staged_copy(which, c + FFN_STAGE_SLOTS).start()

    if stream == "round":
        for c in range(FFN_STAGE_SLOTS):
            for which in range(3):
                staged_copy(which, c).start()
    if x32 is not None:
        h_ref[...] = _rmsnorm_rows(x32, g_ref[...]).astype(jnp.bfloat16)
    for c in range(N_FFN_CHUNKS):
        cols = slice(c * FFN_CHUNK_F, (c + 1) * FFN_CHUNK_F)
        if stream is not None:
            for which in range(3 if stream == "round" else 2):
                land(which, c)
        h = h_ref[...]
        gate = jnp.dot(h, wg_ref[:, cols], preferred_element_type=jnp.float32)
        up = jnp.dot(h, wu_ref[:, cols], preferred_element_type=jnp.float32)
        act = gate * (1.0 / (1.0 + jnp.exp(-gate))) * up
        a_ref[:, cols] = act.astype(jnp.bfloat16)
        if c < len(between):
            between[c]()
    for stage in tuple(between[N_FFN_CHUNKS:]) + tuple(before_down):
        stage()
    if stream == "wait":
        for c in range(N_FFN_CHUNKS):
            land(2, c)
    return jnp.dot(a_ref[...], wd_ref[...], preferred_element_type=jnp.float32)


def _ffn_proj_kernel(x_ref, xn_ref, g1_ref, wg_hbm, wu_hbm, wd_hbm, gm_ref, win_ref, qg_ref, kg_ref,
                     dm_ref, wg2_ref, wu2_ref, wd2_ref, wo_ref, rel_ref, bucket_ref,
                     x1_ref, z_ref, wg2_bf_ref, wu2_bf_ref, wd2_bf_ref, wo_bf_ref, bias_ref,
                     *scratch):
    tm = x_ref.shape[0]
    base_scratch, round_scratch = scratch[:N_FFN_BASE_SCRATCH], scratch[N_FFN_BASE_SCRATCH:]

    first_half = pl.program_id(0) < pl.num_programs(0) // 2

    @pl.when(first_half)
    def _():
        wg2_bf_ref[...] = wg2_ref[...].astype(jnp.bfloat16)
        wd2_bf_ref[...] = wd2_ref[...].astype(jnp.bfloat16)

    @pl.when(jnp.logical_not(first_half))
    def _():
        wu2_bf_ref[...] = wu2_ref[...].astype(jnp.bfloat16)
        wo_bf_ref[...] = wo_ref[...].astype(jnp.bfloat16)

    h_ref = base_scratch[3]

    def normalize_next_tile():
        h_ref[...] = _rmsnorm_rows(xn_ref[...], g1_ref[...]).astype(jnp.bfloat16)

    def body(stream):
        if stream is not None:
            _write_band_bias(rel_ref, bucket_ref, bias_ref)
        y = _swiglu_tile(x_ref[...] if stream is not None else None, g1_ref,
                         (wg_hbm, wu_hbm, wd_hbm), base_scratch, stream, round_scratch,
                         before_down=(normalize_next_tile,))
        x1 = x_ref[...] + 0.5 * y
        x1_ref[...] = x1
        h = _rmsnorm_rows(x1, gm_ref[...]).astype(jnp.bfloat16)
        z = jnp.dot(h, win_ref[...], preferred_element_type=jnp.float32)
        n_slab = (D_ATTN + 2 * D_KV) // V7X_MXU_DIM
        sq = jnp.concatenate(
            [z[:, i * V7X_MXU_DIM:(i + 1) * V7X_MXU_DIM] for i in range(n_slab)], axis=0)
        ms = jnp.dot((sq * sq).astype(jnp.bfloat16), dm_ref[...], preferred_element_type=jnp.float32)
        q_ms = jnp.concatenate([ms[i * tm:(i + 1) * tm] for i in range(n_slab - 1)], axis=1)
        k_ms = ms[(n_slab - 1) * tm:, :D_KV]
        q = z[:, :D_ATTN]
        k = z[:, D_ATTN:D_ATTN + D_KV]
        z_ref[:, :D_ATTN] = (q * lax.rsqrt(q_ms + EPS) * qg_ref[...]).astype(jnp.bfloat16)
        z_ref[:, D_ATTN:D_ATTN + D_KV] = (k * lax.rsqrt(k_ms + EPS) * kg_ref[...]).astype(jnp.bfloat16)
        z_ref[:, D_ATTN + D_KV:] = z[:, D_ATTN + D_KV:].astype(jnp.bfloat16)

    first = pl.program_id(0) == 0
    pl.when(first)(functools.partial(body, "round"))
    pl.when(jnp.logical_not(first))(functools.partial(body, None))


def _head_mean_matrix(width):
    idx = np.arange(width) // HEAD_DIM
    return jnp.asarray((idx[:, None] == idx[None, :]).astype(np.float32) / HEAD_DIM, jnp.bfloat16)


def _ffn_proj(x2d, ffn_gain, w_gate, w_up, w_down, mix_gain, w_in, q_gain, k_gain,
              next_w_gate, next_w_up, next_w_down, w_out, rel_bias):
    m = x2d.shape[0]
    tm = TILE_M
    n_steps = m // tm
    resident = functools.partial(pl.BlockSpec, pipeline_mode=pl.Buffered(1))
    rows = lambda width: pl.BlockSpec((tm, width), lambda i: (i, 0))
    hbm = pl.BlockSpec(memory_space=pltpu.HBM)
    half_steps = n_steps // 2
    wr = D_MODEL // half_steps
    wdr = D_FF // half_steps
    assert 2 * half_steps == n_steps and wr * half_steps == D_MODEL and wdr * half_steps == D_FF
    assert wr % 16 == 0 and wdr % 16 == 0
    early_slice = lambda i: (jnp.minimum(i, half_steps - 1), 0)
    late_slice = lambda i: (jnp.maximum(i - half_steps, 0), 0)
    wg_rows = pl.BlockSpec((wr, D_FF), early_slice)
    wd_rows = pl.BlockSpec((wdr, D_MODEL), early_slice)
    wu_rows = pl.BlockSpec((wr, D_FF), late_slice)
    wo_rows = pl.BlockSpec((wr, D_MODEL), late_slice)
    qg = (jnp.tile(q_gain, N_HEADS) * (HEAD_DIM ** -0.5)).reshape(1, D_ATTN)
    kg = jnp.tile(k_gain, N_KV_HEADS).reshape(1, D_KV)
    est = (_ffn_vmem_bytes(tm) + 2 * D_MODEL * D_IN
           + 3 * tm * D_MODEL * 4 * 2
           + 2 * tm * D_IN * 2
           + 4 * tm * D_IN * 4
           + 2 * 6 * (2 * wr * D_FF + wdr * D_MODEL + wr * D_MODEL)
           + 3 * N_HEADS * BLOCK * 2 * BLOCK * 4)
    bf16 = lambda shape: jax.ShapeDtypeStruct(shape, jnp.bfloat16)
    return pl.pallas_call(
        _ffn_proj_kernel,
        out_shape=(
            jax.ShapeDtypeStruct((m, D_MODEL), jnp.float32),
            bf16((m, D_IN)),
            bf16((D_MODEL, D_FF)), bf16((D_MODEL, D_FF)), bf16((D_FF, D_MODEL)),
            bf16((D_MODEL, D_MODEL)),
            jax.ShapeDtypeStruct((N_HEADS, BLOCK, 2 * BLOCK), jnp.float32),
        ),
        grid=(n_steps,),
        in_specs=[
            rows(D_MODEL),
            pl.BlockSpec((tm, D_MODEL), lambda i: (jnp.minimum(i + 1, n_steps - 1), 0)),
            resident((1, D_MODEL), lambda i: (0, 0)),
            hbm, hbm, hbm,
            resident((1, D_MODEL), lambda i: (0, 0)),
            resident((D_MODEL, D_IN), lambda i: (0, 0)),
            resident((1, D_ATTN), lambda i: (0, 0)),
            resident((1, D_KV), lambda i: (0, 0)),
            resident((V7X_MXU_DIM, V7X_MXU_DIM), lambda i: (0, 0)),
            wg_rows, wu_rows, wd_rows, wo_rows,
            pl.BlockSpec(memory_space=pltpu.SMEM),
            resident((BLOCK, 2 * BLOCK), lambda i: (0, 0)),
        ],
        out_specs=(rows(D_MODEL), rows(D_IN), wg_rows, wu_rows, wd_rows, wo_rows,
                   pl.BlockSpec((N_HEADS, BLOCK, 2 * BLOCK), lambda i: (0, 0, 0))),
        scratch_shapes=_ffn_base_scratch(tm) + _ffn_round_scratch(),
        compiler_params=pltpu.CompilerParams(
            dimension_semantics=("arbitrary",),
            vmem_limit_bytes=_vmem_limit(est + (8 << 20))),
        name="ffn_proj",
    )(x2d, x2d, ffn_gain.reshape(1, D_MODEL), w_gate, w_up, w_down, mix_gain.reshape(1, D_MODEL),
      w_in.astype(jnp.bfloat16), qg, kg, _head_mean_matrix(V7X_MXU_DIM),
      next_w_gate, next_w_up, next_w_down, w_out, rel_bias, jnp.asarray(_band_bucket_table()))


def _t5_bucket(dist):
    n = np.maximum(dist, 0)
    max_exact = N_BUCKETS // 2
    large = max_exact + (np.log(np.maximum(n, 1) / max_exact)
                         / np.log(MAX_DISTANCE / max_exact)
                         * (N_BUCKETS - max_exact)).astype(np.int32)
    large = np.minimum(large, N_BUCKETS - 1)
    return np.where(n < max_exact, n, large).astype(np.int32)


def _band_bucket_table():
    ql = np.arange(BLOCK)[:, None]
    kl = np.arange(2 * BLOCK)[None, :]
    dist = ql + BLOCK - kl
    in_band = (dist >= 0) & (dist < WINDOW)
    return np.where(in_band, _t5_bucket(dist), -1).astype(np.int32)


def _write_band_bias(rel_ref, bucket_ref, o_ref):
    bucket = bucket_ref[...]
    for slot, head in enumerate(HEADS_PLAIN + HEADS_ROLLED):
        acc = jnp.full((BLOCK, 2 * BLOCK), NEG, jnp.float32)
        for b in range(N_BUCKETS):
            acc = jnp.where(bucket == b, rel_ref[b, head], acc)
        o_ref[slot] = acc


def _window_sum_matrices():
    ql = np.arange(BLOCK)[:, None]
    kl = np.arange(2 * BLOCK)[None, :]
    dist = ql + BLOCK - kl
    mats = [((dist >= 0) & (dist < w)).astype(np.float32) for w in POOL_WINDOWS]
    return jnp.asarray(np.stack(mats), jnp.bfloat16)


def _mix_stages(seq_first, t_base, sink_ref, x_ref, z_ref, zh_ref,
                bias_ref, wsum_ref, wo_ref, x2_ref, mix_scratch, g_ref, h_ref):
    kx_ref, kxr_ref, vx_ref, vxr_ref, ux_ref, pooled_ref, y_ref, wc_ref = mix_scratch
    tq = z_ref.shape[1]
    k_cols = slice(D_ATTN, D_ATTN + D_KV)
    v_cols = slice(D_ATTN + D_KV, D_ATTN + 2 * D_KV)
    u_cols = slice(D_ATTN + 2 * D_KV, D_IN)
    n_blocks = tq // BLOCK
    half = HEAD_DIM
    groups = ((HEADS_PLAIN, kx_ref, vx_ref, 0), (HEADS_ROLLED, kxr_ref, vxr_ref, len(HEADS_PLAIN)))
    lane = lax.broadcasted_iota(jnp.int32, (BLOCK, 2 * half), 1)
    low = lane < half
    state = {}

    def rows_of(j):
        return slice(j * BLOCK, (j + 1) * BLOCK)

    def keys_of(j):
        return slice(j * BLOCK, (j + 2) * BLOCK)

    def build_slabs():
        halo = jnp.where(seq_first, jnp.zeros_like(zh_ref[0, :, D_ATTN:]), zh_ref[0, :, D_ATTN:])
        kx_ref[:BLOCK] = halo[:, :D_KV]
        kx_ref[BLOCK:] = z_ref[0, :, k_cols]
        kxr_ref[...] = pltpu.roll(kx_ref[...].astype(jnp.float32), half, 1).astype(jnp.bfloat16)
        ones = jnp.ones((tq + BLOCK, D_KV), jnp.bfloat16)
        vx_ref[:BLOCK, :D_KV] = halo[:, D_KV:2 * D_KV]
        vx_ref[BLOCK:, :D_KV] = z_ref[0, :, v_cols]
        vx_ref[:, D_KV:] = ones
        vxr_ref[:, :D_KV] = pltpu.roll(
            vx_ref[:, :D_KV].astype(jnp.float32), half, 1).astype(jnp.bfloat16)
        vxr_ref[:, D_KV:] = ones
        ux_ref[:BLOCK] = halo[:, 2 * D_KV:]
        ux_ref[BLOCK:] = z_ref[0, :, u_cols]

    def scores(j, grp):
        heads, k_src, _, _ = groups[grp]
        q = z_ref[0, rows_of(j), :D_ATTN]
        zero = jnp.zeros((BLOCK, 2 * half), q.dtype)
        pieces = []
        for hd in heads:
            pair = q[:, (hd // 2) * 2 * half:(hd // 2 + 1) * 2 * half]
            pieces.append(jnp.where(low if hd % 2 == 0 else ~low, pair, zero))
        qs = jnp.concatenate(pieces, axis=0)
        state["logits", j, grp] = lax.dot_general(
            qs, k_src[keys_of(j), :], (((1,), (1,)), ((), ())),
            preferred_element_type=jnp.float32)

    def softmax_numerators(j, grp):
        heads, _, _, slot0 = groups[grp]
        logits = state.pop(("logits", j, grp))
        es, sink_terms = [], []
        for s, hd in enumerate(heads):
            lg = logits[s * BLOCK:(s + 1) * BLOCK] + bias_ref[slot0 + s]
            if j == 0:
                col = lax.broadcasted_iota(jnp.int32, (1, 2 * BLOCK), 1)
                lg = lg + jnp.where(seq_first & (col < BLOCK), NEG, 0.0).astype(jnp.float32)
            sink = sink_ref[hd]
            m = jnp.maximum(jnp.max(lg, axis=-1, keepdims=True), sink)
            es.append(jnp.exp(lg - m).astype(jnp.bfloat16))
            sink_terms.append(jnp.exp(sink - m))
        state["e", j, grp] = (jnp.concatenate(es, axis=0), sink_terms)

    def attend(j, grp):
        heads, _, v_src, _ = groups[grp]
        e, sink_terms = state.pop(("e", j, grp))
        pv = jnp.dot(e, v_src[keys_of(j), :], preferred_element_type=jnp.float32)
        for s, hd in enumerate(heads):
            blk = pv[s * BLOCK:(s + 1) * BLOCK]
            state["out", j, hd] = blk[:, :D_KV] / (blk[:, D_KV:] + sink_terms[s])

    def store_attention(j):
        for p in range(N_HEADS // 2):
            y_ref[rows_of(j), p * 2 * half:(p + 1) * 2 * half] = jnp.where(
                low, state.pop(("out", j, 2 * p)), state.pop(("out", j, 2 * p + 1))
            ).astype(jnp.bfloat16)

    def pool_means(j):
        row = lax.broadcasted_iota(jnp.int32, (BLOCK, 1), 0)
        t_glob = t_base + j * BLOCK + row
        for g, w in enumerate(POOL_WINDOWS):
            gc = slice(g * POOL_GROUP_DIM, (g + 1) * POOL_GROUP_DIM)
            wsum = jnp.dot(wsum_ref[g], ux_ref[keys_of(j), gc], preferred_element_type=jnp.float32)
            cnt = jnp.minimum(t_glob + 1, w).astype(jnp.float32)
            token = z_ref[0, rows_of(j), D_ATTN + 2 * D_KV + g * POOL_GROUP_DIM:
                          D_ATTN + 2 * D_KV + (g + 1) * POOL_GROUP_DIM]
            pooled = wsum * (1.0 / cnt) - token.astype(jnp.float32)
            pooled_ref[rows_of(j), gc] = pooled.astype(jnp.bfloat16)

    def project_pool():
        x2_ref[...] = x_ref[0] + jnp.dot(pooled_ref[...], wc_ref[...],
                                         preferred_element_type=jnp.float32)

    def project_attention():
        x2 = x2_ref[...] + jnp.dot(y_ref[...], wo_ref[:D_ATTN, :], preferred_element_type=jnp.float32)
        x2_ref[...] = x2
        h_ref[...] = _rmsnorm_rows(x2, g_ref[...]).astype(jnp.bfloat16)

    def seq(*calls):
        def run():
            for fn, *args in calls:
                fn(*args)
        return run

    early = [
        seq((build_slabs,)),
        seq(*[(pool_means, j) for j in range(n_blocks // 2)]),
        seq(*[(pool_means, j) for j in range(n_blocks // 2, n_blocks)]),
        seq((scores, 0, 0), (scores, 0, 1), (softmax_numerators, 0, 0)),
        seq((attend, 0, 0), (softmax_numerators, 0, 1)),
        seq((project_pool,), (scores, 1, 0), (attend, 0, 1), (store_attention, 0)),
    ]
    for j in range(1, n_blocks):
        early.append(seq((scores, j, 1), (softmax_numerators, j, 0), (attend, j, 0)))
        tail = [(softmax_numerators, j, 1)]
        if j + 1 < n_blocks:
            tail.append((scores, j + 1, 0))
        tail += [(attend, j, 1), (store_attention, j)]
        early.append(seq(*tail))
    late = [project_attention]
    return early, late


N_MIX_SCRATCH = 9


def _mix_ffn_kernel(sink_ref, x_ref, z_ref, zh_ref,
                    bias_ref, wsum_ref, pw_ref, ps_ref, wo_ref, g2_ref, wg_hbm, wu_hbm, wd_hbm,
                    o_ref, *scratch, tiles_per_seq):
    mix_scratch, x2_ref = scratch[:N_MIX_SCRATCH - 1], scratch[N_MIX_SCRATCH - 1]
    base_scratch = scratch[N_MIX_SCRATCH:N_MIX_SCRATCH + N_FFN_BASE_SCRATCH]
    copy_scratch = scratch[N_MIX_SCRATCH + N_FFN_BASE_SCRATCH:]
    h_ref = base_scratch[3]
    t = pl.program_id(0)
    tq = z_ref.shape[1]
    t_mix = jnp.minimum(t, pl.num_programs(0) - 2)
    i_seq = lax.rem(t_mix, tiles_per_seq)
    stages = functools.partial(
        _mix_stages, i_seq == 0, i_seq * tq, sink_ref, x_ref, z_ref, zh_ref,
        bias_ref, wsum_ref, wo_ref, x2_ref, mix_scratch, g2_ref, h_ref)
    w_hbm = (wg_hbm, wu_hbm, wd_hbm)

    def first_step():
        _ffn_start_weight_copies(w_hbm, base_scratch, copy_scratch)
        for g in range(len(POOL_WINDOWS)):
            gc = slice(g * POOL_GROUP_DIM, (g + 1) * POOL_GROUP_DIM)
            pool_map = (pw_ref[g] * ps_ref[:, gc]).astype(jnp.bfloat16)
            mix_scratch[-1][gc, :] = jnp.dot(
                pool_map, wo_ref[D_ATTN + g * POOL_GROUP_DIM:D_ATTN + (g + 1) * POOL_GROUP_DIM, :],
                preferred_element_type=jnp.float32).astype(jnp.bfloat16)
        early, late = stages()
        for stage in early + late:
            stage()

    def later_step(wait_weights, mix_next):
        early, late = stages() if mix_next else ((), ())
        o_ref[0] = x2_ref[...]
        y = _swiglu_tile(None, g2_ref, w_hbm, base_scratch, "wait" if wait_weights else None,
                         copy_scratch, between=early, before_down=late)
        o_ref[0] += 0.5 * y

    last = pl.num_programs(0) - 1
    pl.when(t == 0)(first_step)
    pl.when(t == 1)(functools.partial(later_step, True, True))
    pl.when(jnp.logical_and(t > 1, t < last))(functools.partial(later_step, False, True))
    pl.when(t == last)(functools.partial(later_step, False, False))


def _mix_ffn(x3d, z3d, sinks, bias, pool_w, pool_scale, w_out_bf16, ffn_gain,
             w_gate_bf16, w_up_bf16, w_down_bf16):
    b, s, _ = x3d.shape
    tq = TILE_M
    nblk = tq // BLOCK
    tiles_per_seq = s // tq
    n_tiles = b * tiles_per_seq
    resident = functools.partial(pl.BlockSpec, pipeline_mode=pl.Buffered(1))
    hbm = pl.BlockSpec(memory_space=pltpu.HBM)

    def mixed(t):
        tm = jnp.minimum(t, n_tiles - 1)
        return tm // tiles_per_seq, lax.rem(tm, tiles_per_seq)

    def tile(width):
        return pl.BlockSpec((1, tq, width), lambda t: (*mixed(t), 0))

    def halo(width):
        def index(t):
            bi, i = mixed(t)
            return bi, jnp.maximum(i * nblk - 1, 0), 0
        return pl.BlockSpec((1, BLOCK, width), index)

    def out_index(t):
        tf = jnp.maximum(t - 1, 0)
        return tf // tiles_per_seq, lax.rem(tf, tiles_per_seq), 0

    est = (_ffn_vmem_bytes(tq)
           + 2 * 2 * tq * D_MODEL * 4 + 2 * (tq + BLOCK) * D_IN * 2
           + N_HEADS * BLOCK * 2 * BLOCK * 4 + 2 * D_MODEL * D_MODEL + 4 * D_POOL * D_POOL
           + 2 * D_POOL * D_MODEL
           + 6 * (tq + BLOCK) * D_KV * 2 + (tq + BLOCK) * D_POOL * 2 + tq * D_MODEL * 2
           + tq * D_MODEL * 4 + 16 * BLOCK * 2 * BLOCK * 4 * 4)
    return pl.pallas_call(
        functools.partial(_mix_ffn_kernel, tiles_per_seq=tiles_per_seq),
        out_shape=jax.ShapeDtypeStruct((b, s, D_MODEL), jnp.float32),
        grid=(n_tiles + 1,),
        in_specs=[
            pl.BlockSpec(memory_space=pltpu.SMEM),
            tile(D_MODEL),
            tile(D_IN), halo(D_IN),
            resident((N_HEADS, BLOCK, 2 * BLOCK), lambda t: (0, 0, 0)),
            resident((len(POOL_WINDOWS), BLOCK, 2 * BLOCK), lambda t: (0, 0, 0)),
            resident((len(POOL_WINDOWS), POOL_GROUP_DIM, POOL_GROUP_DIM), lambda t: (0, 0, 0)),
            resident((1, D_POOL), lambda t: (0, 0)),
            resident((D_MODEL, D_MODEL), lambda t: (0, 0)),
            resident((1, D_MODEL), lambda t: (0, 0)),
            hbm, hbm, hbm,
        ],
        out_specs=pl.BlockSpec((1, tq, D_MODEL), out_index),
        scratch_shapes=[
            pltpu.VMEM((tq + BLOCK, D_KV), jnp.bfloat16),
            pltpu.VMEM((tq + BLOCK, D_KV), jnp.bfloat16),
            pltpu.VMEM((tq + BLOCK, 2 * D_KV), jnp.bfloat16),
            pltpu.VMEM((tq + BLOCK, 2 * D_KV), jnp.bfloat16),
            pltpu.VMEM((tq + BLOCK, D_POOL), jnp.bfloat16),
            pltpu.VMEM((tq, D_POOL), jnp.bfloat16),
            pltpu.VMEM((tq, D_ATTN), jnp.bfloat16),
            pltpu.VMEM((D_POOL, D_MODEL), jnp.bfloat16),
            pltpu.VMEM((tq, D_MODEL), jnp.float32),
        ] + _ffn_base_scratch(tq) + _ffn_copy_scratch(),
        compiler_params=pltpu.CompilerParams(
            dimension_semantics=("arbitrary",),
            vmem_limit_bytes=_vmem_limit(est + (8 << 20))),
        name="mix_ffn",
    )(sinks, x3d, z3d, z3d, bias, _window_sum_matrices(),
      pool_w, pool_scale.reshape(1, D_POOL),
      w_out_bf16, ffn_gain.reshape(1, D_MODEL), w_gate_bf16, w_up_bf16, w_down_bf16)


def kernel(x, ffn1_norm, ffn1_w_gate, ffn1_w_up, ffn1_w_down, mix_norm, w_in, q_norm, k_norm,
           attn_sinks, rel_bias, pool_w, pool_scale, w_out, ffn2_norm, ffn2_w_gate, ffn2_w_up,
           ffn2_w_down):
    b, s, d = x.shape
    assert (d, s % TILE_M) == (D_MODEL, 0)
    for l in range(ffn1_norm.shape[0]):
        x1, z, wg2, wu2, wd2, wo, bias = _ffn_proj(
            x.reshape(b * s, d), ffn1_norm[l], ffn1_w_gate[l], ffn1_w_up[l], ffn1_w_down[l],
            mix_norm[l], w_in[l], q_norm[l], k_norm[l],
            ffn2_w_gate[l], ffn2_w_up[l], ffn2_w_down[l], w_out[l], rel_bias)
        x = _mix_ffn(x1.reshape(b, s, d), z.reshape(b, s, D_IN), attn_sinks[l], bias,
                     pool_w[l], pool_scale[l], wo, ffn2_norm[l], wg2, wu2, wd2)
    return x
```

```python
import functools

import numpy as np
import jax
import jax.numpy as jnp
from jax import lax
from jax.experimental import pallas as pl
from jax.experimental.pallas import tpu as pltpu

D_MODEL = 1024
HEAD_DIM = 64
N_HEADS = 8
N_KV_HEADS = 2
D_ATTN = N_HEADS * HEAD_DIM
D_KV = N_KV_HEADS * HEAD_DIM
D_POOL = D_MODEL - D_ATTN
POOL_WINDOWS = (2, 4, 8, 16)
POOL_GROUP_DIM = D_POOL // len(POOL_WINDOWS)
D_IN = D_ATTN + 2 * D_KV + D_POOL
WINDOW = 128
BLOCK = 128
N_BUCKETS = 32
MAX_DISTANCE = 128
D_FF = 2816
EPS = 1e-6
NEG = -1e30

V7X_LANES = 128
V7X_MXU_DIM = 256
V7X_VMEM_BYTES = 64 * 1024 * 1024

HEADS_PLAIN = (0, 2, 5, 7)
HEADS_ROLLED = (1, 3, 4, 6)

TILE_M = 512
FFN_CHUNK_F = 256
FFN_STAGE_SLOTS = 2


def _vmem_limit(estimate_bytes):
    return int(min(V7X_VMEM_BYTES - (4 << 20), max(estimate_bytes, 16 << 20)))


def _rmsnorm_rows(x32, gain_row):
    ms = jnp.mean(x32 * x32, axis=-1, keepdims=True)
    return x32 * lax.rsqrt(ms + EPS) * gain_row


N_FFN_CHUNKS = D_FF // FFN_CHUNK_F
N_FFN_BASE_SCRATCH = 5


def _ffn_base_scratch(tm):
    return [
        pltpu.VMEM((D_MODEL, D_FF), jnp.bfloat16),
        pltpu.VMEM((D_MODEL, D_FF), jnp.bfloat16),
        pltpu.VMEM((D_FF, D_MODEL), jnp.bfloat16),
        pltpu.VMEM((tm, D_MODEL), jnp.bfloat16),
        pltpu.VMEM((tm, D_FF), jnp.bfloat16),
    ]


FFN_STREAM_BOUNDS = (0, 1024, 2048, D_FF)
FFN_STREAM_ORDER = tuple((which, k) for k in range(len(FFN_STREAM_BOUNDS) - 1) for which in (0, 1)) + tuple(
    (2, k) for k in range(len(FFN_STREAM_BOUNDS) - 1))
FFN_STAGE_DIM = max(hi - lo for lo, hi in zip(FFN_STREAM_BOUNDS, FFN_STREAM_BOUNDS[1:]))


def _ffn_round_scratch():
    assert D_MODEL <= FFN_STAGE_DIM and all(b % FFN_CHUNK_F == 0 for b in FFN_STREAM_BOUNDS)
    return [
        pltpu.VMEM((FFN_STAGE_SLOTS, FFN_STAGE_DIM, FFN_STAGE_DIM), jnp.float32),
        pltpu.SemaphoreType.DMA((FFN_STAGE_SLOTS,)),
    ]


def _ffn_copy_scratch():
    return [pltpu.SemaphoreType.DMA((3,))]


def _ffn_vmem_bytes(tm):
    return (2 * 3 * D_MODEL * D_FF + FFN_STAGE_SLOTS * FFN_STAGE_DIM * FFN_STAGE_DIM * 4
            + tm * D_MODEL * 2 + tm * D_FF * 2 + 6 * tm * FFN_CHUNK_F * 4 + tm * D_MODEL * 4)


def _ffn_weight_copy(which, w_hbm, base_scratch, copy_scratch):
    (sem_ref,) = copy_scratch
    return pltpu.make_async_copy(w_hbm[which], base_scratch[which], sem_ref.at[which])


def _ffn_start_weight_copies(w_hbm, base_scratch, copy_scratch):
    for which in range(3):
        _ffn_weight_copy(which, w_hbm, base_scratch, copy_scratch).start()


def _swiglu_tile(x32, g_ref, w_hbm, base_scratch, stream, stream_scratch, between=(), before_down=()):
    wg_ref, wu_ref, wd_ref, h_ref, a_ref = base_scratch

    def piece_copy(i):
        stage_ref, sem_ref = stream_scratch
        which, k = FFN_STREAM_ORDER[i]
        lo, size = FFN_STREAM_BOUNDS[k], FFN_STREAM_BOUNDS[k + 1] - FFN_STREAM_BOUNDS[k]
        slot = i % FFN_STAGE_SLOTS
        if which == 2:
            return pltpu.make_async_copy(w_hbm[2].at[pl.ds(lo, size), :],
                                         stage_ref.at[slot, pl.ds(0, size), pl.ds(0, D_MODEL)],
                                         sem_ref.at[slot])
        return pltpu.make_async_copy(w_hbm[which].at[:, pl.ds(lo, size)],
                                     stage_ref.at[slot, pl.ds(0, D_MODEL), pl.ds(0, size)],
                                     sem_ref.at[slot])

    def land_piece(i):
        stage_ref, _ = stream_scratch
        which, k = FFN_STREAM_ORDER[i]
        lo, hi = FFN_STREAM_BOUNDS[k], FFN_STREAM_BOUNDS[k + 1]
        slot = i % FFN_STAGE_SLOTS
        piece_copy(i).wait()
        if which == 2:
            wd_ref[lo:hi, :] = stage_ref[slot, :hi - lo, :D_MODEL].astype(jnp.bfloat16)
        else:
            dst = (wg_ref, wu_ref)[which]
            dst[:, lo:hi] = stage_ref[slot, :D_MODEL, :hi - lo].astype(jnp.bfloat16)
        if i + FFN_STAGE_SLOTS < len(FFN_STREAM_ORDER):
            piece_copy(i + FFN_STAGE_SLOTS).start()

    if stream == "round":
        for i in range(FFN_STAGE_SLOTS):
            piece_copy(i).start()
    if x32 is not None:
        h_ref[...] = _rmsnorm_rows(x32, g_ref[...]).astype(jnp.bfloat16)
    for c in range(N_FFN_CHUNKS):
        cols = slice(c * FFN_CHUNK_F, (c + 1) * FFN_CHUNK_F)
        if stream == "round":
            for i, (which, k) in enumerate(FFN_STREAM_ORDER):
                if which != 2 and FFN_STREAM_BOUNDS[k] == c * FFN_CHUNK_F:
                    land_piece(i)
        elif stream == "wait" and c == 0:
            for which in range(2):
                _ffn_weight_copy(which, w_hbm, base_scratch, stream_scratch).wait()
        h = h_ref[...]
        gate = jnp.dot(h, wg_ref[:, cols], preferred_element_type=jnp.float32)
        up = jnp.dot(h, wu_ref[:, cols], preferred_element_type=jnp.float32)
        act = gate * (1.0 / (1.0 + jnp.exp(-gate))) * up
        a_ref[:, cols] = act.astype(jnp.bfloat16)
        if c < len(between):
            between[c]()
    for stage in tuple(between[N_FFN_CHUNKS:]) + tuple(before_down):
        stage()
    if stream == "round":
        for i, (which, _) in enumerate(FFN_STREAM_ORDER):
            if which == 2:
                land_piece(i)
    elif stream == "wait":
        _ffn_weight_copy(2, w_hbm, base_scratch, stream_scratch).wait()
    return jnp.dot(a_ref[...], wd_ref[...], preferred_element_type=jnp.float32)


def _ffn_proj_kernel(x_ref, g1_ref, wg_hbm, wu_hbm, wd_hbm, gm_ref, win_ref, qg_ref, kg_ref, dm_ref,
                     wg2_ref, wu2_ref, wd2_ref, wo_ref, rel_ref, bucket_ref,
                     x1_ref, z_ref, wg2_bf_ref, wu2_bf_ref, wd2_bf_ref, wo_bf_ref, bias_ref,
                     *scratch):
    tm = x_ref.shape[0]
    base_scratch, round_scratch = scratch[:N_FFN_BASE_SCRATCH], scratch[N_FFN_BASE_SCRATCH:]

    first_half = pl.program_id(0) < pl.num_programs(0) // 2

    @pl.when(first_half)
    def _():
        wg2_bf_ref[...] = wg2_ref[...].astype(jnp.bfloat16)
        wd2_bf_ref[...] = wd2_ref[...].astype(jnp.bfloat16)

    @pl.when(jnp.logical_not(first_half))
    def _():
        wu2_bf_ref[...] = wu2_ref[...].astype(jnp.bfloat16)
        wo_bf_ref[...] = wo_ref[...].astype(jnp.bfloat16)

    def body(stream):
        if stream is not None:
            _write_band_bias(rel_ref, bucket_ref, bias_ref)
        y = _swiglu_tile(x_ref[...], g1_ref, (wg_hbm, wu_hbm, wd_hbm), base_scratch, stream,
                         round_scratch)
        x1 = x_ref[...] + 0.5 * y
        x1_ref[...] = x1
        h = _rmsnorm_rows(x1, gm_ref[...]).astype(jnp.bfloat16)
        z = jnp.dot(h, win_ref[...], preferred_element_type=jnp.float32)
        n_slab = (D_ATTN + 2 * D_KV) // V7X_MXU_DIM
        sq = jnp.concatenate(
            [z[:, i * V7X_MXU_DIM:(i + 1) * V7X_MXU_DIM] for i in range(n_slab)], axis=0)
        ms = jnp.dot((sq * sq).astype(jnp.bfloat16), dm_ref[...], preferred_element_type=jnp.float32)
        q_ms = jnp.concatenate([ms[i * tm:(i + 1) * tm] for i in range(n_slab - 1)], axis=1)
        k_ms = ms[(n_slab - 1) * tm:, :D_KV]
        q = z[:, :D_ATTN]
        k = z[:, D_ATTN:D_ATTN + D_KV]
        z_ref[:, :D_ATTN] = (q * lax.rsqrt(q_ms + EPS) * qg_ref[...]).astype(jnp.bfloat16)
        z_ref[:, D_ATTN:D_ATTN + D_KV] = (k * lax.rsqrt(k_ms + EPS) * kg_ref[...]).astype(jnp.bfloat16)
        z_ref[:, D_ATTN + D_KV:] = z[:, D_ATTN + D_KV:].astype(jnp.bfloat16)

    first = pl.program_id(0) == 0
    pl.when(first)(functools.partial(body, "round"))
    pl.when(jnp.logical_not(first))(functools.partial(body, None))


def _head_mean_matrix(width):
    idx = np.arange(width) // HEAD_DIM
    return jnp.asarray((idx[:, None] == idx[None, :]).astype(np.float32) / HEAD_DIM, jnp.bfloat16)


def _ffn_proj(x2d, ffn_gain, w_gate, w_up, w_down, mix_gain, w_in, q_gain, k_gain,
              next_w_gate, next_w_up, next_w_down, w_out, rel_bias):
    m = x2d.shape[0]
    tm = TILE_M
    n_steps = m // tm
    resident = functools.partial(pl.BlockSpec, pipeline_mode=pl.Buffered(1))
    rows = lambda width: pl.BlockSpec((tm, width), lambda i: (i, 0))
    hbm = pl.BlockSpec(memory_space=pltpu.HBM)
    half_steps = n_steps // 2
    wr = D_MODEL // half_steps
    wdr = D_FF // half_steps
    assert 2 * half_steps == n_steps and wr * half_steps == D_MODEL and wdr * half_steps == D_FF
    assert wr % 16 == 0 and wdr % 16 == 0
    early_slice = lambda i: (jnp.minimum(i, half_steps - 1), 0)
    late_slice = lambda i: (jnp.maximum(i - half_steps, 0), 0)
    wg_rows = pl.BlockSpec((wr, D_FF), early_slice)
    wd_rows = pl.BlockSpec((wdr, D_MODEL), early_slice)
    wu_rows = pl.BlockSpec((wr, D_FF), late_slice)
    wo_rows = pl.BlockSpec((wr, D_MODEL), late_slice)
    qg = (jnp.tile(q_gain, N_HEADS) * (HEAD_DIM ** -0.5)).reshape(1, D_ATTN)
    kg = jnp.tile(k_gain, N_KV_HEADS).reshape(1, D_KV)
    est = (_ffn_vmem_bytes(tm) + 2 * D_MODEL * D_IN
           + 2 * tm * D_MODEL * 4 * 2
           + 2 * tm * D_IN * 2
           + 4 * tm * D_IN * 4
           + 2 * 6 * (2 * wr * D_FF + wdr * D_MODEL + wr * D_MODEL)
           + 3 * N_HEADS * BLOCK * 2 * BLOCK * 4)
    bf16 = lambda shape: jax.ShapeDtypeStruct(shape, jnp.bfloat16)
    return pl.pallas_call(
        _ffn_proj_kernel,
        out_shape=(
            jax.ShapeDtypeStruct((m, D_MODEL), jnp.float32),
            bf16((m, D_IN)),
            bf16((D_MODEL, D_FF)), bf16((D_MODEL, D_FF)), bf16((D_FF, D_MODEL)),
            bf16((D_MODEL, D_MODEL)),
            jax.ShapeDtypeStruct((N_HEADS, BLOCK, 2 * BLOCK), jnp.float32),
        ),
        grid=(n_steps,),
        in_specs=[
            rows(D_MODEL),
            resident((1, D_MODEL), lambda i: (0, 0)),
            hbm, hbm, hbm,
            resident((1, D_MODEL), lambda i: (0, 0)),
            resident((D_MODEL, D_IN), lambda i: (0, 0)),
            resident((1, D_ATTN), lambda i: (0, 0)),
            resident((1, D_KV), lambda i: (0, 0)),
            resident((V7X_MXU_DIM, V7X_MXU_DIM), lambda i: (0, 0)),
            wg_rows, wu_rows, wd_rows, wo_rows,
            pl.BlockSpec(memory_space=pltpu.SMEM),
            resident((BLOCK, 2 * BLOCK), lambda i: (0, 0)),
        ],
        out_specs=(rows(D_MODEL), rows(D_IN), wg_rows, wu_rows, wd_rows, wo_rows,
                   pl.BlockSpec((N_HEADS, BLOCK, 2 * BLOCK), lambda i: (0, 0, 0))),
        scratch_shapes=_ffn_base_scratch(tm) + _ffn_round_scratch(),
        compiler_params=pltpu.CompilerParams(
            dimension_semantics=("arbitrary",),
            vmem_limit_bytes=_vmem_limit(est + (8 << 20))),
        name="ffn_proj",
    )(x2d, ffn_gain.reshape(1, D_MODEL), w_gate, w_up, w_down, mix_gain.reshape(1, D_MODEL),
      w_in.astype(jnp.bfloat16), qg, kg, _head_mean_matrix(V7X_MXU_DIM),
      next_w_gate, next_w_up, next_w_down, w_out, rel_bias, jnp.asarray(_band_bucket_table()))


def _t5_bucket(dist):
    n = np.maximum(dist, 0)
    max_exact = N_BUCKETS // 2
    large = max_exact + (np.log(np.maximum(n, 1) / max_exact)
                         / np.log(MAX_DISTANCE / max_exact)
                         * (N_BUCKETS - max_exact)).astype(np.int32)
    large = np.minimum(large, N_BUCKETS - 1)
    return np.where(n < max_exact, n, large).astype(np.int32)


def _band_bucket_table():
    ql = np.arange(BLOCK)[:, None]
    kl = np.arange(2 * BLOCK)[None, :]
    dist = ql + BLOCK - kl
    in_band = (dist >= 0) & (dist < WINDOW)
    return np.where(in_band, _t5_bucket(dist), -1).astype(np.int32)


def _write_band_bias(rel_ref, bucket_ref, o_ref):
    bucket = bucket_ref[...]
    for slot, head in enumerate(HEADS_PLAIN + HEADS_ROLLED):
        acc = jnp.full((BLOCK, 2 * BLOCK), NEG, jnp.float32)
        for b in range(N_BUCKETS):
            acc = jnp.where(bucket == b, rel_ref[b, head], acc)
        o_ref[slot] = acc


def _window_sum_matrices():
    ql = np.arange(BLOCK)[:, None]
    kl = np.arange(2 * BLOCK)[None, :]
    dist = ql + BLOCK - kl
    mats = [((dist >= 0) & (dist < w)).astype(np.float32) for w in POOL_WINDOWS]
    return jnp.asarray(np.stack(mats), jnp.bfloat16)


def _mix_stages(seq_first, t_base, sink_ref, x_ref, z_ref, zh_ref,
                bias_ref, wsum_ref, wo_ref, x2_ref, mix_scratch, g_ref, h_ref):
    kx_ref, kxr_ref, vx_ref, vxr_ref, ux_ref, pooled_ref, y_ref, wc_ref = mix_scratch
    tq = z_ref.shape[1]
    k_cols = slice(D_ATTN, D_ATTN + D_KV)
    v_cols = slice(D_ATTN + D_KV, D_ATTN + 2 * D_KV)
    u_cols = slice(D_ATTN + 2 * D_KV, D_IN)
    n_blocks = tq // BLOCK
    half = HEAD_DIM
    groups = ((HEADS_PLAIN, kx_ref, vx_ref, 0), (HEADS_ROLLED, kxr_ref, vxr_ref, len(HEADS_PLAIN)))
    lane = lax.broadcasted_iota(jnp.int32, (BLOCK, 2 * half), 1)
    low = lane < half
    state = {}

    def rows_of(j):
        return slice(j * BLOCK, (j + 1) * BLOCK)

    def keys_of(j):
        return slice(j * BLOCK, (j + 2) * BLOCK)

    def build_slabs():
        halo = jnp.where(seq_first, jnp.zeros_like(zh_ref[0, :, D_ATTN:]), zh_ref[0, :, D_ATTN:])
        kx_ref[:BLOCK] = halo[:, :D_KV]
        kx_ref[BLOCK:] = z_ref[0, :, k_cols]
        kxr_ref[...] = pltpu.roll(kx_ref[...].astype(jnp.float32), half, 1).astype(jnp.bfloat16)
        ones = jnp.ones((tq + BLOCK, D_KV), jnp.bfloat16)
        vx_ref[:BLOCK, :D_KV] = halo[:, D_KV:2 * D_KV]
        vx_ref[BLOCK:, :D_KV] = z_ref[0, :, v_cols]
        vx_ref[:, D_KV:] = ones
        vxr_ref[:, :D_KV] = pltpu.roll(
            vx_ref[:, :D_KV].astype(jnp.float32), half, 1).astype(jnp.bfloat16)
        vxr_ref[:, D_KV:] = ones
        ux_ref[:BLOCK] = halo[:, 2 * D_KV:]
        ux_ref[BLOCK:] = z_ref[0, :, u_cols]

    def scores(j, grp):
        heads, k_src, _, _ = groups[grp]
        q = z_ref[0, rows_of(j), :D_ATTN]
        zero = jnp.zeros((BLOCK, 2 * half), q.dtype)
        pieces = []
        for hd in heads:
            pair = q[:, (hd // 2) * 2 * half:(hd // 2 + 1) * 2 * half]
            pieces.append(jnp.where(low if hd % 2 == 0 else ~low, pair, zero))
        qs = jnp.concatenate(pieces, axis=0)
        state["logits", j, grp] = lax.dot_general(
            qs, k_src[keys_of(j), :], (((1,), (1,)), ((), ())),
            preferred_element_type=jnp.float32)

    def softmax_numerators(j, grp):
        heads, _, _, slot0 = groups[grp]
        logits = state.pop(("logits", j, grp))
        es, sink_terms = [], []
        for s, hd in enumerate(heads):
            lg = logits[s * BLOCK:(s + 1) * BLOCK] + bias_ref[slot0 + s]
            if j == 0:
                col = lax.broadcasted_iota(jnp.int32, (1, 2 * BLOCK), 1)
                lg = lg + jnp.where(seq_first & (col < BLOCK), NEG, 0.0).astype(jnp.float32)
            sink = sink_ref[hd]
            m = jnp.maximum(jnp.max(lg, axis=-1, keepdims=True), sink)
            es.append(jnp.exp(lg - m).astype(jnp.bfloat16))
            sink_terms.append(jnp.exp(sink - m))
        state["e", j, grp] = (jnp.concatenate(es, axis=0), sink_terms)

    def attend(j, grp):
        heads, _, v_src, _ = groups[grp]
        e, sink_terms = state.pop(("e", j, grp))
        pv = jnp.dot(e, v_src[keys_of(j), :], preferred_element_type=jnp.float32)
        for s, hd in enumerate(heads):
            blk = pv[s * BLOCK:(s + 1) * BLOCK]
            state["out", j, hd] = blk[:, :D_KV] / (blk[:, D_KV:] + sink_terms[s])

    def store_attention(j):
        for p in range(N_HEADS // 2):
            y_ref[rows_of(j), p * 2 * half:(p + 1) * 2 * half] = jnp.where(
                low, state.pop(("out", j, 2 * p)), state.pop(("out", j, 2 * p + 1))
            ).astype(jnp.bfloat16)

    def pool_means(j):
        row = lax.broadcasted_iota(jnp.int32, (BLOCK, 1), 0)
        t_glob = t_base + j * BLOCK + row
        for g, w in enumerate(POOL_WINDOWS):
            gc = slice(g * POOL_GROUP_DIM, (g + 1) * POOL_GROUP_DIM)
            wsum = jnp.dot(wsum_ref[g], ux_ref[keys_of(j), gc], preferred_element_type=jnp.float32)
            cnt = jnp.minimum(t_glob + 1, w).astype(jnp.float32)
            token = z_ref[0, rows_of(j), D_ATTN + 2 * D_KV + g * POOL_GROUP_DIM:
                          D_ATTN + 2 * D_KV + (g + 1) * POOL_GROUP_DIM]
            pooled = wsum * (1.0 / cnt) - token.astype(jnp.float32)
            pooled_ref[rows_of(j), gc] = pooled.astype(jnp.bfloat16)

    def project_pool():
        x2_ref[...] = x_ref[0] + jnp.dot(pooled_ref[...], wc_ref[...],
                                         preferred_element_type=jnp.float32)

    def project_attention():
        x2 = x2_ref[...] + jnp.dot(y_ref[...], wo_ref[:D_ATTN, :], preferred_element_type=jnp.float32)
        x2_ref[...] = x2
        h_ref[...] = _rmsnorm_rows(x2, g_ref[...]).astype(jnp.bfloat16)

    def seq(*calls):
        def run():
            for fn, *args in calls:
                fn(*args)
        return run

    early = [
        seq((build_slabs,)),
        seq(*[(pool_means, j) for j in range(n_blocks // 2)]),
        seq(*[(pool_means, j) for j in range(n_blocks // 2, n_blocks)]),
        seq((scores, 0, 0), (scores, 0, 1), (softmax_numerators, 0, 0)),
        seq((attend, 0, 0), (softmax_numerators, 0, 1)),
        seq((project_pool,), (scores, 1, 0), (attend, 0, 1), (store_attention, 0)),
    ]
    for j in range(1, n_blocks):
        early.append(seq((scores, j, 1), (softmax_numerators, j, 0), (attend, j, 0)))
        tail = [(softmax_numerators, j, 1)]
        if j + 1 < n_blocks:
            tail.append((scores, j + 1, 0))
        tail += [(attend, j, 1), (store_attention, j)]
        early.append(seq(*tail))
    late = [project_attention]
    return early, late


N_MIX_SCRATCH = 9


def _mix_ffn_kernel(sink_ref, x_ref, z_ref, zh_ref,
                    bias_ref, wsum_ref, pw_ref, ps_ref, wo_ref, g2_ref, wg_hbm, wu_hbm, wd_hbm,
                    o_ref, *scratch, tiles_per_seq):
    mix_scratch, x2_ref = scratch[:N_MIX_SCRATCH - 1], scratch[N_MIX_SCRATCH - 1]
    base_scratch = scratch[N_MIX_SCRATCH:N_MIX_SCRATCH + N_FFN_BASE_SCRATCH]
    copy_scratch = scratch[N_MIX_SCRATCH + N_FFN_BASE_SCRATCH:]
    h_ref = base_scratch[3]
    t = pl.program_id(0)
    tq = z_ref.shape[1]
    t_mix = jnp.minimum(t, pl.num_programs(0) - 2)
    i_seq = lax.rem(t_mix, tiles_per_seq)
    stages = functools.partial(
        _mix_stages, i_seq == 0, i_seq * tq, sink_ref, x_ref, z_ref, zh_ref,
        bias_ref, wsum_ref, wo_ref, x2_ref, mix_scratch, g2_ref, h_ref)
    w_hbm = (wg_hbm, wu_hbm, wd_hbm)

    def first_step():
        _ffn_start_weight_copies(w_hbm, base_scratch, copy_scratch)
        for g in range(len(POOL_WINDOWS)):
            gc = slice(g * POOL_GROUP_DIM, (g + 1) * POOL_GROUP_DIM)
            pool_map = (pw_ref[g] * ps_ref[:, gc]).astype(jnp.bfloat16)
            mix_scratch[-1][gc, :] = jnp.dot(
                pool_map, wo_ref[D_ATTN + g * POOL_GROUP_DIM:D_ATTN + (g + 1) * POOL_GROUP_DIM, :],
                preferred_element_type=jnp.float32).astype(jnp.bfloat16)
        early, late = stages()
        for stage in early + late:
            stage()

    def later_step(wait_weights, mix_next):
        early, late = stages() if mix_next else ((), ())
        o_ref[0] = x2_ref[...]
        y = _swiglu_tile(None, g2_ref, w_hbm, base_scratch, "wait" if wait_weights else None,
                         copy_scratch, between=early, before_down=late)
        o_ref[0] += 0.5 * y

    last = pl.num_programs(0) - 1
    pl.when(t == 0)(first_step)
    pl.when(t == 1)(functools.partial(later_step, True, True))
    pl.when(jnp.logical_and(t > 1, t < last))(functools.partial(later_step, False, True))
    pl.when(t == last)(functools.partial(later_step, False, False))


def _mix_ffn(x3d, z3d, sinks, bias, pool_w, pool_scale, w_out_bf16, ffn_gain,
             w_gate_bf16, w_up_bf16, w_down_bf16):
    b, s, _ = x3d.shape
    tq = TILE_M
    nblk = tq // BLOCK
    tiles_per_seq = s // tq
    n_tiles = b * tiles_per_seq
    resident = functools.partial(pl.BlockSpec, pipeline_mode=pl.Buffered(1))
    hbm = pl.BlockSpec(memory_space=pltpu.HBM)

    def mixed(t):
        tm = jnp.minimum(t, n_tiles - 1)
        return tm // tiles_per_seq, lax.rem(tm, tiles_per_seq)

    def tile(width):
        return pl.BlockSpec((1, tq, width), lambda t: (*mixed(t), 0))

    def halo(width):
        def index(t):
            bi, i = mixed(t)
            return bi, jnp.maximum(i * nblk - 1, 0), 0
        return pl.BlockSpec((1, BLOCK, width), index)

    def out_index(t):
        tf = jnp.maximum(t - 1, 0)
        return tf // tiles_per_seq, lax.rem(tf, tiles_per_seq), 0

    est = (_ffn_vmem_bytes(tq)
           + 2 * 2 * tq * D_MODEL * 4 + 2 * (tq + BLOCK) * D_IN * 2
           + N_HEADS * BLOCK * 2 * BLOCK * 4 + 2 * D_MODEL * D_MODEL + 4 * D_POOL * D_POOL
           + 2 * D_POOL * D_MODEL
           + 6 * (tq + BLOCK) * D_KV * 2 + (tq + BLOCK) * D_POOL * 2 + tq * D_MODEL * 2
           + tq * D_MODEL * 4 + 16 * BLOCK * 2 * BLOCK * 4 * 4)
    return pl.pallas_call(
        functools.partial(_mix_ffn_kernel, tiles_per_seq=tiles_per_seq),
        out_shape=jax.ShapeDtypeStruct((b, s, D_MODEL), jnp.float32),
        grid=(n_tiles + 1,),
        in_specs=[
            pl.BlockSpec(memory_space=pltpu.SMEM),
            tile(D_MODEL),
            tile(D_IN), halo(D_IN),
            resident((N_HEADS, BLOCK, 2 * BLOCK), lambda t: (0, 0, 0)),
            resident((len(POOL_WINDOWS), BLOCK, 2 * BLOCK), lambda t: (0, 0, 0)),
            resident((len(POOL_WINDOWS), POOL_GROUP_DIM, POOL_GROUP_DIM), lambda t: (0, 0, 0)),
            resident((1, D_POOL), lambda t: (0, 0)),
            resident((D_MODEL, D_MODEL), lambda t: (0, 0)),
            resident((1, D_MODEL), lambda t: (0, 0)),
            hbm, hbm, hbm,
        ],
        out_specs=pl.BlockSpec((1, tq, D_MODEL), out_index),
        scratch_shapes=[
            pltpu.VMEM((tq + BLOCK, D_KV), jnp.bfloat16),
            pltpu.VMEM((tq + BLOCK, D_KV), jnp.bfloat16),
            pltpu.VMEM((tq + BLOCK, 2 * D_KV), jnp.bfloat16),
            pltpu.VMEM((tq + BLOCK, 2 * D_KV), jnp.bfloat16),
            pltpu.VMEM((tq + BLOCK, D_POOL), jnp.bfloat16),
            pltpu.VMEM((tq, D_POOL), jnp.bfloat16),
            pltpu.VMEM((tq, D_ATTN), jnp.bfloat16),
            pltpu.VMEM((D_POOL, D_MODEL), jnp.bfloat16),
            pltpu.VMEM((tq, D_MODEL), jnp.float32),
        ] + _ffn_base_scratch(tq) + _ffn_copy_scratch(),
        compiler_params=pltpu.CompilerParams(
            dimension_semantics=("arbitrary",),
            vmem_limit_bytes=_vmem_limit(est + (8 << 20))),
        name="mix_ffn",
    )(sinks, x3d, z3d, z3d, bias, _window_sum_matrices(),
      pool_w, pool_scale.reshape(1, D_POOL),
      w_out_bf16, ffn_gain.reshape(1, D_MODEL), w_gate_bf16, w_up_bf16, w_down_bf16)


def kernel(x, ffn1_norm, ffn1_w_gate, ffn1_w_up, ffn1_w_down, mix_norm, w_in, q_norm, k_norm,
           attn_sinks, rel_bias, pool_w, pool_scale, w_out, ffn2_norm, ffn2_w_gate, ffn2_w_up,
           ffn2_w_down):
    b, s, d = x.shape
    assert (d, s % TILE_M) == (D_MODEL, 0)
    for l in range(ffn1_norm.shape[0]):
        x1, z, wg2, wu2, wd2, wo, bias = _ffn_proj(
            x.reshape(b * s, d), ffn1_norm[l], ffn1_w_gate[l], ffn1_w_up[l], ffn1_w_down[l],
            mix_norm[l], w_in[l], q_norm[l], k_norm[l],
            ffn2_w_gate[l], ffn2_w_up[l], ffn2_w_down[l], w_out[l], rel_bias)
        x = _mix_ffn(x1.reshape(b, s, d), z.reshape(b, s, D_IN), attn_sinks[l], bias,
                     pool_w[l], pool_scale[l], wo, ffn2_norm[l], wg2, wu2, wd2)
    return x
```

```python
import functools

import numpy as np
import jax
import jax.numpy as jnp
from jax import lax
from jax.experimental import pallas as pl
from jax.experimental.pallas import tpu as pltpu

D_MODEL = 1024
HEAD_DIM = 64
N_HEADS = 8
N_KV_HEADS = 2
D_ATTN = N_HEADS * HEAD_DIM
D_KV = N_KV_HEADS * HEAD_DIM
D_POOL = D_MODEL - D_ATTN
POOL_WINDOWS = (2, 4, 8, 16)
POOL_GROUP_DIM = D_POOL // len(POOL_WINDOWS)
D_IN = D_ATTN + 2 * D_KV + D_POOL
WINDOW = 128
BLOCK = 128
N_BUCKETS = 32
MAX_DISTANCE = 128
D_FF = 2816
EPS = 1e-6
NEG = -1e30

V7X_LANES = 128
V7X_MXU_DIM = 256
V7X_VMEM_BYTES = 64 * 1024 * 1024

HEADS_PLAIN = (0, 2, 5, 7)
HEADS_ROLLED = (1, 3, 4, 6)

TILE_M = 512
FFN_CHUNK_F = 256
FFN_STAGE_SLOTS = 2


def _vmem_limit(estimate_bytes):
    return int(min(V7X_VMEM_BYTES - (4 << 20), max(estimate_bytes, 16 << 20)))


def _rmsnorm_rows(x32, gain_row):
    ms = jnp.mean(x32 * x32, axis=-1, keepdims=True)
    return x32 * lax.rsqrt(ms + EPS) * gain_row


N_FFN_CHUNKS = D_FF // FFN_CHUNK_F
N_FFN_BASE_SCRATCH = 5


def _ffn_base_scratch(tm):
    return [
        pltpu.VMEM((D_MODEL, D_FF), jnp.bfloat16),
        pltpu.VMEM((D_MODEL, D_FF), jnp.bfloat16),
        pltpu.VMEM((D_FF, D_MODEL), jnp.bfloat16),
        pltpu.VMEM((tm, D_MODEL), jnp.bfloat16),
        pltpu.VMEM((tm, D_FF), jnp.bfloat16),
    ]


FFN_STREAM_BOUNDS = (0, 1024, 2048, D_FF)
FFN_STREAM_ORDER = tuple((which, k) for k in range(len(FFN_STREAM_BOUNDS) - 1) for which in (0, 1)) + tuple(
    (2, k) for k in range(len(FFN_STREAM_BOUNDS) - 1))
FFN_STAGE_DIM = max(hi - lo for lo, hi in zip(FFN_STREAM_BOUNDS, FFN_STREAM_BOUNDS[1:]))


def _ffn_round_scratch():
    assert D_MODEL <= FFN_STAGE_DIM and all(b % FFN_CHUNK_F == 0 for b in FFN_STREAM_BOUNDS)
    return [
        pltpu.VMEM((FFN_STAGE_SLOTS, FFN_STAGE_DIM, FFN_STAGE_DIM), jnp.float32),
        pltpu.SemaphoreType.DMA((FFN_STAGE_SLOTS,)),
    ]


def _ffn_copy_scratch():
    return [pltpu.SemaphoreType.DMA((3,))]


def _ffn_vmem_bytes(tm):
    return (2 * 3 * D_MODEL * D_FF + FFN_STAGE_SLOTS * FFN_STAGE_DIM * FFN_STAGE_DIM * 4
            + tm * D_MODEL * 2 + tm * D_FF * 2 + 6 * tm * FFN_CHUNK_F * 4 + tm * D_MODEL * 4)


def _ffn_weight_copy(which, w_hbm, base_scratch, copy_scratch):
    (sem_ref,) = copy_scratch
    return pltpu.make_async_copy(w_hbm[which], base_scratch[which], sem_ref.at[which])


def _ffn_start_weight_copies(w_hbm, base_scratch, copy_scratch):
    for which in range(3):
        _ffn_weight_copy(which, w_hbm, base_scratch, copy_scratch).start()


def _swiglu_tile(x32, g_ref, w_hbm, base_scratch, stream, stream_scratch, between=(), before_down=()):
    wg_ref, wu_ref, wd_ref, h_ref, a_ref = base_scratch

    def piece_copy(i):
        stage_ref, sem_ref = stream_scratch
        which, k = FFN_STREAM_ORDER[i]
        lo, size = FFN_STREAM_BOUNDS[k], FFN_STREAM_BOUNDS[k + 1] - FFN_STREAM_BOUNDS[k]
        slot = i % FFN_STAGE_SLOTS
        if which == 2:
            return pltpu.make_async_copy(w_hbm[2].at[pl.ds(lo, size), :],
                                         stage_ref.at[slot, pl.ds(0, size), pl.ds(0, D_MODEL)],
                                         sem_ref.at[slot])
        return pltpu.make_async_copy(w_hbm[which].at[:, pl.ds(lo, size)],
                                     stage_ref.at[slot, pl.ds(0, D_MODEL), pl.ds(0, size)],
                                     sem_ref.at[slot])

    def land_piece(i):
        stage_ref, _ = stream_scratch
        which, k = FFN_STREAM_ORDER[i]
        lo, hi = FFN_STREAM_BOUNDS[k], FFN_STREAM_BOUNDS[k + 1]
        slot = i % FFN_STAGE_SLOTS
        piece_copy(i).wait()
        if which == 2:
            wd_ref[lo:hi, :] = stage_ref[slot, :hi - lo, :D_MODEL].astype(jnp.bfloat16)
        else:
            dst = (wg_ref, wu_ref)[which]
            dst[:, lo:hi] = stage_ref[slot, :D_MODEL, :hi - lo].astype(jnp.bfloat16)
        if i + FFN_STAGE_SLOTS < len(FFN_STREAM_ORDER):
            piece_copy(i + FFN_STAGE_SLOTS).start()

    if stream == "round":
        for i in range(FFN_STAGE_SLOTS):
            piece_copy(i).start()
    if x32 is not None:
        h_ref[...] = _rmsnorm_rows(x32, g_ref[...]).astype(jnp.bfloat16)
    for c in range(N_FFN_CHUNKS):
        cols = slice(c * FFN_CHUNK_F, (c + 1) * FFN_CHUNK_F)
        if stream == "round":
            for i, (which, k) in enumerate(FFN_STREAM_ORDER):
                if which != 2 and FFN_STREAM_BOUNDS[k] == c * FFN_CHUNK_F:
                    land_piece(i)
        elif stream == "wait" and c == 0:
            for which in range(2):
                _ffn_weight_copy(which, w_hbm, base_scratch, stream_scratch).wait()
        h = h_ref[...]
        gate = jnp.dot(h, wg_ref[:, cols], preferred_element_type=jnp.float32)
        up = jnp.dot(h, wu_ref[:, cols], preferred_element_type=jnp.float32)
        act = gate * (1.0 / (1.0 + jnp.exp(-gate))) * up
        a_ref[:, cols] = act.astype(jnp.bfloat16)
        if c < len(between):
            between[c]()
    for stage in tuple(between[N_FFN_CHUNKS:]) + tuple(before_down):
        stage()
    if stream == "round":
        for i, (which, _) in enumerate(FFN_STREAM_ORDER):
            if which == 2:
                land_piece(i)
    elif stream == "wait":
        _ffn_weight_copy(2, w_hbm, base_scratch, stream_scratch).wait()
    return jnp.dot(a_ref[...], wd_ref[...], preferred_element_type=jnp.float32)


def _ffn_proj_kernel(x_ref, g1_ref, wg_hbm, wu_hbm, wd_hbm, gm_ref, win_ref, qg_ref, kg_ref, dm_ref,
                     wg2_ref, wu2_ref, wd2_ref, wo_ref, rel_ref, bucket_ref,
                     x1_ref, z_ref, wg2_bf_ref, wu2_bf_ref, wd2_bf_ref, wo_bf_ref, bias_ref,
                     *scratch):
    tm = x_ref.shape[0]
    base_scratch, round_scratch = scratch[:N_FFN_BASE_SCRATCH], scratch[N_FFN_BASE_SCRATCH:]

    first_half = pl.program_id(0) < pl.num_programs(0) // 2

    @pl.when(first_half)
    def _():
        wg2_bf_ref[...] = wg2_ref[...].astype(jnp.bfloat16)
        wd2_bf_ref[...] = wd2_ref[...].astype(jnp.bfloat16)

    @pl.when(jnp.logical_not(first_half))
    def _():
        wu2_bf_ref[...] = wu2_ref[...].astype(jnp.bfloat16)
        wo_bf_ref[...] = wo_ref[...].astype(jnp.bfloat16)

    def body(stream):
        if stream is not None:
            _write_band_bias(rel_ref, bucket_ref, bias_ref)
        y = _swiglu_tile(x_ref[...], g1_ref, (wg_hbm, wu_hbm, wd_hbm), base_scratch, stream,
                         round_scratch)
        x1 = x_ref[...] + 0.5 * y
        x1_ref[...] = x1
        h = _rmsnorm_rows(x1, gm_ref[...]).astype(jnp.bfloat16)
        z = jnp.dot(h, win_ref[...], preferred_element_type=jnp.float32)
        n_slab = (D_ATTN + 2 * D_KV) // V7X_MXU_DIM
        sq = jnp.concatenate(
            [z[:, i * V7X_MXU_DIM:(i + 1) * V7X_MXU_DIM] for i in range(n_slab)], axis=0)
        ms = jnp.dot((sq * sq).astype(jnp.bfloat16), dm_ref[...], preferred_element_type=jnp.float32)
        q_ms = jnp.concatenate([ms[i * tm:(i + 1) * tm] for i in range(n_slab - 1)], axis=1)
        k_ms = ms[(n_slab - 1) * tm:, :D_KV]
        q = z[:, :D_ATTN]
        k = z[:, D_ATTN:D_ATTN + D_KV]
        z_ref[:, :D_ATTN] = (q * lax.rsqrt(q_ms + EPS) * qg_ref[...]).astype(jnp.bfloat16)
        z_ref[:, D_ATTN:D_ATTN + D_KV] = (k * lax.rsqrt(k_ms + EPS) * kg_ref[...]).astype(jnp.bfloat16)
        z_ref[:, D_ATTN + D_KV:] = z[:, D_ATTN + D_KV:].astype(jnp.bfloat16)

    first = pl.program_id(0) == 0
    pl.when(first)(functools.partial(body, "round"))
    pl.when(jnp.logical_not(first))(functools.partial(body, None))


def _head_mean_matrix(width):
    idx = np.arange(width) // HEAD_DIM
    return jnp.asarray((idx[:, None] == idx[None, :]).astype(np.float32) / HEAD_DIM, jnp.bfloat16)


def _ffn_proj(x2d, ffn_gain, w_gate, w_up, w_down, mix_gain, w_in, q_gain, k_gain,
              next_w_gate, next_w_up, next_w_down, w_out, rel_bias):
    m = x2d.shape[0]
    tm = TILE_M
    n_steps = m // tm
    resident = functools.partial(pl.BlockSpec, pipeline_mode=pl.Buffered(1))
    rows = lambda width: pl.BlockSpec((tm, width), lambda i: (i, 0))
    hbm = pl.BlockSpec(memory_space=pltpu.HBM)
    half_steps = n_steps // 2
    wr = D_MODEL // half_steps
    wdr = D_FF // half_steps
    assert 2 * half_steps == n_steps and wr * half_steps == D_MODEL and wdr * half_steps == D_FF
    assert wr % 16 == 0 and wdr % 16 == 0
    early_slice = lambda i: (jnp.minimum(i, half_steps - 1), 0)
    late_slice = lambda i: (jnp.maximum(i - half_steps, 0), 0)
    wg_rows = pl.BlockSpec((wr, D_FF), early_slice)
    wd_rows = pl.BlockSpec((wdr, D_MODEL), early_slice)
    wu_rows = pl.BlockSpec((wr, D_FF), late_slice)
    wo_rows = pl.BlockSpec((wr, D_MODEL), late_slice)
    qg = (jnp.tile(q_gain, N_HEADS) * (HEAD_DIM ** -0.5)).reshape(1, D_ATTN)
    kg = jnp.tile(k_gain, N_KV_HEADS).reshape(1, D_KV)
    est = (_ffn_vmem_bytes(tm) + 2 * D_MODEL * D_IN
           + 2 * tm * D_MODEL * 4 * 2
           + 2 * tm * D_IN * 2
           + 4 * tm * D_IN * 4
           + 2 * 6 * (2 * wr * D_FF + wdr * D_MODEL + wr * D_MODEL)
           + 3 * N_HEADS * BLOCK * 2 * BLOCK * 4)
    bf16 = lambda shape: jax.ShapeDtypeStruct(shape, jnp.bfloat16)
    return pl.pallas_call(
        _ffn_proj_kernel,
        out_shape=(
            jax.ShapeDtypeStruct((m, D_MODEL), jnp.float32),
            bf16((m, D_IN)),
            bf16((D_MODEL, D_FF)), bf16((D_MODEL, D_FF)), bf16((D_FF, D_MODEL)),
            bf16((D_MODEL, D_MODEL)),
            jax.ShapeDtypeStruct((N_HEADS, BLOCK, 2 * BLOCK), jnp.float32),
        ),
        grid=(n_steps,),
        in_specs=[
            rows(D_MODEL),
            resident((1, D_MODEL), lambda i: (0, 0)),
            hbm, hbm, hbm,
            resident((1, D_MODEL), lambda i: (0, 0)),
            resident((D_MODEL, D_IN), lambda i: (0, 0)),
            resident((1, D_ATTN), lambda i: (0, 0)),
            resident((1, D_KV), lambda i: (0, 0)),
            resident((V7X_MXU_DIM, V7X_MXU_DIM), lambda i: (0, 0)),
            wg_rows, wu_rows, wd_rows, wo_rows,
            pl.BlockSpec(memory_space=pltpu.SMEM),
            resident((BLOCK, 2 * BLOCK), lambda i: (0, 0)),
        ],
        out_specs=(rows(D_MODEL), rows(D_IN), wg_rows, wu_rows, wd_rows, wo_rows,
                   pl.BlockSpec((N_HEADS, BLOCK, 2 * BLOCK), lambda i: (0, 0, 0))),
        scratch_shapes=_ffn_base_scratch(tm) + _ffn_round_scratch(),
        compiler_params=pltpu.CompilerParams(
            dimension_semantics=("arbitrary",),
            vmem_limit_bytes=_vmem_limit(est + (8 << 20))),
        name="ffn_proj",
    )(x2d, ffn_gain.reshape(1, D_MODEL), w_gate, w_up, w_down, mix_gain.reshape(1, D_MODEL),
      w_in.astype(jnp.bfloat16), qg, kg, _head_mean_matrix(V7X_MXU_DIM),
      next_w_gate, next_w_up, next_w_down, w_out, rel_bias, jnp.asarray(_band_bucket_table()))


def _t5_bucket(dist):
    n = np.maximum(dist, 0)
    max_exact = N_BUCKETS // 2
    large = max_exact + (np.log(np.maximum(n, 1) / max_exact)
                         / np.log(MAX_DISTANCE / max_exact)
                         * (N_BUCKETS - max_exact)).astype(np.int32)
    large = np.minimum(large, N_BUCKETS - 1)
    return np.where(n < max_exact, n, large).astype(np.int32)


def _band_bucket_table():
    ql = np.arange(BLOCK)[:, None]
    kl = np.arange(2 * BLOCK)[None, :]
    dist = ql + BLOCK - kl
    in_band = (dist >= 0) & (dist < WINDOW)
    return np.where(in_band, _t5_bucket(dist), -1).astype(np.int32)


def _write_band_bias(rel_ref, bucket_ref, o_ref):
    bucket = bucket_ref[...]
    for slot, head in enumerate(HEADS_PLAIN + HEADS_ROLLED):
        acc = jnp.full((BLOCK, 2 * BLOCK), NEG, jnp.float32)
        for b in range(N_BUCKETS):
            acc = jnp.where(bucket == b, rel_ref[b, head], acc)
        o_ref[slot] = acc


def _window_sum_matrices():
    ql = np.arange(BLOCK)[:, None]
    kl = np.arange(2 * BLOCK)[None, :]
    dist = ql + BLOCK - kl
    mats = [((dist >= 0) & (dist < w)).astype(np.float32) for w in POOL_WINDOWS]
    return jnp.asarray(np.stack(mats), jnp.bfloat16)


def _mix_stages(seq_first, t_base, sink_ref, x_ref, z_ref, zh_ref,
                bias_ref, wsum_ref, wo_ref, x2_ref, mix_scratch, g_ref, h_ref):
    kx_ref, kxr_ref, vx_ref, vxr_ref, ux_ref, pooled_ref, y_ref, wc_ref = mix_scratch
    tq = z_ref.shape[1]
    k_cols = slice(D_ATTN, D_ATTN + D_KV)
    v_cols = slice(D_ATTN + D_KV, D_ATTN + 2 * D_KV)
    u_cols = slice(D_ATTN + 2 * D_KV, D_IN)
    n_blocks = tq // BLOCK
    half = HEAD_DIM
    groups = ((HEADS_PLAIN, kx_ref, vx_ref, 0), (HEADS_ROLLED, kxr_ref, vxr_ref, len(HEADS_PLAIN)))
    lane = lax.broadcasted_iota(jnp.int32, (BLOCK, 2 * half), 1)
    low = lane < half
    state = {}

    def rows_of(j):
        return slice(j * BLOCK, (j + 1) * BLOCK)

    def keys_of(j):
        return slice(j * BLOCK, (j + 2) * BLOCK)

    def build_slabs():
        halo = jnp.where(seq_first, jnp.zeros_like(zh_ref[0, :, D_ATTN:]), zh_ref[0, :, D_ATTN:])
        kx_ref[:BLOCK] = halo[:, :D_KV]
        kx_ref[BLOCK:] = z_ref[0, :, k_cols]
        kxr_ref[...] = pltpu.roll(kx_ref[...].astype(jnp.float32), half, 1).astype(jnp.bfloat16)
        ones = jnp.ones((tq + BLOCK, D_KV), jnp.bfloat16)
        vx_ref[:BLOCK, :D_KV] = halo[:, D_KV:2 * D_KV]
        vx_ref[BLOCK:, :D_KV] = z_ref[0, :, v_cols]
        vx_ref[:, D_KV:] = ones
        vxr_ref[:, :D_KV] = pltpu.roll(
            vx_ref[:, :D_KV].astype(jnp.float32), half, 1).astype(jnp.bfloat16)
        vxr_ref[:, D_KV:] = ones
        ux_ref[:BLOCK] = halo[:, 2 * D_KV:]
        ux_ref[BLOCK:] = z_ref[0, :, u_cols]

    def scores(j, grp):
        heads, k_src, _, _ = groups[grp]
        q = z_ref[0, rows_of(j), :D_ATTN]
        zero = jnp.zeros((BLOCK, 2 * half), q.dtype)
        pieces = []
        for hd in heads:
            pair = q[:, (hd // 2) * 2 * half:(hd // 2 + 1) * 2 * half]
            pieces.append(jnp.where(low if hd % 2 == 0 else ~low, pair, zero))
        qs = jnp.concatenate(pieces, axis=0)
        state["logits", j, grp] = lax.dot_general(
            qs, k_src[keys_of(j), :], (((1,), (1,)), ((), ())),
            preferred_element_type=jnp.float32)

    def softmax_numerators(j, grp):
        heads, _, _, slot0 = groups[grp]
        logits = state.pop(("logits", j, grp))
        es, sink_terms = [], []
        for s, hd in enumerate(heads):
            lg = logits[s * BLOCK:(s + 1) * BLOCK] + bias_ref[slot0 + s]
            if j == 0:
                col = lax.broadcasted_iota(jnp.int32, (1, 2 * BLOCK), 1)
                lg = lg + jnp.where(seq_first & (col < BLOCK), NEG, 0.0).astype(jnp.float32)
            sink = sink_ref[hd]
            m = jnp.maximum(jnp.max(lg, axis=-1, keepdims=True), sink)
            es.append(jnp.exp(lg - m).astype(jnp.bfloat16))
            sink_terms.append(jnp.exp(sink - m))
        state["e", j, grp] = (jnp.concatenate(es, axis=0), sink_terms)

    def attend(j, grp):
        heads, _, v_src, _ = groups[grp]
        e, sink_terms = state.pop(("e", j, grp))
        pv = jnp.dot(e, v_src[keys_of(j), :], preferred_element_type=jnp.float32)
        for s, hd in enumerate(heads):
            blk = pv[s * BLOCK:(s + 1) * BLOCK]
            state["out", j, hd] = blk[:, :D_KV] / (blk[:, D_KV:] + sink_terms[s])

    def store_attention(j):
        for p in range(N_HEADS // 2):
            y_ref[rows_of(j), p * 2 * half:(p + 1) * 2 * half] = jnp.where(
                low, state.pop(("out", j, 2 * p)), state.pop(("out", j, 2 * p + 1))
            ).astype(jnp.bfloat16)

    def pool_means(j):
        row = lax.broadcasted_iota(jnp.int32, (BLOCK, 1), 0)
        t_glob = t_base + j * BLOCK + row
        for g, w in enumerate(POOL_WINDOWS):
            gc = slice(g * POOL_GROUP_DIM, (g + 1) * POOL_GROUP_DIM)
            wsum = jnp.dot(wsum_ref[g], ux_ref[keys_of(j), gc], preferred_element_type=jnp.float32)
            cnt = jnp.minimum(t_glob + 1, w).astype(jnp.float32)
            token = z_ref[0, rows_of(j), D_ATTN + 2 * D_KV + g * POOL_GROUP_DIM:
                          D_ATTN + 2 * D_KV + (g + 1) * POOL_GROUP_DIM]
            pooled = wsum * (1.0 / cnt) - token.astype(jnp.float32)
            pooled_ref[rows_of(j), gc] = pooled.astype(jnp.bfloat16)

    def project_pool():
        x2_ref[...] = x_ref[0] + jnp.dot(pooled_ref[...], wc_ref[...],
                                         preferred_element_type=jnp.float32)

    def project_attention():
        x2 = x2_ref[...] + jnp.dot(y_ref[...], wo_ref[:D_ATTN, :], preferred_element_type=jnp.float32)
        x2_ref[...] = x2
        h_ref[...] = _rmsnorm_rows(x2, g_ref[...]).astype(jnp.bfloat16)

    def seq(*calls):
        def run():
            for fn, *args in calls:
                fn(*args)
        return run

    early = [
        seq((build_slabs,)),
        seq(*[(pool_means, j) for j in range(n_blocks // 2)]),
        seq(*[(pool_means, j) for j in range(n_blocks // 2, n_blocks)]),
        seq((scores, 0, 0), (scores, 0, 1), (softmax_numerators, 0, 0)),
        seq((attend, 0, 0), (softmax_numerators, 0, 1)),
        seq((project_pool,), (scores, 1, 0), (attend, 0, 1), (store_attention, 0)),
    ]
    for j in range(1, n_blocks):
        early.append(seq((scores, j, 1), (softmax_numerators, j, 0), (attend, j, 0)))
        tail = [(softmax_numerators, j, 1)]
        if j + 1 < n_blocks:
            tail.append((scores, j + 1, 0))
        tail += [(attend, j, 1), (store_attention, j)]
        early.append(seq(*tail))
    late = [project_attention]
    return early, late


N_MIX_SCRATCH = 9


def _mix_ffn_kernel(sink_ref, x_ref, z_ref, zh_ref,
                    bias_ref, wsum_ref, pw_ref, ps_ref, wo_ref, g2_ref, wg_hbm, wu_hbm, wd_hbm,
                    o_ref, *scratch, tiles_per_seq):
    mix_scratch, x2_ref = scratch[:N_MIX_SCRATCH - 1], scratch[N_MIX_SCRATCH - 1]
    base_scratch = scratch[N_MIX_SCRATCH:N_MIX_SCRATCH + N_FFN_BASE_SCRATCH]
    copy_scratch = scratch[N_MIX_SCRATCH + N_FFN_BASE_SCRATCH:]
    h_ref = base_scratch[3]
    t = pl.program_id(0)
    tq = z_ref.shape[1]
    i_seq = lax.rem(t, tiles_per_seq)
    stages = functools.partial(
        _mix_stages, i_seq == 0, i_seq * tq, sink_ref, x_ref, z_ref, zh_ref,
        bias_ref, wsum_ref, wo_ref, x2_ref, mix_scratch, g2_ref, h_ref)
    w_hbm = (wg_hbm, wu_hbm, wd_hbm)

    def step(first):
        if first:
            _ffn_start_weight_copies(w_hbm, base_scratch, copy_scratch)
            for g in range(len(POOL_WINDOWS)):
                gc = slice(g * POOL_GROUP_DIM, (g + 1) * POOL_GROUP_DIM)
                pool_map = (pw_ref[g] * ps_ref[:, gc]).astype(jnp.bfloat16)
                mix_scratch[-1][gc, :] = jnp.dot(
                    pool_map, wo_ref[D_ATTN + g * POOL_GROUP_DIM:D_ATTN + (g + 1) * POOL_GROUP_DIM, :],
                    preferred_element_type=jnp.float32).astype(jnp.bfloat16)
        early, late = stages()
        for stage in early + late:
            stage()
        y = _swiglu_tile(None, g2_ref, w_hbm, base_scratch, "wait" if first else None, copy_scratch)
        o_ref[0] = x2_ref[...] + 0.5 * y

    pl.when(t == 0)(functools.partial(step, True))
    pl.when(t > 0)(functools.partial(step, False))


def _mix_ffn(x3d, z3d, sinks, bias, pool_w, pool_scale, w_out_bf16, ffn_gain,
             w_gate_bf16, w_up_bf16, w_down_bf16):
    b, s, _ = x3d.shape
    tq = TILE_M
    nblk = tq // BLOCK
    tiles_per_seq = s // tq
    n_tiles = b * tiles_per_seq
    resident = functools.partial(pl.BlockSpec, pipeline_mode=pl.Buffered(1))
    hbm = pl.BlockSpec(memory_space=pltpu.HBM)

    def tile(width):
        return pl.BlockSpec((1, tq, width), lambda t: (t // tiles_per_seq, lax.rem(t, tiles_per_seq), 0))

    def halo(width):
        return pl.BlockSpec(
            (1, BLOCK, width),
            lambda t: (t // tiles_per_seq, jnp.maximum(lax.rem(t, tiles_per_seq) * nblk - 1, 0), 0))

    est = (_ffn_vmem_bytes(tq)
           + 2 * 2 * tq * D_MODEL * 4 + 2 * (tq + BLOCK) * D_IN * 2
           + N_HEADS * BLOCK * 2 * BLOCK * 4 + 2 * D_MODEL * D_MODEL + 4 * D_POOL * D_POOL
           + 2 * D_POOL * D_MODEL
           + 6 * (tq + BLOCK) * D_KV * 2 + (tq + BLOCK) * D_POOL * 2 + tq * D_MODEL * 2
           + tq * D_MODEL * 4 + 16 * BLOCK * 2 * BLOCK * 4 * 4)
    return pl.pallas_call(
        functools.partial(_mix_ffn_kernel, tiles_per_seq=tiles_per_seq),
        out_shape=jax.ShapeDtypeStruct((b, s, D_MODEL), jnp.float32),
        grid=(n_tiles,),
        in_specs=[
            pl.BlockSpec(memory_space=pltpu.SMEM),
            tile(D_MODEL),
            tile(D_IN), halo(D_IN),
            resident((N_HEADS, BLOCK, 2 * BLOCK), lambda t: (0, 0, 0)),
            resident((len(POOL_WINDOWS), BLOCK, 2 * BLOCK), lambda t: (0, 0, 0)),
            resident((len(POOL_WINDOWS), POOL_GROUP_DIM, POOL_GROUP_DIM), lambda t: (0, 0, 0)),
            resident((1, D_POOL), lambda t: (0, 0)),
            resident((D_MODEL, D_MODEL), lambda t: (0, 0)),
            resident((1, D_MODEL), lambda t: (0, 0)),
            hbm, hbm, hbm,
        ],
        out_specs=tile(D_MODEL),
        scratch_shapes=[
            pltpu.VMEM((tq + BLOCK, D_KV), jnp.bfloat16),
            pltpu.VMEM((tq + BLOCK, D_KV), jnp.bfloat16),
            pltpu.VMEM((tq + BLOCK, 2 * D_KV), jnp.bfloat16),
            pltpu.VMEM((tq + BLOCK, 2 * D_KV), jnp.bfloat16),
            pltpu.VMEM((tq + BLOCK, D_POOL), jnp.bfloat16),
            pltpu.VMEM((tq, D_POOL), jnp.bfloat16),
            pltpu.VMEM((tq, D_ATTN), jnp.bfloat16),
            pltpu.VMEM((D_POOL, D_MODEL), jnp.bfloat16),
            pltpu.VMEM((tq, D_MODEL), jnp.float32),
        ] + _ffn_base_scratch(tq) + _ffn_copy_scratch(),
        compiler_params=pltpu.CompilerParams(
            dimension_semantics=("arbitrary",),
            vmem_limit_bytes=_vmem_limit(est + (8 << 20))),
        name="mix_ffn",
    )(sinks, x3d, z3d, z3d, bias, _window_sum_matrices(),
      pool_w, pool_scale.reshape(1, D_POOL),
      w_out_bf16, ffn_gain.reshape(1, D_MODEL), w_gate_bf16, w_up_bf16, w_down_bf16)


def kernel(x, ffn1_norm, ffn1_w_gate, ffn1_w_up, ffn1_w_down, mix_norm, w_in, q_norm, k_norm,
           attn_sinks, rel_bias, pool_w, pool_scale, w_out, ffn2_norm, ffn2_w_gate, ffn2_w_up,
           ffn2_w_down):
    b, s, d = x.shape
    assert (d, s % TILE_M) == (D_MODEL, 0)
    for l in range(ffn1_norm.shape[0]):
        x1, z, wg2, wu2, wd2, wo, bias = _ffn_proj(
            x.reshape(b * s, d), ffn1_norm[l], ffn1_w_gate[l], ffn1_w_up[l], ffn1_w_down[l],
            mix_norm[l], w_in[l], q_norm[l], k_norm[l],
            ffn2_w_gate[l], ffn2_w_up[l], ffn2_w_down[l], w_out[l], rel_bias)
        x = _mix_ffn(x1.reshape(b, s, d), z.reshape(b, s, D_IN), attn_sinks[l], bias,
                     pool_w[l], pool_scale[l], wo, ffn2_norm[l], wg2, wu2, wd2)
    return x
```

```python
import functools

import numpy as np
import jax
import jax.numpy as jnp
from jax import lax
from jax.experimental import pallas as pl
from jax.experimental.pallas import tpu as pltpu

D_MODEL = 1024
HEAD_DIM = 64
N_HEADS = 8
N_KV_HEADS = 2
D_ATTN = N_HEADS * HEAD_DIM
D_KV = N_KV_HEADS * HEAD_DIM
D_POOL = D_MODEL - D_ATTN
POOL_WINDOWS = (2, 4, 8, 16)
POOL_GROUP_DIM = D_POOL // len(POOL_WINDOWS)
D_IN = D_ATTN + 2 * D_KV + D_POOL
WINDOW = 128
BLOCK = 128
N_BUCKETS = 32
MAX_DISTANCE = 128
D_FF = 2816
EPS = 1e-6
NEG = -1e30

V7X_MXU_DIM = 256
V7X_VMEM_BYTES = 64 * 1024 * 1024

HEADS_PLAIN = (0, 2, 5, 7)
HEADS_ROLLED = (1, 3, 4, 6)

TILE_M = 512
FFN_CHUNK_F = 256
FFN_STAGE_SLOTS = 2


VMEM_COMPILER_TEMPORARIES_BYTES = 8 << 20
VMEM_LIMIT_FLOOR_BYTES = 16 << 20
VMEM_LIMIT_CEILING_BYTES = V7X_VMEM_BYTES - (4 << 20)


def _vmem_limit(buffer_bytes):
    wanted = buffer_bytes + VMEM_COMPILER_TEMPORARIES_BYTES
    return int(min(VMEM_LIMIT_CEILING_BYTES, max(wanted, VMEM_LIMIT_FLOOR_BYTES)))


def _rmsnorm_rows(x32, gain_row):
    ms = jnp.mean(x32 * x32, axis=-1, keepdims=True)
    return x32 * lax.rsqrt(ms + EPS) * gain_row


N_FFN_CHUNKS = D_FF // FFN_CHUNK_F
N_FFN_BASE_SCRATCH = 5


def _ffn_base_scratch(tm):
    return [
        pltpu.VMEM((D_MODEL, D_FF), jnp.bfloat16),
        pltpu.VMEM((D_MODEL, D_FF), jnp.bfloat16),
        pltpu.VMEM((D_FF, D_MODEL), jnp.bfloat16),
        pltpu.VMEM((tm, D_MODEL), jnp.bfloat16),
        pltpu.VMEM((tm, D_FF), jnp.bfloat16),
    ]


FFN_STREAM_BOUNDS = (0, 1024, 2048, D_FF)
FFN_STREAM_ORDER = tuple((which, k) for k in range(len(FFN_STREAM_BOUNDS) - 1) for which in (0, 1)) + tuple(
    (2, k) for k in range(len(FFN_STREAM_BOUNDS) - 1))
FFN_STAGE_DIM = max(hi - lo for lo, hi in zip(FFN_STREAM_BOUNDS, FFN_STREAM_BOUNDS[1:]))


def _ffn_round_scratch():
    assert D_MODEL <= FFN_STAGE_DIM and all(b % FFN_CHUNK_F == 0 for b in FFN_STREAM_BOUNDS)
    return [
        pltpu.VMEM((FFN_STAGE_SLOTS, FFN_STAGE_DIM, FFN_STAGE_DIM), jnp.float32),
        pltpu.SemaphoreType.DMA((FFN_STAGE_SLOTS,)),
    ]


def _ffn_copy_scratch():
    return [pltpu.SemaphoreType.DMA((3,))]


def _ffn_vmem_bytes(tm):
    return (2 * 3 * D_MODEL * D_FF + FFN_STAGE_SLOTS * FFN_STAGE_DIM * FFN_STAGE_DIM * 4
            + tm * D_MODEL * 2 + tm * D_FF * 2 + 6 * tm * FFN_CHUNK_F * 4 + tm * D_MODEL * 4)


def _ffn_weight_copy(which, w_hbm, base_scratch, copy_scratch):
    (sem_ref,) = copy_scratch
    return pltpu.make_async_copy(w_hbm[which], base_scratch[which], sem_ref.at[which])


def _ffn_start_weight_copies(w_hbm, base_scratch, copy_scratch):
    for which in range(3):
        _ffn_weight_copy(which, w_hbm, base_scratch, copy_scratch).start()


def _swiglu_tile(x32, g_ref, w_hbm, base_scratch, stream, stream_scratch):
    wg_ref, wu_ref, wd_ref, h_ref, a_ref = base_scratch

    def piece_copy(i):
        stage_ref, sem_ref = stream_scratch
        which, k = FFN_STREAM_ORDER[i]
        lo, size = FFN_STREAM_BOUNDS[k], FFN_STREAM_BOUNDS[k + 1] - FFN_STREAM_BOUNDS[k]
        slot = i % FFN_STAGE_SLOTS
        if which == 2:
            return pltpu.make_async_copy(w_hbm[2].at[pl.ds(lo, size), :],
                                         stage_ref.at[slot, pl.ds(0, size), pl.ds(0, D_MODEL)],
                                         sem_ref.at[slot])
        return pltpu.make_async_copy(w_hbm[which].at[:, pl.ds(lo, size)],
                                     stage_ref.at[slot, pl.ds(0, D_MODEL), pl.ds(0, size)],
                                     sem_ref.at[slot])

    def land_piece(i):
        stage_ref, _ = stream_scratch
        which, k = FFN_STREAM_ORDER[i]
        lo, hi = FFN_STREAM_BOUNDS[k], FFN_STREAM_BOUNDS[k + 1]
        slot = i % FFN_STAGE_SLOTS
        piece_copy(i).wait()
        if which == 2:
            wd_ref[lo:hi, :] = stage_ref[slot, :hi - lo, :D_MODEL].astype(jnp.bfloat16)
        else:
            dst = (wg_ref, wu_ref)[which]
            dst[:, lo:hi] = stage_ref[slot, :D_MODEL, :hi - lo].astype(jnp.bfloat16)
        if i + FFN_STAGE_SLOTS < len(FFN_STREAM_ORDER):
            piece_copy(i + FFN_STAGE_SLOTS).start()

    if stream == "round":
        for i in range(FFN_STAGE_SLOTS):
            piece_copy(i).start()
    if x32 is not None:
        h_ref[...] = _rmsnorm_rows(x32, g_ref[...]).astype(jnp.bfloat16)
    for c in range(N_FFN_CHUNKS):
        cols = slice(c * FFN_CHUNK_F, (c + 1) * FFN_CHUNK_F)
        if stream == "round":
            for i, (which, k) in enumerate(FFN_STREAM_ORDER):
                if which != 2 and FFN_STREAM_BOUNDS[k] == c * FFN_CHUNK_F:
                    land_piece(i)
        elif stream == "wait" and c == 0:
            for which in range(2):
                _ffn_weight_copy(which, w_hbm, base_scratch, stream_scratch).wait()
        h = h_ref[...]
        gate = jnp.dot(h, wg_ref[:, cols], preferred_element_type=jnp.float32)
        up = jnp.dot(h, wu_ref[:, cols], preferred_element_type=jnp.float32)
        act = gate * (1.0 / (1.0 + jnp.exp(-gate))) * up
        a_ref[:, cols] = act.astype(jnp.bfloat16)
    if stream == "round":
        for i, (which, _) in enumerate(FFN_STREAM_ORDER):
            if which == 2:
                land_piece(i)
    elif stream == "wait":
        _ffn_weight_copy(2, w_hbm, base_scratch, stream_scratch).wait()
    return jnp.dot(a_ref[...], wd_ref[...], preferred_element_type=jnp.float32)


def _ffn_proj_kernel(x_ref, g1_ref, wg_hbm, wu_hbm, wd_hbm, gm_ref, win_ref, qg_ref, kg_ref, dm_ref,
                     wg2_ref, wu2_ref, wd2_ref, wo_ref, rel_ref, bucket_ref,
                     x1_ref, z_ref, wg2_bf_ref, wu2_bf_ref, wd2_bf_ref, wo_bf_ref, bias_ref,
                     *scratch):
    tm = x_ref.shape[0]
    base_scratch, round_scratch = scratch[:N_FFN_BASE_SCRATCH], scratch[N_FFN_BASE_SCRATCH:]

    first_half = pl.program_id(0) < pl.num_programs(0) // 2

    @pl.when(first_half)
    def _():
        wg2_bf_ref[...] = wg2_ref[...].astype(jnp.bfloat16)
        wd2_bf_ref[...] = wd2_ref[...].astype(jnp.bfloat16)

    @pl.when(jnp.logical_not(first_half))
    def _():
        wu2_bf_ref[...] = wu2_ref[...].astype(jnp.bfloat16)
        wo_bf_ref[...] = wo_ref[...].astype(jnp.bfloat16)

    def body(stream):
        if stream is not None:
            _write_band_bias(rel_ref, bucket_ref, bias_ref)
        y = _swiglu_tile(x_ref[...], g1_ref, (wg_hbm, wu_hbm, wd_hbm), base_scratch, stream,
                         round_scratch)
        x1 = x_ref[...] + 0.5 * y
        x1_ref[...] = x1
        h = _rmsnorm_rows(x1, gm_ref[...]).astype(jnp.bfloat16)
        z = jnp.dot(h, win_ref[...], preferred_element_type=jnp.float32)
        n_slab = (D_ATTN + 2 * D_KV) // V7X_MXU_DIM
        sq = jnp.concatenate(
            [z[:, i * V7X_MXU_DIM:(i + 1) * V7X_MXU_DIM] for i in range(n_slab)], axis=0)
        ms = jnp.dot((sq * sq).astype(jnp.bfloat16), dm_ref[...], preferred_element_type=jnp.float32)
        q_ms = jnp.concatenate([ms[i * tm:(i + 1) * tm] for i in range(n_slab - 1)], axis=1)
        k_ms = ms[(n_slab - 1) * tm:, :D_KV]
        q = z[:, :D_ATTN]
        k = z[:, D_ATTN:D_ATTN + D_KV]
        z_ref[:, :D_ATTN] = (q * lax.rsqrt(q_ms + EPS) * qg_ref[...]).astype(jnp.bfloat16)
        z_ref[:, D_ATTN:D_ATTN + D_KV] = (k * lax.rsqrt(k_ms + EPS) * kg_ref[...]).astype(jnp.bfloat16)
        z_ref[:, D_ATTN + D_KV:] = z[:, D_ATTN + D_KV:].astype(jnp.bfloat16)

    first = pl.program_id(0) == 0
    pl.when(first)(functools.partial(body, "round"))
    pl.when(jnp.logical_not(first))(functools.partial(body, None))


def _head_mean_matrix(width):
    idx = np.arange(width) // HEAD_DIM
    return jnp.asarray((idx[:, None] == idx[None, :]).astype(np.float32) / HEAD_DIM, jnp.bfloat16)


def _ffn_proj(x2d, ffn_gain, w_gate, w_up, w_down, mix_gain, w_in, q_gain, k_gain,
              next_w_gate, next_w_up, next_w_down, w_out, rel_bias):
    m = x2d.shape[0]
    tm = TILE_M
    n_steps = m // tm
    resident = functools.partial(pl.BlockSpec, pipeline_mode=pl.Buffered(1))
    rows = lambda width: pl.BlockSpec((tm, width), lambda i: (i, 0))
    hbm = pl.BlockSpec(memory_space=pltpu.HBM)
    half_steps = n_steps // 2
    wr = D_MODEL // half_steps
    wdr = D_FF // half_steps
    assert 2 * half_steps == n_steps and wr * half_steps == D_MODEL and wdr * half_steps == D_FF
    assert wr % 16 == 0 and wdr % 16 == 0
    early_slice = lambda i: (jnp.minimum(i, half_steps - 1), 0)
    late_slice = lambda i: (jnp.maximum(i - half_steps, 0), 0)
    wg_rows = pl.BlockSpec((wr, D_FF), early_slice)
    wd_rows = pl.BlockSpec((wdr, D_MODEL), early_slice)
    wu_rows = pl.BlockSpec((wr, D_FF), late_slice)
    wo_rows = pl.BlockSpec((wr, D_MODEL), late_slice)
    qg = (jnp.tile(q_gain, N_HEADS) * (HEAD_DIM ** -0.5)).reshape(1, D_ATTN)
    kg = jnp.tile(k_gain, N_KV_HEADS).reshape(1, D_KV)
    est = (_ffn_vmem_bytes(tm) + 2 * D_MODEL * D_IN
           + 2 * tm * D_MODEL * 4 * 2
           + 2 * tm * D_IN * 2
           + 4 * tm * D_IN * 4
           + 2 * 6 * (2 * wr * D_FF + wdr * D_MODEL + wr * D_MODEL)
           + 3 * N_HEADS * BLOCK * 2 * BLOCK * 4)
    bf16 = lambda shape: jax.ShapeDtypeStruct(shape, jnp.bfloat16)
    return pl.pallas_call(
        _ffn_proj_kernel,
        out_shape=(
            jax.ShapeDtypeStruct((m, D_MODEL), jnp.float32),
            bf16((m, D_IN)),
            bf16((D_MODEL, D_FF)), bf16((D_MODEL, D_FF)), bf16((D_FF, D_MODEL)),
            bf16((D_MODEL, D_MODEL)),
            jax.ShapeDtypeStruct((N_HEADS, BLOCK, 2 * BLOCK), jnp.float32),
        ),
        grid=(n_steps,),
        in_specs=[
            rows(D_MODEL),
            resident((1, D_MODEL), lambda i: (0, 0)),
            hbm, hbm, hbm,
            resident((1, D_MODEL), lambda i: (0, 0)),
            resident((D_MODEL, D_IN), lambda i: (0, 0)),
            resident((1, D_ATTN), lambda i: (0, 0)),
            resident((1, D_KV), lambda i: (0, 0)),
            resident((V7X_MXU_DIM, V7X_MXU_DIM), lambda i: (0, 0)),
            wg_rows, wu_rows, wd_rows, wo_rows,
            pl.BlockSpec(memory_space=pltpu.SMEM),
            resident((BLOCK, 2 * BLOCK), lambda i: (0, 0)),
        ],
        out_specs=(rows(D_MODEL), rows(D_IN), wg_rows, wu_rows, wd_rows, wo_rows,
                   pl.BlockSpec((N_HEADS, BLOCK, 2 * BLOCK), lambda i: (0, 0, 0))),
        scratch_shapes=_ffn_base_scratch(tm) + _ffn_round_scratch(),
        compiler_params=pltpu.CompilerParams(
            dimension_semantics=("arbitrary",),
            vmem_limit_bytes=_vmem_limit(est)),
        name="ffn_proj",
    )(x2d, ffn_gain.reshape(1, D_MODEL), w_gate, w_up, w_down, mix_gain.reshape(1, D_MODEL),
      w_in.astype(jnp.bfloat16), qg, kg, _head_mean_matrix(V7X_MXU_DIM),
      next_w_gate, next_w_up, next_w_down, w_out, rel_bias, jnp.asarray(_band_bucket_table()))


def _t5_bucket(dist):
    n = np.maximum(dist, 0)
    max_exact = N_BUCKETS // 2
    large = max_exact + (np.log(np.maximum(n, 1) / max_exact)
                         / np.log(MAX_DISTANCE / max_exact)
                         * (N_BUCKETS - max_exact)).astype(np.int32)
    large = np.minimum(large, N_BUCKETS - 1)
    return np.where(n < max_exact, n, large).astype(np.int32)


def _band_bucket_table():
    ql = np.arange(BLOCK)[:, None]
    kl = np.arange(2 * BLOCK)[None, :]
    dist = ql + BLOCK - kl
    in_band = (dist >= 0) & (dist < WINDOW)
    return np.where(in_band, _t5_bucket(dist), -1).astype(np.int32)


def _write_band_bias(rel_ref, bucket_ref, o_ref):
    bucket = bucket_ref[...]
    for slot, head in enumerate(HEADS_PLAIN + HEADS_ROLLED):
        acc = jnp.full((BLOCK, 2 * BLOCK), NEG, jnp.float32)
        for b in range(N_BUCKETS):
            acc = jnp.where(bucket == b, rel_ref[b, head], acc)
        o_ref[slot] = acc


def _window_sum_matrices():
    ql = np.arange(BLOCK)[:, None]
    kl = np.arange(2 * BLOCK)[None, :]
    dist = ql + BLOCK - kl
    mats = [((dist >= 0) & (dist < w)).astype(np.float32) for w in POOL_WINDOWS]
    return jnp.asarray(np.stack(mats), jnp.bfloat16)


def _mix_tile(seq_first, t_base, sink_ref, x_ref, z_ref, zh_ref,
              bias_ref, wsum_ref, wo_ref, x2_ref, mix_scratch, g_ref, h_ref):
    kx_ref, kxr_ref, vx_ref, vxr_ref, ux_ref, pooled_ref, y_ref, wc_ref = mix_scratch
    tq = z_ref.shape[1]
    k_cols = slice(D_ATTN, D_ATTN + D_KV)
    v_cols = slice(D_ATTN + D_KV, D_ATTN + 2 * D_KV)
    u_cols = slice(D_ATTN + 2 * D_KV, D_IN)
    n_blocks = tq // BLOCK
    half = HEAD_DIM
    groups = ((HEADS_PLAIN, kx_ref, vx_ref, 0), (HEADS_ROLLED, kxr_ref, vxr_ref, len(HEADS_PLAIN)))
    lane = lax.broadcasted_iota(jnp.int32, (BLOCK, 2 * half), 1)
    low = lane < half
    state = {}

    def rows_of(j):
        return slice(j * BLOCK, (j + 1) * BLOCK)

    def keys_of(j):
        return slice(j * BLOCK, (j + 2) * BLOCK)

    def build_slabs():
        halo = jnp.where(seq_first, jnp.zeros_like(zh_ref[0, :, D_ATTN:]), zh_ref[0, :, D_ATTN:])
        kx_ref[:BLOCK] = halo[:, :D_KV]
        kx_ref[BLOCK:] = z_ref[0, :, k_cols]
        kxr_ref[...] = pltpu.roll(kx_ref[...].astype(jnp.float32), half, 1).astype(jnp.bfloat16)
        ones = jnp.ones((tq + BLOCK, D_KV), jnp.bfloat16)
        vx_ref[:BLOCK, :D_KV] = halo[:, D_KV:2 * D_KV]
        vx_ref[BLOCK:, :D_KV] = z_ref[0, :, v_cols]
        vx_ref[:, D_KV:] = ones
        vxr_ref[:, :D_KV] = pltpu.roll(
            vx_ref[:, :D_KV].astype(jnp.float32), half, 1).astype(jnp.bfloat16)
        vxr_ref[:, D_KV:] = ones
        ux_ref[:BLOCK] = halo[:, 2 * D_KV:]
        ux_ref[BLOCK:] = z_ref[0, :, u_cols]

    def scores(j, grp):
        heads, k_src, _, _ = groups[grp]
        q = z_ref[0, rows_of(j), :D_ATTN]
        zero = jnp.zeros((BLOCK, 2 * half), q.dtype)
        pieces = []
        for hd in heads:
            pair = q[:, (hd // 2) * 2 * half:(hd // 2 + 1) * 2 * half]
            pieces.append(jnp.where(low if hd % 2 == 0 else ~low, pair, zero))
        qs = jnp.concatenate(pieces, axis=0)
        state["logits", j, grp] = lax.dot_general(
            qs, k_src[keys_of(j), :], (((1,), (1,)), ((), ())),
            preferred_element_type=jnp.float32)

    def softmax_numerators(j, grp):
        heads, _, _, slot0 = groups[grp]
        logits = state.pop(("logits", j, grp))
        es, sink_terms = [], []
        for s, hd in enumerate(heads):
            lg = logits[s * BLOCK:(s + 1) * BLOCK] + bias_ref[slot0 + s]
            if j == 0:
                col = lax.broadcasted_iota(jnp.int32, (1, 2 * BLOCK), 1)
                lg = lg + jnp.where(seq_first & (col < BLOCK), NEG, 0.0).astype(jnp.float32)
            sink = sink_ref[hd]
            m = jnp.maximum(jnp.max(lg, axis=-1, keepdims=True), sink)
            es.append(jnp.exp(lg - m).astype(jnp.bfloat16))
            sink_terms.append(jnp.exp(sink - m))
        state["e", j, grp] = (jnp.concatenate(es, axis=0), sink_terms)

    def attend(j, grp):
        heads, _, v_src, _ = groups[grp]
        e, sink_terms = state.pop(("e", j, grp))
        pv = jnp.dot(e, v_src[keys_of(j), :], preferred_element_type=jnp.float32)
        for s, hd in enumerate(heads):
            blk = pv[s * BLOCK:(s + 1) * BLOCK]
            state["out", j, hd] = blk[:, :D_KV] / (blk[:, D_KV:] + sink_terms[s])

    def store_attention(j):
        for p in range(N_HEADS // 2):
            y_ref[rows_of(j), p * 2 * half:(p + 1) * 2 * half] = jnp.where(
                low, state.pop(("out", j, 2 * p)), state.pop(("out", j, 2 * p + 1))
            ).astype(jnp.bfloat16)

    def pool_means(j):
        row = lax.broadcasted_iota(jnp.int32, (BLOCK, 1), 0)
        t_glob = t_base + j * BLOCK + row
        for g, w in enumerate(POOL_WINDOWS):
            gc = slice(g * POOL_GROUP_DIM, (g + 1) * POOL_GROUP_DIM)
            wsum = jnp.dot(wsum_ref[g], ux_ref[keys_of(j), gc], preferred_element_type=jnp.float32)
            cnt = jnp.minimum(t_glob + 1, w).astype(jnp.float32)
            token = z_ref[0, rows_of(j), D_ATTN + 2 * D_KV + g * POOL_GROUP_DIM:
                          D_ATTN + 2 * D_KV + (g + 1) * POOL_GROUP_DIM]
            pooled = wsum * (1.0 / cnt) - token.astype(jnp.float32)
            pooled_ref[rows_of(j), gc] = pooled.astype(jnp.bfloat16)

    def project_pool():
        x2_ref[...] = x_ref[0] + jnp.dot(pooled_ref[...], wc_ref[...],
                                         preferred_element_type=jnp.float32)

    def project_attention():
        x2 = x2_ref[...] + jnp.dot(y_ref[...], wo_ref[:D_ATTN, :], preferred_element_type=jnp.float32)
        x2_ref[...] = x2
        h_ref[...] = _rmsnorm_rows(x2, g_ref[...]).astype(jnp.bfloat16)

    build_slabs()
    for j in range(n_blocks):
        pool_means(j)
    scores(0, 0)
    for j in range(n_blocks):
        scores(j, 1)
        softmax_numerators(j, 0)
        attend(j, 0)
        softmax_numerators(j, 1)
        if j == 0:
            project_pool()
        if j + 1 < n_blocks:
            scores(j + 1, 0)
        attend(j, 1)
        store_attention(j)
    project_attention()


N_MIX_SCRATCH = 9


def _mix_ffn_kernel(sink_ref, x_ref, z_ref, zh_ref,
                    bias_ref, wsum_ref, pw_ref, ps_ref, wo_ref, g2_ref, wg_hbm, wu_hbm, wd_hbm,
                    o_ref, *scratch, tiles_per_seq):
    mix_scratch, x2_ref = scratch[:N_MIX_SCRATCH - 1], scratch[N_MIX_SCRATCH - 1]
    base_scratch = scratch[N_MIX_SCRATCH:N_MIX_SCRATCH + N_FFN_BASE_SCRATCH]
    copy_scratch = scratch[N_MIX_SCRATCH + N_FFN_BASE_SCRATCH:]
    h_ref = base_scratch[3]
    t = pl.program_id(0)
    tq = z_ref.shape[1]
    i_seq = lax.rem(t, tiles_per_seq)
    w_hbm = (wg_hbm, wu_hbm, wd_hbm)

    def step(first):
        if first:
            _ffn_start_weight_copies(w_hbm, base_scratch, copy_scratch)
            for g in range(len(POOL_WINDOWS)):
                gc = slice(g * POOL_GROUP_DIM, (g + 1) * POOL_GROUP_DIM)
                pool_map = (pw_ref[g] * ps_ref[:, gc]).astype(jnp.bfloat16)
                mix_scratch[-1][gc, :] = jnp.dot(
                    pool_map, wo_ref[D_ATTN + g * POOL_GROUP_DIM:D_ATTN + (g + 1) * POOL_GROUP_DIM, :],
                    preferred_element_type=jnp.float32).astype(jnp.bfloat16)
        _mix_tile(i_seq == 0, i_seq * tq, sink_ref, x_ref, z_ref, zh_ref,
                  bias_ref, wsum_ref, wo_ref, x2_ref, mix_scratch, g2_ref, h_ref)
        y = _swiglu_tile(None, g2_ref, w_hbm, base_scratch, "wait" if first else None, copy_scratch)
        o_ref[0] = x2_ref[...] + 0.5 * y

    pl.when(t == 0)(functools.partial(step, True))
    pl.when(t > 0)(functools.partial(step, False))


def _mix_ffn(x3d, z3d, sinks, bias, pool_w, pool_scale, w_out_bf16, ffn_gain,
             w_gate_bf16, w_up_bf16, w_down_bf16):
    b, s, _ = x3d.shape
    tq = TILE_M
    nblk = tq // BLOCK
    tiles_per_seq = s // tq
    n_tiles = b * tiles_per_seq
    resident = functools.partial(pl.BlockSpec, pipeline_mode=pl.Buffered(1))
    hbm = pl.BlockSpec(memory_space=pltpu.HBM)

    def tile(width):
        return pl.BlockSpec((1, tq, width), lambda t: (t // tiles_per_seq, lax.rem(t, tiles_per_seq), 0))

    def halo(width):
        return pl.BlockSpec(
            (1, BLOCK, width),
            lambda t: (t // tiles_per_seq, jnp.maximum(lax.rem(t, tiles_per_seq) * nblk - 1, 0), 0))

    est = (_ffn_vmem_bytes(tq)
           + 2 * 2 * tq * D_MODEL * 4 + 2 * (tq + BLOCK) * D_IN * 2
           + N_HEADS * BLOCK * 2 * BLOCK * 4 + 2 * D_MODEL * D_MODEL + 4 * D_POOL * D_POOL
           + 2 * D_POOL * D_MODEL
           + 6 * (tq + BLOCK) * D_KV * 2 + (tq + BLOCK) * D_POOL * 2 + tq * D_MODEL * 2
           + tq * D_MODEL * 4 + 16 * BLOCK * 2 * BLOCK * 4 * 4)
    return pl.pallas_call(
        functools.partial(_mix_ffn_kernel, tiles_per_seq=tiles_per_seq),
        out_shape=jax.ShapeDtypeStruct((b, s, D_MODEL), jnp.float32),
        grid=(n_tiles,),
        in_specs=[
            pl.BlockSpec(memory_space=pltpu.SMEM),
            tile(D_MODEL),
            tile(D_IN), halo(D_IN),
            resident((N_HEADS, BLOCK, 2 * BLOCK), lambda t: (0, 0, 0)),
            resident((len(POOL_WINDOWS), BLOCK, 2 * BLOCK), lambda t: (0, 0, 0)),
            resident((len(POOL_WINDOWS), POOL_GROUP_DIM, POOL_GROUP_DIM), lambda t: (0, 0, 0)),
            resident((1, D_POOL), lambda t: (0, 0)),
            resident((D_MODEL, D_MODEL), lambda t: (0, 0)),
            resident((1, D_MODEL), lambda t: (0, 0)),
            hbm, hbm, hbm,
        ],
        out_specs=tile(D_MODEL),
        scratch_shapes=[
            pltpu.VMEM((tq + BLOCK, D_KV), jnp.bfloat16),
            pltpu.VMEM((tq + BLOCK, D_KV), jnp.bfloat16),
            pltpu.VMEM((tq + BLOCK, 2 * D_KV), jnp.bfloat16),
            pltpu.VMEM((tq + BLOCK, 2 * D_KV), jnp.bfloat16),
            pltpu.VMEM((tq + BLOCK, D_POOL), jnp.bfloat16),
            pltpu.VMEM((tq, D_POOL), jnp.bfloat16),
            pltpu.VMEM((tq, D_ATTN), jnp.bfloat16),
            pltpu.VMEM((D_POOL, D_MODEL), jnp.bfloat16),
            pltpu.VMEM((tq, D_MODEL), jnp.float32),
        ] + _ffn_base_scratch(tq) + _ffn_copy_scratch(),
        compiler_params=pltpu.CompilerParams(
            dimension_semantics=("arbitrary",),
            vmem_limit_bytes=_vmem_limit(est)),
        name="mix_ffn",
    )(sinks, x3d, z3d, z3d, bias, _window_sum_matrices(),
      pool_w, pool_scale.reshape(1, D_POOL),
      w_out_bf16, ffn_gain.reshape(1, D_MODEL), w_gate_bf16, w_up_bf16, w_down_bf16)


def kernel(x, ffn1_norm, ffn1_w_gate, ffn1_w_up, ffn1_w_down, mix_norm, w_in, q_norm, k_norm,
           attn_sinks, rel_bias, pool_w, pool_scale, w_out, ffn2_norm, ffn2_w_gate, ffn2_w_up,
           ffn2_w_down):
    b, s, d = x.shape
    assert (d, s % TILE_M) == (D_MODEL, 0)
    for l in range(ffn1_norm.shape[0]):
        x1, z, wg2, wu2, wd2, wo, bias = _ffn_proj(
            x.reshape(b * s, d), ffn1_norm[l], ffn1_w_gate[l], ffn1_w_up[l], ffn1_w_down[l],
            mix_norm[l], w_in[l], q_norm[l], k_norm[l],
            ffn2_w_gate[l], ffn2_w_up[l], ffn2_w_down[l], w_out[l], rel_bias)
        x = _mix_ffn(x1.reshape(b, s, d), z.reshape(b, s, D_IN), attn_sinks[l], bias,
                     pool_w[l], pool_scale[l], wo, ffn2_norm[l], wg2, wu2, wd2)
    return x
```

```python
import functools

import numpy as np
import jax
import jax.numpy as jnp
from jax import lax
from jax.experimental import pallas as pl
from jax.experimental.pallas import tpu as pltpu

D_MODEL = 1024
HEAD_DIM = 64
N_HEADS = 8
N_KV_HEADS = 2
D_ATTN = N_HEADS * HEAD_DIM
D_KV = N_KV_HEADS * HEAD_DIM
D_POOL = D_MODEL - D_ATTN
POOL_WINDOWS = (2, 4, 8, 16)
POOL_GROUP_DIM = D_POOL // len(POOL_WINDOWS)
D_IN = D_ATTN + 2 * D_KV + D_POOL
WINDOW = 128
BLOCK = 128
N_BUCKETS = 32
MAX_DISTANCE = 128
D_FF = 2816
EPS = 1e-6
NEG = -1e30

V7X_MXU_DIM = 256
V7X_VMEM_BYTES = 64 * 1024 * 1024

HEADS_PLAIN = (0, 2, 5, 7)
HEADS_ROLLED = (1, 3, 4, 6)

TILE_M = 512
FFN_CHUNK_F = 256
FFN_STAGE_SLOTS = 2


VMEM_COMPILER_TEMPORARIES_BYTES = 8 << 20
VMEM_LIMIT_FLOOR_BYTES = 16 << 20
VMEM_LIMIT_CEILING_BYTES = V7X_VMEM_BYTES - (4 << 20)


def _vmem_limit(buffer_bytes):
    wanted = buffer_bytes + VMEM_COMPILER_TEMPORARIES_BYTES
    return int(min(VMEM_LIMIT_CEILING_BYTES, max(wanted, VMEM_LIMIT_FLOOR_BYTES)))


def _rmsnorm_rows(x32, gain_row):
    ms = jnp.mean(x32 * x32, axis=-1, keepdims=True)
    return x32 * lax.rsqrt(ms + EPS) * gain_row


N_FFN_CHUNKS = D_FF // FFN_CHUNK_F
N_FFN_BASE_SCRATCH = 5


def _ffn_base_scratch(tm):
    return [
        pltpu.VMEM((D_MODEL, D_FF), jnp.bfloat16),
        pltpu.VMEM((D_MODEL, D_FF), jnp.bfloat16),
        pltpu.VMEM((D_FF, D_MODEL), jnp.bfloat16),
        pltpu.VMEM((tm, D_MODEL), jnp.bfloat16),
        pltpu.VMEM((tm, D_FF), jnp.bfloat16),
    ]


FFN_GATE_UP_BOUNDS = (0, 256, 512, 1024, 2048, D_FF)
FFN_DOWN_BOUNDS = (0, 1024, 2048, D_FF)
FFN_STREAM_PIECES = tuple(
    (which, lo, hi) for lo, hi in zip(FFN_GATE_UP_BOUNDS, FFN_GATE_UP_BOUNDS[1:]) for which in (0, 1)
) + tuple((2, lo, hi) for lo, hi in zip(FFN_DOWN_BOUNDS, FFN_DOWN_BOUNDS[1:]))
FFN_STAGE_DIM = max(hi - lo for _, lo, hi in FFN_STREAM_PIECES)


def _ffn_round_scratch():
    assert D_MODEL <= FFN_STAGE_DIM and all(lo % FFN_CHUNK_F == 0 for _, lo, _ in FFN_STREAM_PIECES)
    return [
        pltpu.VMEM((FFN_STAGE_SLOTS, FFN_STAGE_DIM, FFN_STAGE_DIM), jnp.float32),
        pltpu.SemaphoreType.DMA((FFN_STAGE_SLOTS,)),
    ]


def _ffn_copy_scratch():
    return [pltpu.SemaphoreType.DMA((3,))]


def _ffn_vmem_bytes(tm):
    return (2 * 3 * D_MODEL * D_FF + FFN_STAGE_SLOTS * FFN_STAGE_DIM * FFN_STAGE_DIM * 4
            + tm * D_MODEL * 2 + tm * D_FF * 2 + 6 * tm * FFN_CHUNK_F * 4 + tm * D_MODEL * 4)


def _ffn_weight_copy(which, w_hbm, base_scratch, copy_scratch):
    (sem_ref,) = copy_scratch
    return pltpu.make_async_copy(w_hbm[which], base_scratch[which], sem_ref.at[which])


def _ffn_start_weight_copies(w_hbm, base_scratch, copy_scratch):
    for which in range(3):
        _ffn_weight_copy(which, w_hbm, base_scratch, copy_scratch).start()


def _swiglu_tile(x32, g_ref, w_hbm, base_scratch, stream, stream_scratch, on_streamed=None):
    wg_ref, wu_ref, wd_ref, h_ref, a_ref = base_scratch

    def piece_copy(i):
        stage_ref, sem_ref = stream_scratch
        which, lo, hi = FFN_STREAM_PIECES[i]
        size = hi - lo
        slot = i % FFN_STAGE_SLOTS
        if which == 2:
            return pltpu.make_async_copy(w_hbm[2].at[pl.ds(lo, size), :],
                                         stage_ref.at[slot, pl.ds(0, size), pl.ds(0, D_MODEL)],
                                         sem_ref.at[slot])
        return pltpu.make_async_copy(w_hbm[which].at[:, pl.ds(lo, size)],
                                     stage_ref.at[slot, pl.ds(0, D_MODEL), pl.ds(0, size)],
                                     sem_ref.at[slot])

    def land_piece(i):
        stage_ref, _ = stream_scratch
        which, lo, hi = FFN_STREAM_PIECES[i]
        slot = i % FFN_STAGE_SLOTS
        piece_copy(i).wait()
        if which == 2:
            wd_ref[lo:hi, :] = stage_ref[slot, :hi - lo, :D_MODEL].astype(jnp.bfloat16)
        else:
            dst = (wg_ref, wu_ref)[which]
            dst[:, lo:hi] = stage_ref[slot, :D_MODEL, :hi - lo].astype(jnp.bfloat16)
        if i + FFN_STAGE_SLOTS < len(FFN_STREAM_PIECES):
            piece_copy(i + FFN_STAGE_SLOTS).start()

    if stream == "round":
        for i in range(FFN_STAGE_SLOTS):
            piece_copy(i).start()
    if x32 is not None:
        h_ref[...] = _rmsnorm_rows(x32, g_ref[...]).astype(jnp.bfloat16)
    for c in range(N_FFN_CHUNKS):
        cols = slice(c * FFN_CHUNK_F, (c + 1) * FFN_CHUNK_F)
        if stream == "round":
            for i, (which, lo, _) in enumerate(FFN_STREAM_PIECES):
                if which != 2 and lo == c * FFN_CHUNK_F:
                    land_piece(i)
        elif stream == "wait" and c == 0:
            for which in range(2):
                _ffn_weight_copy(which, w_hbm, base_scratch, stream_scratch).wait()
        h = h_ref[...]
        gate = jnp.dot(h, wg_ref[:, cols], preferred_element_type=jnp.float32)
        up = jnp.dot(h, wu_ref[:, cols], preferred_element_type=jnp.float32)
        act = gate * (1.0 / (1.0 + jnp.exp(-gate))) * up
        a_ref[:, cols] = act.astype(jnp.bfloat16)
    if stream == "round":
        for i, (which, _, _) in enumerate(FFN_STREAM_PIECES):
            if which == 2:
                land_piece(i)
        if on_streamed is not None:
            on_streamed()
    elif stream == "wait":
        _ffn_weight_copy(2, w_hbm, base_scratch, stream_scratch).wait()
    return jnp.dot(a_ref[...], wd_ref[...], preferred_element_type=jnp.float32)


def _ffn_proj_kernel(x_ref, g1_ref, wg_hbm, wu_hbm, wd_hbm, gm_ref, win_hbm, qg_ref, kg_ref, dm_ref,
                     wg2_ref, wu2_ref, wd2_ref, wo_ref, rel_ref, bucket_ref,
                     x1_ref, z_ref, wg2_bf_ref, wu2_bf_ref, wd2_bf_ref, wo_bf_ref, bias_ref,
                     *scratch):
    tm = x_ref.shape[0]
    base_scratch = scratch[:N_FFN_BASE_SCRATCH]
    round_scratch, win_ref = scratch[N_FFN_BASE_SCRATCH:-1], scratch[-1]

    win_pieces = tuple((lo, hi - lo, slot) for slot, (lo, hi) in
                       enumerate(((0, FFN_STAGE_DIM), (FFN_STAGE_DIM, D_IN))))

    def win_copy(lo, size, slot):
        stage_ref, sem_ref = round_scratch
        return pltpu.make_async_copy(win_hbm.at[:, pl.ds(lo, size)],
                                     stage_ref.at[slot, pl.ds(0, D_MODEL), pl.ds(0, size)],
                                     sem_ref.at[slot])

    def start_win_copies():
        for piece in win_pieces:
            win_copy(*piece).start()

    def land_win():
        stage_ref, _ = round_scratch
        for lo, size, slot in win_pieces:
            win_copy(lo, size, slot).wait()
            win_ref[:, lo:lo + size] = stage_ref[slot, :D_MODEL, :size].astype(jnp.bfloat16)

    first_half = pl.program_id(0) < pl.num_programs(0) // 2

    @pl.when(first_half)
    def _():
        wg2_bf_ref[...] = wg2_ref[...].astype(jnp.bfloat16)
        wd2_bf_ref[...] = wd2_ref[...].astype(jnp.bfloat16)

    @pl.when(jnp.logical_not(first_half))
    def _():
        wu2_bf_ref[...] = wu2_ref[...].astype(jnp.bfloat16)
        wo_bf_ref[...] = wo_ref[...].astype(jnp.bfloat16)

    def body(stream):
        if stream is not None:
            _write_band_bias(rel_ref, bucket_ref, bias_ref)
        y = _swiglu_tile(x_ref[...], g1_ref, (wg_hbm, wu_hbm, wd_hbm), base_scratch, stream,
                         round_scratch, on_streamed=start_win_copies)
        x1 = x_ref[...] + 0.5 * y
        x1_ref[...] = x1
        if stream is not None:
            land_win()
        h = _rmsnorm_rows(x1, gm_ref[...]).astype(jnp.bfloat16)
        z = jnp.dot(h, win_ref[...], preferred_element_type=jnp.float32)
        n_slab = (D_ATTN + 2 * D_KV) // V7X_MXU_DIM
        sq = jnp.concatenate(
            [z[:, i * V7X_MXU_DIM:(i + 1) * V7X_MXU_DIM] for i in range(n_slab)], axis=0)
        ms = jnp.dot((sq * sq).astype(jnp.bfloat16), dm_ref[...], preferred_element_type=jnp.float32)
        q_ms = jnp.concatenate([ms[i * tm:(i + 1) * tm] for i in range(n_slab - 1)], axis=1)
        k_ms = ms[(n_slab - 1) * tm:, :D_KV]
        q = z[:, :D_ATTN]
        k = z[:, D_ATTN:D_ATTN + D_KV]
        z_ref[:, :D_ATTN] = (q * lax.rsqrt(q_ms + EPS) * qg_ref[...]).astype(jnp.bfloat16)
        z_ref[:, D_ATTN:D_ATTN + D_KV] = (k * lax.rsqrt(k_ms + EPS) * kg_ref[...]).astype(jnp.bfloat16)
        z_ref[:, D_ATTN + D_KV:] = z[:, D_ATTN + D_KV:].astype(jnp.bfloat16)

    first = pl.program_id(0) == 0
    pl.when(first)(functools.partial(body, "round"))
    pl.when(jnp.logical_not(first))(functools.partial(body, None))


def _head_mean_matrix(width):
    idx = np.arange(width) // HEAD_DIM
    return jnp.asarray((idx[:, None] == idx[None, :]).astype(np.float32) / HEAD_DIM, jnp.bfloat16)


def _ffn_proj(x2d, ffn_gain, w_gate, w_up, w_down, mix_gain, w_in, q_gain, k_gain,
              next_w_gate, next_w_up, next_w_down, w_out, rel_bias):
    m = x2d.shape[0]
    tm = TILE_M
    n_steps = m // tm
    resident = functools.partial(pl.BlockSpec, pipeline_mode=pl.Buffered(1))
    rows = lambda width: pl.BlockSpec((tm, width), lambda i: (i, 0))
    hbm = pl.BlockSpec(memory_space=pltpu.HBM)
    half_steps = n_steps // 2
    wr = D_MODEL // half_steps
    wdr = D_FF // half_steps
    assert 2 * half_steps == n_steps and wr * half_steps == D_MODEL and wdr * half_steps == D_FF
    assert wr % 16 == 0 and wdr % 16 == 0
    early_slice = lambda i: (jnp.minimum(i, half_steps - 1), 0)
    late_slice = lambda i: (jnp.maximum(i - half_steps, 0), 0)
    wg_rows = pl.BlockSpec((wr, D_FF), early_slice)
    wd_rows = pl.BlockSpec((wdr, D_MODEL), early_slice)
    wu_rows = pl.BlockSpec((wr, D_FF), late_slice)
    wo_rows = pl.BlockSpec((wr, D_MODEL), late_slice)
    qg = (jnp.tile(q_gain, N_HEADS) * (HEAD_DIM ** -0.5)).reshape(1, D_ATTN)
    kg = jnp.tile(k_gain, N_KV_HEADS).reshape(1, D_KV)
    est = (_ffn_vmem_bytes(tm) + 2 * D_MODEL * D_IN
           + 2 * tm * D_MODEL * 4 * 2
           + 2 * tm * D_IN * 2
           + 4 * tm * D_IN * 4
           + 2 * 6 * (2 * wr * D_FF + wdr * D_MODEL + wr * D_MODEL)
           + 3 * N_HEADS * BLOCK * 2 * BLOCK * 4)
    bf16 = lambda shape: jax.ShapeDtypeStruct(shape, jnp.bfloat16)
    return pl.pallas_call(
        _ffn_proj_kernel,
        out_shape=(
            jax.ShapeDtypeStruct((m, D_MODEL), jnp.float32),
            bf16((m, D_IN)),
            bf16((D_MODEL, D_FF)), bf16((D_MODEL, D_FF)), bf16((D_FF, D_MODEL)),
            bf16((D_MODEL, D_MODEL)),
            jax.ShapeDtypeStruct((N_HEADS, BLOCK, 2 * BLOCK), jnp.float32),
        ),
        grid=(n_steps,),
        in_specs=[
            rows(D_MODEL),
            resident((1, D_MODEL), lambda i: (0, 0)),
            hbm, hbm, hbm,
            resident((1, D_MODEL), lambda i: (0, 0)),
            hbm,
            resident((1, D_ATTN), lambda i: (0, 0)),
            resident((1, D_KV), lambda i: (0, 0)),
            resident((V7X_MXU_DIM, V7X_MXU_DIM), lambda i: (0, 0)),
            wg_rows, wu_rows, wd_rows, wo_rows,
            pl.BlockSpec(memory_space=pltpu.SMEM),
            resident((BLOCK, 2 * BLOCK), lambda i: (0, 0)),
        ],
        out_specs=(rows(D_MODEL), rows(D_IN), wg_rows, wu_rows, wd_rows, wo_rows,
                   pl.BlockSpec((N_HEADS, BLOCK, 2 * BLOCK), lambda i: (0, 0, 0))),
        scratch_shapes=(_ffn_base_scratch(tm) + _ffn_round_scratch()
                        + [pltpu.VMEM((D_MODEL, D_IN), jnp.bfloat16)]),
        compiler_params=pltpu.CompilerParams(
            dimension_semantics=("arbitrary",),
            vmem_limit_bytes=_vmem_limit(est)),
        name="ffn_proj",
    )(x2d, ffn_gain.reshape(1, D_MODEL), w_gate, w_up, w_down, mix_gain.reshape(1, D_MODEL),
      w_in, qg, kg, _head_mean_matrix(V7X_MXU_DIM),
      next_w_gate, next_w_up, next_w_down, w_out, rel_bias, jnp.asarray(_band_bucket_table()))


def _t5_bucket(dist):
    n = np.maximum(dist, 0)
    max_exact = N_BUCKETS // 2
    large = max_exact + (np.log(np.maximum(n, 1) / max_exact)
                         / np.log(MAX_DISTANCE / max_exact)
                         * (N_BUCKETS - max_exact)).astype(np.int32)
    large = np.minimum(large, N_BUCKETS - 1)
    return np.where(n < max_exact, n, large).astype(np.int32)


def _band_bucket_table():
    ql = np.arange(BLOCK)[:, None]
    kl = np.arange(2 * BLOCK)[None, :]
    dist = ql + BLOCK - kl
    in_band = (dist >= 0) & (dist < WINDOW)
    return np.where(in_band, _t5_bucket(dist), -1).astype(np.int32)


def _write_band_bias(rel_ref, bucket_ref, o_ref):
    bucket = bucket_ref[...]
    for slot, head in enumerate(HEADS_PLAIN + HEADS_ROLLED):
        acc = jnp.full((BLOCK, 2 * BLOCK), NEG, jnp.float32)
        for b in range(N_BUCKETS):
            acc = jnp.where(bucket == b, rel_ref[b, head], acc)
        o_ref[slot] = acc


def _window_sum_matrices():
    ql = np.arange(BLOCK)[:, None]
    kl = np.arange(2 * BLOCK)[None, :]
    dist = ql + BLOCK - kl
    mats = [((dist >= 0) & (dist < w)).astype(np.float32) for w in POOL_WINDOWS]
    return jnp.asarray(np.stack(mats), jnp.bfloat16)


def _mix_tile(seq_first, t_base, sink_ref, x_ref, z_ref, zh_ref,
              bias_ref, wsum_ref, wo_ref, x2_ref, mix_scratch, g_ref, h_ref):
    kx_ref, kxr_ref, vx_ref, vxr_ref, ux_ref, pooled_ref, y_ref, wc_ref = mix_scratch
    tq = z_ref.shape[1]
    k_cols = slice(D_ATTN, D_ATTN + D_KV)
    v_cols = slice(D_ATTN + D_KV, D_ATTN + 2 * D_KV)
    u_cols = slice(D_ATTN + 2 * D_KV, D_IN)
    n_blocks = tq // BLOCK
    half = HEAD_DIM
    groups = ((HEADS_PLAIN, kx_ref, vx_ref, 0), (HEADS_ROLLED, kxr_ref, vxr_ref, len(HEADS_PLAIN)))
    lane = lax.broadcasted_iota(jnp.int32, (BLOCK, 2 * half), 1)
    low = lane < half
    state = {}

    def rows_of(j):
        return slice(j * BLOCK, (j + 1) * BLOCK)

    def keys_of(j):
        return slice(j * BLOCK, (j + 2) * BLOCK)

    def build_slabs():
        halo = jnp.where(seq_first, jnp.zeros_like(zh_ref[0, :, D_ATTN:]), zh_ref[0, :, D_ATTN:])
        kx_ref[:BLOCK] = halo[:, :D_KV]
        kx_ref[BLOCK:] = z_ref[0, :, k_cols]
        kxr_ref[...] = pltpu.roll(kx_ref[...].astype(jnp.float32), half, 1).astype(jnp.bfloat16)
        ones = jnp.ones((tq + BLOCK, D_KV), jnp.bfloat16)
        vx_ref[:BLOCK, :D_KV] = halo[:, D_KV:2 * D_KV]
        vx_ref[BLOCK:, :D_KV] = z_ref[0, :, v_cols]
        vx_ref[:, D_KV:] = ones
        vxr_ref[:, :D_KV] = pltpu.roll(
            vx_ref[:, :D_KV].astype(jnp.float32), half, 1).astype(jnp.bfloat16)
        vxr_ref[:, D_KV:] = ones
        ux_ref[:BLOCK] = halo[:, 2 * D_KV:]
        ux_ref[BLOCK:] = z_ref[0, :, u_cols]

    def scores(j, grp):
        heads, k_src, _, _ = groups[grp]
        q = z_ref[0, rows_of(j), :D_ATTN]
        zero = jnp.zeros((BLOCK, 2 * half), q.dtype)
        pieces = []
        for hd in heads:
            pair = q[:, (hd // 2) * 2 * half:(hd // 2 + 1) * 2 * half]
            pieces.append(jnp.where(low if hd % 2 == 0 else ~low, pair, zero))
        qs = jnp.concatenate(pieces, axis=0)
        state["logits", j, grp] = lax.dot_general(
            qs, k_src[keys_of(j), :], (((1,), (1,)), ((), ())),
            preferred_element_type=jnp.float32)

    def softmax_numerators(j, grp):
        heads, _, _, slot0 = groups[grp]
        logits = state.pop(("logits", j, grp))
        es, sink_terms = [], []
        for s, hd in enumerate(heads):
            lg = logits[s * BLOCK:(s + 1) * BLOCK] + bias_ref[slot0 + s]
            if j == 0:
                col = lax.broadcasted_iota(jnp.int32, (1, 2 * BLOCK), 1)
                lg = lg + jnp.where(seq_first & (col < BLOCK), NEG, 0.0).astype(jnp.float32)
            sink = sink_ref[hd]
            m = jnp.maximum(jnp.max(lg, axis=-1, keepdims=True), sink)
            es.append(jnp.exp(lg - m).astype(jnp.bfloat16))
            sink_terms.append(jnp.exp(sink - m))
        state["e", j, grp] = (jnp.concatenate(es, axis=0), sink_terms)

    def attend(j, grp):
        heads, _, v_src, _ = groups[grp]
        e, sink_terms = state.pop(("e", j, grp))
        pv = jnp.dot(e, v_src[keys_of(j), :], preferred_element_type=jnp.float32)
        for s, hd in enumerate(heads):
            blk = pv[s * BLOCK:(s + 1) * BLOCK]
            state["out", j, hd] = blk[:, :D_KV] / (blk[:, D_KV:] + sink_terms[s])

    def store_attention(j):
        for p in range(N_HEADS // 2):
            y_ref[rows_of(j), p * 2 * half:(p + 1) * 2 * half] = jnp.where(
                low, state.pop(("out", j, 2 * p)), state.pop(("out", j, 2 * p + 1))
            ).astype(jnp.bfloat16)

    def pool_means(j):
        row = lax.broadcasted_iota(jnp.int32, (BLOCK, 1), 0)
        t_glob = t_base + j * BLOCK + row
        for g, w in enumerate(POOL_WINDOWS):
            gc = slice(g * POOL_GROUP_DIM, (g + 1) * POOL_GROUP_DIM)
            wsum = jnp.dot(wsum_ref[g], ux_ref[keys_of(j), gc], preferred_element_type=jnp.float32)
            cnt = jnp.minimum(t_glob + 1, w).astype(jnp.float32)
            token = z_ref[0, rows_of(j), D_ATTN + 2 * D_KV + g * POOL_GROUP_DIM:
                          D_ATTN + 2 * D_KV + (g + 1) * POOL_GROUP_DIM]
            pooled = wsum * (1.0 / cnt) - token.astype(jnp.float32)
            pooled_ref[rows_of(j), gc] = pooled.astype(jnp.bfloat16)

    def project_pool():
        x2_ref[...] = x_ref[0] + jnp.dot(pooled_ref[...], wc_ref[...],
                                         preferred_element_type=jnp.float32)

    def project_attention():
        x2 = x2_ref[...] + jnp.dot(y_ref[...], wo_ref[:D_ATTN, :], preferred_element_type=jnp.float32)
        x2_ref[...] = x2
        h_ref[...] = _rmsnorm_rows(x2, g_ref[...]).astype(jnp.bfloat16)

    build_slabs()
    for j in range(n_blocks):
        pool_means(j)
    scores(0, 0)
    for j in range(n_blocks):
        scores(j, 1)
        softmax_numerators(j, 0)
        attend(j, 0)
        softmax_numerators(j, 1)
        if j == 0:
            project_pool()
        if j + 1 < n_blocks:
            scores(j + 1, 0)
        attend(j, 1)
        store_attention(j)
    project_attention()


N_MIX_SCRATCH = 9


def _mix_ffn_kernel(sink_ref, x_ref, z_ref, zh_ref,
                    bias_ref, wsum_ref, pw_ref, ps_ref, wo_ref, g2_ref, wg_hbm, wu_hbm, wd_hbm,
                    o_ref, *scratch, tiles_per_seq):
    mix_scratch, x2_ref = scratch[:N_MIX_SCRATCH - 1], scratch[N_MIX_SCRATCH - 1]
    base_scratch = scratch[N_MIX_SCRATCH:N_MIX_SCRATCH + N_FFN_BASE_SCRATCH]
    copy_scratch = scratch[N_MIX_SCRATCH + N_FFN_BASE_SCRATCH:]
    h_ref = base_scratch[3]
    t = pl.program_id(0)
    tq = z_ref.shape[1]
    i_seq = lax.rem(t, tiles_per_seq)
    w_hbm = (wg_hbm, wu_hbm, wd_hbm)

    def step(first):
        if first:
            _ffn_start_weight_copies(w_hbm, base_scratch, copy_scratch)
            for g in range(len(POOL_WINDOWS)):
                gc = slice(g * POOL_GROUP_DIM, (g + 1) * POOL_GROUP_DIM)
                pool_map = (pw_ref[g] * ps_ref[:, gc]).astype(jnp.bfloat16)
                mix_scratch[-1][gc, :] = jnp.dot(
                    pool_map, wo_ref[D_ATTN + g * POOL_GROUP_DIM:D_ATTN + (g + 1) * POOL_GROUP_DIM, :],
                    preferred_element_type=jnp.float32).astype(jnp.bfloat16)
        _mix_tile(i_seq == 0, i_seq * tq, sink_ref, x_ref, z_ref, zh_ref,
                  bias_ref, wsum_ref, wo_ref, x2_ref, mix_scratch, g2_ref, h_ref)
        y = _swiglu_tile(None, g2_ref, w_hbm, base_scratch, "wait" if first else None, copy_scratch)
        o_ref[0] = x2_ref[...] + 0.5 * y

    pl.when(t == 0)(functools.partial(step, True))
    pl.when(t > 0)(functools.partial(step, False))


def _mix_ffn(x3d, z3d, sinks, bias, pool_w, pool_scale, w_out_bf16, ffn_gain,
             w_gate_bf16, w_up_bf16, w_down_bf16):
    b, s, _ = x3d.shape
    tq = TILE_M
    nblk = tq // BLOCK
    tiles_per_seq = s // tq
    n_tiles = b * tiles_per_seq
    resident = functools.partial(pl.BlockSpec, pipeline_mode=pl.Buffered(1))
    hbm = pl.BlockSpec(memory_space=pltpu.HBM)

    def tile(width):
        return pl.BlockSpec((1, tq, width), lambda t: (t // tiles_per_seq, lax.rem(t, tiles_per_seq), 0))

    def halo(width):
        return pl.BlockSpec(
            (1, BLOCK, width),
            lambda t: (t // tiles_per_seq, jnp.maximum(lax.rem(t, tiles_per_seq) * nblk - 1, 0), 0))

    est = (_ffn_vmem_bytes(tq)
           + 2 * 2 * tq * D_MODEL * 4 + 2 * (tq + BLOCK) * D_IN * 2
           + N_HEADS * BLOCK * 2 * BLOCK * 4 + 2 * D_MODEL * D_MODEL + 4 * D_POOL * D_POOL
           + 2 * D_POOL * D_MODEL
           + 6 * (tq + BLOCK) * D_KV * 2 + (tq + BLOCK) * D_POOL * 2 + tq * D_MODEL * 2
           + tq * D_MODEL * 4 + 16 * BLOCK * 2 * BLOCK * 4 * 4)
    return pl.pallas_call(
        functools.partial(_mix_ffn_kernel, tiles_per_seq=tiles_per_seq),
        out_shape=jax.ShapeDtypeStruct((b, s, D_MODEL), jnp.float32),
        grid=(n_tiles,),
        in_specs=[
            pl.BlockSpec(memory_space=pltpu.SMEM),
            tile(D_MODEL),
            tile(D_IN), halo(D_IN),
            resident((N_HEADS, BLOCK, 2 * BLOCK), lambda t: (0, 0, 0)),
            resident((len(POOL_WINDOWS), BLOCK, 2 * BLOCK), lambda t: (0, 0, 0)),
            resident((len(POOL_WINDOWS), POOL_GROUP_DIM, POOL_GROUP_DIM), lambda t: (0, 0, 0)),
            resident((1, D_POOL), lambda t: (0, 0)),
            resident((D_MODEL, D_MODEL), lambda t: (0, 0)),
            resident((1, D_MODEL), lambda t: (0, 0)),
            hbm, hbm, hbm,
        ],
        out_specs=tile(D_MODEL),
        scratch_shapes=[
            pltpu.VMEM((tq + BLOCK, D_KV), jnp.bfloat16),
            pltpu.VMEM((tq + BLOCK, D_KV), jnp.bfloat16),
            pltpu.VMEM((tq + BLOCK, 2 * D_KV), jnp.bfloat16),
            pltpu.VMEM((tq + BLOCK, 2 * D_KV), jnp.bfloat16),
            pltpu.VMEM((tq + BLOCK, D_POOL), jnp.bfloat16),
            pltpu.VMEM((tq, D_POOL), jnp.bfloat16),
            pltpu.VMEM((tq, D_ATTN), jnp.bfloat16),
            pltpu.VMEM((D_POOL, D_MODEL), jnp.bfloat16),
            pltpu.VMEM((tq, D_MODEL), jnp.float32),
        ] + _ffn_base_scratch(tq) + _ffn_copy_scratch(),
        compiler_params=pltpu.CompilerParams(
            dimension_semantics=("arbitrary",),
            vmem_limit_bytes=_vmem_limit(est)),
        name="mix_ffn",
    )(sinks, x3d, z3d, z3d, bias, _window_sum_matrices(),
      pool_w, pool_scale.reshape(1, D_POOL),
      w_out_bf16, ffn_gain.reshape(1, D_MODEL), w_gate_bf16, w_up_bf16, w_down_bf16)


def kernel(x, ffn1_norm, ffn1_w_gate, ffn1_w_up, ffn1_w_down, mix_norm, w_in, q_norm, k_norm,
           attn_sinks, rel_bias, pool_w, pool_scale, w_out, ffn2_norm, ffn2_w_gate, ffn2_w_up,
           ffn2_w_down):
    b, s, d = x.shape
    assert (d, s % TILE_M) == (D_MODEL, 0)
    for l in range(ffn1_norm.shape[0]):
        x1, z, wg2, wu2, wd2, wo, bias = _ffn_proj(
            x.reshape(b * s, d), ffn1_norm[l], ffn1_w_gate[l], ffn1_w_up[l], ffn1_w_down[l],
            mix_norm[l], w_in[l], q_norm[l], k_norm[l],
            ffn2_w_gate[l], ffn2_w_up[l], ffn2_w_down[l], w_out[l], rel_bias)
        x = _mix_ffn(x1.reshape(b, s, d), z.reshape(b, s, D_IN), attn_sinks[l], bias,
                     pool_w[l], pool_scale[l], wo, ffn2_norm[l], wg2, wu2, wd2)
    return x
```

```python
import functools

import numpy as np
import jax
import jax.numpy as jnp
from jax import lax
from jax.experimental import pallas as pl
from jax.experimental.pallas import tpu as pltpu

D_MODEL = 1024
HEAD_DIM = 64
N_HEADS = 8
N_KV_HEADS = 2
D_ATTN = N_HEADS * HEAD_DIM
D_KV = N_KV_HEADS * HEAD_DIM
D_POOL = D_MODEL - D_ATTN
POOL_WINDOWS = (2, 4, 8, 16)
POOL_GROUP_DIM = D_POOL // len(POOL_WINDOWS)
D_IN = D_ATTN + 2 * D_KV + D_POOL
WINDOW = 128
BLOCK = 128
N_BUCKETS = 32
MAX_DISTANCE = 128
D_FF = 2816
EPS = 1e-6
NEG = -1e30

V7X_MXU_DIM = 256
V7X_VMEM_BYTES = 64 * 1024 * 1024

HEADS_PLAIN = (0, 2, 5, 7)
HEADS_ROLLED = (1, 3, 4, 6)

TILE_M = 512
FFN_CHUNK_F = 256
FFN_STAGE_SLOTS = 2


VMEM_COMPILER_TEMPORARIES_BYTES = 8 << 20
VMEM_LIMIT_FLOOR_BYTES = 16 << 20
VMEM_LIMIT_CEILING_BYTES = V7X_VMEM_BYTES - (4 << 20)


def _vmem_limit(buffer_bytes):
    wanted = buffer_bytes + VMEM_COMPILER_TEMPORARIES_BYTES
    return int(min(VMEM_LIMIT_CEILING_BYTES, max(wanted, VMEM_LIMIT_FLOOR_BYTES)))


def _rmsnorm_rows(x32, gain_row):
    ms = jnp.mean(x32 * x32, axis=-1, keepdims=True)
    return x32 * lax.rsqrt(ms + EPS) * gain_row


N_FFN_CHUNKS = D_FF // FFN_CHUNK_F
N_FFN_BASE_SCRATCH = 5


def _ffn_base_scratch(tm):
    return [
        pltpu.VMEM((D_MODEL, D_FF), jnp.bfloat16),
        pltpu.VMEM((D_MODEL, D_FF), jnp.bfloat16),
        pltpu.VMEM((D_FF, D_MODEL), jnp.bfloat16),
        pltpu.VMEM((tm, D_MODEL), jnp.bfloat16),
        pltpu.VMEM((tm, D_FF), jnp.bfloat16),
    ]


FFN_STREAM_BOUNDS = (0, 1024, 2048, D_FF)
FFN_STREAM_ORDER = tuple((which, k) for k in range(len(FFN_STREAM_BOUNDS) - 1) for which in (0, 1)) + tuple(
    (2, k) for k in range(len(FFN_STREAM_BOUNDS) - 1))
FFN_STAGE_DIM = max(hi - lo for lo, hi in zip(FFN_STREAM_BOUNDS, FFN_STREAM_BOUNDS[1:]))


def _ffn_round_scratch():
    assert D_MODEL <= FFN_STAGE_DIM and all(b % FFN_CHUNK_F == 0 for b in FFN_STREAM_BOUNDS)
    return [
        pltpu.VMEM((FFN_STAGE_SLOTS, FFN_STAGE_DIM, FFN_STAGE_DIM), jnp.float32),
        pltpu.SemaphoreType.DMA((FFN_STAGE_SLOTS,)),
    ]


def _ffn_copy_scratch():
    return [pltpu.SemaphoreType.DMA((3,))]


def _ffn_vmem_bytes(tm):
    return (2 * 3 * D_MODEL * D_FF + FFN_STAGE_SLOTS * FFN_STAGE_DIM * FFN_STAGE_DIM * 4
            + tm * D_MODEL * 2 + tm * D_FF * 2 + 6 * tm * FFN_CHUNK_F * 4 + tm * D_MODEL * 4)


def _ffn_weight_copy(which, w_hbm, base_scratch, copy_scratch):
    (sem_ref,) = copy_scratch
    return pltpu.make_async_copy(w_hbm[which], base_scratch[which], sem_ref.at[which])


def _ffn_start_weight_copies(w_hbm, base_scratch, copy_scratch):
    for which in range(3):
        _ffn_weight_copy(which, w_hbm, base_scratch, copy_scratch).start()


def _swiglu_tile(x32, g_ref, w_hbm, base_scratch, stream, stream_scratch, on_streamed=None):
    wg_ref, wu_ref, wd_ref, h_ref, a_ref = base_scratch

    def piece_copy(i):
        stage_ref, sem_ref = stream_scratch
        which, k = FFN_STREAM_ORDER[i]
        lo, size = FFN_STREAM_BOUNDS[k], FFN_STREAM_BOUNDS[k + 1] - FFN_STREAM_BOUNDS[k]
        slot = i % FFN_STAGE_SLOTS
        if which == 2:
            return pltpu.make_async_copy(w_hbm[2].at[pl.ds(lo, size), :],
                                         stage_ref.at[slot, pl.ds(0, size), pl.ds(0, D_MODEL)],
                                         sem_ref.at[slot])
        return pltpu.make_async_copy(w_hbm[which].at[:, pl.ds(lo, size)],
                                     stage_ref.at[slot, pl.ds(0, D_MODEL), pl.ds(0, size)],
                                     sem_ref.at[slot])

    def land_piece(i):
        stage_ref, _ = stream_scratch
        which, k = FFN_STREAM_ORDER[i]
        lo, hi = FFN_STREAM_BOUNDS[k], FFN_STREAM_BOUNDS[k + 1]
        slot = i % FFN_STAGE_SLOTS
        piece_copy(i).wait()
        if which == 2:
            wd_ref[lo:hi, :] = stage_ref[slot, :hi - lo, :D_MODEL].astype(jnp.bfloat16)
        else:
            dst = (wg_ref, wu_ref)[which]
            dst[:, lo:hi] = stage_ref[slot, :D_MODEL, :hi - lo].astype(jnp.bfloat16)
        if i + FFN_STAGE_SLOTS < len(FFN_STREAM_ORDER):
            piece_copy(i + FFN_STAGE_SLOTS).start()

    if stream == "round":
        for i in range(FFN_STAGE_SLOTS):
            piece_copy(i).start()
    if x32 is not None:
        h_ref[...] = _rmsnorm_rows(x32, g_ref[...]).astype(jnp.bfloat16)
    for c in range(N_FFN_CHUNKS):
        cols = slice(c * FFN_CHUNK_F, (c + 1) * FFN_CHUNK_F)
        if stream == "round":
            for i, (which, k) in enumerate(FFN_STREAM_ORDER):
                if which != 2 and FFN_STREAM_BOUNDS[k] == c * FFN_CHUNK_F:
                    land_piece(i)
        elif stream == "wait" and c == 0:
            for which in range(2):
                _ffn_weight_copy(which, w_hbm, base_scratch, stream_scratch).wait()
        h = h_ref[...]
        gate = jnp.dot(h, wg_ref[:, cols], preferred_element_type=jnp.float32)
        up = jnp.dot(h, wu_ref[:, cols], preferred_element_type=jnp.float32)
        act = gate * (1.0 / (1.0 + jnp.exp(-gate))) * up
        a_ref[:, cols] = act.astype(jnp.bfloat16)
    if stream == "round":
        for i, (which, _) in enumerate(FFN_STREAM_ORDER):
            if which == 2:
                land_piece(i)
        if on_streamed is not None:
            on_streamed()
    elif stream == "wait":
        _ffn_weight_copy(2, w_hbm, base_scratch, stream_scratch).wait()
    return jnp.dot(a_ref[...], wd_ref[...], preferred_element_type=jnp.float32)


def _ffn_proj_kernel(x_ref, g1_ref, wg_hbm, wu_hbm, wd_hbm, gm_ref, win_hbm, qg_ref, kg_ref, dm_ref,
                     wg2_ref, wu2_ref, wd2_ref, wo_ref, rel_ref, bucket_ref,
                     x1_ref, z_ref, wg2_bf_ref, wu2_bf_ref, wd2_bf_ref, wo_bf_ref, bias_ref,
                     *scratch):
    tm = x_ref.shape[0]
    base_scratch = scratch[:N_FFN_BASE_SCRATCH]
    round_scratch, win_ref = scratch[N_FFN_BASE_SCRATCH:-1], scratch[-1]

    win_pieces = tuple((lo, hi - lo, slot) for slot, (lo, hi) in
                       enumerate(((0, FFN_STAGE_DIM), (FFN_STAGE_DIM, D_IN))))

    def win_copy(lo, size, slot):
        stage_ref, sem_ref = round_scratch
        return pltpu.make_async_copy(win_hbm.at[:, pl.ds(lo, size)],
                                     stage_ref.at[slot, pl.ds(0, D_MODEL), pl.ds(0, size)],
                                     sem_ref.at[slot])

    def start_win_copies():
        for piece in win_pieces:
            win_copy(*piece).start()

    def land_win():
        stage_ref, _ = round_scratch
        for lo, size, slot in win_pieces:
            win_copy(lo, size, slot).wait()
            win_ref[:, lo:lo + size] = stage_ref[slot, :D_MODEL, :size].astype(jnp.bfloat16)

    first_half = pl.program_id(0) < pl.num_programs(0) // 2

    @pl.when(first_half)
    def _():
        wg2_bf_ref[...] = wg2_ref[...].astype(jnp.bfloat16)
        wd2_bf_ref[...] = wd2_ref[...].astype(jnp.bfloat16)

    @pl.when(jnp.logical_not(first_half))
    def _():
        wu2_bf_ref[...] = wu2_ref[...].astype(jnp.bfloat16)
        wo_bf_ref[...] = wo_ref[...].astype(jnp.bfloat16)

    def body(stream):
        if stream is not None:
            _write_band_bias(rel_ref, bucket_ref, bias_ref)
        y = _swiglu_tile(x_ref[...], g1_ref, (wg_hbm, wu_hbm, wd_hbm), base_scratch, stream,
                         round_scratch, on_streamed=start_win_copies)
        x1 = x_ref[...] + 0.5 * y
        x1_ref[...] = x1
        if stream is not None:
            land_win()
        h = _rmsnorm_rows(x1, gm_ref[...]).astype(jnp.bfloat16)
        z = jnp.dot(h, win_ref[...], preferred_element_type=jnp.float32)
        n_slab = (D_ATTN + 2 * D_KV) // V7X_MXU_DIM
        sq = jnp.concatenate(
            [z[:, i * V7X_MXU_DIM:(i + 1) * V7X_MXU_DIM] for i in range(n_slab)], axis=0)
        ms = jnp.dot((sq * sq).astype(jnp.bfloat16), dm_ref[...], preferred_element_type=jnp.float32)
        q_ms = jnp.concatenate([ms[i * tm:(i + 1) * tm] for i in range(n_slab - 1)], axis=1)
        k_ms = ms[(n_slab - 1) * tm:, :D_KV]
        q = z[:, :D_ATTN]
        k = z[:, D_ATTN:D_ATTN + D_KV]
        z_ref[:, :D_ATTN] = (q * lax.rsqrt(q_ms + EPS) * qg_ref[...]).astype(jnp.bfloat16)
        z_ref[:, D_ATTN:D_ATTN + D_KV] = (k * lax.rsqrt(k_ms + EPS) * kg_ref[...]).astype(jnp.bfloat16)
        z_ref[:, D_ATTN + D_KV:] = z[:, D_ATTN + D_KV:].astype(jnp.bfloat16)

    first = pl.program_id(0) == 0
    pl.when(first)(functools.partial(body, "round"))
    pl.when(jnp.logical_not(first))(functools.partial(body, None))


def _head_mean_matrix(width):
    idx = np.arange(width) // HEAD_DIM
    return jnp.asarray((idx[:, None] == idx[None, :]).astype(np.float32) / HEAD_DIM, jnp.bfloat16)


def _ffn_proj(x2d, ffn_gain, w_gate, w_up, w_down, mix_gain, w_in, q_gain, k_gain,
              next_w_gate, next_w_up, next_w_down, w_out, rel_bias):
    m = x2d.shape[0]
    tm = TILE_M
    n_steps = m // tm
    resident = functools.partial(pl.BlockSpec, pipeline_mode=pl.Buffered(1))
    rows = lambda width: pl.BlockSpec((tm, width), lambda i: (i, 0))
    hbm = pl.BlockSpec(memory_space=pltpu.HBM)
    half_steps = n_steps // 2
    wr = D_MODEL // half_steps
    wdr = D_FF // half_steps
    assert 2 * half_steps == n_steps and wr * half_steps == D_MODEL and wdr * half_steps == D_FF
    assert wr % 16 == 0 and wdr % 16 == 0
    early_slice = lambda i: (jnp.minimum(i, half_steps - 1), 0)
    late_slice = lambda i: (jnp.maximum(i - half_steps, 0), 0)
    wg_rows = pl.BlockSpec((wr, D_FF), early_slice)
    wd_rows = pl.BlockSpec((wdr, D_MODEL), early_slice)
    wu_rows = pl.BlockSpec((wr, D_FF), late_slice)
    wo_rows = pl.BlockSpec((wr, D_MODEL), late_slice)
    qg = (jnp.tile(q_gain, N_HEADS) * (HEAD_DIM ** -0.5)).reshape(1, D_ATTN)
    kg = jnp.tile(k_gain, N_KV_HEADS).reshape(1, D_KV)
    est = (_ffn_vmem_bytes(tm) + 2 * D_MODEL * D_IN
           + 2 * tm * D_MODEL * 4 * 2
           + 2 * tm * D_IN * 2
           + 4 * tm * D_IN * 4
           + 2 * 6 * (2 * wr * D_FF + wdr * D_MODEL + wr * D_MODEL)
           + 3 * N_HEADS * BLOCK * 2 * BLOCK * 4)
    bf16 = lambda shape: jax.ShapeDtypeStruct(shape, jnp.bfloat16)
    return pl.pallas_call(
        _ffn_proj_kernel,
        out_shape=(
            jax.ShapeDtypeStruct((m, D_MODEL), jnp.float32),
            bf16((m, D_IN)),
            bf16((D_MODEL, D_FF)), bf16((D_MODEL, D_FF)), bf16((D_FF, D_MODEL)),
            bf16((D_MODEL, D_MODEL)),
            jax.ShapeDtypeStruct((N_HEADS, BLOCK, 2 * BLOCK), jnp.float32),
        ),
        grid=(n_steps,),
        in_specs=[
            rows(D_MODEL),
            resident((1, D_MODEL), lambda i: (0, 0)),
            hbm, hbm, hbm,
            resident((1, D_MODEL), lambda i: (0, 0)),
            hbm,
            resident((1, D_ATTN), lambda i: (0, 0)),
            resident((1, D_KV), lambda i: (0, 0)),
            resident((V7X_MXU_DIM, V7X_MXU_DIM), lambda i: (0, 0)),
            wg_rows, wu_rows, wd_rows, wo_rows,
            pl.BlockSpec(memory_space=pltpu.SMEM),
            resident((BLOCK, 2 * BLOCK), lambda i: (0, 0)),
        ],
        out_specs=(rows(D_MODEL), rows(D_IN), wg_rows, wu_rows, wd_rows, wo_rows,
                   pl.BlockSpec((N_HEADS, BLOCK, 2 * BLOCK), lambda i: (0, 0, 0))),
        scratch_shapes=(_ffn_base_scratch(tm) + _ffn_round_scratch()
                        + [pltpu.VMEM((D_MODEL, D_IN), jnp.bfloat16)]),
        compiler_params=pltpu.CompilerParams(
            dimension_semantics=("arbitrary",),
            vmem_limit_bytes=_vmem_limit(est)),
        name="ffn_proj",
    )(x2d, ffn_gain.reshape(1, D_MODEL), w_gate, w_up, w_down, mix_gain.reshape(1, D_MODEL),
      w_in, qg, kg, _head_mean_matrix(V7X_MXU_DIM),
      next_w_gate, next_w_up, next_w_down, w_out, rel_bias, jnp.asarray(_band_bucket_table()))


def _t5_bucket(dist):
    n = np.maximum(dist, 0)
    max_exact = N_BUCKETS // 2
    large = max_exact + (np.log(np.maximum(n, 1) / max_exact)
                         / np.log(MAX_DISTANCE / max_exact)
                         * (N_BUCKETS - max_exact)).astype(np.int32)
    large = np.minimum(large, N_BUCKETS - 1)
    return np.where(n < max_exact, n, large).astype(np.int32)


def _band_bucket_table():
    ql = np.arange(BLOCK)[:, None]
    kl = np.arange(2 * BLOCK)[None, :]
    dist = ql + BLOCK - kl
    in_band = (dist >= 0) & (dist < WINDOW)
    return np.where(in_band, _t5_bucket(dist), -1).astype(np.int32)


def _write_band_bias(rel_ref, bucket_ref, o_ref):
    bucket = bucket_ref[...]
    for slot, head in enumerate(HEADS_PLAIN + HEADS_ROLLED):
        acc = jnp.full((BLOCK, 2 * BLOCK), NEG, jnp.float32)
        for b in range(N_BUCKETS):
            acc = jnp.where(bucket == b, rel_ref[b, head], acc)
        o_ref[slot] = acc


def _window_sum_matrices():
    ql = np.arange(BLOCK)[:, None]
    kl = np.arange(2 * BLOCK)[None, :]
    dist = ql + BLOCK - kl
    mats = [((dist >= 0) & (dist < w)).astype(np.float32) for w in POOL_WINDOWS]
    return jnp.asarray(np.stack(mats), jnp.bfloat16)


def _mix_tile(seq_first, t_base, sink_ref, x_ref, z_ref, zh_ref,
              bias_ref, wsum_ref, wo_ref, x2_ref, mix_scratch, g_ref, h_ref):
    kx_ref, kxr_ref, vx_ref, vxr_ref, ux_ref, pooled_ref, y_ref, wc_ref = mix_scratch
    tq = z_ref.shape[1]
    k_cols = slice(D_ATTN, D_ATTN + D_KV)
    v_cols = slice(D_ATTN + D_KV, D_ATTN + 2 * D_KV)
    u_cols = slice(D_ATTN + 2 * D_KV, D_IN)
    n_blocks = tq // BLOCK
    half = HEAD_DIM
    groups = ((HEADS_PLAIN, kx_ref, vx_ref, 0), (HEADS_ROLLED, kxr_ref, vxr_ref, len(HEADS_PLAIN)))
    lane = lax.broadcasted_iota(jnp.int32, (BLOCK, 2 * half), 1)
    low = lane < half
    state = {}

    def rows_of(j):
        return slice(j * BLOCK, (j + 1) * BLOCK)

    def keys_of(j):
        return slice(j * BLOCK, (j + 2) * BLOCK)

    def build_slabs():
        halo = jnp.where(seq_first, jnp.zeros_like(zh_ref[0, :, D_ATTN:]), zh_ref[0, :, D_ATTN:])
        k_all = jnp.concatenate([halo[:, :D_KV], z_ref[0, :, k_cols]], axis=0).astype(jnp.float32)
        kx_ref[...] = k_all.T.astype(jnp.bfloat16)
        kxr_ref[...] = pltpu.roll(k_all, half, 1).T.astype(jnp.bfloat16)
        ones = jnp.ones((tq + BLOCK, D_KV), jnp.bfloat16)
        vx_ref[:BLOCK, :D_KV] = halo[:, D_KV:2 * D_KV]
        vx_ref[BLOCK:, :D_KV] = z_ref[0, :, v_cols]
        vx_ref[:, D_KV:] = ones
        vxr_ref[:, :D_KV] = pltpu.roll(
            vx_ref[:, :D_KV].astype(jnp.float32), half, 1).astype(jnp.bfloat16)
        vxr_ref[:, D_KV:] = ones
        ux_ref[:BLOCK] = halo[:, 2 * D_KV:]
        ux_ref[BLOCK:] = z_ref[0, :, u_cols]

    def scores(j, grp):
        heads, k_src, _, _ = groups[grp]
        q = z_ref[0, rows_of(j), :D_ATTN]
        zero = jnp.zeros((BLOCK, 2 * half), q.dtype)
        pieces = []
        for hd in heads:
            pair = q[:, (hd // 2) * 2 * half:(hd // 2 + 1) * 2 * half]
            pieces.append(jnp.where(low if hd % 2 == 0 else ~low, pair, zero))
        qs = jnp.concatenate(pieces, axis=0)
        state["logits", j, grp] = jnp.dot(
            qs, k_src[:, keys_of(j)], preferred_element_type=jnp.float32)

    def softmax_numerators(j, grp):
        heads, _, _, slot0 = groups[grp]
        logits = state.pop(("logits", j, grp))
        es, sink_terms = [], []
        for s, hd in enumerate(heads):
            lg = logits[s * BLOCK:(s + 1) * BLOCK] + bias_ref[slot0 + s]
            if j == 0:
                col = lax.broadcasted_iota(jnp.int32, (1, 2 * BLOCK), 1)
                lg = lg + jnp.where(seq_first & (col < BLOCK), NEG, 0.0).astype(jnp.float32)
            sink = sink_ref[hd]
            m = jnp.maximum(jnp.max(lg, axis=-1, keepdims=True), sink)
            es.append(jnp.exp(lg - m).astype(jnp.bfloat16))
            sink_terms.append(jnp.exp(sink - m))
        state["e", j, grp] = (jnp.concatenate(es, axis=0), sink_terms)

    def attend(j, grp):
        heads, _, v_src, _ = groups[grp]
        e, sink_terms = state.pop(("e", j, grp))
        pv = jnp.dot(e, v_src[keys_of(j), :], preferred_element_type=jnp.float32)
        for s, hd in enumerate(heads):
            blk = pv[s * BLOCK:(s + 1) * BLOCK]
            state["out", j, hd] = blk[:, :D_KV] / (blk[:, D_KV:] + sink_terms[s])

    def store_attention(j):
        for p in range(N_HEADS // 2):
            y_ref[rows_of(j), p * 2 * half:(p + 1) * 2 * half] = jnp.where(
                low, state.pop(("out", j, 2 * p)), state.pop(("out", j, 2 * p + 1))
            ).astype(jnp.bfloat16)

    def pool_means(j):
        row = lax.broadcasted_iota(jnp.int32, (BLOCK, 1), 0)
        t_glob = t_base + j * BLOCK + row
        for g, w in enumerate(POOL_WINDOWS):
            gc = slice(g * POOL_GROUP_DIM, (g + 1) * POOL_GROUP_DIM)
            wsum = jnp.dot(wsum_ref[g], ux_ref[keys_of(j), gc], preferred_element_type=jnp.float32)
            cnt = jnp.minimum(t_glob + 1, w).astype(jnp.float32)
            token = z_ref[0, rows_of(j), D_ATTN + 2 * D_KV + g * POOL_GROUP_DIM:
                          D_ATTN + 2 * D_KV + (g + 1) * POOL_GROUP_DIM]
            pooled = wsum * (1.0 / cnt) - token.astype(jnp.float32)
            pooled_ref[rows_of(j), gc] = pooled.astype(jnp.bfloat16)

    def project_pool():
        x2_ref[...] = x_ref[0] + jnp.dot(pooled_ref[...], wc_ref[...],
                                         preferred_element_type=jnp.float32)

    def project_attention():
        x2 = x2_ref[...] + jnp.dot(y_ref[...], wo_ref[:D_ATTN, :], preferred_element_type=jnp.float32)
        x2_ref[...] = x2
        h_ref[...] = _rmsnorm_rows(x2, g_ref[...]).astype(jnp.bfloat16)

    build_slabs()
    for j in range(n_blocks):
        pool_means(j)
    scores(0, 0)
    for j in range(n_blocks):
        scores(j, 1)
        softmax_numerators(j, 0)
        attend(j, 0)
        softmax_numerators(j, 1)
        if j == 0:
            project_pool()
        if j + 1 < n_blocks:
            scores(j + 1, 0)
        attend(j, 1)
        store_attention(j)
    project_attention()


N_MIX_SCRATCH = 9


def _mix_ffn_kernel(sink_ref, x_ref, z_ref, zh_ref,
                    bias_ref, wsum_ref, pw_ref, ps_ref, wo_ref, g2_ref, wg_hbm, wu_hbm, wd_hbm,
                    o_ref, *scratch, tiles_per_seq):
    mix_scratch, x2_ref = scratch[:N_MIX_SCRATCH - 1], scratch[N_MIX_SCRATCH - 1]
    base_scratch = scratch[N_MIX_SCRATCH:N_MIX_SCRATCH + N_FFN_BASE_SCRATCH]
    copy_scratch = scratch[N_MIX_SCRATCH + N_FFN_BASE_SCRATCH:]
    h_ref = base_scratch[3]
    t = pl.program_id(0)
    tq = z_ref.shape[1]
    i_seq = lax.rem(t, tiles_per_seq)
    w_hbm = (wg_hbm, wu_hbm, wd_hbm)

    def step(first):
        if first:
            _ffn_start_weight_copies(w_hbm, base_scratch, copy_scratch)
            for g in range(len(POOL_WINDOWS)):
                gc = slice(g * POOL_GROUP_DIM, (g + 1) * POOL_GROUP_DIM)
                pool_map = (pw_ref[g] * ps_ref[:, gc]).astype(jnp.bfloat16)
                mix_scratch[-1][gc, :] = jnp.dot(
                    pool_map, wo_ref[D_ATTN + g * POOL_GROUP_DIM:D_ATTN + (g + 1) * POOL_GROUP_DIM, :],
                    preferred_element_type=jnp.float32).astype(jnp.bfloat16)
        _mix_tile(i_seq == 0, i_seq * tq, sink_ref, x_ref, z_ref, zh_ref,
                  bias_ref, wsum_ref, wo_ref, x2_ref, mix_scratch, g2_ref, h_ref)
        y = _swiglu_tile(None, g2_ref, w_hbm, base_scratch, "wait" if first else None, copy_scratch)
        o_ref[0] = x2_ref[...] + 0.5 * y

    pl.when(t == 0)(functools.partial(step, True))
    pl.when(t > 0)(functools.partial(step, False))


def _mix_ffn(x3d, z3d, sinks, bias, pool_w, pool_scale, w_out_bf16, ffn_gain,
             w_gate_bf16, w_up_bf16, w_down_bf16):
    b, s, _ = x3d.shape
    tq = TILE_M
    nblk = tq // BLOCK
    tiles_per_seq = s // tq
    n_tiles = b * tiles_per_seq
    resident = functools.partial(pl.BlockSpec, pipeline_mode=pl.Buffered(1))
    hbm = pl.BlockSpec(memory_space=pltpu.HBM)

    def tile(width):
        return pl.BlockSpec((1, tq, width), lambda t: (t // tiles_per_seq, lax.rem(t, tiles_per_seq), 0))

    def halo(width):
        return pl.BlockSpec(
            (1, BLOCK, width),
            lambda t: (t // tiles_per_seq, jnp.maximum(lax.rem(t, tiles_per_seq) * nblk - 1, 0), 0))

    est = (_ffn_vmem_bytes(tq)
           + 2 * 2 * tq * D_MODEL * 4 + 2 * (tq + BLOCK) * D_IN * 2
           + N_HEADS * BLOCK * 2 * BLOCK * 4 + 2 * D_MODEL * D_MODEL + 4 * D_POOL * D_POOL
           + 2 * D_POOL * D_MODEL
           + 6 * (tq + BLOCK) * D_KV * 2 + (tq + BLOCK) * D_POOL * 2 + tq * D_MODEL * 2
           + tq * D_MODEL * 4 + 16 * BLOCK * 2 * BLOCK * 4 * 4)
    return pl.pallas_call(
        functools.partial(_mix_ffn_kernel, tiles_per_seq=tiles_per_seq),
        out_shape=jax.ShapeDtypeStruct((b, s, D_MODEL), jnp.float32),
        grid=(n_tiles,),
        in_specs=[
            pl.BlockSpec(memory_space=pltpu.SMEM),
            tile(D_MODEL),
            tile(D_IN), halo(D_IN),
            resident((N_HEADS, BLOCK, 2 * BLOCK), lambda t: (0, 0, 0)),
            resident((len(POOL_WINDOWS), BLOCK, 2 * BLOCK), lambda t: (0, 0, 0)),
            resident((len(POOL_WINDOWS), POOL_GROUP_DIM, POOL_GROUP_DIM), lambda t: (0, 0, 0)),
            resident((1, D_POOL), lambda t: (0, 0)),
            resident((D_MODEL, D_MODEL), lambda t: (0, 0)),
            resident((1, D_MODEL), lambda t: (0, 0)),
            hbm, hbm, hbm,
        ],
        out_specs=tile(D_MODEL),
        scratch_shapes=[
            pltpu.VMEM((D_KV, tq + BLOCK), jnp.bfloat16),
            pltpu.VMEM((D_KV, tq + BLOCK), jnp.bfloat16),
            pltpu.VMEM((tq + BLOCK, 2 * D_KV), jnp.bfloat16),
            pltpu.VMEM((tq + BLOCK, 2 * D_KV), jnp.bfloat16),
            pltpu.VMEM((tq + BLOCK, D_POOL), jnp.bfloat16),
            pltpu.VMEM((tq, D_POOL), jnp.bfloat16),
            pltpu.VMEM((tq, D_ATTN), jnp.bfloat16),
            pltpu.VMEM((D_POOL, D_MODEL), jnp.bfloat16),
            pltpu.VMEM((tq, D_MODEL), jnp.float32),
        ] + _ffn_base_scratch(tq) + _ffn_copy_scratch(),
        compiler_params=pltpu.CompilerParams(
            dimension_semantics=("arbitrary",),
            vmem_limit_bytes=_vmem_limit(est)),
        name="mix_ffn",
    )(sinks, x3d, z3d, z3d, bias, _window_sum_matrices(),
      pool_w, pool_scale.reshape(1, D_POOL),
      w_out_bf16, ffn_gain.reshape(1, D_MODEL), w_gate_bf16, w_up_bf16, w_down_bf16)


def kernel(x, ffn1_norm, ffn1_w_gate, ffn1_w_up, ffn1_w_down, mix_norm, w_in, q_norm, k_norm,
           attn_sinks, rel_bias, pool_w, pool_scale, w_out, ffn2_norm, ffn2_w_gate, ffn2_w_up,
           ffn2_w_down):
    b, s, d = x.shape
    assert (d, s % TILE_M) == (D_MODEL, 0)
    for l in range(ffn1_norm.shape[0]):
        x1, z, wg2, wu2, wd2, wo, bias = _ffn_proj(
            x.reshape(b * s, d), ffn1_norm[l], ffn1_w_gate[l], ffn1_w_up[l], ffn1_w_down[l],
            mix_norm[l], w_in[l], q_norm[l], k_norm[l],
            ffn2_w_gate[l], ffn2_w_up[l], ffn2_w_down[l], w_out[l], rel_bias)
        x = _mix_ffn(x1.reshape(b, s, d), z.reshape(b, s, D_IN), attn_sinks[l], bias,
                     pool_w[l], pool_scale[l], wo, ffn2_norm[l], wg2, wu2, wd2)
    return x
```

```python
import functools

import numpy as np
import jax
import jax.numpy as jnp
from jax import lax
from jax.experimental import pallas as pl
from jax.experimental.pallas import tpu as pltpu

D_MODEL = 1024
HEAD_DIM = 64
N_HEADS = 8
N_KV_HEADS = 2
D_ATTN = N_HEADS * HEAD_DIM
D_KV = N_KV_HEADS * HEAD_DIM
D_POOL = D_MODEL - D_ATTN
POOL_WINDOWS = (2, 4, 8, 16)
POOL_GROUP_DIM = D_POOL // len(POOL_WINDOWS)
D_IN = D_ATTN + 2 * D_KV + D_POOL
WINDOW = 128
BLOCK = 128
N_BUCKETS = 32
MAX_DISTANCE = 128
D_FF = 2816
EPS = 1e-6
NEG = -1e30

V7X_MXU_DIM = 256
V7X_VMEM_BYTES = 64 * 1024 * 1024

HEADS_PLAIN = (0, 2, 5, 7)
HEADS_ROLLED = (1, 3, 4, 6)

TILE_M = 512
FFN_CHUNK_F = 256
FFN_STAGE_SLOTS = 2


VMEM_COMPILER_TEMPORARIES_BYTES = 8 << 20
VMEM_LIMIT_FLOOR_BYTES = 16 << 20
VMEM_LIMIT_CEILING_BYTES = V7X_VMEM_BYTES - (4 << 20)


def _vmem_limit(buffer_bytes):
    wanted = buffer_bytes + VMEM_COMPILER_TEMPORARIES_BYTES
    return int(min(VMEM_LIMIT_CEILING_BYTES, max(wanted, VMEM_LIMIT_FLOOR_BYTES)))


def _rmsnorm_rows(x32, gain_row):
    ms = jnp.mean(x32 * x32, axis=-1, keepdims=True)
    return x32 * lax.rsqrt(ms + EPS) * gain_row


N_FFN_CHUNKS = D_FF // FFN_CHUNK_F
N_FFN_BASE_SCRATCH = 5


def _ffn_base_scratch(tm):
    return [
        pltpu.VMEM((D_MODEL, D_FF), jnp.bfloat16),
        pltpu.VMEM((D_MODEL, D_FF), jnp.bfloat16),
        pltpu.VMEM((D_FF, D_MODEL), jnp.bfloat16),
        pltpu.VMEM((tm, D_MODEL), jnp.bfloat16),
        pltpu.VMEM((tm, D_FF), jnp.bfloat16),
    ]


FFN_STREAM_BOUNDS = (0, 1024, 2048, D_FF)
FFN_STREAM_ORDER = tuple((which, k) for k in range(len(FFN_STREAM_BOUNDS) - 1) for which in (0, 1)) + tuple(
    (2, k) for k in range(len(FFN_STREAM_BOUNDS) - 1))
FFN_STAGE_DIM = max(hi - lo for lo, hi in zip(FFN_STREAM_BOUNDS, FFN_STREAM_BOUNDS[1:]))


def _ffn_round_scratch():
    assert D_MODEL <= FFN_STAGE_DIM and all(b % FFN_CHUNK_F == 0 for b in FFN_STREAM_BOUNDS)
    return [
        pltpu.VMEM((FFN_STAGE_SLOTS, FFN_STAGE_DIM, FFN_STAGE_DIM), jnp.float32),
        pltpu.SemaphoreType.DMA((FFN_STAGE_SLOTS,)),
    ]


def _ffn_copy_scratch():
    return [pltpu.SemaphoreType.DMA((3,))]


def _ffn_vmem_bytes(tm):
    return (2 * 3 * D_MODEL * D_FF + FFN_STAGE_SLOTS * FFN_STAGE_DIM * FFN_STAGE_DIM * 4
            + tm * D_MODEL * 2 + tm * D_FF * 2 + 6 * tm * FFN_CHUNK_F * 4 + tm * D_MODEL * 4)


def _ffn_weight_copy(which, w_hbm, base_scratch, copy_scratch):
    (sem_ref,) = copy_scratch
    return pltpu.make_async_copy(w_hbm[which], base_scratch[which], sem_ref.at[which])


def _ffn_start_weight_copies(w_hbm, base_scratch, copy_scratch):
    for which in range(3):
        _ffn_weight_copy(which, w_hbm, base_scratch, copy_scratch).start()


def _swiglu_tile(x32, g_ref, w_hbm, base_scratch, stream, stream_scratch, on_streamed=None):
    wg_ref, wu_ref, wd_ref, h_ref, a_ref = base_scratch

    def piece_copy(i):
        stage_ref, sem_ref = stream_scratch
        which, k = FFN_STREAM_ORDER[i]
        lo, size = FFN_STREAM_BOUNDS[k], FFN_STREAM_BOUNDS[k + 1] - FFN_STREAM_BOUNDS[k]
        slot = i % FFN_STAGE_SLOTS
        if which == 2:
            return pltpu.make_async_copy(w_hbm[2].at[pl.ds(lo, size), :],
                                         stage_ref.at[slot, pl.ds(0, size), pl.ds(0, D_MODEL)],
                                         sem_ref.at[slot])
        return pltpu.make_async_copy(w_hbm[which].at[:, pl.ds(lo, size)],
                                     stage_ref.at[slot, pl.ds(0, D_MODEL), pl.ds(0, size)],
                                     sem_ref.at[slot])

    def land_piece(i):
        stage_ref, _ = stream_scratch
        which, k = FFN_STREAM_ORDER[i]
        lo, hi = FFN_STREAM_BOUNDS[k], FFN_STREAM_BOUNDS[k + 1]
        slot = i % FFN_STAGE_SLOTS
        piece_copy(i).wait()
        if which == 2:
            wd_ref[lo:hi, :] = stage_ref[slot, :hi - lo, :D_MODEL].astype(jnp.bfloat16)
        else:
            dst = (wg_ref, wu_ref)[which]
            dst[:, lo:hi] = stage_ref[slot, :D_MODEL, :hi - lo].astype(jnp.bfloat16)
        if i + FFN_STAGE_SLOTS < len(FFN_STREAM_ORDER):
            piece_copy(i + FFN_STAGE_SLOTS).start()

    if stream == "round":
        for i in range(FFN_STAGE_SLOTS):
            piece_copy(i).start()
    if x32 is not None:
        h_ref[...] = _rmsnorm_rows(x32, g_ref[...]).astype(jnp.bfloat16)
    for c in range(N_FFN_CHUNKS):
        cols = slice(c * FFN_CHUNK_F, (c + 1) * FFN_CHUNK_F)
        if stream == "round":
            for i, (which, k) in enumerate(FFN_STREAM_ORDER):
                if which != 2 and FFN_STREAM_BOUNDS[k] == c * FFN_CHUNK_F:
                    land_piece(i)
        elif stream == "wait" and c == 0:
            for which in range(2):
                _ffn_weight_copy(which, w_hbm, base_scratch, stream_scratch).wait()
        h = h_ref[...]
        gate = jnp.dot(h, wg_ref[:, cols], preferred_element_type=jnp.float32)
        up = jnp.dot(h, wu_ref[:, cols], preferred_element_type=jnp.float32)
        act = gate * (1.0 / (1.0 + jnp.exp(-gate))) * up
        a_ref[:, cols] = act.astype(jnp.bfloat16)
    if stream == "round":
        for i, (which, _) in enumerate(FFN_STREAM_ORDER):
            if which == 2:
                land_piece(i)
        if on_streamed is not None:
            on_streamed()
    elif stream == "wait":
        _ffn_weight_copy(2, w_hbm, base_scratch, stream_scratch).wait()
    return jnp.dot(a_ref[...], wd_ref[...], preferred_element_type=jnp.float32)


def _ffn_proj_kernel(x_ref, g1_ref, wg_hbm, wu_hbm, wd_hbm, gm_ref, win_hbm, qg_ref, kg_ref, dm_ref,
                     wg2_ref, wu2_ref, wd2_ref, wo_ref, rel_ref, bucket_ref,
                     x1_ref, z_ref, wg2_bf_ref, wu2_bf_ref, wd2_bf_ref, wo_bf_ref, bias_ref,
                     *scratch):
    tm = x_ref.shape[0]
    base_scratch = scratch[:N_FFN_BASE_SCRATCH]
    round_scratch, win_ref = scratch[N_FFN_BASE_SCRATCH:-1], scratch[-1]

    win_pieces = tuple((lo, hi - lo, slot) for slot, (lo, hi) in
                       enumerate(((0, FFN_STAGE_DIM), (FFN_STAGE_DIM, D_IN))))

    def win_copy(lo, size, slot):
        stage_ref, sem_ref = round_scratch
        return pltpu.make_async_copy(win_hbm.at[:, pl.ds(lo, size)],
                                     stage_ref.at[slot, pl.ds(0, D_MODEL), pl.ds(0, size)],
                                     sem_ref.at[slot])

    def start_win_copies():
        for piece in win_pieces:
            win_copy(*piece).start()

    def land_win():
        stage_ref, _ = round_scratch
        for lo, size, slot in win_pieces:
            win_copy(lo, size, slot).wait()
            win_ref[:, lo:lo + size] = stage_ref[slot, :D_MODEL, :size].astype(jnp.bfloat16)

    first_half = pl.program_id(0) < pl.num_programs(0) // 2

    @pl.when(first_half)
    def _():
        wg2_bf_ref[...] = wg2_ref[...].astype(jnp.bfloat16)
        wd2_bf_ref[...] = wd2_ref[...].astype(jnp.bfloat16)

    @pl.when(jnp.logical_not(first_half))
    def _():
        wu2_bf_ref[...] = wu2_ref[...].astype(jnp.bfloat16)
        wo_bf_ref[...] = wo_ref[...].astype(jnp.bfloat16)

    def body(stream):
        if stream is not None:
            _write_band_bias(rel_ref, bucket_ref, bias_ref)
        y = _swiglu_tile(x_ref[...], g1_ref, (wg_hbm, wu_hbm, wd_hbm), base_scratch, stream,
                         round_scratch, on_streamed=start_win_copies)
        x1 = x_ref[...] + 0.5 * y
        x1_ref[...] = x1
        if stream is not None:
            land_win()
        h = _rmsnorm_rows(x1, gm_ref[...]).astype(jnp.bfloat16)
        z = jnp.dot(h, win_ref[...], preferred_element_type=jnp.float32)
        n_slab = (D_ATTN + 2 * D_KV) // V7X_MXU_DIM
        sq = jnp.concatenate(
            [z[:, i * V7X_MXU_DIM:(i + 1) * V7X_MXU_DIM] for i in range(n_slab)], axis=0)
        ms = jnp.dot((sq * sq).astype(jnp.bfloat16), dm_ref[...], preferred_element_type=jnp.float32)
        q_ms = jnp.concatenate([ms[i * tm:(i + 1) * tm] for i in range(n_slab - 1)], axis=1)
        k_ms = ms[(n_slab - 1) * tm:, :D_KV]
        q = z[:, :D_ATTN]
        k = z[:, D_ATTN:D_ATTN + D_KV]
        z_ref[:, :D_ATTN] = (q * lax.rsqrt(q_ms + EPS) * qg_ref[...]).astype(jnp.bfloat16)
        z_ref[:, D_ATTN:D_ATTN + D_KV] = (k * lax.rsqrt(k_ms + EPS) * kg_ref[...]).astype(jnp.bfloat16)
        z_ref[:, D_ATTN + D_KV:] = z[:, D_ATTN + D_KV:].astype(jnp.bfloat16)

    first = pl.program_id(0) == 0
    pl.when(first)(functools.partial(body, "round"))
    pl.when(jnp.logical_not(first))(functools.partial(body, None))


def _head_mean_matrix(width):
    idx = np.arange(width) // HEAD_DIM
    return jnp.asarray((idx[:, None] == idx[None, :]).astype(np.float32) / HEAD_DIM, jnp.bfloat16)


def _ffn_proj(x2d, ffn_gain, w_gate, w_up, w_down, mix_gain, w_in, q_gain, k_gain,
              next_w_gate, next_w_up, next_w_down, w_out, rel_bias):
    m = x2d.shape[0]
    tm = TILE_M
    n_steps = m // tm
    resident = functools.partial(pl.BlockSpec, pipeline_mode=pl.Buffered(1))
    rows = lambda width: pl.BlockSpec((tm, width), lambda i: (i, 0))
    hbm = pl.BlockSpec(memory_space=pltpu.HBM)
    half_steps = n_steps // 2
    wr = D_MODEL // half_steps
    wdr = D_FF // half_steps
    assert 2 * half_steps == n_steps and wr * half_steps == D_MODEL and wdr * half_steps == D_FF
    assert wr % 16 == 0 and wdr % 16 == 0
    early_slice = lambda i: (jnp.minimum(i, half_steps - 1), 0)
    late_slice = lambda i: (jnp.maximum(i - half_steps, 0), 0)
    wg_rows = pl.BlockSpec((wr, D_FF), early_slice)
    wd_rows = pl.BlockSpec((wdr, D_MODEL), early_slice)
    wu_rows = pl.BlockSpec((wr, D_FF), late_slice)
    wo_rows = pl.BlockSpec((wr, D_MODEL), late_slice)
    qg = (jnp.tile(q_gain, N_HEADS) * (HEAD_DIM ** -0.5)).reshape(1, D_ATTN)
    kg = jnp.tile(k_gain, N_KV_HEADS).reshape(1, D_KV)
    est = (_ffn_vmem_bytes(tm) + 2 * D_MODEL * D_IN
           + 2 * tm * D_MODEL * 4 * 2
           + 2 * tm * D_IN * 2
           + 4 * tm * D_IN * 4
           + 2 * 6 * (2 * wr * D_FF + wdr * D_MODEL + wr * D_MODEL)
           + 3 * N_HEADS * BLOCK * 2 * BLOCK * 4)
    bf16 = lambda shape: jax.ShapeDtypeStruct(shape, jnp.bfloat16)
    return pl.pallas_call(
        _ffn_proj_kernel,
        out_shape=(
            jax.ShapeDtypeStruct((m, D_MODEL), jnp.float32),
            bf16((m, D_IN)),
            bf16((D_MODEL, D_FF)), bf16((D_MODEL, D_FF)), bf16((D_FF, D_MODEL)),
            bf16((D_MODEL, D_MODEL)),
            jax.ShapeDtypeStruct((N_HEADS, BLOCK, 2 * BLOCK), jnp.float32),
        ),
        grid=(n_steps,),
        in_specs=[
            rows(D_MODEL),
            resident((1, D_MODEL), lambda i: (0, 0)),
            hbm, hbm, hbm,
            resident((1, D_MODEL), lambda i: (0, 0)),
            hbm,
            resident((1, D_ATTN), lambda i: (0, 0)),
            resident((1, D_KV), lambda i: (0, 0)),
            resident((V7X_MXU_DIM, V7X_MXU_DIM), lambda i: (0, 0)),
            wg_rows, wu_rows, wd_rows, wo_rows,
            pl.BlockSpec(memory_space=pltpu.SMEM),
            resident((BLOCK, 2 * BLOCK), lambda i: (0, 0)),
        ],
        out_specs=(rows(D_MODEL), rows(D_IN), wg_rows, wu_rows, wd_rows, wo_rows,
                   pl.BlockSpec((N_HEADS, BLOCK, 2 * BLOCK), lambda i: (0, 0, 0))),
        scratch_shapes=(_ffn_base_scratch(tm) + _ffn_round_scratch()
                        + [pltpu.VMEM((D_MODEL, D_IN), jnp.bfloat16)]),
        compiler_params=pltpu.CompilerParams(
            dimension_semantics=("arbitrary",),
            vmem_limit_bytes=_vmem_limit(est)),
        name="ffn_proj",
    )(x2d, ffn_gain.reshape(1, D_MODEL), w_gate, w_up, w_down, mix_gain.reshape(1, D_MODEL),
      w_in, qg, kg, _head_mean_matrix(V7X_MXU_DIM),
      next_w_gate, next_w_up, next_w_down, w_out, rel_bias, jnp.asarray(_band_bucket_table()))


def _t5_bucket(dist):
    n = np.maximum(dist, 0)
    max_exact = N_BUCKETS // 2
    large = max_exact + (np.log(np.maximum(n, 1) / max_exact)
                         / np.log(MAX_DISTANCE / max_exact)
                         * (N_BUCKETS - max_exact)).astype(np.int32)
    large = np.minimum(large, N_BUCKETS - 1)
    return np.where(n < max_exact, n, large).astype(np.int32)


def _band_bucket_table():
    ql = np.arange(BLOCK)[:, None]
    kl = np.arange(2 * BLOCK)[None, :]
    dist = ql + BLOCK - kl
    in_band = (dist >= 0) & (dist < WINDOW)
    return np.where(in_band, _t5_bucket(dist), -1).astype(np.int32)


def _write_band_bias(rel_ref, bucket_ref, o_ref):
    bucket = bucket_ref[...]
    for slot, head in enumerate(HEADS_PLAIN + HEADS_ROLLED):
        acc = jnp.full((BLOCK, 2 * BLOCK), NEG, jnp.float32)
        for b in range(N_BUCKETS):
            acc = jnp.where(bucket == b, rel_ref[b, head], acc)
        o_ref[slot] = acc


def _window_sum_matrices():
    ql = np.arange(BLOCK)[:, None]
    kl = np.arange(2 * BLOCK)[None, :]
    dist = ql + BLOCK - kl
    mats = [((dist >= 0) & (dist < w)).astype(np.float32) for w in POOL_WINDOWS]
    return jnp.asarray(np.stack(mats), jnp.bfloat16)


def _mix_tile(seq_first, t_base, sink_ref, x_ref, z_ref, zh_ref,
              bias_ref, wsum_ref, wo_ref, x2_ref, mix_scratch, g_ref, h_ref):
    kx_ref, kxr_ref, vx_ref, vxr_ref, ux_ref, pooled_ref, y_ref, wc_ref = mix_scratch
    tq = z_ref.shape[1]
    k_cols = slice(D_ATTN, D_ATTN + D_KV)
    v_cols = slice(D_ATTN + D_KV, D_ATTN + 2 * D_KV)
    u_cols = slice(D_ATTN + 2 * D_KV, D_IN)
    n_blocks = tq // BLOCK
    half = HEAD_DIM
    groups = ((HEADS_PLAIN, kx_ref, vx_ref, 0), (HEADS_ROLLED, kxr_ref, vxr_ref, len(HEADS_PLAIN)))
    lane = lax.broadcasted_iota(jnp.int32, (BLOCK, 2 * half), 1)
    low = lane < half
    state = {}

    def rows_of(j):
        return slice(j * BLOCK, (j + 1) * BLOCK)

    def keys_of(j):
        return slice(j * BLOCK, (j + 2) * BLOCK)

    def build_slabs():
        halo = jnp.where(seq_first, jnp.zeros_like(zh_ref[0, :, D_ATTN:]), zh_ref[0, :, D_ATTN:])
        kx_ref[:BLOCK] = halo[:, :D_KV]
        kx_ref[BLOCK:] = z_ref[0, :, k_cols]
        kxr_ref[...] = pltpu.roll(kx_ref[...].astype(jnp.float32), half, 1).astype(jnp.bfloat16)
        ones = jnp.ones((tq + BLOCK, D_KV), jnp.bfloat16)
        vx_ref[:BLOCK, :D_KV] = halo[:, D_KV:2 * D_KV]
        vx_ref[BLOCK:, :D_KV] = z_ref[0, :, v_cols]
        vx_ref[:, D_KV:] = ones
        vxr_ref[:, :D_KV] = pltpu.roll(
            vx_ref[:, :D_KV].astype(jnp.float32), half, 1).astype(jnp.bfloat16)
        vxr_ref[:, D_KV:] = ones
        ux_ref[:BLOCK] = halo[:, 2 * D_KV:]
        ux_ref[BLOCK:] = z_ref[0, :, u_cols]

    def scores(j, grp):
        heads, k_src, _, _ = groups[grp]
        q = z_ref[0, rows_of(j), :D_ATTN]
        zero = jnp.zeros((BLOCK, 2 * half), q.dtype)
        pieces = []
        for hd in heads:
            pair = q[:, (hd // 2) * 2 * half:(hd // 2 + 1) * 2 * half]
            pieces.append(jnp.where(low if hd % 2 == 0 else ~low, pair, zero))
        qs = jnp.concatenate(pieces, axis=0)
        state["logits", j, grp] = lax.dot_general(
            qs, k_src[keys_of(j), :], (((1,), (1,)), ((), ())),
            preferred_element_type=jnp.float32)

    def softmax_numerators(j, grp):
        heads, _, _, slot0 = groups[grp]
        logits = state.pop(("logits", j, grp))
        es, sink_terms = [], []
        for s, hd in enumerate(heads):
            lg = logits[s * BLOCK:(s + 1) * BLOCK] + bias_ref[slot0 + s]
            if j == 0:
                col = lax.broadcasted_iota(jnp.int32, (1, 2 * BLOCK), 1)
                lg = lg + jnp.where(seq_first & (col < BLOCK), NEG, 0.0).astype(jnp.float32)
            sink = sink_ref[hd]
            m = jnp.maximum(jnp.max(lg, axis=-1, keepdims=True), sink)
            es.append(jnp.exp(lg - m).astype(jnp.bfloat16))
            sink_terms.append(jnp.exp(sink - m))
        state["e", j, grp] = (jnp.concatenate(es, axis=0), sink_terms)

    def attend(j, grp):
        heads, _, v_src, _ = groups[grp]
        e, sink_terms = state.pop(("e", j, grp))
        pv = jnp.dot(e, v_src[keys_of(j), :], preferred_element_type=jnp.float32)
        for s, hd in enumerate(heads):
            blk = pv[s * BLOCK:(s + 1) * BLOCK]
            state["out", j, hd] = blk[:, :D_KV] / (blk[:, D_KV:] + sink_terms[s])

    def store_attention(j):
        for p in range(N_HEADS // 2):
            y_ref[rows_of(j), p * 2 * half:(p + 1) * 2 * half] = jnp.where(
                low, state.pop(("out", j, 2 * p)), state.pop(("out", j, 2 * p + 1))
            ).astype(jnp.bfloat16)

    def pool_means(g):
        w = POOL_WINDOWS[g]
        gc = slice(g * POOL_GROUP_DIM, (g + 1) * POOL_GROUP_DIM)
        windows = jnp.concatenate([ux_ref[keys_of(j), gc] for j in range(n_blocks)], axis=1)
        wsum = jnp.dot(wsum_ref[g], windows, preferred_element_type=jnp.float32)
        row = lax.broadcasted_iota(jnp.int32, (BLOCK, 1), 0)
        for j in range(n_blocks):
            cnt = jnp.minimum(t_base + j * BLOCK + row + 1, w).astype(jnp.float32)
            token = z_ref[0, rows_of(j), D_ATTN + 2 * D_KV + g * POOL_GROUP_DIM:
                          D_ATTN + 2 * D_KV + (g + 1) * POOL_GROUP_DIM]
            pooled = wsum[:, j * BLOCK:(j + 1) * BLOCK] * (1.0 / cnt) - token.astype(jnp.float32)
            pooled_ref[rows_of(j), gc] = pooled.astype(jnp.bfloat16)

    def project_pool():
        x2_ref[...] = x_ref[0] + jnp.dot(pooled_ref[...], wc_ref[...],
                                         preferred_element_type=jnp.float32)

    def project_attention():
        x2 = x2_ref[...] + jnp.dot(y_ref[...], wo_ref[:D_ATTN, :], preferred_element_type=jnp.float32)
        x2_ref[...] = x2
        h_ref[...] = _rmsnorm_rows(x2, g_ref[...]).astype(jnp.bfloat16)

    build_slabs()
    for g in range(len(POOL_WINDOWS)):
        pool_means(g)
    scores(0, 0)
    for j in range(n_blocks):
        scores(j, 1)
        softmax_numerators(j, 0)
        attend(j, 0)
        softmax_numerators(j, 1)
        if j == 0:
            project_pool()
        if j + 1 < n_blocks:
            scores(j + 1, 0)
        attend(j, 1)
        store_attention(j)
    project_attention()


N_MIX_SCRATCH = 9


def _mix_ffn_kernel(sink_ref, x_ref, z_ref, zh_ref,
                    bias_ref, wsum_ref, pw_ref, ps_ref, wo_ref, g2_ref, wg_hbm, wu_hbm, wd_hbm,
                    o_ref, *scratch, tiles_per_seq):
    mix_scratch, x2_ref = scratch[:N_MIX_SCRATCH - 1], scratch[N_MIX_SCRATCH - 1]
    base_scratch = scratch[N_MIX_SCRATCH:N_MIX_SCRATCH + N_FFN_BASE_SCRATCH]
    copy_scratch = scratch[N_MIX_SCRATCH + N_FFN_BASE_SCRATCH:]
    h_ref = base_scratch[3]
    t = pl.program_id(0)
    tq = z_ref.shape[1]
    i_seq = lax.rem(t, tiles_per_seq)
    w_hbm = (wg_hbm, wu_hbm, wd_hbm)

    def step(first):
        if first:
            _ffn_start_weight_copies(w_hbm, base_scratch, copy_scratch)
            for g in range(len(POOL_WINDOWS)):
                gc = slice(g * POOL_GROUP_DIM, (g + 1) * POOL_GROUP_DIM)
                pool_map = (pw_ref[g] * ps_ref[:, gc]).astype(jnp.bfloat16)
                mix_scratch[-1][gc, :] = jnp.dot(
                    pool_map, wo_ref[D_ATTN + g * POOL_GROUP_DIM:D_ATTN + (g + 1) * POOL_GROUP_DIM, :],
                    preferred_element_type=jnp.float32).astype(jnp.bfloat16)
        _mix_tile(i_seq == 0, i_seq * tq, sink_ref, x_ref, z_ref, zh_ref,
                  bias_ref, wsum_ref, wo_ref, x2_ref, mix_scratch, g2_ref, h_ref)
        y = _swiglu_tile(None, g2_ref, w_hbm, base_scratch, "wait" if first else None, copy_scratch)
        o_ref[0] = x2_ref[...] + 0.5 * y

    pl.when(t == 0)(functools.partial(step, True))
    pl.when(t > 0)(functools.partial(step, False))


def _mix_ffn(x3d, z3d, sinks, bias, pool_w, pool_scale, w_out_bf16, ffn_gain,
             w_gate_bf16, w_up_bf16, w_down_bf16):
    b, s, _ = x3d.shape
    tq = TILE_M
    nblk = tq // BLOCK
    tiles_per_seq = s // tq
    n_tiles = b * tiles_per_seq
    resident = functools.partial(pl.BlockSpec, pipeline_mode=pl.Buffered(1))
    hbm = pl.BlockSpec(memory_space=pltpu.HBM)

    def tile(width):
        return pl.BlockSpec((1, tq, width), lambda t: (t // tiles_per_seq, lax.rem(t, tiles_per_seq), 0))

    def halo(width):
        return pl.BlockSpec(
            (1, BLOCK, width),
            lambda t: (t // tiles_per_seq, jnp.maximum(lax.rem(t, tiles_per_seq) * nblk - 1, 0), 0))

    est = (_ffn_vmem_bytes(tq)
           + 2 * 2 * tq * D_MODEL * 4 + 2 * (tq + BLOCK) * D_IN * 2
           + N_HEADS * BLOCK * 2 * BLOCK * 4 + 2 * D_MODEL * D_MODEL + 4 * D_POOL * D_POOL
           + 2 * D_POOL * D_MODEL
           + 6 * (tq + BLOCK) * D_KV * 2 + (tq + BLOCK) * D_POOL * 2 + tq * D_MODEL * 2
           + tq * D_MODEL * 4 + 16 * BLOCK * 2 * BLOCK * 4 * 4)
    return pl.pallas_call(
        functools.partial(_mix_ffn_kernel, tiles_per_seq=tiles_per_seq),
        out_shape=jax.ShapeDtypeStruct((b, s, D_MODEL), jnp.float32),
        grid=(n_tiles,),
        in_specs=[
            pl.BlockSpec(memory_space=pltpu.SMEM),
            tile(D_MODEL),
            tile(D_IN), halo(D_IN),
            resident((N_HEADS, BLOCK, 2 * BLOCK), lambda t: (0, 0, 0)),
            resident((len(POOL_WINDOWS), BLOCK, 2 * BLOCK), lambda t: (0, 0, 0)),
            resident((len(POOL_WINDOWS), POOL_GROUP_DIM, POOL_GROUP_DIM), lambda t: (0, 0, 0)),
            resident((1, D_POOL), lambda t: (0, 0)),
            resident((D_MODEL, D_MODEL), lambda t: (0, 0)),
            resident((1, D_MODEL), lambda t: (0, 0)),
            hbm, hbm, hbm,
        ],
        out_specs=tile(D_MODEL),
        scratch_shapes=[
            pltpu.VMEM((tq + BLOCK, D_KV), jnp.bfloat16),
            pltpu.VMEM((tq + BLOCK, D_KV), jnp.bfloat16),
            pltpu.VMEM((tq + BLOCK, 2 * D_KV), jnp.bfloat16),
            pltpu.VMEM((tq + BLOCK, 2 * D_KV), jnp.bfloat16),
            pltpu.VMEM((tq + BLOCK, D_POOL), jnp.bfloat16),
            pltpu.VMEM((tq, D_POOL), jnp.bfloat16),
            pltpu.VMEM((tq, D_ATTN), jnp.bfloat16),
            pltpu.VMEM((D_POOL, D_MODEL), jnp.bfloat16),
            pltpu.VMEM((tq, D_MODEL), jnp.float32),
        ] + _ffn_base_scratch(tq) + _ffn_copy_scratch(),
        compiler_params=pltpu.CompilerParams(
            dimension_semantics=("arbitrary",),
            vmem_limit_bytes=_vmem_limit(est)),
        name="mix_ffn",
    )(sinks, x3d, z3d, z3d, bias, _window_sum_matrices(),
      pool_w, pool_scale.reshape(1, D_POOL),
      w_out_bf16, ffn_gain.reshape(1, D_MODEL), w_gate_bf16, w_up_bf16, w_down_bf16)


def kernel(x, ffn1_norm, ffn1_w_gate, ffn1_w_up, ffn1_w_down, mix_norm, w_in, q_norm, k_norm,
           attn_sinks, rel_bias, pool_w, pool_scale, w_out, ffn2_norm, ffn2_w_gate, ffn2_w_up,
           ffn2_w_down):
    b, s, d = x.shape
    assert (d, s % TILE_M) == (D_MODEL, 0)
    for l in range(ffn1_norm.shape[0]):
        x1, z, wg2, wu2, wd2, wo, bias = _ffn_proj(
            x.reshape(b * s, d), ffn1_norm[l], ffn1_w_gate[l], ffn1_w_up[l], ffn1_w_down[l],
            mix_norm[l], w_in[l], q_norm[l], k_norm[l],
            ffn2_w_gate[l], ffn2_w_up[l], ffn2_w_down[l], w_out[l], rel_bias)
        x = _mix_ffn(x1.reshape(b, s, d), z.reshape(b, s, D_IN), attn_sinks[l], bias,
                     pool_w[l], pool_scale[l], wo, ffn2_norm[l], wg2, wu2, wd2)
    return x
```

```python
import functools

import numpy as np
import jax
import jax.numpy as jnp
from jax import lax
from jax.experimental import pallas as pl
from jax.experimental.pallas import tpu as pltpu

D_MODEL = 1024
HEAD_DIM = 64
N_HEADS = 8
N_KV_HEADS = 2
D_ATTN = N_HEADS * HEAD_DIM
D_KV = N_KV_HEADS * HEAD_DIM
D_POOL = D_MODEL - D_ATTN
POOL_WINDOWS = (2, 4, 8, 16)
POOL_GROUP_DIM = D_POOL // len(POOL_WINDOWS)
D_IN = D_ATTN + 2 * D_KV + D_POOL
WINDOW = 128
BLOCK = 128
N_BUCKETS = 32
MAX_DISTANCE = 128
D_FF = 2816
EPS = 1e-6
NEG = -1e30

V7X_MXU_DIM = 256
V7X_VMEM_BYTES = 64 * 1024 * 1024

HEADS_PLAIN = (0, 2, 5, 7)
HEADS_ROLLED = (1, 3, 4, 6)

TILE_M = 512
FFN_CHUNK_F = 256
FFN_STAGE_SLOTS = 2


VMEM_COMPILER_TEMPORARIES_BYTES = 8 << 20
VMEM_LIMIT_FLOOR_BYTES = 16 << 20
VMEM_LIMIT_CEILING_BYTES = V7X_VMEM_BYTES - (4 << 20)


def _vmem_limit(buffer_bytes):
    wanted = buffer_bytes + VMEM_COMPILER_TEMPORARIES_BYTES
    return int(min(VMEM_LIMIT_CEILING_BYTES, max(wanted, VMEM_LIMIT_FLOOR_BYTES)))


def _rmsnorm_rows(x32, gain_row):
    ms = jnp.mean(x32 * x32, axis=-1, keepdims=True)
    return x32 * lax.rsqrt(ms + EPS) * gain_row


N_FFN_CHUNKS = D_FF // FFN_CHUNK_F
N_FFN_BASE_SCRATCH = 5


def _ffn_base_scratch(tm):
    return [
        pltpu.VMEM((D_MODEL, D_FF), jnp.bfloat16),
        pltpu.VMEM((D_MODEL, D_FF), jnp.bfloat16),
        pltpu.VMEM((D_FF, D_MODEL), jnp.bfloat16),
        pltpu.VMEM((tm, D_MODEL), jnp.bfloat16),
        pltpu.VMEM((tm, D_FF), jnp.bfloat16),
    ]


FFN_STREAM_BOUNDS = (0, 1024, 2048, D_FF)
FFN_STREAM_ORDER = tuple((which, k) for k in range(len(FFN_STREAM_BOUNDS) - 1) for which in (0, 1)) + tuple(
    (2, k) for k in range(len(FFN_STREAM_BOUNDS) - 1))
FFN_STAGE_DIM = max(hi - lo for lo, hi in zip(FFN_STREAM_BOUNDS, FFN_STREAM_BOUNDS[1:]))


def _ffn_round_scratch():
    assert D_MODEL <= FFN_STAGE_DIM and all(b % FFN_CHUNK_F == 0 for b in FFN_STREAM_BOUNDS)
    return [
        pltpu.VMEM((FFN_STAGE_SLOTS, FFN_STAGE_DIM, FFN_STAGE_DIM), jnp.float32),
        pltpu.SemaphoreType.DMA((FFN_STAGE_SLOTS,)),
    ]


def _ffn_copy_scratch():
    return [pltpu.SemaphoreType.DMA((3,))]


def _ffn_vmem_bytes(tm):
    return (2 * 3 * D_MODEL * D_FF + FFN_STAGE_SLOTS * FFN_STAGE_DIM * FFN_STAGE_DIM * 4
            + tm * D_MODEL * 2 + tm * D_FF * 2 + 6 * tm * FFN_CHUNK_F * 4 + tm * D_MODEL * 4)


def _ffn_weight_copy(which, w_hbm, base_scratch, copy_scratch):
    (sem_ref,) = copy_scratch
    return pltpu.make_async_copy(w_hbm[which], base_scratch[which], sem_ref.at[which])


def _ffn_start_weight_copies(w_hbm, base_scratch, copy_scratch):
    for which in range(3):
        _ffn_weight_copy(which, w_hbm, base_scratch, copy_scratch).start()


def _swiglu_tile(x32, g_ref, w_hbm, base_scratch, stream, stream_scratch, on_streamed=None):
    wg_ref, wu_ref, wd_ref, h_ref, a_ref = base_scratch

    def piece_copy(i):
        stage_ref, sem_ref = stream_scratch
        which, k = FFN_STREAM_ORDER[i]
        lo, size = FFN_STREAM_BOUNDS[k], FFN_STREAM_BOUNDS[k + 1] - FFN_STREAM_BOUNDS[k]
        slot = i % FFN_STAGE_SLOTS
        if which == 2:
            return pltpu.make_async_copy(w_hbm[2].at[pl.ds(lo, size), :],
                                         stage_ref.at[slot, pl.ds(0, size), pl.ds(0, D_MODEL)],
                                         sem_ref.at[slot])
        return pltpu.make_async_copy(w_hbm[which].at[:, pl.ds(lo, size)],
                                     stage_ref.at[slot, pl.ds(0, D_MODEL), pl.ds(0, size)],
                                     sem_ref.at[slot])

    def land_piece(i):
        stage_ref, _ = stream_scratch
        which, k = FFN_STREAM_ORDER[i]
        lo, hi = FFN_STREAM_BOUNDS[k], FFN_STREAM_BOUNDS[k + 1]
        slot = i % FFN_STAGE_SLOTS
        piece_copy(i).wait()
        if which == 2:
            wd_ref[lo:hi, :] = stage_ref[slot, :hi - lo, :D_MODEL].astype(jnp.bfloat16)
        else:
            dst = (wg_ref, wu_ref)[which]
            dst[:, lo:hi] = stage_ref[slot, :D_MODEL, :hi - lo].astype(jnp.bfloat16)
        if i + FFN_STAGE_SLOTS < len(FFN_STREAM_ORDER):
            piece_copy(i + FFN_STAGE_SLOTS).start()

    if stream == "round":
        for i in range(FFN_STAGE_SLOTS):
            piece_copy(i).start()
    if x32 is not None:
        h_ref[...] = _rmsnorm_rows(x32, g_ref[...]).astype(jnp.bfloat16)
    for c in range(N_FFN_CHUNKS):
        cols = slice(c * FFN_CHUNK_F, (c + 1) * FFN_CHUNK_F)
        if stream == "round":
            for i, (which, k) in enumerate(FFN_STREAM_ORDER):
                if which != 2 and FFN_STREAM_BOUNDS[k] == c * FFN_CHUNK_F:
                    land_piece(i)
        elif stream == "wait" and c == 0:
            for which in range(2):
                _ffn_weight_copy(which, w_hbm, base_scratch, stream_scratch).wait()
        h = h_ref[...]
        gate = jnp.dot(h, wg_ref[:, cols], preferred_element_type=jnp.float32)
        up = jnp.dot(h, wu_ref[:, cols], preferred_element_type=jnp.float32)
        act = gate * (1.0 / (1.0 + jnp.exp(-gate))) * up
        a_ref[:, cols] = act.astype(jnp.bfloat16)
    if stream == "round":
        for i, (which, _) in enumerate(FFN_STREAM_ORDER):
            if which == 2:
                land_piece(i)
        if on_streamed is not None:
            on_streamed()
    elif stream == "wait":
        _ffn_weight_copy(2, w_hbm, base_scratch, stream_scratch).wait()
    return jnp.dot(a_ref[...], wd_ref[...], preferred_element_type=jnp.float32)


def _ffn_proj_kernel(x_ref, g1_ref, wg_hbm, wu_hbm, wd_hbm, gm_ref, win_hbm, qg_ref, kg_ref, dm_ref,
                     wg2_ref, wu2_ref, wd2_ref, wo_ref, rel_ref, bucket_ref,
                     x1_ref, z_ref, wg2_bf_ref, wu2_bf_ref, wd2_bf_ref, wo_bf_ref, bias_ref,
                     *scratch):
    tm = x_ref.shape[0]
    base_scratch = scratch[:N_FFN_BASE_SCRATCH]
    round_scratch, win_ref = scratch[N_FFN_BASE_SCRATCH:-1], scratch[-1]

    win_pieces = tuple((lo, hi - lo, slot) for slot, (lo, hi) in
                       enumerate(((0, FFN_STAGE_DIM), (FFN_STAGE_DIM, D_IN))))

    def win_copy(lo, size, slot):
        stage_ref, sem_ref = round_scratch
        return pltpu.make_async_copy(win_hbm.at[:, pl.ds(lo, size)],
                                     stage_ref.at[slot, pl.ds(0, D_MODEL), pl.ds(0, size)],
                                     sem_ref.at[slot])

    def start_win_copies():
        for piece in win_pieces:
            win_copy(*piece).start()

    def land_win():
        stage_ref, _ = round_scratch
        for lo, size, slot in win_pieces:
            win_copy(lo, size, slot).wait()
            win_ref[:, lo:lo + size] = stage_ref[slot, :D_MODEL, :size].astype(jnp.bfloat16)

    first_half = pl.program_id(0) < pl.num_programs(0) // 2

    @pl.when(first_half)
    def _():
        wg2_bf_ref[...] = wg2_ref[...].astype(jnp.bfloat16)
        wd2_bf_ref[...] = wd2_ref[...].astype(jnp.bfloat16)

    @pl.when(jnp.logical_not(first_half))
    def _():
        wu2_bf_ref[...] = wu2_ref[...].astype(jnp.bfloat16)
        wo_bf_ref[...] = wo_ref[...].astype(jnp.bfloat16)

    def body(stream):
        if stream is not None:
            _write_band_bias(rel_ref, bucket_ref, bias_ref)
        y = _swiglu_tile(x_ref[...], g1_ref, (wg_hbm, wu_hbm, wd_hbm), base_scratch, stream,
                         round_scratch, on_streamed=start_win_copies)
        x1 = x_ref[...] + 0.5 * y
        x1_ref[...] = x1
        if stream is not None:
            land_win()
        h = _rmsnorm_rows(x1, gm_ref[...]).astype(jnp.bfloat16)
        z = jnp.dot(h, win_ref[...], preferred_element_type=jnp.float32)
        n_slab = (D_ATTN + 2 * D_KV) // V7X_MXU_DIM
        sq = jnp.concatenate(
            [z[:, i * V7X_MXU_DIM:(i + 1) * V7X_MXU_DIM] for i in range(n_slab)], axis=0)
        ms = jnp.dot((sq * sq).astype(jnp.bfloat16), dm_ref[...], preferred_element_type=jnp.float32)
        q_ms = jnp.concatenate([ms[i * tm:(i + 1) * tm] for i in range(n_slab - 1)], axis=1)
        k_ms = ms[(n_slab - 1) * tm:, :D_KV]
        q = z[:, :D_ATTN]
        k = z[:, D_ATTN:D_ATTN + D_KV]
        z_ref[:, :D_ATTN] = (q * lax.rsqrt(q_ms + EPS) * qg_ref[...]).astype(jnp.bfloat16)
        z_ref[:, D_ATTN:D_ATTN + D_KV] = (k * lax.rsqrt(k_ms + EPS) * kg_ref[...]).astype(jnp.bfloat16)
        z_ref[:, D_ATTN + D_KV:] = z[:, D_ATTN + D_KV:].astype(jnp.bfloat16)

    first = pl.program_id(0) == 0
    pl.when(first)(functools.partial(body, "round"))
    pl.when(jnp.logical_not(first))(functools.partial(body, None))


def _head_mean_matrix(width):
    idx = np.arange(width) // HEAD_DIM
    return jnp.asarray((idx[:, None] == idx[None, :]).astype(np.float32) / HEAD_DIM, jnp.bfloat16)


def _ffn_proj(x2d, ffn_gain, w_gate, w_up, w_down, mix_gain, w_in, q_gain, k_gain,
              next_w_gate, next_w_up, next_w_down, w_out, rel_bias):
    m = x2d.shape[0]
    tm = TILE_M
    n_steps = m // tm
    resident = functools.partial(pl.BlockSpec, pipeline_mode=pl.Buffered(1))
    rows = lambda width: pl.BlockSpec((tm, width), lambda i: (i, 0))
    hbm = pl.BlockSpec(memory_space=pltpu.HBM)
    half_steps = n_steps // 2
    wr = D_MODEL // half_steps
    wdr = D_FF // half_steps
    assert 2 * half_steps == n_steps and wr * half_steps == D_MODEL and wdr * half_steps == D_FF
    assert wr % 16 == 0 and wdr % 16 == 0
    early_slice = lambda i: (jnp.minimum(i, half_steps - 1), 0)
    late_slice = lambda i: (jnp.maximum(i - half_steps, 0), 0)
    wg_rows = pl.BlockSpec((wr, D_FF), early_slice)
    wd_rows = pl.BlockSpec((wdr, D_MODEL), early_slice)
    wu_rows = pl.BlockSpec((wr, D_FF), late_slice)
    wo_rows = pl.BlockSpec((wr, D_MODEL), late_slice)
    qg = (jnp.tile(q_gain, N_HEADS) * (HEAD_DIM ** -0.5)).reshape(1, D_ATTN)
    kg = jnp.tile(k_gain, N_KV_HEADS).reshape(1, D_KV)
    est = (_ffn_vmem_bytes(tm) + 2 * D_MODEL * D_IN
           + 2 * tm * D_MODEL * 4 * 2
           + 2 * tm * D_IN * 2
           + 4 * tm * D_IN * 4
           + 2 * 6 * (2 * wr * D_FF + wdr * D_MODEL + wr * D_MODEL)
           + 3 * N_HEADS * BLOCK * 2 * BLOCK * 4)
    bf16 = lambda shape: jax.ShapeDtypeStruct(shape, jnp.bfloat16)
    return pl.pallas_call(
        _ffn_proj_kernel,
        out_shape=(
            jax.ShapeDtypeStruct((m, D_MODEL), jnp.float32),
            bf16((m, D_IN)),
            bf16((D_MODEL, D_FF)), bf16((D_MODEL, D_FF)), bf16((D_FF, D_MODEL)),
            bf16((D_MODEL, D_MODEL)),
            jax.ShapeDtypeStruct((N_HEADS, BLOCK, 2 * BLOCK), jnp.float32),
        ),
        grid=(n_steps,),
        in_specs=[
            rows(D_MODEL),
            resident((1, D_MODEL), lambda i: (0, 0)),
            hbm, hbm, hbm,
            resident((1, D_MODEL), lambda i: (0, 0)),
            hbm,
            resident((1, D_ATTN), lambda i: (0, 0)),
            resident((1, D_KV), lambda i: (0, 0)),
            resident((V7X_MXU_DIM, V7X_MXU_DIM), lambda i: (0, 0)),
            wg_rows, wu_rows, wd_rows, wo_rows,
            pl.BlockSpec(memory_space=pltpu.SMEM),
            resident((BLOCK, 2 * BLOCK), lambda i: (0, 0)),
        ],
        out_specs=(rows(D_MODEL), rows(D_IN), wg_rows, wu_rows, wd_rows, wo_rows,
                   pl.BlockSpec((N_HEADS, BLOCK, 2 * BLOCK), lambda i: (0, 0, 0))),
        scratch_shapes=(_ffn_base_scratch(tm) + _ffn_round_scratch()
                        + [pltpu.VMEM((D_MODEL, D_IN), jnp.bfloat16)]),
        compiler_params=pltpu.CompilerParams(
            dimension_semantics=("arbitrary",),
            vmem_limit_bytes=_vmem_limit(est)),
        name="ffn_proj",
    )(x2d, ffn_gain.reshape(1, D_MODEL), w_gate, w_up, w_down, mix_gain.reshape(1, D_MODEL),
      w_in, qg, kg, _head_mean_matrix(V7X_MXU_DIM),
      next_w_gate, next_w_up, next_w_down, w_out, rel_bias, jnp.asarray(_band_bucket_table()))


def _t5_bucket(dist):
    n = np.maximum(dist, 0)
    max_exact = N_BUCKETS // 2
    large = max_exact + (np.log(np.maximum(n, 1) / max_exact)
                         / np.log(MAX_DISTANCE / max_exact)
                         * (N_BUCKETS - max_exact)).astype(np.int32)
    large = np.minimum(large, N_BUCKETS - 1)
    return np.where(n < max_exact, n, large).astype(np.int32)


def _band_bucket_table():
    ql = np.arange(BLOCK)[:, None]
    kl = np.arange(2 * BLOCK)[None, :]
    dist = ql + BLOCK - kl
    in_band = (dist >= 0) & (dist < WINDOW)
    return np.where(in_band, _t5_bucket(dist), -1).astype(np.int32)


def _write_band_bias(rel_ref, bucket_ref, o_ref):
    bucket = bucket_ref[...]
    for slot, head in enumerate(HEADS_PLAIN + HEADS_ROLLED):
        acc = jnp.full((BLOCK, 2 * BLOCK), NEG, jnp.float32)
        for b in range(N_BUCKETS):
            acc = jnp.where(bucket == b, rel_ref[b, head], acc)
        o_ref[slot] = acc


def _window_sum_matrices():
    ql = np.arange(BLOCK)[:, None]
    kl = np.arange(2 * BLOCK)[None, :]
    dist = ql + BLOCK - kl
    mats = [((dist >= 0) & (dist < w)).astype(np.float32) for w in POOL_WINDOWS]
    return jnp.asarray(np.stack(mats), jnp.bfloat16)


def _mix_tile(seq_first, t_base, sink_ref, x_ref, z_ref, zh_ref,
              bias_ref, wsum_ref, wo_ref, x2_ref, mix_scratch, g_ref, h_ref):
    kx_ref, kxr_ref, vx_ref, vxr_ref, ux_ref, yp_ref, wcat_ref = mix_scratch
    tq = z_ref.shape[1]
    k_cols = slice(D_ATTN, D_ATTN + D_KV)
    v_cols = slice(D_ATTN + D_KV, D_ATTN + 2 * D_KV)
    u_cols = slice(D_ATTN + 2 * D_KV, D_IN)
    n_blocks = tq // BLOCK
    half = HEAD_DIM
    groups = ((HEADS_PLAIN, kx_ref, vx_ref, 0), (HEADS_ROLLED, kxr_ref, vxr_ref, len(HEADS_PLAIN)))
    lane = lax.broadcasted_iota(jnp.int32, (BLOCK, 2 * half), 1)
    low = lane < half
    state = {}

    def rows_of(j):
        return slice(j * BLOCK, (j + 1) * BLOCK)

    def keys_of(j):
        return slice(j * BLOCK, (j + 2) * BLOCK)

    def build_slabs():
        halo = jnp.where(seq_first, jnp.zeros_like(zh_ref[0, :, D_ATTN:]), zh_ref[0, :, D_ATTN:])
        kx_ref[:BLOCK] = halo[:, :D_KV]
        kx_ref[BLOCK:] = z_ref[0, :, k_cols]
        kxr_ref[...] = pltpu.roll(kx_ref[...].astype(jnp.float32), half, 1).astype(jnp.bfloat16)
        ones = jnp.ones((tq + BLOCK, D_KV), jnp.bfloat16)
        vx_ref[:BLOCK, :D_KV] = halo[:, D_KV:2 * D_KV]
        vx_ref[BLOCK:, :D_KV] = z_ref[0, :, v_cols]
        vx_ref[:, D_KV:] = ones
        vxr_ref[:, :D_KV] = pltpu.roll(
            vx_ref[:, :D_KV].astype(jnp.float32), half, 1).astype(jnp.bfloat16)
        vxr_ref[:, D_KV:] = ones
        ux_ref[:BLOCK] = halo[:, 2 * D_KV:]
        ux_ref[BLOCK:] = z_ref[0, :, u_cols]

    def scores(j, grp):
        heads, k_src, _, _ = groups[grp]
        q = z_ref[0, rows_of(j), :D_ATTN]
        zero = jnp.zeros((BLOCK, 2 * half), q.dtype)
        pieces = []
        for hd in heads:
            pair = q[:, (hd // 2) * 2 * half:(hd // 2 + 1) * 2 * half]
            pieces.append(jnp.where(low if hd % 2 == 0 else ~low, pair, zero))
        qs = jnp.concatenate(pieces, axis=0)
        state["logits", j, grp] = lax.dot_general(
            qs, k_src[keys_of(j), :], (((1,), (1,)), ((), ())),
            preferred_element_type=jnp.float32)

    def softmax_numerators(j, grp):
        heads, _, _, slot0 = groups[grp]
        logits = state.pop(("logits", j, grp))
        es, sink_terms = [], []
        for s, hd in enumerate(heads):
            lg = logits[s * BLOCK:(s + 1) * BLOCK] + bias_ref[slot0 + s]
            if j == 0:
                col = lax.broadcasted_iota(jnp.int32, (1, 2 * BLOCK), 1)
                lg = lg + jnp.where(seq_first & (col < BLOCK), NEG, 0.0).astype(jnp.float32)
            sink = sink_ref[hd]
            m = jnp.maximum(jnp.max(lg, axis=-1, keepdims=True), sink)
            es.append(jnp.exp(lg - m).astype(jnp.bfloat16))
            sink_terms.append(jnp.exp(sink - m))
        state["e", j, grp] = (jnp.concatenate(es, axis=0), sink_terms)

    def attend(j, grp):
        heads, _, v_src, _ = groups[grp]
        e, sink_terms = state.pop(("e", j, grp))
        pv = jnp.dot(e, v_src[keys_of(j), :], preferred_element_type=jnp.float32)
        for s, hd in enumerate(heads):
            blk = pv[s * BLOCK:(s + 1) * BLOCK]
            state["out", j, hd] = blk[:, :D_KV] / (blk[:, D_KV:] + sink_terms[s])

    def store_attention(j):
        for p in range(N_HEADS // 2):
            yp_ref[rows_of(j), p * 2 * half:(p + 1) * 2 * half] = jnp.where(
                low, state.pop(("out", j, 2 * p)), state.pop(("out", j, 2 * p + 1))
            ).astype(jnp.bfloat16)

    def pool_means(g):
        w = POOL_WINDOWS[g]
        gc = slice(g * POOL_GROUP_DIM, (g + 1) * POOL_GROUP_DIM)
        windows = jnp.concatenate([ux_ref[keys_of(j), gc] for j in range(n_blocks)], axis=1)
        wsum = jnp.dot(wsum_ref[g], windows, preferred_element_type=jnp.float32)
        row = lax.broadcasted_iota(jnp.int32, (BLOCK, 1), 0)
        for j in range(n_blocks):
            cnt = jnp.minimum(t_base + j * BLOCK + row + 1, w).astype(jnp.float32)
            token = z_ref[0, rows_of(j), D_ATTN + 2 * D_KV + g * POOL_GROUP_DIM:
                          D_ATTN + 2 * D_KV + (g + 1) * POOL_GROUP_DIM]
            pooled = wsum[:, j * BLOCK:(j + 1) * BLOCK] * (1.0 / cnt) - token.astype(jnp.float32)
            yp_ref[rows_of(j), D_ATTN + g * POOL_GROUP_DIM:D_ATTN + (g + 1) * POOL_GROUP_DIM] = (
                pooled.astype(jnp.bfloat16))

    def project():
        x2 = x_ref[0] + jnp.dot(yp_ref[...], wcat_ref[...], preferred_element_type=jnp.float32)
        x2_ref[...] = x2
        h_ref[...] = _rmsnorm_rows(x2, g_ref[...]).astype(jnp.bfloat16)

    build_slabs()
    for g in range(len(POOL_WINDOWS)):
        pool_means(g)
    scores(0, 0)
    for j in range(n_blocks):
        scores(j, 1)
        softmax_numerators(j, 0)
        attend(j, 0)
        softmax_numerators(j, 1)
        if j + 1 < n_blocks:
            scores(j + 1, 0)
        attend(j, 1)
        store_attention(j)
    project()


N_MIX_SCRATCH = 8


def _mix_ffn_kernel(sink_ref, x_ref, z_ref, zh_ref,
                    bias_ref, wsum_ref, pw_ref, ps_ref, wo_ref, g2_ref, wg_hbm, wu_hbm, wd_hbm,
                    o_ref, *scratch, tiles_per_seq):
    mix_scratch, x2_ref = scratch[:N_MIX_SCRATCH - 1], scratch[N_MIX_SCRATCH - 1]
    base_scratch = scratch[N_MIX_SCRATCH:N_MIX_SCRATCH + N_FFN_BASE_SCRATCH]
    copy_scratch = scratch[N_MIX_SCRATCH + N_FFN_BASE_SCRATCH:]
    h_ref = base_scratch[3]
    t = pl.program_id(0)
    tq = z_ref.shape[1]
    i_seq = lax.rem(t, tiles_per_seq)
    w_hbm = (wg_hbm, wu_hbm, wd_hbm)

    def step(first):
        if first:
            _ffn_start_weight_copies(w_hbm, base_scratch, copy_scratch)
            wcat_ref = mix_scratch[-1]
            wcat_ref[:D_ATTN, :] = wo_ref[:D_ATTN, :]
            for g in range(len(POOL_WINDOWS)):
                gc = slice(g * POOL_GROUP_DIM, (g + 1) * POOL_GROUP_DIM)
                rows = slice(D_ATTN + g * POOL_GROUP_DIM, D_ATTN + (g + 1) * POOL_GROUP_DIM)
                pool_map = (pw_ref[g] * ps_ref[:, gc]).astype(jnp.bfloat16)
                wcat_ref[rows, :] = jnp.dot(pool_map, wo_ref[rows, :],
                                            preferred_element_type=jnp.float32).astype(jnp.bfloat16)
        _mix_tile(i_seq == 0, i_seq * tq, sink_ref, x_ref, z_ref, zh_ref,
                  bias_ref, wsum_ref, wo_ref, x2_ref, mix_scratch, g2_ref, h_ref)
        y = _swiglu_tile(None, g2_ref, w_hbm, base_scratch, "wait" if first else None, copy_scratch)
        o_ref[0] = x2_ref[...] + 0.5 * y

    pl.when(t == 0)(functools.partial(step, True))
    pl.when(t > 0)(functools.partial(step, False))


def _mix_ffn(x3d, z3d, sinks, bias, pool_w, pool_scale, w_out_bf16, ffn_gain,
             w_gate_bf16, w_up_bf16, w_down_bf16):
    b, s, _ = x3d.shape
    tq = TILE_M
    nblk = tq // BLOCK
    tiles_per_seq = s // tq
    n_tiles = b * tiles_per_seq
    resident = functools.partial(pl.BlockSpec, pipeline_mode=pl.Buffered(1))
    hbm = pl.BlockSpec(memory_space=pltpu.HBM)

    def tile(width):
        return pl.BlockSpec((1, tq, width), lambda t: (t // tiles_per_seq, lax.rem(t, tiles_per_seq), 0))

    def halo(width):
        return pl.BlockSpec(
            (1, BLOCK, width),
            lambda t: (t // tiles_per_seq, jnp.maximum(lax.rem(t, tiles_per_seq) * nblk - 1, 0), 0))

    est = (_ffn_vmem_bytes(tq)
           + 2 * 2 * tq * D_MODEL * 4 + 2 * (tq + BLOCK) * D_IN * 2
           + N_HEADS * BLOCK * 2 * BLOCK * 4 + 2 * D_MODEL * D_MODEL + 4 * D_POOL * D_POOL
           + 2 * D_MODEL * D_MODEL
           + 6 * (tq + BLOCK) * D_KV * 2 + (tq + BLOCK) * D_POOL * 2 + tq * D_MODEL * 2
           + tq * D_MODEL * 4 + 16 * BLOCK * 2 * BLOCK * 4 * 4)
    return pl.pallas_call(
        functools.partial(_mix_ffn_kernel, tiles_per_seq=tiles_per_seq),
        out_shape=jax.ShapeDtypeStruct((b, s, D_MODEL), jnp.float32),
        grid=(n_tiles,),
        in_specs=[
            pl.BlockSpec(memory_space=pltpu.SMEM),
            tile(D_MODEL),
            tile(D_IN), halo(D_IN),
            resident((N_HEADS, BLOCK, 2 * BLOCK), lambda t: (0, 0, 0)),
            resident((len(POOL_WINDOWS), BLOCK, 2 * BLOCK), lambda t: (0, 0, 0)),
            resident((len(POOL_WINDOWS), POOL_GROUP_DIM, POOL_GROUP_DIM), lambda t: (0, 0, 0)),
            resident((1, D_POOL), lambda t: (0, 0)),
            resident((D_MODEL, D_MODEL), lambda t: (0, 0)),
            resident((1, D_MODEL), lambda t: (0, 0)),
            hbm, hbm, hbm,
        ],
        out_specs=tile(D_MODEL),
        scratch_shapes=[
            pltpu.VMEM((tq + BLOCK, D_KV), jnp.bfloat16),
            pltpu.VMEM((tq + BLOCK, D_KV), jnp.bfloat16),
            pltpu.VMEM((tq + BLOCK, 2 * D_KV), jnp.bfloat16),
            pltpu.VMEM((tq + BLOCK, 2 * D_KV), jnp.bfloat16),
            pltpu.VMEM((tq + BLOCK, D_POOL), jnp.bfloat16),
            pltpu.VMEM((tq, D_MODEL), jnp.bfloat16),
            pltpu.VMEM((D_MODEL, D_MODEL), jnp.bfloat16),
            pltpu.VMEM((tq, D_MODEL), jnp.float32),
        ] + _ffn_base_scratch(tq) + _ffn_copy_scratch(),
        compiler_params=pltpu.CompilerParams(
            dimension_semantics=("arbitrary",),
            vmem_limit_bytes=_vmem_limit(est)),
        name="mix_ffn",
    )(sinks, x3d, z3d, z3d, bias, _window_sum_matrices(),
      pool_w, pool_scale.reshape(1, D_POOL),
      w_out_bf16, ffn_gain.reshape(1, D_MODEL), w_gate_bf16, w_up_bf16, w_down_bf16)


def kernel(x, ffn1_norm, ffn1_w_gate, ffn1_w_up, ffn1_w_down, mix_norm, w_in, q_norm, k_norm,
           attn_sinks, rel_bias, pool_w, pool_scale, w_out, ffn2_norm, ffn2_w_gate, ffn2_w_up,
           ffn2_w_down):
    b, s, d = x.shape
    assert (d, s % TILE_M) == (D_MODEL, 0)
    for l in range(ffn1_norm.shape[0]):
        x1, z, wg2, wu2, wd2, wo, bias = _ffn_proj(
            x.reshape(b * s, d), ffn1_norm[l], ffn1_w_gate[l], ffn1_w_up[l], ffn1_w_down[l],
            mix_norm[l], w_in[l], q_norm[l], k_norm[l],
            ffn2_w_gate[l], ffn2_w_up[l], ffn2_w_down[l], w_out[l], rel_bias)
        x = _mix_ffn(x1.reshape(b, s, d), z.reshape(b, s, D_IN), attn_sinks[l], bias,
                     pool_w[l], pool_scale[l], wo, ffn2_norm[l], wg2, wu2, wd2)
    return x
```

```python
import functools

import numpy as np
import jax
import jax.numpy as jnp
from jax import lax
from jax.experimental import pallas as pl
from jax.experimental.pallas import tpu as pltpu

D_MODEL = 1024
HEAD_DIM = 64
N_HEADS = 8
N_KV_HEADS = 2
D_ATTN = N_HEADS * HEAD_DIM
D_KV = N_KV_HEADS * HEAD_DIM
D_POOL = D_MODEL - D_ATTN
POOL_WINDOWS = (2, 4, 8, 16)
POOL_GROUP_DIM = D_POOL // len(POOL_WINDOWS)
D_IN = D_ATTN + 2 * D_KV + D_POOL
WINDOW = 128
BLOCK = 128
N_BUCKETS = 32
MAX_DISTANCE = 128
D_FF = 2816
EPS = 1e-6
NEG = -1e30

V7X_MXU_DIM = 256
V7X_VMEM_BYTES = 64 * 1024 * 1024

HEADS_PLAIN = (0, 2, 5, 7)
HEADS_ROLLED = (1, 3, 4, 6)

TILE_M = 512
FFN_CHUNK_F = 256
FFN_STAGE_SLOTS = 2


VMEM_COMPILER_TEMPORARIES_BYTES = 8 << 20
VMEM_LIMIT_FLOOR_BYTES = 16 << 20
VMEM_LIMIT_CEILING_BYTES = V7X_VMEM_BYTES - (4 << 20)


def _vmem_limit(buffer_bytes):
    wanted = buffer_bytes + VMEM_COMPILER_TEMPORARIES_BYTES
    return int(min(VMEM_LIMIT_CEILING_BYTES, max(wanted, VMEM_LIMIT_FLOOR_BYTES)))


def _rmsnorm_rows(x32, gain_row):
    ms = jnp.mean(x32 * x32, axis=-1, keepdims=True)
    return x32 * lax.rsqrt(ms + EPS) * gain_row


N_FFN_CHUNKS = D_FF // FFN_CHUNK_F
N_FFN_BASE_SCRATCH = 5


def _ffn_base_scratch(tm):
    return [
        pltpu.VMEM((D_MODEL, D_FF), jnp.bfloat16),
        pltpu.VMEM((D_MODEL, D_FF), jnp.bfloat16),
        pltpu.VMEM((D_FF, D_MODEL), jnp.bfloat16),
        pltpu.VMEM((tm, D_MODEL), jnp.bfloat16),
        pltpu.VMEM((tm, D_FF), jnp.bfloat16),
    ]


FFN_STREAM_BOUNDS = (0, 1024, 2048, D_FF)
FFN_STREAM_ORDER = tuple((which, k) for k in range(len(FFN_STREAM_BOUNDS) - 1) for which in (0, 1)) + tuple(
    (2, k) for k in range(len(FFN_STREAM_BOUNDS) - 1))
FFN_STAGE_DIM = max(hi - lo for lo, hi in zip(FFN_STREAM_BOUNDS, FFN_STREAM_BOUNDS[1:]))


def _ffn_round_scratch():
    assert D_MODEL <= FFN_STAGE_DIM and all(b % FFN_CHUNK_F == 0 for b in FFN_STREAM_BOUNDS)
    return [
        pltpu.VMEM((FFN_STAGE_SLOTS, FFN_STAGE_DIM, FFN_STAGE_DIM), jnp.float32),
        pltpu.SemaphoreType.DMA((FFN_STAGE_SLOTS,)),
    ]


def _ffn_copy_scratch():
    return [pltpu.SemaphoreType.DMA((3,))]


def _ffn_vmem_bytes(tm):
    return (2 * 3 * D_MODEL * D_FF + FFN_STAGE_SLOTS * FFN_STAGE_DIM * FFN_STAGE_DIM * 4
            + tm * D_MODEL * 2 + tm * D_FF * 2 + 6 * tm * FFN_CHUNK_F * 4 + tm * D_MODEL * 4)


def _ffn_weight_copy(which, w_hbm, base_scratch, copy_scratch):
    (sem_ref,) = copy_scratch
    return pltpu.make_async_copy(w_hbm[which], base_scratch[which], sem_ref.at[which])


def _ffn_start_weight_copies(w_hbm, base_scratch, copy_scratch):
    for which in range(3):
        _ffn_weight_copy(which, w_hbm, base_scratch, copy_scratch).start()


def _swiglu_tile(x32, g_ref, w_hbm, base_scratch, stream, stream_scratch, on_streamed=None):
    wg_ref, wu_ref, wd_ref, h_ref, a_ref = base_scratch

    def piece_copy(i):
        stage_ref, sem_ref = stream_scratch
        which, k = FFN_STREAM_ORDER[i]
        lo, size = FFN_STREAM_BOUNDS[k], FFN_STREAM_BOUNDS[k + 1] - FFN_STREAM_BOUNDS[k]
        slot = i % FFN_STAGE_SLOTS
        if which == 2:
            return pltpu.make_async_copy(w_hbm[2].at[pl.ds(lo, size), :],
                                         stage_ref.at[slot, pl.ds(0, size), pl.ds(0, D_MODEL)],
                                         sem_ref.at[slot])
        return pltpu.make_async_copy(w_hbm[which].at[:, pl.ds(lo, size)],
                                     stage_ref.at[slot, pl.ds(0, D_MODEL), pl.ds(0, size)],
                                     sem_ref.at[slot])

    def land_piece(i):
        stage_ref, _ = stream_scratch
        which, k = FFN_STREAM_ORDER[i]
        lo, hi = FFN_STREAM_BOUNDS[k], FFN_STREAM_BOUNDS[k + 1]
        slot = i % FFN_STAGE_SLOTS
        piece_copy(i).wait()
        if which == 2:
            wd_ref[lo:hi, :] = stage_ref[slot, :hi - lo, :D_MODEL].astype(jnp.bfloat16)
        else:
            dst = (wg_ref, wu_ref)[which]
            dst[:, lo:hi] = stage_ref[slot, :D_MODEL, :hi - lo].astype(jnp.bfloat16)
        if i + FFN_STAGE_SLOTS < len(FFN_STREAM_ORDER):
            piece_copy(i + FFN_STAGE_SLOTS).start()

    if stream == "round":
        for i in range(FFN_STAGE_SLOTS):
            piece_copy(i).start()
    if x32 is not None:
        h_ref[...] = _rmsnorm_rows(x32, g_ref[...]).astype(jnp.bfloat16)
    for c in range(N_FFN_CHUNKS):
        cols = slice(c * FFN_CHUNK_F, (c + 1) * FFN_CHUNK_F)
        if stream == "round":
            for i, (which, k) in enumerate(FFN_STREAM_ORDER):
                if which != 2 and FFN_STREAM_BOUNDS[k] == c * FFN_CHUNK_F:
                    land_piece(i)
        elif stream == "wait" and c == 0:
            for which in range(2):
                _ffn_weight_copy(which, w_hbm, base_scratch, stream_scratch).wait()
        h = h_ref[...]
        gate = jnp.dot(h, wg_ref[:, cols], preferred_element_type=jnp.float32)
        up = jnp.dot(h, wu_ref[:, cols], preferred_element_type=jnp.float32)
        act = gate * (1.0 / (1.0 + jnp.exp(-gate))) * up
        a_ref[:, cols] = act.astype(jnp.bfloat16)
    if stream == "round":
        for i, (which, _) in enumerate(FFN_STREAM_ORDER):
            if which == 2:
                land_piece(i)
        if on_streamed is not None:
            on_streamed()
    elif stream == "wait":
        _ffn_weight_copy(2, w_hbm, base_scratch, stream_scratch).wait()
    return jnp.dot(a_ref[...], wd_ref[...], preferred_element_type=jnp.float32)


def _ffn_proj_kernel(x_ref, g1_ref, wg_hbm, wu_hbm, wd_hbm, gm_ref, win_hbm, qg_ref, kg_ref, dm_ref,
                     wg2_ref, wu2_ref, wd2_ref, wo_ref, rel_ref, bucket_ref,
                     x1_ref, z_ref, wg2_bf_ref, wu2_bf_ref, wd2_bf_ref, wo_bf_ref, bias_ref,
                     *scratch):
    tm = x_ref.shape[0]
    base_scratch = scratch[:N_FFN_BASE_SCRATCH]
    round_scratch, win_ref = scratch[N_FFN_BASE_SCRATCH:-1], scratch[-1]

    win_pieces = tuple((lo, hi - lo, slot) for slot, (lo, hi) in
                       enumerate(((0, FFN_STAGE_DIM), (FFN_STAGE_DIM, D_IN))))

    def win_copy(lo, size, slot):
        stage_ref, sem_ref = round_scratch
        return pltpu.make_async_copy(win_hbm.at[:, pl.ds(lo, size)],
                                     stage_ref.at[slot, pl.ds(0, D_MODEL), pl.ds(0, size)],
                                     sem_ref.at[slot])

    def start_win_copies():
        for piece in win_pieces:
            win_copy(*piece).start()

    def land_win():
        stage_ref, _ = round_scratch
        for lo, size, slot in win_pieces:
            win_copy(lo, size, slot).wait()
            win_ref[:, lo:lo + size] = stage_ref[slot, :D_MODEL, :size].astype(jnp.bfloat16)

    first_half = pl.program_id(0) < pl.num_programs(0) // 2

    @pl.when(first_half)
    def _():
        wg2_bf_ref[...] = wg2_ref[...].astype(jnp.bfloat16)
        wd2_bf_ref[...] = wd2_ref[...].astype(jnp.bfloat16)

    @pl.when(jnp.logical_not(first_half))
    def _():
        wu2_bf_ref[...] = wu2_ref[...].astype(jnp.bfloat16)
        wo_bf_ref[...] = wo_ref[...].astype(jnp.bfloat16)

    def body(stream):
        if stream is not None:
            _write_band_bias(rel_ref, bucket_ref, bias_ref)
        y = _swiglu_tile(x_ref[...], g1_ref, (wg_hbm, wu_hbm, wd_hbm), base_scratch, stream,
                         round_scratch, on_streamed=start_win_copies)
        x1 = x_ref[...] + 0.5 * y
        x1_ref[...] = x1
        if stream is not None:
            land_win()
        h = _rmsnorm_rows(x1, gm_ref[...]).astype(jnp.bfloat16)
        z = jnp.dot(h, win_ref[...], preferred_element_type=jnp.float32)
        n_slab = (D_ATTN + 2 * D_KV) // V7X_MXU_DIM
        sq = jnp.concatenate(
            [z[:, i * V7X_MXU_DIM:(i + 1) * V7X_MXU_DIM] for i in range(n_slab)], axis=0)
        ms = jnp.dot((sq * sq).astype(jnp.bfloat16), dm_ref[...], preferred_element_type=jnp.float32)
        q_ms = jnp.concatenate([ms[i * tm:(i + 1) * tm] for i in range(n_slab - 1)], axis=1)
        k_ms = ms[(n_slab - 1) * tm:, :D_KV]
        q = z[:, :D_ATTN]
        k = z[:, D_ATTN:D_ATTN + D_KV]
        z_ref[:, :D_ATTN] = (q * lax.rsqrt(q_ms + EPS) * qg_ref[...]).astype(jnp.bfloat16)
        z_ref[:, D_ATTN:D_ATTN + D_KV] = (k * lax.rsqrt(k_ms + EPS) * kg_ref[...]).astype(jnp.bfloat16)
        z_ref[:, D_ATTN + D_KV:] = z[:, D_ATTN + D_KV:].astype(jnp.bfloat16)

    first = pl.program_id(0) == 0
    pl.when(first)(functools.partial(body, "round"))
    pl.when(jnp.logical_not(first))(functools.partial(body, None))


def _head_mean_matrix(width):
    idx = np.arange(width) // HEAD_DIM
    return jnp.asarray((idx[:, None] == idx[None, :]).astype(np.float32) / HEAD_DIM, jnp.bfloat16)


def _ffn_proj(x2d, ffn_gain, w_gate, w_up, w_down, mix_gain, w_in, q_gain, k_gain,
              next_w_gate, next_w_up, next_w_down, w_out, rel_bias):
    m = x2d.shape[0]
    tm = TILE_M
    n_steps = m // tm
    resident = functools.partial(pl.BlockSpec, pipeline_mode=pl.Buffered(1))
    rows = lambda width: pl.BlockSpec((tm, width), lambda i: (i, 0))
    hbm = pl.BlockSpec(memory_space=pltpu.HBM)
    half_steps = n_steps // 2
    wr = D_MODEL // half_steps
    wdr = D_FF // half_steps
    assert 2 * half_steps == n_steps and wr * half_steps == D_MODEL and wdr * half_steps == D_FF
    assert wr % 16 == 0 and wdr % 16 == 0
    early_slice = lambda i: (jnp.minimum(i, half_steps - 1), 0)
    late_slice = lambda i: (jnp.maximum(i - half_steps, 0), 0)
    wg_rows = pl.BlockSpec((wr, D_FF), early_slice)
    wd_rows = pl.BlockSpec((wdr, D_MODEL), early_slice)
    wu_rows = pl.BlockSpec((wr, D_FF), late_slice)
    wo_rows = pl.BlockSpec((wr, D_MODEL), late_slice)
    qg = (jnp.tile(q_gain, N_HEADS) * (HEAD_DIM ** -0.5)).reshape(1, D_ATTN)
    kg = jnp.tile(k_gain, N_KV_HEADS).reshape(1, D_KV)
    est = (_ffn_vmem_bytes(tm) + 2 * D_MODEL * D_IN
           + 2 * tm * D_MODEL * 4 * 2
           + 2 * tm * D_IN * 2
           + 4 * tm * D_IN * 4
           + 2 * 6 * (2 * wr * D_FF + wdr * D_MODEL + wr * D_MODEL)
           + 3 * N_HEADS * BLOCK * 2 * BLOCK * 4)
    bf16 = lambda shape: jax.ShapeDtypeStruct(shape, jnp.bfloat16)
    return pl.pallas_call(
        _ffn_proj_kernel,
        out_shape=(
            jax.ShapeDtypeStruct((m, D_MODEL), jnp.float32),
            bf16((m, D_IN)),
            bf16((D_MODEL, D_FF)), bf16((D_MODEL, D_FF)), bf16((D_FF, D_MODEL)),
            bf16((D_MODEL, D_MODEL)),
            jax.ShapeDtypeStruct((N_HEADS, BLOCK, 2 * BLOCK), jnp.float32),
        ),
        grid=(n_steps,),
        in_specs=[
            rows(D_MODEL),
            resident((1, D_MODEL), lambda i: (0, 0)),
            hbm, hbm, hbm,
            resident((1, D_MODEL), lambda i: (0, 0)),
            hbm,
            resident((1, D_ATTN), lambda i: (0, 0)),
            resident((1, D_KV), lambda i: (0, 0)),
            resident((V7X_MXU_DIM, V7X_MXU_DIM), lambda i: (0, 0)),
            wg_rows, wu_rows, wd_rows, wo_rows,
            pl.BlockSpec(memory_space=pltpu.SMEM),
            resident((BLOCK, 2 * BLOCK), lambda i: (0, 0)),
        ],
        out_specs=(rows(D_MODEL), rows(D_IN), wg_rows, wu_rows, wd_rows, wo_rows,
                   pl.BlockSpec((N_HEADS, BLOCK, 2 * BLOCK), lambda i: (0, 0, 0))),
        scratch_shapes=(_ffn_base_scratch(tm) + _ffn_round_scratch()
                        + [pltpu.VMEM((D_MODEL, D_IN), jnp.bfloat16)]),
        compiler_params=pltpu.CompilerParams(
            dimension_semantics=("arbitrary",),
            vmem_limit_bytes=_vmem_limit(est)),
        name="ffn_proj",
    )(x2d, ffn_gain.reshape(1, D_MODEL), w_gate, w_up, w_down, mix_gain.reshape(1, D_MODEL),
      w_in, qg, kg, _head_mean_matrix(V7X_MXU_DIM),
      next_w_gate, next_w_up, next_w_down, w_out, rel_bias, jnp.asarray(_band_bucket_table()))


def _t5_bucket(dist):
    n = np.maximum(dist, 0)
    max_exact = N_BUCKETS // 2
    large = max_exact + (np.log(np.maximum(n, 1) / max_exact)
                         / np.log(MAX_DISTANCE / max_exact)
                         * (N_BUCKETS - max_exact)).astype(np.int32)
    large = np.minimum(large, N_BUCKETS - 1)
    return np.where(n < max_exact, n, large).astype(np.int32)


def _band_bucket_table():
    ql = np.arange(BLOCK)[:, None]
    kl = np.arange(2 * BLOCK)[None, :]
    dist = ql + BLOCK - kl
    in_band = (dist >= 0) & (dist < WINDOW)
    return np.where(in_band, _t5_bucket(dist), -1).astype(np.int32)


def _write_band_bias(rel_ref, bucket_ref, o_ref):
    bucket = bucket_ref[...]
    for slot, head in enumerate(HEADS_PLAIN + HEADS_ROLLED):
        acc = jnp.full((BLOCK, 2 * BLOCK), NEG, jnp.float32)
        for b in range(N_BUCKETS):
            acc = jnp.where(bucket == b, rel_ref[b, head], acc)
        o_ref[slot] = acc


def _window_sum_matrices():
    ql = np.arange(BLOCK)[:, None]
    kl = np.arange(2 * BLOCK)[None, :]
    dist = ql + BLOCK - kl
    mats = [((dist >= 0) & (dist < w)).astype(np.float32) for w in POOL_WINDOWS]
    return jnp.asarray(np.stack(mats), jnp.bfloat16)


def _mix_tile(seq_first, t_base, sink_ref, x_ref, z_ref, zh_ref,
              bias_ref, wsum_ref, wo_ref, x2_ref, mix_scratch, g_ref, h_ref):
    kx_ref, kxr_ref, vx_ref, vxr_ref, ux_ref, pooled_ref, y_ref, wc_ref = mix_scratch
    tq = z_ref.shape[1]
    k_cols = slice(D_ATTN, D_ATTN + D_KV)
    v_cols = slice(D_ATTN + D_KV, D_ATTN + 2 * D_KV)
    u_cols = slice(D_ATTN + 2 * D_KV, D_IN)
    n_blocks = tq // BLOCK
    half = HEAD_DIM
    groups = ((HEADS_PLAIN, kx_ref, vx_ref, 0), (HEADS_ROLLED, kxr_ref, vxr_ref, len(HEADS_PLAIN)))
    lane = lax.broadcasted_iota(jnp.int32, (BLOCK, 2 * half), 1)
    low = lane < half
    state = {}

    def rows_of(j):
        return slice(j * BLOCK, (j + 1) * BLOCK)

    def keys_of(j):
        return slice(j * BLOCK, (j + 2) * BLOCK)

    def build_slabs():
        halo = jnp.where(seq_first, jnp.zeros_like(zh_ref[0, :, D_ATTN:]), zh_ref[0, :, D_ATTN:])
        kx_ref[:BLOCK] = halo[:, :D_KV]
        kx_ref[BLOCK:] = z_ref[0, :, k_cols]
        kxr_ref[...] = pltpu.roll(kx_ref[...].astype(jnp.float32), half, 1).astype(jnp.bfloat16)
        ones = jnp.ones((tq + BLOCK, D_KV), jnp.bfloat16)
        vx_ref[:BLOCK, :D_KV] = halo[:, D_KV:2 * D_KV]
        vx_ref[BLOCK:, :D_KV] = z_ref[0, :, v_cols]
        vx_ref[:, D_KV:] = ones
        vxr_ref[:, :D_KV] = pltpu.roll(
            vx_ref[:, :D_KV].astype(jnp.float32), half, 1).astype(jnp.bfloat16)
        vxr_ref[:, D_KV:] = ones
        ux_ref[:BLOCK] = halo[:, 2 * D_KV:]
        ux_ref[BLOCK:] = z_ref[0, :, u_cols]

    def scores(j, grp):
        heads, k_src, _, _ = groups[grp]
        q = z_ref[0, rows_of(j), :D_ATTN]
        zero = jnp.zeros((BLOCK, 2 * half), q.dtype)
        pieces = []
        for hd in heads:
            pair = q[:, (hd // 2) * 2 * half:(hd // 2 + 1) * 2 * half]
            pieces.append(jnp.where(low if hd % 2 == 0 else ~low, pair, zero))
        qs = jnp.concatenate(pieces, axis=0)
        state["logits", j, grp] = lax.dot_general(
            qs, k_src[keys_of(j), :], (((1,), (1,)), ((), ())),
            preferred_element_type=jnp.float32)

    def softmax_numerators(j, grp):
        heads, _, _, slot0 = groups[grp]
        logits = state.pop(("logits", j, grp))
        es, sink_terms = [], []
        for s, hd in enumerate(heads):
            lg = logits[s * BLOCK:(s + 1) * BLOCK] + bias_ref[slot0 + s]
            if j == 0:
                col = lax.broadcasted_iota(jnp.int32, (1, 2 * BLOCK), 1)
                lg = lg + jnp.where(seq_first & (col < BLOCK), NEG, 0.0).astype(jnp.float32)
            sink = sink_ref[hd]
            m = jnp.maximum(jnp.max(lg, axis=-1, keepdims=True), sink)
            es.append(jnp.exp(lg - m).astype(jnp.bfloat16))
            sink_terms.append(jnp.exp(sink - m))
        state["e", j, grp] = (jnp.concatenate(es, axis=0), sink_terms)

    def attend(j, grp):
        heads, _, v_src, _ = groups[grp]
        e, sink_terms = state.pop(("e", j, grp))
        pv = jnp.dot(e, v_src[keys_of(j), :], preferred_element_type=jnp.float32)
        for s, hd in enumerate(heads):
            blk = pv[s * BLOCK:(s + 1) * BLOCK]
            state["out", j, hd] = blk[:, :D_KV] / (blk[:, D_KV:] + sink_terms[s])

    def store_attention(j):
        for p in range(N_HEADS // 2):
            y_ref[rows_of(j), p * 2 * half:(p + 1) * 2 * half] = jnp.where(
                low, state.pop(("out", j, 2 * p)), state.pop(("out", j, 2 * p + 1))
            ).astype(jnp.bfloat16)

    def pool_means(g):
        w = POOL_WINDOWS[g]
        gc = slice(g * POOL_GROUP_DIM, (g + 1) * POOL_GROUP_DIM)
        windows = jnp.concatenate([ux_ref[keys_of(j), gc] for j in range(n_blocks)], axis=1)
        wsum = jnp.dot(wsum_ref[g], windows, preferred_element_type=jnp.float32)
        row = lax.broadcasted_iota(jnp.int32, (BLOCK, 1), 0)
        for j in range(n_blocks):
            cnt = jnp.minimum(t_base + j * BLOCK + row + 1, w).astype(jnp.float32)
            token = z_ref[0, rows_of(j), D_ATTN + 2 * D_KV + g * POOL_GROUP_DIM:
                          D_ATTN + 2 * D_KV + (g + 1) * POOL_GROUP_DIM]
            pooled = wsum[:, j * BLOCK:(j + 1) * BLOCK] * (1.0 / cnt) - token.astype(jnp.float32)
            pooled_ref[rows_of(j), gc] = pooled.astype(jnp.bfloat16)

    def project_pool():
        x2_ref[...] = x_ref[0] + jnp.dot(pooled_ref[...], wc_ref[...],
                                         preferred_element_type=jnp.float32)

    def project_attention():
        x2 = x2_ref[...] + jnp.dot(y_ref[...], wo_ref[:D_ATTN, :], preferred_element_type=jnp.float32)
        x2_ref[...] = x2
        h_ref[...] = _rmsnorm_rows(x2, g_ref[...]).astype(jnp.bfloat16)

    build_slabs()
    for g in range(len(POOL_WINDOWS)):
        pool_means(g)
    scores(0, 0)
    for j in range(n_blocks):
        scores(j, 1)
        softmax_numerators(j, 0)
        attend(j, 0)
        softmax_numerators(j, 1)
        if j == 1:
            project_pool()
        if j + 1 < n_blocks:
            scores(j + 1, 0)
        attend(j, 1)
        store_attention(j)
    project_attention()


N_MIX_SCRATCH = 9


def _mix_ffn_kernel(sink_ref, x_ref, z_ref, zh_ref,
                    bias_ref, wsum_ref, pw_ref, ps_ref, wo_ref, g2_ref, wg_hbm, wu_hbm, wd_hbm,
                    o_ref, *scratch, tiles_per_seq):
    mix_scratch, x2_ref = scratch[:N_MIX_SCRATCH - 1], scratch[N_MIX_SCRATCH - 1]
    base_scratch = scratch[N_MIX_SCRATCH:N_MIX_SCRATCH + N_FFN_BASE_SCRATCH]
    copy_scratch = scratch[N_MIX_SCRATCH + N_FFN_BASE_SCRATCH:]
    h_ref = base_scratch[3]
    t = pl.program_id(0)
    tq = z_ref.shape[1]
    i_seq = lax.rem(t, tiles_per_seq)
    w_hbm = (wg_hbm, wu_hbm, wd_hbm)

    def step(first):
        if first:
            _ffn_start_weight_copies(w_hbm, base_scratch, copy_scratch)
            for g in range(len(POOL_WINDOWS)):
                gc = slice(g * POOL_GROUP_DIM, (g + 1) * POOL_GROUP_DIM)
                pool_map = (pw_ref[g] * ps_ref[:, gc]).astype(jnp.bfloat16)
                mix_scratch[-1][gc, :] = jnp.dot(
                    pool_map, wo_ref[D_ATTN + g * POOL_GROUP_DIM:D_ATTN + (g + 1) * POOL_GROUP_DIM, :],
                    preferred_element_type=jnp.float32).astype(jnp.bfloat16)
        _mix_tile(i_seq == 0, i_seq * tq, sink_ref, x_ref, z_ref, zh_ref,
                  bias_ref, wsum_ref, wo_ref, x2_ref, mix_scratch, g2_ref, h_ref)
        y = _swiglu_tile(None, g2_ref, w_hbm, base_scratch, "wait" if first else None, copy_scratch)
        o_ref[0] = x2_ref[...] + 0.5 * y

    pl.when(t == 0)(functools.partial(step, True))
    pl.when(t > 0)(functools.partial(step, False))


def _mix_ffn(x3d, z3d, sinks, bias, pool_w, pool_scale, w_out_bf16, ffn_gain,
             w_gate_bf16, w_up_bf16, w_down_bf16):
    b, s, _ = x3d.shape
    tq = TILE_M
    nblk = tq // BLOCK
    tiles_per_seq = s // tq
    n_tiles = b * tiles_per_seq
    resident = functools.partial(pl.BlockSpec, pipeline_mode=pl.Buffered(1))
    hbm = pl.BlockSpec(memory_space=pltpu.HBM)

    def tile(width):
        return pl.BlockSpec((1, tq, width), lambda t: (t // tiles_per_seq, lax.rem(t, tiles_per_seq), 0))

    def halo(width):
        return pl.BlockSpec(
            (1, BLOCK, width),
            lambda t: (t // tiles_per_seq, jnp.maximum(lax.rem(t, tiles_per_seq) * nblk - 1, 0), 0))

    est = (_ffn_vmem_bytes(tq)
           + 2 * 2 * tq * D_MODEL * 4 + 2 * (tq + BLOCK) * D_IN * 2
           + N_HEADS * BLOCK * 2 * BLOCK * 4 + 2 * D_MODEL * D_MODEL + 4 * D_POOL * D_POOL
           + 2 * D_POOL * D_MODEL
           + 6 * (tq + BLOCK) * D_KV * 2 + (tq + BLOCK) * D_POOL * 2 + tq * D_MODEL * 2
           + tq * D_MODEL * 4 + 16 * BLOCK * 2 * BLOCK * 4 * 4)
    return pl.pallas_call(
        functools.partial(_mix_ffn_kernel, tiles_per_seq=tiles_per_seq),
        out_shape=jax.ShapeDtypeStruct((b, s, D_MODEL), jnp.float32),
        grid=(n_tiles,),
        in_specs=[
            pl.BlockSpec(memory_space=pltpu.SMEM),
            tile(D_MODEL),
            tile(D_IN), halo(D_IN),
            resident((N_HEADS, BLOCK, 2 * BLOCK), lambda t: (0, 0, 0)),
            resident((len(POOL_WINDOWS), BLOCK, 2 * BLOCK), lambda t: (0, 0, 0)),
            resident((len(POOL_WINDOWS), POOL_GROUP_DIM, POOL_GROUP_DIM), lambda t: (0, 0, 0)),
            resident((1, D_POOL), lambda t: (0, 0)),
            resident((D_MODEL, D_MODEL), lambda t: (0, 0)),
            resident((1, D_MODEL), lambda t: (0, 0)),
            hbm, hbm, hbm,
        ],
        out_specs=tile(D_MODEL),
        scratch_shapes=[
            pltpu.VMEM((tq + BLOCK, D_KV), jnp.bfloat16),
            pltpu.VMEM((tq + BLOCK, D_KV), jnp.bfloat16),
            pltpu.VMEM((tq + BLOCK, 2 * D_KV), jnp.bfloat16),
            pltpu.VMEM((tq + BLOCK, 2 * D_KV), jnp.bfloat16),
            pltpu.VMEM((tq + BLOCK, D_POOL), jnp.bfloat16),
            pltpu.VMEM((tq, D_POOL), jnp.bfloat16),
            pltpu.VMEM((tq, D_ATTN), jnp.bfloat16),
            pltpu.VMEM((D_POOL, D_MODEL), jnp.bfloat16),
            pltpu.VMEM((tq, D_MODEL), jnp.float32),
        ] + _ffn_base_scratch(tq) + _ffn_copy_scratch(),
        compiler_params=pltpu.CompilerParams(
            dimension_semantics=("arbitrary",),
            vmem_limit_bytes=_vmem_limit(est)),
        name="mix_ffn",
    )(sinks, x3d, z3d, z3d, bias, _window_sum_matrices(),
      pool_w, pool_scale.reshape(1, D_POOL),
      w_out_bf16, ffn_gain.reshape(1, D_MODEL), w_gate_bf16, w_up_bf16, w_down_bf16)


def kernel(x, ffn1_norm, ffn1_w_gate, ffn1_w_up, ffn1_w_down, mix_norm, w_in, q_norm, k_norm,
           attn_sinks, rel_bias, pool_w, pool_scale, w_out, ffn2_norm, ffn2_w_gate, ffn2_w_up,
           ffn2_w_down):
    b, s, d = x.shape
    assert (d, s % TILE_M) == (D_MODEL, 0)
    for l in range(ffn1_norm.shape[0]):
        x1, z, wg2, wu2, wd2, wo, bias = _ffn_proj(
            x.reshape(b * s, d), ffn1_norm[l], ffn1_w_gate[l], ffn1_w_up[l], ffn1_w_down[l],
            mix_norm[l], w_in[l], q_norm[l], k_norm[l],
            ffn2_w_gate[l], ffn2_w_up[l], ffn2_w_down[l], w_out[l], rel_bias)
        x = _mix_ffn(x1.reshape(b, s, d), z.reshape(b, s, D_IN), attn_sinks[l], bias,
                     pool_w[l], pool_scale[l], wo, ffn2_norm[l], wg2, wu2, wd2)
    return x
```

```python
import functools

import numpy as np
import jax
import jax.numpy as jnp
from jax import lax
from jax.experimental import pallas as pl
from jax.experimental.pallas import tpu as pltpu

D_MODEL = 1024
HEAD_DIM = 64
N_HEADS = 8
N_KV_HEADS = 2
D_ATTN = N_HEADS * HEAD_DIM
D_KV = N_KV_HEADS * HEAD_DIM
D_POOL = D_MODEL - D_ATTN
POOL_WINDOWS = (2, 4, 8, 16)
POOL_GROUP_DIM = D_POOL // len(POOL_WINDOWS)
D_IN = D_ATTN + 2 * D_KV + D_POOL
WINDOW = 128
BLOCK = 128
N_BUCKETS = 32
MAX_DISTANCE = 128
D_FF = 2816
EPS = 1e-6
NEG = -1e30

V7X_MXU_DIM = 256
V7X_VMEM_BYTES = 64 * 1024 * 1024

HEADS_PLAIN = (0, 2, 5, 7)
HEADS_ROLLED = (1, 3, 4, 6)

TILE_M = 512
FFN_CHUNK_F = 256
FFN_STAGE_SLOTS = 2


VMEM_COMPILER_TEMPORARIES_BYTES = 8 << 20
VMEM_LIMIT_FLOOR_BYTES = 16 << 20
VMEM_LIMIT_CEILING_BYTES = V7X_VMEM_BYTES - (4 << 20)


def _vmem_limit(buffer_bytes):
    wanted = buffer_bytes + VMEM_COMPILER_TEMPORARIES_BYTES
    return int(min(VMEM_LIMIT_CEILING_BYTES, max(wanted, VMEM_LIMIT_FLOOR_BYTES)))


def _rmsnorm_rows(x32, gain_row):
    ms = jnp.mean(x32 * x32, axis=-1, keepdims=True)
    return x32 * lax.rsqrt(ms + EPS) * gain_row


N_FFN_CHUNKS = D_FF // FFN_CHUNK_F
N_FFN_BASE_SCRATCH = 5


def _ffn_base_scratch(tm):
    return [
        pltpu.VMEM((D_MODEL, D_FF), jnp.bfloat16),
        pltpu.VMEM((D_MODEL, D_FF), jnp.bfloat16),
        pltpu.VMEM((D_FF, D_MODEL), jnp.bfloat16),
        pltpu.VMEM((tm, D_MODEL), jnp.bfloat16),
        pltpu.VMEM((tm, D_FF), jnp.bfloat16),
    ]


FFN_STREAM_BOUNDS = (0, 1024, 2048, D_FF)
FFN_STREAM_ORDER = tuple((which, k) for k in range(len(FFN_STREAM_BOUNDS) - 1) for which in (0, 1)) + tuple(
    (2, k) for k in range(len(FFN_STREAM_BOUNDS) - 1))
FFN_STAGE_DIM = max(hi - lo for lo, hi in zip(FFN_STREAM_BOUNDS, FFN_STREAM_BOUNDS[1:]))


def _ffn_round_scratch():
    assert D_MODEL <= FFN_STAGE_DIM and all(b % FFN_CHUNK_F == 0 for b in FFN_STREAM_BOUNDS)
    return [
        pltpu.VMEM((FFN_STAGE_SLOTS, FFN_STAGE_DIM, FFN_STAGE_DIM), jnp.float32),
        pltpu.SemaphoreType.DMA((FFN_STAGE_SLOTS,)),
    ]


def _ffn_copy_scratch():
    return [pltpu.SemaphoreType.DMA((3,))]


def _ffn_vmem_bytes(tm):
    return (2 * 3 * D_MODEL * D_FF + FFN_STAGE_SLOTS * FFN_STAGE_DIM * FFN_STAGE_DIM * 4
            + tm * D_MODEL * 2 + tm * D_FF * 2 + 6 * tm * FFN_CHUNK_F * 4 + tm * D_MODEL * 4)


def _ffn_weight_copy(which, w_hbm, base_scratch, copy_scratch):
    (sem_ref,) = copy_scratch
    return pltpu.make_async_copy(w_hbm[which], base_scratch[which], sem_ref.at[which])


def _ffn_start_weight_copies(w_hbm, base_scratch, copy_scratch):
    for which in range(3):
        _ffn_weight_copy(which, w_hbm, base_scratch, copy_scratch).start()


def _swiglu_tile(x32, g_ref, w_hbm, base_scratch, stream, stream_scratch, on_streamed=None):
    wg_ref, wu_ref, wd_ref, h_ref, a_ref = base_scratch

    def piece_copy(i):
        stage_ref, sem_ref = stream_scratch
        which, k = FFN_STREAM_ORDER[i]
        lo, size = FFN_STREAM_BOUNDS[k], FFN_STREAM_BOUNDS[k + 1] - FFN_STREAM_BOUNDS[k]
        slot = i % FFN_STAGE_SLOTS
        if which == 2:
            return pltpu.make_async_copy(w_hbm[2].at[pl.ds(lo, size), :],
                                         stage_ref.at[slot, pl.ds(0, size), pl.ds(0, D_MODEL)],
                                         sem_ref.at[slot])
        return pltpu.make_async_copy(w_hbm[which].at[:, pl.ds(lo, size)],
                                     stage_ref.at[slot, pl.ds(0, D_MODEL), pl.ds(0, size)],
                                     sem_ref.at[slot])

    def land_piece(i):
        stage_ref, _ = stream_scratch
        which, k = FFN_STREAM_ORDER[i]
        lo, hi = FFN_STREAM_BOUNDS[k], FFN_STREAM_BOUNDS[k + 1]
        slot = i % FFN_STAGE_SLOTS
        piece_copy(i).wait()
        if which == 2:
            wd_ref[lo:hi, :] = stage_ref[slot, :hi - lo, :D_MODEL].astype(jnp.bfloat16)
        else:
            dst = (wg_ref, wu_ref)[which]
            dst[:, lo:hi] = stage_ref[slot, :D_MODEL, :hi - lo].astype(jnp.bfloat16)
        if i + FFN_STAGE_SLOTS < len(FFN_STREAM_ORDER):
            piece_copy(i + FFN_STAGE_SLOTS).start()

    if stream == "round":
        for i in range(FFN_STAGE_SLOTS):
            piece_copy(i).start()
    if x32 is not None:
        h_ref[...] = _rmsnorm_rows(x32, g_ref[...]).astype(jnp.bfloat16)
    for c in range(N_FFN_CHUNKS):
        cols = slice(c * FFN_CHUNK_F, (c + 1) * FFN_CHUNK_F)
        if stream == "round":
            for i, (which, k) in enumerate(FFN_STREAM_ORDER):
                if which != 2 and FFN_STREAM_BOUNDS[k] == c * FFN_CHUNK_F:
                    land_piece(i)
        elif stream == "wait" and c == 0:
            for which in range(2):
                _ffn_weight_copy(which, w_hbm, base_scratch, stream_scratch).wait()
        h = h_ref[...]
        gate = jnp.dot(h, wg_ref[:, cols], preferred_element_type=jnp.float32)
        up = jnp.dot(h, wu_ref[:, cols], preferred_element_type=jnp.float32)
        act = (0.5 * gate) * (1.0 + jnp.tanh(0.5 * gate)) * up
        a_ref[:, cols] = act.astype(jnp.bfloat16)
    if stream == "round":
        for i, (which, _) in enumerate(FFN_STREAM_ORDER):
            if which == 2:
                land_piece(i)
        if on_streamed is not None:
            on_streamed()
    elif stream == "wait":
        _ffn_weight_copy(2, w_hbm, base_scratch, stream_scratch).wait()
    return jnp.dot(a_ref[...], wd_ref[...], preferred_element_type=jnp.float32)


def _ffn_proj_kernel(x_ref, g1_ref, wg_hbm, wu_hbm, wd_hbm, gm_ref, win_hbm, qg_ref, kg_ref, dm_ref,
                     wg2_ref, wu2_ref, wd2_ref, wo_ref, rel_ref, bucket_ref,
                     x1_ref, z_ref, wg2_bf_ref, wu2_bf_ref, wd2_bf_ref, wo_bf_ref, bias_ref,
                     *scratch):
    tm = x_ref.shape[0]
    base_scratch = scratch[:N_FFN_BASE_SCRATCH]
    round_scratch, win_ref = scratch[N_FFN_BASE_SCRATCH:-1], scratch[-1]

    win_pieces = tuple((lo, hi - lo, slot) for slot, (lo, hi) in
                       enumerate(((0, FFN_STAGE_DIM), (FFN_STAGE_DIM, D_IN))))

    def win_copy(lo, size, slot):
        stage_ref, sem_ref = round_scratch
        return pltpu.make_async_copy(win_hbm.at[:, pl.ds(lo, size)],
                                     stage_ref.at[slot, pl.ds(0, D_MODEL), pl.ds(0, size)],
                                     sem_ref.at[slot])

    def start_win_copies():
        for piece in win_pieces:
            win_copy(*piece).start()

    def land_win():
        stage_ref, _ = round_scratch
        for lo, size, slot in win_pieces:
            win_copy(lo, size, slot).wait()
            win_ref[:, lo:lo + size] = stage_ref[slot, :D_MODEL, :size].astype(jnp.bfloat16)

    first_half = pl.program_id(0) < pl.num_programs(0) // 2

    @pl.when(first_half)
    def _():
        wg2_bf_ref[...] = wg2_ref[...].astype(jnp.bfloat16)
        wd2_bf_ref[...] = wd2_ref[...].astype(jnp.bfloat16)

    @pl.when(jnp.logical_not(first_half))
    def _():
        wu2_bf_ref[...] = wu2_ref[...].astype(jnp.bfloat16)
        wo_bf_ref[...] = wo_ref[...].astype(jnp.bfloat16)

    def body(stream):
        if stream is not None:
            _write_band_bias(rel_ref, bucket_ref, bias_ref)
        y = _swiglu_tile(x_ref[...], g1_ref, (wg_hbm, wu_hbm, wd_hbm), base_scratch, stream,
                         round_scratch, on_streamed=start_win_copies)
        x1 = x_ref[...] + 0.5 * y
        x1_ref[...] = x1
        if stream is not None:
            land_win()
        h = _rmsnorm_rows(x1, gm_ref[...]).astype(jnp.bfloat16)
        z = jnp.dot(h, win_ref[...], preferred_element_type=jnp.float32)
        n_slab = (D_ATTN + 2 * D_KV) // V7X_MXU_DIM
        sq = jnp.concatenate(
            [z[:, i * V7X_MXU_DIM:(i + 1) * V7X_MXU_DIM] for i in range(n_slab)], axis=0)
        ms = jnp.dot((sq * sq).astype(jnp.bfloat16), dm_ref[...], preferred_element_type=jnp.float32)
        q_ms = jnp.concatenate([ms[i * tm:(i + 1) * tm] for i in range(n_slab - 1)], axis=1)
        k_ms = ms[(n_slab - 1) * tm:, :D_KV]
        q = z[:, :D_ATTN]
        k = z[:, D_ATTN:D_ATTN + D_KV]
        z_ref[:, :D_ATTN] = (q * lax.rsqrt(q_ms + EPS) * qg_ref[...]).astype(jnp.bfloat16)
        z_ref[:, D_ATTN:D_ATTN + D_KV] = (k * lax.rsqrt(k_ms + EPS) * kg_ref[...]).astype(jnp.bfloat16)
        z_ref[:, D_ATTN + D_KV:] = z[:, D_ATTN + D_KV:].astype(jnp.bfloat16)

    first = pl.program_id(0) == 0
    pl.when(first)(functools.partial(body, "round"))
    pl.when(jnp.logical_not(first))(functools.partial(body, None))


def _head_mean_matrix(width):
    idx = np.arange(width) // HEAD_DIM
    return jnp.asarray((idx[:, None] == idx[None, :]).astype(np.float32) / HEAD_DIM, jnp.bfloat16)


def _ffn_proj(x2d, ffn_gain, w_gate, w_up, w_down, mix_gain, w_in, q_gain, k_gain,
              next_w_gate, next_w_up, next_w_down, w_out, rel_bias):
    m = x2d.shape[0]
    tm = TILE_M
    n_steps = m // tm
    resident = functools.partial(pl.BlockSpec, pipeline_mode=pl.Buffered(1))
    rows = lambda width: pl.BlockSpec((tm, width), lambda i: (i, 0))
    hbm = pl.BlockSpec(memory_space=pltpu.HBM)
    half_steps = n_steps // 2
    wr = D_MODEL // half_steps
    wdr = D_FF // half_steps
    assert 2 * half_steps == n_steps and wr * half_steps == D_MODEL and wdr * half_steps == D_FF
    assert wr % 16 == 0 and wdr % 16 == 0
    early_slice = lambda i: (jnp.minimum(i, half_steps - 1), 0)
    late_slice = lambda i: (jnp.maximum(i - half_steps, 0), 0)
    wg_rows = pl.BlockSpec((wr, D_FF), early_slice)
    wd_rows = pl.BlockSpec((wdr, D_MODEL), early_slice)
    wu_rows = pl.BlockSpec((wr, D_FF), late_slice)
    wo_rows = pl.BlockSpec((wr, D_MODEL), late_slice)
    qg = (jnp.tile(q_gain, N_HEADS) * (HEAD_DIM ** -0.5)).reshape(1, D_ATTN)
    kg = jnp.tile(k_gain, N_KV_HEADS).reshape(1, D_KV)
    est = (_ffn_vmem_bytes(tm) + 2 * D_MODEL * D_IN
           + 2 * tm * D_MODEL * 4 * 2
           + 2 * tm * D_IN * 2
           + 4 * tm * D_IN * 4
           + 2 * 6 * (2 * wr * D_FF + wdr * D_MODEL + wr * D_MODEL)
           + 3 * N_HEADS * BLOCK * 2 * BLOCK * 4)
    bf16 = lambda shape: jax.ShapeDtypeStruct(shape, jnp.bfloat16)
    return pl.pallas_call(
        _ffn_proj_kernel,
        out_shape=(
            jax.ShapeDtypeStruct((m, D_MODEL), jnp.float32),
            bf16((m, D_IN)),
            bf16((D_MODEL, D_FF)), bf16((D_MODEL, D_FF)), bf16((D_FF, D_MODEL)),
            bf16((D_MODEL, D_MODEL)),
            jax.ShapeDtypeStruct((N_HEADS, BLOCK, 2 * BLOCK), jnp.float32),
        ),
        grid=(n_steps,),
        in_specs=[
            rows(D_MODEL),
            resident((1, D_MODEL), lambda i: (0, 0)),
            hbm, hbm, hbm,
            resident((1, D_MODEL), lambda i: (0, 0)),
            hbm,
            resident((1, D_ATTN), lambda i: (0, 0)),
            resident((1, D_KV), lambda i: (0, 0)),
            resident((V7X_MXU_DIM, V7X_MXU_DIM), lambda i: (0, 0)),
            wg_rows, wu_rows, wd_rows, wo_rows,
            pl.BlockSpec(memory_space=pltpu.SMEM),
            resident((BLOCK, 2 * BLOCK), lambda i: (0, 0)),
        ],
        out_specs=(rows(D_MODEL), rows(D_IN), wg_rows, wu_rows, wd_rows, wo_rows,
                   pl.BlockSpec((N_HEADS, BLOCK, 2 * BLOCK), lambda i: (0, 0, 0))),
        scratch_shapes=(_ffn_base_scratch(tm) + _ffn_round_scratch()
                        + [pltpu.VMEM((D_MODEL, D_IN), jnp.bfloat16)]),
        compiler_params=pltpu.CompilerParams(
            dimension_semantics=("arbitrary",),
            vmem_limit_bytes=_vmem_limit(est)),
        name="ffn_proj",
    )(x2d, ffn_gain.reshape(1, D_MODEL), w_gate, w_up, w_down, mix_gain.reshape(1, D_MODEL),
      w_in, qg, kg, _head_mean_matrix(V7X_MXU_DIM),
      next_w_gate, next_w_up, next_w_down, w_out, rel_bias, jnp.asarray(_band_bucket_table()))


def _t5_bucket(dist):
    n = np.maximum(dist, 0)
    max_exact = N_BUCKETS // 2
    large = max_exact + (np.log(np.maximum(n, 1) / max_exact)
                         / np.log(MAX_DISTANCE / max_exact)
                         * (N_BUCKETS - max_exact)).astype(np.int32)
    large = np.minimum(large, N_BUCKETS - 1)
    return np.where(n < max_exact, n, large).astype(np.int32)


def _band_bucket_table():
    ql = np.arange(BLOCK)[:, None]
    kl = np.arange(2 * BLOCK)[None, :]
    dist = ql + BLOCK - kl
    in_band = (dist >= 0) & (dist < WINDOW)
    return np.where(in_band, _t5_bucket(dist), -1).astype(np.int32)


def _write_band_bias(rel_ref, bucket_ref, o_ref):
    bucket = bucket_ref[...]
    for slot, head in enumerate(HEADS_PLAIN + HEADS_ROLLED):
        acc = jnp.full((BLOCK, 2 * BLOCK), NEG, jnp.float32)
        for b in range(N_BUCKETS):
            acc = jnp.where(bucket == b, rel_ref[b, head], acc)
        o_ref[slot] = acc


def _window_sum_matrices():
    ql = np.arange(BLOCK)[:, None]
    kl = np.arange(2 * BLOCK)[None, :]
    dist = ql + BLOCK - kl
    mats = [((dist >= 0) & (dist < w)).astype(np.float32) for w in POOL_WINDOWS]
    return jnp.asarray(np.stack(mats), jnp.bfloat16)


def _mix_tile(seq_first, t_base, sink_ref, x_ref, z_ref, zh_ref,
              bias_ref, wsum_ref, wo_ref, x2_ref, mix_scratch, g_ref, h_ref):
    kx_ref, kxr_ref, vx_ref, vxr_ref, ux_ref, pooled_ref, y_ref, wc_ref = mix_scratch
    tq = z_ref.shape[1]
    k_cols = slice(D_ATTN, D_ATTN + D_KV)
    v_cols = slice(D_ATTN + D_KV, D_ATTN + 2 * D_KV)
    u_cols = slice(D_ATTN + 2 * D_KV, D_IN)
    n_blocks = tq // BLOCK
    half = HEAD_DIM
    groups = ((HEADS_PLAIN, kx_ref, vx_ref, 0), (HEADS_ROLLED, kxr_ref, vxr_ref, len(HEADS_PLAIN)))
    lane = lax.broadcasted_iota(jnp.int32, (BLOCK, 2 * half), 1)
    low = lane < half
    state = {}

    def rows_of(j):
        return slice(j * BLOCK, (j + 1) * BLOCK)

    def keys_of(j):
        return slice(j * BLOCK, (j + 2) * BLOCK)

    def build_slabs():
        halo = jnp.where(seq_first, jnp.zeros_like(zh_ref[0, :, D_ATTN:]), zh_ref[0, :, D_ATTN:])
        kx_ref[:BLOCK] = halo[:, :D_KV]
        kx_ref[BLOCK:] = z_ref[0, :, k_cols]
        kxr_ref[...] = pltpu.roll(kx_ref[...].astype(jnp.float32), half, 1).astype(jnp.bfloat16)
        ones = jnp.ones((tq + BLOCK, D_KV), jnp.bfloat16)
        vx_ref[:BLOCK, :D_KV] = halo[:, D_KV:2 * D_KV]
        vx_ref[BLOCK:, :D_KV] = z_ref[0, :, v_cols]
        vx_ref[:, D_KV:] = ones
        vxr_ref[:, :D_KV] = pltpu.roll(
            vx_ref[:, :D_KV].astype(jnp.float32), half, 1).astype(jnp.bfloat16)
        vxr_ref[:, D_KV:] = ones
        ux_ref[:BLOCK] = halo[:, 2 * D_KV:]
        ux_ref[BLOCK:] = z_ref[0, :, u_cols]

    def scores(j, grp):
        heads, k_src, _, _ = groups[grp]
        q = z_ref[0, rows_of(j), :D_ATTN]
        zero = jnp.zeros((BLOCK, 2 * half), q.dtype)
        pieces = []
        for hd in heads:
            pair = q[:, (hd // 2) * 2 * half:(hd // 2 + 1) * 2 * half]
            pieces.append(jnp.where(low if hd % 2 == 0 else ~low, pair, zero))
        qs = jnp.concatenate(pieces, axis=0)
        state["logits", j, grp] = lax.dot_general(
            qs, k_src[keys_of(j), :], (((1,), (1,)), ((), ())),
            preferred_element_type=jnp.float32)

    def softmax_numerators(j, grp):
        heads, _, _, slot0 = groups[grp]
        logits = state.pop(("logits", j, grp))
        es, sink_terms = [], []
        for s, hd in enumerate(heads):
            lg = logits[s * BLOCK:(s + 1) * BLOCK] + bias_ref[slot0 + s]
            if j == 0:
                col = lax.broadcasted_iota(jnp.int32, (1, 2 * BLOCK), 1)
                lg = lg + jnp.where(seq_first & (col < BLOCK), NEG, 0.0).astype(jnp.float32)
            sink = sink_ref[hd]
            m = jnp.maximum(jnp.max(lg, axis=-1, keepdims=True), sink)
            es.append(jnp.exp(lg - m).astype(jnp.bfloat16))
            sink_terms.append(jnp.exp(sink - m))
        state["e", j, grp] = (jnp.concatenate(es, axis=0), sink_terms)

    def attend(j, grp):
        heads, _, v_src, _ = groups[grp]
        e, sink_terms = state.pop(("e", j, grp))
        pv = jnp.dot(e, v_src[keys_of(j), :], preferred_element_type=jnp.float32)
        for s, hd in enumerate(heads):
            blk = pv[s * BLOCK:(s + 1) * BLOCK]
            state["out", j, hd] = blk[:, :D_KV] / (blk[:, D_KV:] + sink_terms[s])

    def store_attention(j):
        for p in range(N_HEADS // 2):
            y_ref[rows_of(j), p * 2 * half:(p + 1) * 2 * half] = jnp.where(
                low, state.pop(("out", j, 2 * p)), state.pop(("out", j, 2 * p + 1))
            ).astype(jnp.bfloat16)

    def pool_means(g):
        w = POOL_WINDOWS[g]
        gc = slice(g * POOL_GROUP_DIM, (g + 1) * POOL_GROUP_DIM)
        windows = jnp.concatenate([ux_ref[keys_of(j), gc] for j in range(n_blocks)], axis=1)
        wsum = jnp.dot(wsum_ref[g], windows, preferred_element_type=jnp.float32)
        row = lax.broadcasted_iota(jnp.int32, (BLOCK, 1), 0)
        for j in range(n_blocks):
            cnt = jnp.minimum(t_base + j * BLOCK + row + 1, w).astype(jnp.float32)
            token = z_ref[0, rows_of(j), D_ATTN + 2 * D_KV + g * POOL_GROUP_DIM:
                          D_ATTN + 2 * D_KV + (g + 1) * POOL_GROUP_DIM]
            pooled = wsum[:, j * BLOCK:(j + 1) * BLOCK] * (1.0 / cnt) - token.astype(jnp.float32)
            pooled_ref[rows_of(j), gc] = pooled.astype(jnp.bfloat16)

    def project_pool():
        x2_ref[...] = x_ref[0] + jnp.dot(pooled_ref[...], wc_ref[...],
                                         preferred_element_type=jnp.float32)

    def project_attention():
        x2 = x2_ref[...] + jnp.dot(y_ref[...], wo_ref[:D_ATTN, :], preferred_element_type=jnp.float32)
        x2_ref[...] = x2
        h_ref[...] = _rmsnorm_rows(x2, g_ref[...]).astype(jnp.bfloat16)

    build_slabs()
    for g in range(len(POOL_WINDOWS)):
        pool_means(g)
    scores(0, 0)
    for j in range(n_blocks):
        scores(j, 1)
        softmax_numerators(j, 0)
        attend(j, 0)
        softmax_numerators(j, 1)
        if j == 0:
            project_pool()
        if j + 1 < n_blocks:
            scores(j + 1, 0)
        attend(j, 1)
        store_attention(j)
    project_attention()


N_MIX_SCRATCH = 9


def _mix_ffn_kernel(sink_ref, x_ref, z_ref, zh_ref,
                    bias_ref, wsum_ref, pw_ref, ps_ref, wo_ref, g2_ref, wg_hbm, wu_hbm, wd_hbm,
                    o_ref, *scratch, tiles_per_seq):
    mix_scratch, x2_ref = scratch[:N_MIX_SCRATCH - 1], scratch[N_MIX_SCRATCH - 1]
    base_scratch = scratch[N_MIX_SCRATCH:N_MIX_SCRATCH + N_FFN_BASE_SCRATCH]
    copy_scratch = scratch[N_MIX_SCRATCH + N_FFN_BASE_SCRATCH:]
    h_ref = base_scratch[3]
    t = pl.program_id(0)
    tq = z_ref.shape[1]
    i_seq = lax.rem(t, tiles_per_seq)
    w_hbm = (wg_hbm, wu_hbm, wd_hbm)

    def step(first):
        if first:
            _ffn_start_weight_copies(w_hbm, base_scratch, copy_scratch)
            for g in range(len(POOL_WINDOWS)):
                gc = slice(g * POOL_GROUP_DIM, (g + 1) * POOL_GROUP_DIM)
                pool_map = (pw_ref[g] * ps_ref[:, gc]).astype(jnp.bfloat16)
                mix_scratch[-1][gc, :] = jnp.dot(
                    pool_map, wo_ref[D_ATTN + g * POOL_GROUP_DIM:D_ATTN + (g + 1) * POOL_GROUP_DIM, :],
                    preferred_element_type=jnp.float32).astype(jnp.bfloat16)
        _mix_tile(i_seq == 0, i_seq * tq, sink_ref, x_ref, z_ref, zh_ref,
                  bias_ref, wsum_ref, wo_ref, x2_ref, mix_scratch, g2_ref, h_ref)
        y = _swiglu_tile(None, g2_ref, w_hbm, base_scratch, "wait" if first else None, copy_scratch)
        o_ref[0] = x2_ref[...] + 0.5 * y

    pl.when(t == 0)(functools.partial(step, True))
    pl.when(t > 0)(functools.partial(step, False))


def _mix_ffn(x3d, z3d, sinks, bias, pool_w, pool_scale, w_out_bf16, ffn_gain,
             w_gate_bf16, w_up_bf16, w_down_bf16):
    b, s, _ = x3d.shape
    tq = TILE_M
    nblk = tq // BLOCK
    tiles_per_seq = s // tq
    n_tiles = b * tiles_per_seq
    resident = functools.partial(pl.BlockSpec, pipeline_mode=pl.Buffered(1))
    hbm = pl.BlockSpec(memory_space=pltpu.HBM)

    def tile(width):
        return pl.BlockSpec((1, tq, width), lambda t: (t // tiles_per_seq, lax.rem(t, tiles_per_seq), 0))

    def halo(width):
        return pl.BlockSpec(
            (1, BLOCK, width),
            lambda t: (t // tiles_per_seq, jnp.maximum(lax.rem(t, tiles_per_seq) * nblk - 1, 0), 0))

    est = (_ffn_vmem_bytes(tq)
           + 2 * 2 * tq * D_MODEL * 4 + 2 * (tq + BLOCK) * D_IN * 2
           + N_HEADS * BLOCK * 2 * BLOCK * 4 + 2 * D_MODEL * D_MODEL + 4 * D_POOL * D_POOL
           + 2 * D_POOL * D_MODEL
           + 6 * (tq + BLOCK) * D_KV * 2 + (tq + BLOCK) * D_POOL * 2 + tq * D_MODEL * 2
           + tq * D_MODEL * 4 + 16 * BLOCK * 2 * BLOCK * 4 * 4)
    return pl.pallas_call(
        functools.partial(_mix_ffn_kernel, tiles_per_seq=tiles_per_seq),
        out_shape=jax.ShapeDtypeStruct((b, s, D_MODEL), jnp.float32),
        grid=(n_tiles,),
        in_specs=[
            pl.BlockSpec(memory_space=pltpu.SMEM),
            tile(D_MODEL),
            tile(D_IN), halo(D_IN),
            resident((N_HEADS, BLOCK, 2 * BLOCK), lambda t: (0, 0, 0)),
            resident((len(POOL_WINDOWS), BLOCK, 2 * BLOCK), lambda t: (0, 0, 0)),
            resident((len(POOL_WINDOWS), POOL_GROUP_DIM, POOL_GROUP_DIM), lambda t: (0, 0, 0)),
            resident((1, D_POOL), lambda t: (0, 0)),
            resident((D_MODEL, D_MODEL), lambda t: (0, 0)),
            resident((1, D_MODEL), lambda t: (0, 0)),
            hbm, hbm, hbm,
        ],
        out_specs=tile(D_MODEL),
        scratch_shapes=[
            pltpu.VMEM((tq + BLOCK, D_KV), jnp.bfloat16),
            pltpu.VMEM((tq + BLOCK, D_KV), jnp.bfloat16),
            pltpu.VMEM((tq + BLOCK, 2 * D_KV), jnp.bfloat16),
            pltpu.VMEM((tq + BLOCK, 2 * D_KV), jnp.bfloat16),
            pltpu.VMEM((tq + BLOCK, D_POOL), jnp.bfloat16),
            pltpu.VMEM((tq, D_POOL), jnp.bfloat16),
            pltpu.VMEM((tq, D_ATTN), jnp.bfloat16),
            pltpu.VMEM((D_POOL, D_MODEL), jnp.bfloat16),
            pltpu.VMEM((tq, D_MODEL), jnp.float32),
        ] + _ffn_base_scratch(tq) + _ffn_copy_scratch(),
        compiler_params=pltpu.CompilerParams(
            dimension_semantics=("arbitrary",),
            vmem_limit_bytes=_vmem_limit(est)),
        name="mix_ffn",
    )(sinks, x3d, z3d, z3d, bias, _window_sum_matrices(),
      pool_w, pool_scale.reshape(1, D_POOL),
      w_out_bf16, ffn_gain.reshape(1, D_MODEL), w_gate_bf16, w_up_bf16, w_down_bf16)


def kernel(x, ffn1_norm, ffn1_w_gate, ffn1_w_up, ffn1_w_down, mix_norm, w_in, q_norm, k_norm,
           attn_sinks, rel_bias, pool_w, pool_scale, w_out, ffn2_norm, ffn2_w_gate, ffn2_w_up,
           ffn2_w_down):
    b, s, d = x.shape
    assert (d, s % TILE_M) == (D_MODEL, 0)
    for l in range(ffn1_norm.shape[0]):
        x1, z, wg2, wu2, wd2, wo, bias = _ffn_proj(
            x.reshape(b * s, d), ffn1_norm[l], ffn1_w_gate[l], ffn1_w_up[l], ffn1_w_down[l],
            mix_norm[l], w_in[l], q_norm[l], k_norm[l],
            ffn2_w_gate[l], ffn2_w_up[l], ffn2_w_down[l], w_out[l], rel_bias)
        x = _mix_ffn(x1.reshape(b, s, d), z.reshape(b, s, D_IN), attn_sinks[l], bias,
                     pool_w[l], pool_scale[l], wo, ffn2_norm[l], wg2, wu2, wd2)
    return x
```

```python
import functools

import numpy as np
import jax
import jax.numpy as jnp
from jax import lax
from jax.experimental import pallas as pl
from jax.experimental.pallas import tpu as pltpu

D_MODEL = 1024
HEAD_DIM = 64
N_HEADS = 8
N_KV_HEADS = 2
D_ATTN = N_HEADS * HEAD_DIM
D_KV = N_KV_HEADS * HEAD_DIM
D_POOL = D_MODEL - D_ATTN
POOL_WINDOWS = (2, 4, 8, 16)
POOL_GROUP_DIM = D_POOL // len(POOL_WINDOWS)
D_IN = D_ATTN + 2 * D_KV + D_POOL
WINDOW = 128
BLOCK = 128
N_BUCKETS = 32
MAX_DISTANCE = 128
D_FF = 2816
EPS = 1e-6
NEG = -1e30
LOG2_E = 1.4426950408889634

V7X_MXU_DIM = 256
V7X_VMEM_BYTES = 64 * 1024 * 1024

HEADS_PLAIN = (0, 2, 5, 7)
HEADS_ROLLED = (1, 3, 4, 6)

TILE_M = 512
FFN_CHUNK_F = 256
FFN_STAGE_SLOTS = 2


VMEM_COMPILER_TEMPORARIES_BYTES = 8 << 20
VMEM_LIMIT_FLOOR_BYTES = 16 << 20
VMEM_LIMIT_CEILING_BYTES = V7X_VMEM_BYTES - (4 << 20)


def _vmem_limit(buffer_bytes):
    wanted = buffer_bytes + VMEM_COMPILER_TEMPORARIES_BYTES
    return int(min(VMEM_LIMIT_CEILING_BYTES, max(wanted, VMEM_LIMIT_FLOOR_BYTES)))


def _rmsnorm_rows(x32, gain_row):
    ms = jnp.mean(x32 * x32, axis=-1, keepdims=True)
    return x32 * lax.rsqrt(ms + EPS) * gain_row


N_FFN_CHUNKS = D_FF // FFN_CHUNK_F
N_FFN_BASE_SCRATCH = 5


def _ffn_base_scratch(tm):
    return [
        pltpu.VMEM((D_MODEL, D_FF), jnp.bfloat16),
        pltpu.VMEM((D_MODEL, D_FF), jnp.bfloat16),
        pltpu.VMEM((D_FF, D_MODEL), jnp.bfloat16),
        pltpu.VMEM((tm, D_MODEL), jnp.bfloat16),
        pltpu.VMEM((tm, D_FF), jnp.bfloat16),
    ]


FFN_STREAM_BOUNDS = (0, 1024, 2048, D_FF)
FFN_STREAM_ORDER = tuple((which, k) for k in range(len(FFN_STREAM_BOUNDS) - 1) for which in (0, 1)) + tuple(
    (2, k) for k in range(len(FFN_STREAM_BOUNDS) - 1))
FFN_STAGE_DIM = max(hi - lo for lo, hi in zip(FFN_STREAM_BOUNDS, FFN_STREAM_BOUNDS[1:]))


def _ffn_round_scratch():
    assert D_MODEL <= FFN_STAGE_DIM and all(b % FFN_CHUNK_F == 0 for b in FFN_STREAM_BOUNDS)
    return [
        pltpu.VMEM((FFN_STAGE_SLOTS, FFN_STAGE_DIM, FFN_STAGE_DIM), jnp.float32),
        pltpu.SemaphoreType.DMA((FFN_STAGE_SLOTS,)),
    ]


def _ffn_copy_scratch():
    return [pltpu.SemaphoreType.DMA((3,))]


def _ffn_vmem_bytes(tm):
    return (2 * 3 * D_MODEL * D_FF + FFN_STAGE_SLOTS * FFN_STAGE_DIM * FFN_STAGE_DIM * 4
            + tm * D_MODEL * 2 + tm * D_FF * 2 + 6 * tm * FFN_CHUNK_F * 4 + tm * D_MODEL * 4)


def _ffn_weight_copy(which, w_hbm, base_scratch, copy_scratch):
    (sem_ref,) = copy_scratch
    return pltpu.make_async_copy(w_hbm[which], base_scratch[which], sem_ref.at[which])


def _ffn_start_weight_copies(w_hbm, base_scratch, copy_scratch):
    for which in range(3):
        _ffn_weight_copy(which, w_hbm, base_scratch, copy_scratch).start()


def _swiglu_tile(x32, g_ref, w_hbm, base_scratch, stream, stream_scratch, on_streamed=None):
    wg_ref, wu_ref, wd_ref, h_ref, a_ref = base_scratch

    def piece_copy(i):
        stage_ref, sem_ref = stream_scratch
        which, k = FFN_STREAM_ORDER[i]
        lo, size = FFN_STREAM_BOUNDS[k], FFN_STREAM_BOUNDS[k + 1] - FFN_STREAM_BOUNDS[k]
        slot = i % FFN_STAGE_SLOTS
        if which == 2:
            return pltpu.make_async_copy(w_hbm[2].at[pl.ds(lo, size), :],
                                         stage_ref.at[slot, pl.ds(0, size), pl.ds(0, D_MODEL)],
                                         sem_ref.at[slot])
        return pltpu.make_async_copy(w_hbm[which].at[:, pl.ds(lo, size)],
                                     stage_ref.at[slot, pl.ds(0, D_MODEL), pl.ds(0, size)],
                                     sem_ref.at[slot])

    def land_piece(i):
        stage_ref, _ = stream_scratch
        which, k = FFN_STREAM_ORDER[i]
        lo, hi = FFN_STREAM_BOUNDS[k], FFN_STREAM_BOUNDS[k + 1]
        slot = i % FFN_STAGE_SLOTS
        piece_copy(i).wait()
        if which == 2:
            wd_ref[lo:hi, :] = stage_ref[slot, :hi - lo, :D_MODEL].astype(jnp.bfloat16)
        else:
            dst = (wg_ref, wu_ref)[which]
            dst[:, lo:hi] = stage_ref[slot, :D_MODEL, :hi - lo].astype(jnp.bfloat16)
        if i + FFN_STAGE_SLOTS < len(FFN_STREAM_ORDER):
            piece_copy(i + FFN_STAGE_SLOTS).start()

    if stream == "round":
        for i in range(FFN_STAGE_SLOTS):
            piece_copy(i).start()
    if x32 is not None:
        h_ref[...] = _rmsnorm_rows(x32, g_ref[...]).astype(jnp.bfloat16)
    for c in range(N_FFN_CHUNKS):
        cols = slice(c * FFN_CHUNK_F, (c + 1) * FFN_CHUNK_F)
        if stream == "round":
            for i, (which, k) in enumerate(FFN_STREAM_ORDER):
                if which != 2 and FFN_STREAM_BOUNDS[k] == c * FFN_CHUNK_F:
                    land_piece(i)
        elif stream == "wait" and c == 0:
            for which in range(2):
                _ffn_weight_copy(which, w_hbm, base_scratch, stream_scratch).wait()
        h = h_ref[...]
        gate = jnp.dot(h, wg_ref[:, cols], preferred_element_type=jnp.float32)
        up = jnp.dot(h, wu_ref[:, cols], preferred_element_type=jnp.float32)
        act = (0.5 * gate) * (1.0 + jnp.tanh(0.5 * gate)) * up
        a_ref[:, cols] = act.astype(jnp.bfloat16)
    if stream == "round":
        for i, (which, _) in enumerate(FFN_STREAM_ORDER):
            if which == 2:
                land_piece(i)
        if on_streamed is not None:
            on_streamed()
    elif stream == "wait":
        _ffn_weight_copy(2, w_hbm, base_scratch, stream_scratch).wait()
    return jnp.dot(a_ref[...], wd_ref[...], preferred_element_type=jnp.float32)


def _ffn_proj_kernel(x_ref, g1_ref, wg_hbm, wu_hbm, wd_hbm, gm_ref, win_hbm, qg_ref, kg_ref, dm_ref,
                     wg2_ref, wu2_ref, wd2_ref, wo_ref, rel_ref, bucket_ref,
                     x1_ref, z_ref, wg2_bf_ref, wu2_bf_ref, wd2_bf_ref, wo_bf_ref, bias_ref,
                     *scratch):
    tm = x_ref.shape[0]
    base_scratch = scratch[:N_FFN_BASE_SCRATCH]
    round_scratch, win_ref = scratch[N_FFN_BASE_SCRATCH:-1], scratch[-1]

    win_pieces = tuple((lo, hi - lo, slot) for slot, (lo, hi) in
                       enumerate(((0, FFN_STAGE_DIM), (FFN_STAGE_DIM, D_IN))))

    def win_copy(lo, size, slot):
        stage_ref, sem_ref = round_scratch
        return pltpu.make_async_copy(win_hbm.at[:, pl.ds(lo, size)],
                                     stage_ref.at[slot, pl.ds(0, D_MODEL), pl.ds(0, size)],
                                     sem_ref.at[slot])

    def start_win_copies():
        for piece in win_pieces:
            win_copy(*piece).start()

    def land_win():
        stage_ref, _ = round_scratch
        for lo, size, slot in win_pieces:
            win_copy(lo, size, slot).wait()
            win_ref[:, lo:lo + size] = stage_ref[slot, :D_MODEL, :size].astype(jnp.bfloat16)

    first_half = pl.program_id(0) < pl.num_programs(0) // 2

    @pl.when(first_half)
    def _():
        wg2_bf_ref[...] = wg2_ref[...].astype(jnp.bfloat16)
        wd2_bf_ref[...] = wd2_ref[...].astype(jnp.bfloat16)

    @pl.when(jnp.logical_not(first_half))
    def _():
        wu2_bf_ref[...] = wu2_ref[...].astype(jnp.bfloat16)
        wo_bf_ref[...] = wo_ref[...].astype(jnp.bfloat16)

    def body(stream):
        if stream is not None:
            _write_band_bias(rel_ref, bucket_ref, bias_ref)
        y = _swiglu_tile(x_ref[...], g1_ref, (wg_hbm, wu_hbm, wd_hbm), base_scratch, stream,
                         round_scratch, on_streamed=start_win_copies)
        x1 = x_ref[...] + 0.5 * y
        x1_ref[...] = x1
        if stream is not None:
            land_win()
        h = _rmsnorm_rows(x1, gm_ref[...]).astype(jnp.bfloat16)
        z = jnp.dot(h, win_ref[...], preferred_element_type=jnp.float32)
        n_slab = (D_ATTN + 2 * D_KV) // V7X_MXU_DIM
        sq = jnp.concatenate(
            [z[:, i * V7X_MXU_DIM:(i + 1) * V7X_MXU_DIM] for i in range(n_slab)], axis=0)
        ms = jnp.dot((sq * sq).astype(jnp.bfloat16), dm_ref[...], preferred_element_type=jnp.float32)
        q_ms = jnp.concatenate([ms[i * tm:(i + 1) * tm] for i in range(n_slab - 1)], axis=1)
        k_ms = ms[(n_slab - 1) * tm:, :D_KV]
        q = z[:, :D_ATTN]
        k = z[:, D_ATTN:D_ATTN + D_KV]
        z_ref[:, :D_ATTN] = (q * lax.rsqrt(q_ms + EPS) * qg_ref[...]).astype(jnp.bfloat16)
        z_ref[:, D_ATTN:D_ATTN + D_KV] = (k * lax.rsqrt(k_ms + EPS) * kg_ref[...]).astype(jnp.bfloat16)
        z_ref[:, D_ATTN + D_KV:] = z[:, D_ATTN + D_KV:].astype(jnp.bfloat16)

    first = pl.program_id(0) == 0
    pl.when(first)(functools.partial(body, "round"))
    pl.when(jnp.logical_not(first))(functools.partial(body, None))


def _head_mean_matrix(width):
    idx = np.arange(width) // HEAD_DIM
    return jnp.asarray((idx[:, None] == idx[None, :]).astype(np.float32) / HEAD_DIM, jnp.bfloat16)


def _ffn_proj(x2d, ffn_gain, w_gate, w_up, w_down, mix_gain, w_in, q_gain, k_gain,
              next_w_gate, next_w_up, next_w_down, w_out, rel_bias):
    m = x2d.shape[0]
    tm = TILE_M
    n_steps = m // tm
    resident = functools.partial(pl.BlockSpec, pipeline_mode=pl.Buffered(1))
    rows = lambda width: pl.BlockSpec((tm, width), lambda i: (i, 0))
    hbm = pl.BlockSpec(memory_space=pltpu.HBM)
    half_steps = n_steps // 2
    wr = D_MODEL // half_steps
    wdr = D_FF // half_steps
    assert 2 * half_steps == n_steps and wr * half_steps == D_MODEL and wdr * half_steps == D_FF
    assert wr % 16 == 0 and wdr % 16 == 0
    early_slice = lambda i: (jnp.minimum(i, half_steps - 1), 0)
    late_slice = lambda i: (jnp.maximum(i - half_steps, 0), 0)
    wg_rows = pl.BlockSpec((wr, D_FF), early_slice)
    wd_rows = pl.BlockSpec((wdr, D_MODEL), early_slice)
    wu_rows = pl.BlockSpec((wr, D_FF), late_slice)
    wo_rows = pl.BlockSpec((wr, D_MODEL), late_slice)
    qg = (jnp.tile(q_gain, N_HEADS) * (HEAD_DIM ** -0.5 * LOG2_E)).reshape(1, D_ATTN)
    kg = jnp.tile(k_gain, N_KV_HEADS).reshape(1, D_KV)
    est = (_ffn_vmem_bytes(tm) + 2 * D_MODEL * D_IN
           + 2 * tm * D_MODEL * 4 * 2
           + 2 * tm * D_IN * 2
           + 4 * tm * D_IN * 4
           + 2 * 6 * (2 * wr * D_FF + wdr * D_MODEL + wr * D_MODEL)
           + 3 * N_HEADS * BLOCK * 2 * BLOCK * 4)
    bf16 = lambda shape: jax.ShapeDtypeStruct(shape, jnp.bfloat16)
    return pl.pallas_call(
        _ffn_proj_kernel,
        out_shape=(
            jax.ShapeDtypeStruct((m, D_MODEL), jnp.float32),
            bf16((m, D_IN)),
            bf16((D_MODEL, D_FF)), bf16((D_MODEL, D_FF)), bf16((D_FF, D_MODEL)),
            bf16((D_MODEL, D_MODEL)),
            jax.ShapeDtypeStruct((N_HEADS, BLOCK, 2 * BLOCK), jnp.float32),
        ),
        grid=(n_steps,),
        in_specs=[
            rows(D_MODEL),
            resident((1, D_MODEL), lambda i: (0, 0)),
            hbm, hbm, hbm,
            resident((1, D_MODEL), lambda i: (0, 0)),
            hbm,
            resident((1, D_ATTN), lambda i: (0, 0)),
            resident((1, D_KV), lambda i: (0, 0)),
            resident((V7X_MXU_DIM, V7X_MXU_DIM), lambda i: (0, 0)),
            wg_rows, wu_rows, wd_rows, wo_rows,
            pl.BlockSpec(memory_space=pltpu.SMEM),
            resident((BLOCK, 2 * BLOCK), lambda i: (0, 0)),
        ],
        out_specs=(rows(D_MODEL), rows(D_IN), wg_rows, wu_rows, wd_rows, wo_rows,
                   pl.BlockSpec((N_HEADS, BLOCK, 2 * BLOCK), lambda i: (0, 0, 0))),
        scratch_shapes=(_ffn_base_scratch(tm) + _ffn_round_scratch()
                        + [pltpu.VMEM((D_MODEL, D_IN), jnp.bfloat16)]),
        compiler_params=pltpu.CompilerParams(
            dimension_semantics=("arbitrary",),
            vmem_limit_bytes=_vmem_limit(est)),
        name="ffn_proj",
    )(x2d, ffn_gain.reshape(1, D_MODEL), w_gate, w_up, w_down, mix_gain.reshape(1, D_MODEL),
      w_in, qg, kg, _head_mean_matrix(V7X_MXU_DIM),
      next_w_gate, next_w_up, next_w_down, w_out, rel_bias, jnp.asarray(_band_bucket_table()))


def _t5_bucket(dist):
    n = np.maximum(dist, 0)
    max_exact = N_BUCKETS // 2
    large = max_exact + (np.log(np.maximum(n, 1) / max_exact)
                         / np.log(MAX_DISTANCE / max_exact)
                         * (N_BUCKETS - max_exact)).astype(np.int32)
    large = np.minimum(large, N_BUCKETS - 1)
    return np.where(n < max_exact, n, large).astype(np.int32)


def _band_bucket_table():
    ql = np.arange(BLOCK)[:, None]
    kl = np.arange(2 * BLOCK)[None, :]
    dist = ql + BLOCK - kl
    in_band = (dist >= 0) & (dist < WINDOW)
    return np.where(in_band, _t5_bucket(dist), -1).astype(np.int32)


def _write_band_bias(rel_ref, bucket_ref, o_ref):
    bucket = bucket_ref[...]
    for slot, head in enumerate(HEADS_PLAIN + HEADS_ROLLED):
        acc = jnp.full((BLOCK, 2 * BLOCK), NEG, jnp.float32)
        for b in range(N_BUCKETS):
            acc = jnp.where(bucket == b, rel_ref[b, head] * LOG2_E, acc)
        o_ref[slot] = acc


def _window_sum_matrices():
    ql = np.arange(BLOCK)[:, None]
    kl = np.arange(2 * BLOCK)[None, :]
    dist = ql + BLOCK - kl
    mats = [((dist >= 0) & (dist < w)).astype(np.float32) for w in POOL_WINDOWS]
    return jnp.asarray(np.stack(mats), jnp.bfloat16)


def _mix_tile(seq_first, t_base, sink_ref, x_ref, z_ref, zh_ref,
              bias_ref, wsum_ref, wo_ref, x2_ref, mix_scratch, g_ref, h_ref):
    kx_ref, kxr_ref, vx_ref, vxr_ref, ux_ref, pooled_ref, y_ref, wc_ref = mix_scratch
    tq = z_ref.shape[1]
    k_cols = slice(D_ATTN, D_ATTN + D_KV)
    v_cols = slice(D_ATTN + D_KV, D_ATTN + 2 * D_KV)
    u_cols = slice(D_ATTN + 2 * D_KV, D_IN)
    n_blocks = tq // BLOCK
    half = HEAD_DIM
    groups = ((HEADS_PLAIN, kx_ref, vx_ref, 0), (HEADS_ROLLED, kxr_ref, vxr_ref, len(HEADS_PLAIN)))
    lane = lax.broadcasted_iota(jnp.int32, (BLOCK, 2 * half), 1)
    low = lane < half
    state = {}

    def rows_of(j):
        return slice(j * BLOCK, (j + 1) * BLOCK)

    def keys_of(j):
        return slice(j * BLOCK, (j + 2) * BLOCK)

    def build_slabs():
        halo = jnp.where(seq_first, jnp.zeros_like(zh_ref[0, :, D_ATTN:]), zh_ref[0, :, D_ATTN:])
        kx_ref[:BLOCK] = halo[:, :D_KV]
        kx_ref[BLOCK:] = z_ref[0, :, k_cols]
        kxr_ref[...] = pltpu.roll(kx_ref[...].astype(jnp.float32), half, 1).astype(jnp.bfloat16)
        ones = jnp.ones((tq + BLOCK, D_KV), jnp.bfloat16)
        vx_ref[:BLOCK, :D_KV] = halo[:, D_KV:2 * D_KV]
        vx_ref[BLOCK:, :D_KV] = z_ref[0, :, v_cols]
        vx_ref[:, D_KV:] = ones
        vxr_ref[:, :D_KV] = pltpu.roll(
            vx_ref[:, :D_KV].astype(jnp.float32), half, 1).astype(jnp.bfloat16)
        vxr_ref[:, D_KV:] = ones
        ux_ref[:BLOCK] = halo[:, 2 * D_KV:]
        ux_ref[BLOCK:] = z_ref[0, :, u_cols]

    def scores(j, grp):
        heads, k_src, _, _ = groups[grp]
        q = z_ref[0, rows_of(j), :D_ATTN]
        zero = jnp.zeros((BLOCK, 2 * half), q.dtype)
        pieces = []
        for hd in heads:
            pair = q[:, (hd // 2) * 2 * half:(hd // 2 + 1) * 2 * half]
            pieces.append(jnp.where(low if hd % 2 == 0 else ~low, pair, zero))
        qs = jnp.concatenate(pieces, axis=0)
        state["logits", j, grp] = lax.dot_general(
            qs, k_src[keys_of(j), :], (((1,), (1,)), ((), ())),
            preferred_element_type=jnp.float32)

    def softmax_numerators(j, grp):
        heads, _, _, slot0 = groups[grp]
        logits = state.pop(("logits", j, grp))
        es, sink_terms = [], []
        for s, hd in enumerate(heads):
            lg = logits[s * BLOCK:(s + 1) * BLOCK] + bias_ref[slot0 + s]
            if j == 0:
                col = lax.broadcasted_iota(jnp.int32, (1, 2 * BLOCK), 1)
                lg = lg + jnp.where(seq_first & (col < BLOCK), NEG, 0.0).astype(jnp.float32)
            sink = sink_ref[hd] * LOG2_E
            m = jnp.maximum(jnp.max(lg, axis=-1, keepdims=True), sink)
            es.append(jnp.exp2(lg - m).astype(jnp.bfloat16))
            sink_terms.append(jnp.exp2(sink - m))
        state["e", j, grp] = (jnp.concatenate(es, axis=0), sink_terms)

    def attend(j, grp):
        heads, _, v_src, _ = groups[grp]
        e, sink_terms = state.pop(("e", j, grp))
        pv = jnp.dot(e, v_src[keys_of(j), :], preferred_element_type=jnp.float32)
        for s, hd in enumerate(heads):
            blk = pv[s * BLOCK:(s + 1) * BLOCK]
            state["out", j, hd] = blk[:, :D_KV] / (blk[:, D_KV:] + sink_terms[s])

    def store_attention(j):
        for p in range(N_HEADS // 2):
            y_ref[rows_of(j), p * 2 * half:(p + 1) * 2 * half] = jnp.where(
                low, state.pop(("out", j, 2 * p)), state.pop(("out", j, 2 * p + 1))
            ).astype(jnp.bfloat16)

    def pool_means(g):
        w = POOL_WINDOWS[g]
        gc = slice(g * POOL_GROUP_DIM, (g + 1) * POOL_GROUP_DIM)
        windows = jnp.concatenate([ux_ref[keys_of(j), gc] for j in range(n_blocks)], axis=1)
        wsum = jnp.dot(wsum_ref[g], windows, preferred_element_type=jnp.float32)
        row = lax.broadcasted_iota(jnp.int32, (BLOCK, 1), 0)
        for j in range(n_blocks):
            cnt = jnp.minimum(t_base + j * BLOCK + row + 1, w).astype(jnp.float32)
            token = z_ref[0, rows_of(j), D_ATTN + 2 * D_KV + g * POOL_GROUP_DIM:
                          D_ATTN + 2 * D_KV + (g + 1) * POOL_GROUP_DIM]
            pooled = wsum[:, j * BLOCK:(j + 1) * BLOCK] * (1.0 / cnt) - token.astype(jnp.float32)
            pooled_ref[rows_of(j), gc] = pooled.astype(jnp.bfloat16)

    def project_pool():
        x2_ref[...] = x_ref[0] + jnp.dot(pooled_ref[...], wc_ref[...],
                                         preferred_element_type=jnp.float32)

    def project_attention():
        x2 = x2_ref[...] + jnp.dot(y_ref[...], wo_ref[:D_ATTN, :], preferred_element_type=jnp.float32)
        x2_ref[...] = x2
        h_ref[...] = _rmsnorm_rows(x2, g_ref[...]).astype(jnp.bfloat16)

    build_slabs()
    for g in range(len(POOL_WINDOWS)):
        pool_means(g)
    scores(0, 0)
    for j in range(n_blocks):
        scores(j, 1)
        softmax_numerators(j, 0)
        attend(j, 0)
        softmax_numerators(j, 1)
        if j == 0:
            project_pool()
        if j + 1 < n_blocks:
            scores(j + 1, 0)
        attend(j, 1)
        store_attention(j)
    project_attention()


N_MIX_SCRATCH = 9


def _mix_ffn_kernel(sink_ref, x_ref, z_ref, zh_ref,
                    bias_ref, wsum_ref, pw_ref, ps_ref, wo_ref, g2_ref, wg_hbm, wu_hbm, wd_hbm,
                    o_ref, *scratch, tiles_per_seq):
    mix_scratch, x2_ref = scratch[:N_MIX_SCRATCH - 1], scratch[N_MIX_SCRATCH - 1]
    base_scratch = scratch[N_MIX_SCRATCH:N_MIX_SCRATCH + N_FFN_BASE_SCRATCH]
    copy_scratch = scratch[N_MIX_SCRATCH + N_FFN_BASE_SCRATCH:]
    h_ref = base_scratch[3]
    t = pl.program_id(0)
    tq = z_ref.shape[1]
    i_seq = lax.rem(t, tiles_per_seq)
    w_hbm = (wg_hbm, wu_hbm, wd_hbm)

    def step(first):
        if first:
            _ffn_start_weight_copies(w_hbm, base_scratch, copy_scratch)
            for g in range(len(POOL_WINDOWS)):
                gc = slice(g * POOL_GROUP_DIM, (g + 1) * POOL_GROUP_DIM)
                pool_map = (pw_ref[g] * ps_ref[:, gc]).astype(jnp.bfloat16)
                mix_scratch[-1][gc, :] = jnp.dot(
                    pool_map, wo_ref[D_ATTN + g * POOL_GROUP_DIM:D_ATTN + (g + 1) * POOL_GROUP_DIM, :],
                    preferred_element_type=jnp.float32).astype(jnp.bfloat16)
        _mix_tile(i_seq == 0, i_seq * tq, sink_ref, x_ref, z_ref, zh_ref,
                  bias_ref, wsum_ref, wo_ref, x2_ref, mix_scratch, g2_ref, h_ref)
        y = _swiglu_tile(None, g2_ref, w_hbm, base_scratch, "wait" if first else None, copy_scratch)
        o_ref[0] = x2_ref[...] + 0.5 * y

    pl.when(t == 0)(functools.partial(step, True))
    pl.when(t > 0)(functools.partial(step, False))


def _mix_ffn(x3d, z3d, sinks, bias, pool_w, pool_scale, w_out_bf16, ffn_gain,
             w_gate_bf16, w_up_bf16, w_down_bf16):
    b, s, _ = x3d.shape
    tq = TILE_M
    nblk = tq // BLOCK
    tiles_per_seq = s // tq
    n_tiles = b * tiles_per_seq
    resident = functools.partial(pl.BlockSpec, pipeline_mode=pl.Buffered(1))
    hbm = pl.BlockSpec(memory_space=pltpu.HBM)

    def tile(width):
        return pl.BlockSpec((1, tq, width), lambda t: (t // tiles_per_seq, lax.rem(t, tiles_per_seq), 0))

    def halo(width):
        return pl.BlockSpec(
            (1, BLOCK, width),
            lambda t: (t // tiles_per_seq, jnp.maximum(lax.rem(t, tiles_per_seq) * nblk - 1, 0), 0))

    est = (_ffn_vmem_bytes(tq)
           + 2 * 2 * tq * D_MODEL * 4 + 2 * (tq + BLOCK) * D_IN * 2
           + N_HEADS * BLOCK * 2 * BLOCK * 4 + 2 * D_MODEL * D_MODEL + 4 * D_POOL * D_POOL
           + 2 * D_POOL * D_MODEL
           + 6 * (tq + BLOCK) * D_KV * 2 + (tq + BLOCK) * D_POOL * 2 + tq * D_MODEL * 2
           + tq * D_MODEL * 4 + 16 * BLOCK * 2 * BLOCK * 4 * 4)
    return pl.pallas_call(
        functools.partial(_mix_ffn_kernel, tiles_per_seq=tiles_per_seq),
        out_shape=jax.ShapeDtypeStruct((b, s, D_MODEL), jnp.float32),
        grid=(n_tiles,),
        in_specs=[
            pl.BlockSpec(memory_space=pltpu.SMEM),
            tile(D_MODEL),
            tile(D_IN), halo(D_IN),
            resident((N_HEADS, BLOCK, 2 * BLOCK), lambda t: (0, 0, 0)),
            resident((len(POOL_WINDOWS), BLOCK, 2 * BLOCK), lambda t: (0, 0, 0)),
            resident((len(POOL_WINDOWS), POOL_GROUP_DIM, POOL_GROUP_DIM), lambda t: (0, 0, 0)),
            resident((1, D_POOL), lambda t: (0, 0)),
            resident((D_MODEL, D_MODEL), lambda t: (0, 0)),
            resident((1, D_MODEL), lambda t: (0, 0)),
            hbm, hbm, hbm,
        ],
        out_specs=tile(D_MODEL),
        scratch_shapes=[
            pltpu.VMEM((tq + BLOCK, D_KV), jnp.bfloat16),
            pltpu.VMEM((tq + BLOCK, D_KV), jnp.bfloat16),
            pltpu.VMEM((tq + BLOCK, 2 * D_KV), jnp.bfloat16),
            pltpu.VMEM((tq + BLOCK, 2 * D_KV), jnp.bfloat16),
            pltpu.VMEM((tq + BLOCK, D_POOL), jnp.bfloat16),
            pltpu.VMEM((tq, D_POOL), jnp.bfloat16),
            pltpu.VMEM((tq, D_ATTN), jnp.bfloat16),
            pltpu.VMEM((D_POOL, D_MODEL), jnp.bfloat16),
            pltpu.VMEM((tq, D_MODEL), jnp.float32),
        ] + _ffn_base_scratch(tq) + _ffn_copy_scratch(),
        compiler_params=pltpu.CompilerParams(
            dimension_semantics=("arbitrary",),
            vmem_limit_bytes=_vmem_limit(est)),
        name="mix_ffn",
    )(sinks, x3d, z3d, z3d, bias, _window_sum_matrices(),
      pool_w, pool_scale.reshape(1, D_POOL),
      w_out_bf16, ffn_gain.reshape(1, D_MODEL), w_gate_bf16, w_up_bf16, w_down_bf16)


def kernel(x, ffn1_norm, ffn1_w_gate, ffn1_w_up, ffn1_w_down, mix_norm, w_in, q_norm, k_norm,
           attn_sinks, rel_bias, pool_w, pool_scale, w_out, ffn2_norm, ffn2_w_gate, ffn2_w_up,
           ffn2_w_down):
    b, s, d = x.shape
    assert (d, s % TILE_M) == (D_MODEL, 0)
    for l in range(ffn1_norm.shape[0]):
        x1, z, wg2, wu2, wd2, wo, bias = _ffn_proj(
            x.reshape(b * s, d), ffn1_norm[l], ffn1_w_gate[l], ffn1_w_up[l], ffn1_w_down[l],
            mix_norm[l], w_in[l], q_norm[l], k_norm[l],
            ffn2_w_gate[l], ffn2_w_up[l], ffn2_w_down[l], w_out[l], rel_bias)
        x = _mix_ffn(x1.reshape(b, s, d), z.reshape(b, s, D_IN), attn_sinks[l], bias,
                     pool_w[l], pool_scale[l], wo, ffn2_norm[l], wg2, wu2, wd2)
    return x
```
